```python
import math
import jax, jax.numpy as jnp
from jax import lax
import numpy as np

D_MODEL = 1024
BATCH = 8
SEQ = 8192
DEPTH = 1

N_ATTN_HEADS = 8
ATTN_HEAD_DIM = 64
N_KV_GROUPS = 2
Q_LORA_RANK = 256
N_IDX_HEADS = 8
IDX_HEAD_DIM = 64
TOPK_MAX = 256
Q_BLOCK = 128
N_BUCKETS = 32
MAX_DISTANCE = 128
N_REC_HEADS = 4
REC_HEAD_DIM = 128
CHUNK = 64
FFN_HIDDEN = -(-8 * D_MODEL // (3 * 256)) * 256

ATTN_WIDTH = N_ATTN_HEADS * ATTN_HEAD_DIM
KV_WIDTH = N_KV_GROUPS * ATTN_HEAD_DIM
IDX_Q_WIDTH = N_IDX_HEADS * IDX_HEAD_DIM
REC_WIDTH = N_REC_HEADS * REC_HEAD_DIM
IN_SPLITS = [Q_LORA_RANK, KV_WIDTH, KV_WIDTH, IDX_HEAD_DIM, N_IDX_HEADS,
             REC_WIDTH, REC_WIDTH, REC_WIDTH, REC_WIDTH, 2 * D_MODEL]
IN_WIDTH = sum(IN_SPLITS)
IN_POINTS = [int(p) for p in np.cumsum(IN_SPLITS)[:-1]]
EPS = 1e-6

kernel_name = "hybrid_dsa_hgrn2_gated_block"


def rmsnorm(x, w):
    xf = x.astype(jnp.float32)
    y = xf * lax.rsqrt(jnp.mean(xf * xf, axis=-1, keepdims=True) + EPS)
    return (y * w.astype(jnp.float32)).astype(x.dtype)


def t5_bucket(n):
    max_exact = N_BUCKETS // 2
    nf = jnp.maximum(n, 1).astype(jnp.float32)
    large = max_exact + (jnp.log(nf / max_exact) / math.log(MAX_DISTANCE / max_exact)
                         * (N_BUCKETS - max_exact)).astype(jnp.int32)
    large = jnp.minimum(large, N_BUCKETS - 1)
    return jnp.where(n < max_exact, n, large)


def dsa_attention(q, k, v, qi, ki, wi, rel_bias):
    B, S = q.shape[0], q.shape[1]
    k_sel = min(TOPK_MAX, S // 4)
    nblk = S // Q_BLOCK
    R = N_ATTN_HEADS // N_KV_GROUPS
    key_pos = jnp.arange(S, dtype=jnp.int32)
    idx_scale = IDX_HEAD_DIM ** -0.5
    attn_scale = ATTN_HEAD_DIM ** -0.5
    wi = wi * (N_IDX_HEADS ** -0.5)

    def to_blocks(a):
        return a.reshape(B, nblk, Q_BLOCK, *a.shape[2:]).swapaxes(0, 1)

    def block(args):
        qb, qib, wib, start = args
        pos_q = start + jnp.arange(Q_BLOCK, dtype=jnp.int32)
        logits = jnp.einsum('bqhd,bsd->bqhs', qib, ki) * idx_scale
        score = jnp.einsum('bqhs,bqh->bqs', jax.nn.relu(logits), wib).astype(jnp.float32)
        causal = key_pos[None, :] <= pos_q[:, None]
        score = jnp.where(causal[None], score, -jnp.inf)
        _, idx = lax.top_k(score, k_sel)
        ksel = jax.vmap(lambda kb, ib: kb[ib])(k, idx)
        vsel = jax.vmap(lambda vb, ib: vb[ib])(v, idx)
        dist = pos_q[None, :, None] - idx
        valid = dist >= 0
        bias = rel_bias[t5_bucket(jnp.maximum(dist, 0))].astype(jnp.float32)
        bias = bias.reshape(B, Q_BLOCK, k_sel, N_KV_GROUPS, R).transpose(0, 1, 3, 4, 2)
        qg = qb.reshape(B, Q_BLOCK, N_KV_GROUPS, R, ATTN_HEAD_DIM)
        s = jnp.einsum('bqgrd,bqkgd->bqgrk', qg, ksel).astype(jnp.float32) * attn_scale + bias
        s = jnp.where(valid[:, :, None, None, :], s, -jnp.inf)
        p = jax.nn.softmax(s, axis=-1).astype(vsel.dtype)
        o = jnp.einsum('bqgrk,bqkgd->bqgrd', p, vsel)
        return o.reshape(B, Q_BLOCK, ATTN_WIDTH)

    starts = jnp.arange(nblk, dtype=jnp.int32) * Q_BLOCK
    out = lax.map(block, (to_blocks(q), to_blocks(qi), to_blocks(wi), starts))
    return out.swapaxes(0, 1).reshape(B, S, ATTN_WIDTH)


def hgrn2(q_raw, f_raw, i_raw, lb):
    B, S = q_raw.shape[0], q_raw.shape[1]
    n = S // CHUNK
    H, dk = N_REC_HEADS, REC_HEAD_DIM
    f = lb + (1.0 - lb) * jax.nn.sigmoid(f_raw.astype(jnp.float32))
    g = jnp.log(f)
    kk = 1.0 - f
    qq = jax.nn.silu(q_raw.astype(jnp.float32))
    vv = i_raw.astype(jnp.float32)

    def to_chunks(a):
        return a.reshape(B, n, CHUNK, H, dk).transpose(1, 0, 3, 2, 4)

    mask = jnp.tril(jnp.ones((CHUNK, CHUNK), dtype=bool))

    def step(state, inp):
        qc, kc, vc, gc = inp
        A = jnp.cumsum(gc, axis=-2)
        o_inter = jnp.einsum('bhtc,bhcv->bhtv', qc * jnp.exp(A), state)
        diff = A[:, :, :, None, :] - A[:, :, None, :, :]
        decay = jnp.exp(jnp.where(mask[:, :, None], diff, -jnp.inf))
        P = jnp.einsum('bhtc,bhsc,bhtsc->bhts', qc, kc, decay)
        o = o_inter + jnp.einsum('bhts,bhsv->bhtv', P, vc)
        A_last = A[:, :, -1:, :]
        new_state = jnp.exp(A_last[:, :, 0, :])[..., None] * state + \
            jnp.einsum('bhsc,bhsv->bhcv', kc * jnp.exp(A_last - A), vc)
        return new_state, o

    s0 = jnp.zeros((B, H, dk, dk), jnp.float32)
    _, o = lax.scan(step, s0, (to_chunks(qq), to_chunks(kk), to_chunks(vv), to_chunks(g)))
    return o.transpose(1, 0, 3, 2, 4).reshape(B, S, H, dk)


def setup_inputs(seed: int = 0) -> dict:
    key = jax.random.key(seed)
    ks = jax.random.split(key, 20)
    f32 = jnp.float32

    def w(k, shape, fan_in):
        return jax.random.normal(k, shape, f32) * (fan_in ** -0.5)

    def gain(k, shape):
        return 1.0 + 0.02 * jax.random.normal(k, shape, f32)

    return {
        "x": jax.random.normal(ks[0], (BATCH, SEQ, D_MODEL), f32),
        "c": jax.random.normal(ks[1], (BATCH, D_MODEL), f32),
        "w_ada": w(ks[2], (DEPTH, D_MODEL, 6 * D_MODEL), D_MODEL),
        "b_ada": 0.02 * jax.random.normal(ks[3], (DEPTH, 6 * D_MODEL), f32),
        "norm1_w": gain(ks[4], (DEPTH, D_MODEL)),
        "w_in": w(ks[5], (DEPTH, D_MODEL, IN_WIDTH), D_MODEL),
        "q_norm_w": gain(ks[6], (DEPTH, Q_LORA_RANK)),
        "w_uq": w(ks[7], (DEPTH, Q_LORA_RANK, ATTN_WIDTH), Q_LORA_RANK),
        "w_uq_idx": w(ks[8], (DEPTH, Q_LORA_RANK, IDX_Q_WIDTH), Q_LORA_RANK),
        "rel_bias": 0.5 * jax.random.normal(ks[9], (N_BUCKETS, N_ATTN_HEADS), f32),
        "lb_logits": 0.5 * jax.random.normal(ks[10], (DEPTH + 1, REC_WIDTH), f32),
        "rec_norm_w": gain(ks[11], (DEPTH, REC_HEAD_DIM)),
        "w_branch_attn": w(ks[12], (DEPTH, ATTN_WIDTH, D_MODEL), ATTN_WIDTH),
        "w_branch_rec": w(ks[13], (DEPTH, REC_WIDTH, D_MODEL), REC_WIDTH),
        "w_out": w(ks[14], (DEPTH, D_MODEL, D_MODEL), D_MODEL),
        "norm2_w": gain(ks[15], (DEPTH, D_MODEL)),
        "w_ffn_gate": w(ks[16], (DEPTH, D_MODEL, FFN_HIDDEN), D_MODEL),
        "w_ffn_up": w(ks[17], (DEPTH, D_MODEL, FFN_HIDDEN), D_MODEL),
        "w_ffn_down": w(ks[18], (DEPTH, FFN_HIDDEN, D_MODEL), FFN_HIDDEN),
        "final_norm_w": gain(ks[19], (D_MODEL,)),
    }


def reference(x, c, w_ada, b_ada, norm1_w, w_in, q_norm_w, w_uq, w_uq_idx, rel_bias,
              lb_logits, rec_norm_w, w_branch_attn, w_branch_rec, w_out, norm2_w,
              w_ffn_gate, w_ffn_up, w_ffn_down, final_norm_w):
    B, S, _ = x.shape
    lower_bounds = jnp.cumsum(jax.nn.softmax(lb_logits.astype(jnp.float32), axis=0), axis=0)
    c_act = jax.nn.silu(c)
    for layer in range(DEPTH):
        mod = (c_act @ w_ada[layer] + b_ada[layer])[:, None, :]
        sh1, sc1, g1, sh2, sc2, g2 = jnp.split(mod, 6, axis=-1)
        h = rmsnorm(x, norm1_w[layer]) * (1.0 + sc1) + sh1
        proj = h @ w_in[layer]
        cq, k, v, ki, wi, rq, rf, ri, rg, gates = jnp.split(proj, IN_POINTS, axis=-1)
        cq = rmsnorm(cq, q_norm_w[layer])
        q = (cq @ w_uq[layer]).reshape(B, S, N_ATTN_HEADS, ATTN_HEAD_DIM)
        qi = (cq @ w_uq_idx[layer]).reshape(B, S, N_IDX_HEADS, IDX_HEAD_DIM)
        k = k.reshape(B, S, N_KV_GROUPS, ATTN_HEAD_DIM)
        v = v.reshape(B, S, N_KV_GROUPS, ATTN_HEAD_DIM)
        attn = dsa_attention(q, k, v, qi, ki, wi, rel_bias)
        y_attn = attn @ w_branch_attn[layer]
        rec = hgrn2(rq, rf, ri, lower_bounds[layer]).astype(x.dtype)
        rec = rmsnorm(rec, rec_norm_w[layer]).reshape(B, S, REC_WIDTH) * jax.nn.silu(rg)
        y_rec = rec @ w_branch_rec[layer]
        g_attn, g_rec = jnp.split(jax.nn.sigmoid(gates), 2, axis=-1)
        mixed = (g_attn * y_attn + g_rec * y_rec) @ w_out[layer]
        x = x + g1 * mixed
        h2 = rmsnorm(x, norm2_w[layer]) * (1.0 + sc2) + sh2
        ffn = (jax.nn.silu(h2 @ w_ffn_gate[layer]) * (h2 @ w_ffn_up[layer])) @ w_ffn_down[layer]
        x = x + g2 * ffn
    return rmsnorm(x, final_norm_w)
```

```python
import functools
import math

import jax
import jax.numpy as jnp
import numpy as np
from jax import lax
from jax.experimental import pallas as pl
from jax.experimental.pallas import tpu as pltpu

F32 = jnp.float32
BF16 = jnp.bfloat16

N_ATTN_HEADS = 8
ATTN_HEAD_DIM = 64
N_KV_GROUPS = 2
HEADS_PER_GROUP = N_ATTN_HEADS // N_KV_GROUPS
Q_LORA_RANK = 256
N_IDX_HEADS = 8
IDX_HEAD_DIM = 64
TOPK_MAX = 256
N_BUCKETS = 32
MAX_DISTANCE = 128
N_REC_HEADS = 4
REC_HEAD_DIM = 128
EPS = 1e-6

ATTN_WIDTH = N_ATTN_HEADS * ATTN_HEAD_DIM
KV_WIDTH = N_KV_GROUPS * ATTN_HEAD_DIM
REC_WIDTH = N_REC_HEADS * REC_HEAD_DIM

LANES = 128
SUBLANES = 8
VMEM_LIMIT_BYTES = 56 * 1024 * 1024

TOKEN_TILE = 512
Q_TILE = 256
K_TILE = 256
REC_CHUNK = 128
NEG_BIG = -1e30
BISECT_CAP = 400


def _rms(x, w):
    return x * lax.rsqrt(jnp.mean(x * x, axis=-1, keepdims=True) + EPS) * w


def _sigmoid(x):
    return 1.0 / (1.0 + jnp.exp(-x))


def _silu(x):
    return x * _sigmoid(x)


def _params(*sem):
    return pltpu.CompilerParams(dimension_semantics=sem, vmem_limit_bytes=VMEM_LIMIT_BYTES)


def _const_spec(shape):
    nd = len(shape)
    return pl.BlockSpec(shape, lambda *_: (0,) * nd, pipeline_mode=pl.Buffered(1))


def _ada_body(c_ref, w_ref, b_ref, o_ref):
    ca = _silu(c_ref[...])
    o_ref[...] = jnp.dot(ca, w_ref[...], precision=lax.Precision.HIGHEST,
                         preferred_element_type=F32) + b_ref[...]


def _ada(c, w, b):
    bsz, d = c.shape
    n = w.shape[1]
    tn = 1024
    return pl.pallas_call(
        _ada_body,
        grid=(n // tn,),
        in_specs=[pl.BlockSpec((bsz, d), lambda j: (0, 0)),
                  pl.BlockSpec((d, tn), lambda j: (0, j)),
                  pl.BlockSpec((1, tn), lambda j: (0, j))],
        out_specs=pl.BlockSpec((bsz, tn), lambda j: (0, j)),
        out_shape=jax.ShapeDtypeStruct((bsz, n), F32),
        compiler_params=_params("arbitrary"),
        name="ada",
    )(c, w, b.reshape(1, n))


def _in_proj_body(x_ref, mod_ref, n1_ref, wa_ref, wr_ref, wg_ref, qn_ref, wuq_ref, wui_ref,
                  q_ref, qi_ref, kv_ref, kiw_ref, rec_ref, rf_ref, gates_ref):
    x = x_ref[0]
    sh1 = mod_ref[0, 0:1, :]
    sc1 = mod_ref[0, 1:2, :]
    h = (_rms(x, n1_ref[...]) * (1.0 + sc1) + sh1).astype(BF16)

    pa = jnp.dot(h, wa_ref[...], preferred_element_type=F32)
    kv_ref[0] = pa[:, Q_LORA_RANK:Q_LORA_RANK + 2 * KV_WIDTH].astype(BF16)
    kiw_ref[0] = pa[:, Q_LORA_RANK + 2 * KV_WIDTH:]
    cqn = _rms(pa[:, :Q_LORA_RANK], qn_ref[...]).astype(BF16)
    qf = jnp.dot(cqn, wuq_ref[...], preferred_element_type=F32).astype(BF16)
    qif = jnp.dot(cqn, wui_ref[...], preferred_element_type=F32).astype(BF16)
    for hd in range(N_ATTN_HEADS):
        q_ref[0, hd] = qf[:, hd * ATTN_HEAD_DIM:(hd + 1) * ATTN_HEAD_DIM]
    for hd in range(N_IDX_HEADS):
        qi_ref[0, hd] = qif[:, hd * IDX_HEAD_DIM:(hd + 1) * IDX_HEAD_DIM]

    pr = jnp.dot(h, wr_ref[...], preferred_element_type=F32)
    rec_ref[0] = pr[:, :3 * REC_WIDTH].astype(BF16)
    rf_ref[0] = pr[:, 3 * REC_WIDTH:]
    gates_ref[0] = jnp.dot(h, wg_ref[...], preferred_element_type=F32).astype(BF16)


def _in_proj(x, mod, n1w, wa, wr, wg, qnw, wuq, wui, tm):
    bsz, s, d = x.shape
    wa_n = wa.shape[1]
    grid = (bsz, s // tm)
    tok = lambda w: pl.BlockSpec((1, tm, w), lambda b, i: (b, i, 0))
    head = lambda nh, hd: pl.BlockSpec((1, nh, tm, hd), lambda b, i: (b, 0, i, 0))
    out_shape = (
        jax.ShapeDtypeStruct((bsz, N_ATTN_HEADS, s, ATTN_HEAD_DIM), BF16),
        jax.ShapeDtypeStruct((bsz, N_IDX_HEADS, s, IDX_HEAD_DIM), BF16),
        jax.ShapeDtypeStruct((bsz, s, 2 * KV_WIDTH), BF16),
        jax.ShapeDtypeStruct((bsz, s, LANES), F32),
        jax.ShapeDtypeStruct((bsz, s, 3 * REC_WIDTH), BF16),
        jax.ShapeDtypeStruct((bsz, s, REC_WIDTH), F32),
        jax.ShapeDtypeStruct((bsz, s, 2 * d), BF16),
    )
    out_specs = (head(N_ATTN_HEADS, ATTN_HEAD_DIM), head(N_IDX_HEADS, IDX_HEAD_DIM),
                 tok(2 * KV_WIDTH), tok(LANES), tok(3 * REC_WIDTH), tok(REC_WIDTH), tok(2 * d))
    return pl.pallas_call(
        _in_proj_body,
        grid=grid,
        in_specs=[tok(d),
                  pl.BlockSpec((1, 6, d), lambda b, i: (b, 0, 0)),
                  _const_spec((1, d)),
                  _const_spec((d, wa_n)),
                  _const_spec((d, 4 * REC_WIDTH)),
                  _const_spec((d, 2 * d)),
                  _const_spec((1, Q_LORA_RANK)),
                  _const_spec((Q_LORA_RANK, ATTN_WIDTH)),
                  _const_spec((Q_LORA_RANK, N_IDX_HEADS * IDX_HEAD_DIM))],
        out_specs=out_specs,
        out_shape=out_shape,
        compiler_params=_params("parallel", "parallel"),
        name="in_proj",
    )(x, mod, n1w, wa, wr, wg, qnw, wuq, wui)


def _t5_bias_by_distance(rel_bias, max_dist):
    n = jnp.arange(max_dist, dtype=jnp.int32)
    max_exact = N_BUCKETS // 2
    nf = jnp.maximum(n, 1).astype(F32)
    large = max_exact + (jnp.log(nf / max_exact) / math.log(MAX_DISTANCE / max_exact)
                         * (N_BUCKETS - max_exact)).astype(jnp.int32)
    large = jnp.minimum(large, N_BUCKETS - 1)
    bucket = jnp.where(n < max_exact, n, large)
    return rel_bias[bucket].astype(F32)


def _near_bias_tiles(rel_bias, tq, tk, n_near):
    span = tq + n_near * tk
    bbd = _t5_bias_by_distance(rel_bias, span)
    far = rel_bias[N_BUCKETS - 1].astype(F32)
    t = jnp.arange(tq, dtype=jnp.int32)[:, None]
    tiles = []
    for u in range(n_near):
        key0 = (tq - tk) - (n_near - 1 - u) * tk
        sidx = key0 + jnp.arange(tk, dtype=jnp.int32)[None, :]
        dist = t - sidx
        b = bbd[jnp.clip(dist, 0, span - 1)] - far
        b = jnp.where((dist >= 0)[..., None], b, 0.0)
        b = b.transpose(2, 0, 1).reshape(N_KV_GROUPS, HEADS_PER_GROUP * tq, tk)
        tiles.append(b)
    return jnp.stack(tiles)


def _dsa_body(qi_ref, kiw_ref, q_ref, kit_ref, kt_ref, va_ref, bias_ref, o_ref,
              sc_ref, acc_ref, m_ref, *, tq, tk, k_sel, n_near):
    i = pl.program_id(1)
    row0 = i * tq
    n_kt = (i + 1) * (tq // tk)
    rows = lax.broadcasted_iota(jnp.int32, (tq, tk), 0)
    cols = lax.broadcasted_iota(jnp.int32, (tq, tk), 1)
    rel = cols - rows
    kf = float(k_sel)

    qi_all = qi_ref[0].reshape(N_IDX_HEADS * tq, IDX_HEAD_DIM)
    w = kiw_ref[0][:, IDX_HEAD_DIM:IDX_HEAD_DIM + N_IDX_HEADS] * (N_IDX_HEADS ** -0.5)
    w_cols = [w[:, h:h + 1] for h in range(N_IDX_HEADS)]

    def score_tile(j, carry):
        mx, mn = carry
        c0 = pl.multiple_of(j * tk, tk)
        lg = jnp.dot(qi_all, kit_ref[0, :, pl.ds(c0, tk)], preferred_element_type=F32)
        sc = jnp.maximum(lg[0:tq], 0.0) * w_cols[0]
        for h in range(1, N_IDX_HEADS):
            sc = sc + jnp.maximum(lg[h * tq:(h + 1) * tq], 0.0) * w_cols[h]
        causal = rel <= (row0 - c0)
        sc_ref[:, pl.ds(c0, tk)] = jnp.where(causal, sc, NEG_BIG)
        mx = jnp.maximum(mx, jnp.max(jnp.where(causal, sc, NEG_BIG), axis=1, keepdims=True))
        mn = jnp.minimum(mn, jnp.min(jnp.where(causal, sc, -NEG_BIG), axis=1, keepdims=True))
        return mx, mn

    mx, mn = lax.fori_loop(0, n_kt, score_tile,
                           (jnp.full((tq, 1), NEG_BIG, F32), jnp.full((tq, 1), -NEG_BIG, F32)))

    def count_ge(thr):
        def body(j, acc):
            c0 = pl.multiple_of(j * tk, tk)
            s = sc_ref[:, pl.ds(c0, tk)]
            for u in range(tk // LANES):
                acc = acc + jnp.where(s[:, u * LANES:(u + 1) * LANES] >= thr, 1.0, 0.0)
            return acc
        acc = lax.fori_loop(0, n_kt, body, jnp.zeros((tq, LANES), F32))
        return jnp.sum(acc, axis=1, keepdims=True)

    n_valid = (row0 + 1 + lax.broadcasted_iota(jnp.int32, (tq, 1), 0)).astype(F32)
    take_all = n_valid <= kf
    cnt_top = count_ge(mx)
    at_top = jnp.logical_and(jnp.logical_not(take_all), cnt_top >= kf)
    lo0 = jnp.where(take_all, NEG_BIG, jnp.where(at_top, mx, mn))
    done0 = jnp.logical_or(take_all, at_top)

    def bis_cond(st):
        it, lo, hi, done_f = st
        return jnp.logical_and(it < BISECT_CAP, jnp.min(done_f) < 0.5)

    def bis_step(st):
        it, lo, hi, done_f = st
        done = done_f > 0.5
        mid = lo + (hi - lo) * 0.5
        stuck = jnp.logical_or(mid <= lo, mid >= hi)
        cnt = count_ge(mid)
        ge = cnt >= kf
        upd = jnp.logical_not(jnp.logical_or(done, stuck))
        lo = jnp.where(jnp.logical_and(upd, ge), mid, lo)
        hi = jnp.where(jnp.logical_and(upd, jnp.logical_not(ge)), mid, hi)
        done = jnp.logical_or(jnp.logical_or(done, stuck), jnp.logical_and(ge, cnt == kf))
        return it + 1, lo, hi, done.astype(F32)

    _, thr, _, _ = lax.while_loop(bis_cond, bis_step,
                                  (jnp.int32(0), lo0, mx, done0.astype(F32)))

    def count_gt_eq(thr):
        def body(j, carry):
            agt, aeq = carry
            c0 = pl.multiple_of(j * tk, tk)
            s = sc_ref[:, pl.ds(c0, tk)]
            for u in range(tk // LANES):
                su = s[:, u * LANES:(u + 1) * LANES]
                agt = agt + jnp.where(su > thr, 1.0, 0.0)
                aeq = aeq + jnp.where(su == thr, 1.0, 0.0)
            return agt, aeq
        z = jnp.zeros((tq, LANES), F32)
        agt, aeq = lax.fori_loop(0, n_kt, body, (z, z))
        return jnp.sum(agt, axis=1, keepdims=True), jnp.sum(aeq, axis=1, keepdims=True)

    c_gt, c_eq = count_gt_eq(thr)
    need = kf - c_gt
    excess = jnp.logical_and(jnp.logical_not(take_all), c_eq > need)
    any_excess = jnp.max(excess.astype(F32)) > 0.5

    def tie_cut(_):
        def count_eq_upto(jcut):
            def body(j, acc):
                c0 = pl.multiple_of(j * tk, tk)
                s = sc_ref[:, pl.ds(c0, tk)]
                hit = jnp.logical_and(s == thr, (cols + c0) <= jcut)
                for u in range(tk // LANES):
                    acc = acc + jnp.where(hit[:, u * LANES:(u + 1) * LANES], 1.0, 0.0)
                return acc
            acc = lax.fori_loop(0, n_kt, body, jnp.zeros((tq, LANES), F32))
            return jnp.sum(acc, axis=1, keepdims=True)

        def step(_, st):
            lo_j, hi_j = st
            mid_j = lo_j + lax.shift_right_arithmetic(hi_j - lo_j, 1)
            ok = count_eq_upto(mid_j) >= need
            return jnp.where(ok, lo_j, mid_j), jnp.where(ok, mid_j, hi_j)

        lo_j = jnp.full((tq, 1), -1, jnp.int32)
        hi_j = jnp.full((tq, 1), 1, jnp.int32) * (row0 + tq)
        n_steps = 1 + int(math.ceil(math.log2(sc_ref.shape[1] + 2)))
        _, hi_j = lax.fori_loop(0, n_steps, step, (lo_j, hi_j))
        return hi_j

    big_j = jnp.full((tq, 1), 1, jnp.int32) * (row0 + tq)
    jcut = lax.cond(any_excess, tie_cut, lambda _: big_j, 0)
    jcut = jnp.where(excess, jcut, big_j)

    qs = [q_ref[0, g * HEADS_PER_GROUP:(g + 1) * HEADS_PER_GROUP].reshape(
        HEADS_PER_GROUP * tq, ATTN_HEAD_DIM) for g in range(N_KV_GROUPS)]
    acc_ref[...] = jnp.zeros(acc_ref.shape, F32)
    m_ref[...] = jnp.full(m_ref.shape, NEG_BIG, F32)

    def attend(j, near_u):
        c0 = pl.multiple_of(j * tk, tk)
        s_idx = sc_ref[:, pl.ds(c0, tk)]
        keep = jnp.logical_or(s_idx > thr,
                              jnp.logical_and(s_idx == thr, (cols + c0) <= jcut))
        keep = jnp.logical_and(keep, rel <= (row0 - c0))
        keep4 = jnp.concatenate([keep] * HEADS_PER_GROUP, axis=0)
        for g in range(N_KV_GROUPS):
            kt = kt_ref[0, g * ATTN_HEAD_DIM:(g + 1) * ATTN_HEAD_DIM, pl.ds(c0, tk)]
            s = jnp.dot(qs[g], kt, preferred_element_type=F32)
            if near_u is not None:
                s = s + bias_ref[near_u, g]
            s = jnp.where(keep4, s, NEG_BIG)
            m_old = m_ref[g]
            m_new = jnp.maximum(m_old, jnp.max(s, axis=1, keepdims=True))
            alpha = jnp.exp(m_old - m_new)
            p = jnp.exp(s - m_new).astype(BF16)
            pv = jnp.dot(p, va_ref[0, g, pl.ds(c0, tk), :], preferred_element_type=F32)
            acc_ref[g] = acc_ref[g] * alpha + pv
            m_ref[g] = m_new

    n_far = jnp.maximum(n_kt - n_near, 0)

    def far_step(j, c):
        attend(j, None)
        return c

    lax.fori_loop(0, n_far, far_step, 0)
    for u in range(n_near):
        j_u = n_kt - n_near + u

        @pl.when(j_u >= 0)
        def _():
            attend(j_u, u)

    lane = lax.broadcasted_iota(jnp.int32, (tq, LANES), 1)
    lane4 = lax.broadcasted_iota(jnp.int32, (HEADS_PER_GROUP * tq, LANES), 1)
    for g in range(N_KV_GROUPS):
        a = acc_ref[g]
        on = a / jnp.where(lane4 < ATTN_HEAD_DIM, pltpu.roll(a, ATTN_HEAD_DIM, axis=1), 1.0)
        for pr_ in range(HEADS_PER_GROUP // 2):
            lo_h = on[(2 * pr_) * tq:(2 * pr_ + 1) * tq]
            hi_h = pltpu.roll(on[(2 * pr_ + 1) * tq:(2 * pr_ + 2) * tq], ATTN_HEAD_DIM, axis=1)
            col = (g * HEADS_PER_GROUP + 2 * pr_) * ATTN_HEAD_DIM
            o_ref[0, :, col:col + LANES] = jnp.where(lane < ATTN_HEAD_DIM, lo_h, hi_h).astype(BF16)


def _dsa(q, qi, kiw, kit, kt, vaug, bias_tiles, *, tq, tk, k_sel):
    bsz, _, s, _ = q.shape
    n_near = bias_tiles.shape[0]
    body = functools.partial(_dsa_body, tq=tq, tk=tk, k_sel=k_sel, n_near=n_near)
    rt = HEADS_PER_GROUP * tq
    return pl.pallas_call(
        body,
        grid=(bsz, s // tq),
        in_specs=[pl.BlockSpec((1, N_IDX_HEADS, tq, IDX_HEAD_DIM), lambda b, i: (b, 0, i, 0)),
                  pl.BlockSpec((1, tq, LANES), lambda b, i: (b, i, 0)),
                  pl.BlockSpec((1, N_ATTN_HEADS, tq, ATTN_HEAD_DIM), lambda b, i: (b, 0, i, 0)),
                  pl.BlockSpec((1, IDX_HEAD_DIM, s), lambda b, i: (b, 0, 0)),
                  pl.BlockSpec((1, KV_WIDTH, s), lambda b, i: (b, 0, 0)),
                  pl.BlockSpec((1, N_KV_GROUPS, s, LANES), lambda b, i: (b, 0, 0, 0)),
                  _const_spec(bias_tiles.shape)],
        out_specs=pl.BlockSpec((1, tq, ATTN_WIDTH), lambda b, i: (b, i, 0)),
        out_shape=jax.ShapeDtypeStruct((bsz, s, ATTN_WIDTH), BF16),
        scratch_shapes=[pltpu.VMEM((tq, s), F32),
                        pltpu.VMEM((N_KV_GROUPS, rt, LANES), F32),
                        pltpu.VMEM((N_KV_GROUPS, rt, 1), F32)],
        compiler_params=_params("parallel", "arbitrary"),
        name="dsa",
    )(qi, kiw, q, kit, kt, vaug, bias_tiles)


def _split3(a):
    hi = a.astype(BF16)
    r1 = a - hi.astype(F32)
    mid = r1.astype(BF16)
    lo = (r1 - mid.astype(F32)).astype(BF16)
    return hi, mid, lo


def _hgrn2_body(rq_ref, ri_ref, rg_ref, rf_ref, lb_ref, nw_ref, o_ref,
                a_scr, k_scr, v_scr, *, chunk):
    s_len = rq_ref.shape[1]
    c = chunk
    d = REC_HEAD_DIM
    pad = SUBLANES
    lb = lb_ref[0]
    nw = nw_ref[...]
    ti = lax.broadcasted_iota(jnp.int32, (c, c), 0)
    si = lax.broadcasted_iota(jnp.int32, (c, c), 1)
    tril = (si <= ti).astype(BF16)
    eye = (lax.broadcasted_iota(jnp.int32, (d, d), 0)
           == lax.broadcasted_iota(jnp.int32, (d, d), 1)).astype(F32)
    sub = lax.broadcasted_iota(jnp.int32, (c, d), 0) & (SUBLANES - 1)
    levels = []
    b = SUBLANES
    while b < c:
        blk = 2 * b
        sh = blk.bit_length() - 1
        m = jnp.logical_and(lax.shift_right_logical(ti, sh) == lax.shift_right_logical(si, sh),
                            jnp.logical_and((ti & (blk - 1)) >= b, (si & (blk - 1)) < b))
        levels.append((b, m))
        b *= 2

    zpad = jnp.zeros((pad, d), F32)
    a_scr[0:pad, :] = zpad
    k_scr[0:pad, :] = zpad
    v_scr[0:pad, :] = zpad

    def step(n, state):
        t0 = pl.multiple_of(n * c, c)
        f = lb + (1.0 - lb) * _sigmoid(rf_ref[0, pl.ds(t0, c), :])
        g = jnp.log(f)
        kk = 1.0 - f
        qq = _silu(rq_ref[0, pl.ds(t0, c), :].astype(F32))
        vb = ri_ref[0, pl.ds(t0, c), :]
        vv = vb.astype(F32)

        g3 = _split3(g)
        a = (jnp.dot(tril, g3[0], preferred_element_type=F32)
             + jnp.dot(tril, g3[1], preferred_element_type=F32)
             + jnp.dot(tril, g3[2], preferred_element_type=F32))
        a_last = a[c - 1:c, :]

        o = jnp.dot((qq * jnp.exp(a)).astype(BF16), state.astype(BF16),
                    preferred_element_type=F32)

        p = jnp.zeros((c, c), F32)
        for b, m in levels:
            blk = 2 * b
            r = jnp.broadcast_to(a.reshape(c // blk, blk, d)[:, b - 1:b, :],
                                 (c // blk, blk, d)).reshape(c, d)
            qt = (qq * jnp.exp(jnp.minimum(a - r, 0.0))).astype(BF16)
            kt = (kk * jnp.exp(jnp.minimum(r - a, 0.0))).astype(BF16)
            pb = lax.dot_general(qt, kt, (((1,), (1,)), ((), ())), preferred_element_type=F32)
            p = p + jnp.where(m, pb, 0.0)
        o = o + jnp.dot(p.astype(BF16), vb, preferred_element_type=F32)

        a_scr[pad:pad + c, :] = a
        k_scr[pad:pad + c, :] = kk
        v_scr[pad:pad + c, :] = vv
        for dl in range(SUBLANES):
            a_s = a_scr[pad - dl:pad - dl + c, :]
            k_s = k_scr[pad - dl:pad - dl + c, :]
            v_s = v_scr[pad - dl:pad - dl + c, :]
            e = jnp.exp(jnp.where(sub >= dl, a - a_s, NEG_BIG))
            pd = jnp.sum(qq * k_s * e, axis=1, keepdims=True)
            o = o + pd * v_s

        kd = (kk * jnp.exp(a_last - a)).astype(BF16)
        upd = lax.dot_general(kd, vb, (((0,), (0,)), ((), ())), preferred_element_type=F32)
        e_col = jnp.sum(eye * jnp.exp(a_last), axis=1, keepdims=True)
        state = e_col * state + upd

        y = _rms(o, nw) * _silu(rg_ref[0, pl.ds(t0, c), :].astype(F32))
        o_ref[0, pl.ds(t0, c), :] = y.astype(BF16)
        return state

    lax.fori_loop(0, s_len // c, step, jnp.zeros((d, d), F32))


def _hgrn2(rec3, rf, lb, nw, *, chunk):
    bsz, s, _ = rec3.shape
    d = REC_HEAD_DIM
    h = N_REC_HEADS
    body = functools.partial(_hgrn2_body, chunk=chunk)
    col = lambda off: pl.BlockSpec((1, s, d), lambda b, hh: (b, 0, off + hh))
    return pl.pallas_call(
        body,
        grid=(bsz, h),
        in_specs=[col(0), col(h), col(2 * h),
                  pl.BlockSpec((1, s, d), lambda b, hh: (b, 0, hh)),
                  pl.BlockSpec((1, 1, d), lambda b, hh: (hh, 0, 0)),
                  pl.BlockSpec((1, d), lambda b, hh: (0, 0))],
        out_specs=pl.BlockSpec((1, s, d), lambda b, hh: (b, 0, hh)),
        out_shape=jax.ShapeDtypeStruct((bsz, s, REC_WIDTH), BF16),
        scratch_shapes=[pltpu.VMEM((chunk + SUBLANES, d), F32)] * 3,
        compiler_params=_params("parallel", "parallel"),
        name="hgrn2",
    )(rec3, rec3, rec3, rf, lb.reshape(h, 1, d), nw.reshape(1, d))


def _out_ffn_body(x_ref, attn_ref, rec_ref, gates_ref, mod_ref, wba_ref, wbr_ref, wo_ref,
                  n2_ref, wg_ref, wu_ref, wd_ref, fn_ref, o_ref, *, final):
    d = x_ref.shape[2]
    x = x_ref[0]
    g1 = mod_ref[0, 2:3, :]
    sh2 = mod_ref[0, 3:4, :]
    sc2 = mod_ref[0, 4:5, :]
    g2 = mod_ref[0, 5:6, :]
    ya = jnp.dot(attn_ref[0], wba_ref[...], preferred_element_type=F32)
    yr = jnp.dot(rec_ref[0], wbr_ref[...], preferred_element_type=F32)
    gates = gates_ref[0].astype(F32)
    mix = (_sigmoid(gates[:, :d]) * ya + _sigmoid(gates[:, d:]) * yr).astype(BF16)
    x1 = x + g1 * jnp.dot(mix, wo_ref[...], preferred_element_type=F32)
    h2 = (_rms(x1, n2_ref[...]) * (1.0 + sc2) + sh2).astype(BF16)
    gate = jnp.dot(h2, wg_ref[...], preferred_element_type=F32)
    up = jnp.dot(h2, wu_ref[...], preferred_element_type=F32)
    act = (_silu(gate) * up).astype(BF16)
    x2 = x1 + g2 * jnp.dot(act, wd_ref[...], preferred_element_type=F32)
    o_ref[0] = _rms(x2, fn_ref[...]) if final else x2


def _out_ffn(x, attn, rec, gates, mod, wba, wbr, wo, n2w, wg, wu, wd, fnw, tm, final):
    bsz, s, d = x.shape
    hid = wg.shape[1]
    tok = lambda w: pl.BlockSpec((1, tm, w), lambda b, i: (b, i, 0))
    return pl.pallas_call(
        functools.partial(_out_ffn_body, final=final),
        grid=(bsz, s // tm),
        in_specs=[tok(d), tok(ATTN_WIDTH), tok(REC_WIDTH), tok(2 * d),
                  pl.BlockSpec((1, 6, d), lambda b, i: (b, 0, 0)),
                  _const_spec((ATTN_WIDTH, d)), _const_spec((REC_WIDTH, d)), _const_spec((d, d)),
                  _const_spec((1, d)), _const_spec((d, hid)), _const_spec((d, hid)),
                  _const_spec((hid, d)), _const_spec((1, d))],
        out_specs=tok(d),
        out_shape=jax.ShapeDtypeStruct((bsz, s, d), F32),
        compiler_params=_params("parallel", "parallel"),
        name="out_ffn",
    )(x, attn, rec, gates, mod, wba, wbr, wo, n2w, wg, wu, wd, fnw)


def _pick_tile(n, want):
    t = min(want, n)
    while n % t:
        t //= 2
    return t


def kernel(x, c, w_ada, b_ada, norm1_w, w_in, q_norm_w, w_uq, w_uq_idx, rel_bias, lb_logits,
           rec_norm_w, w_branch_attn, w_branch_rec, w_out, norm2_w, w_ffn_gate, w_ffn_up,
           w_ffn_down, final_norm_w):
    bsz, s, d = x.shape
    depth = w_in.shape[0]
    k_sel = min(TOPK_MAX, s // 4)
    tm = _pick_tile(s, TOKEN_TILE)
    tq = _pick_tile(s, Q_TILE)
    tk = _pick_tile(tq, K_TILE)
    chunk = _pick_tile(s, REC_CHUNK)
    far_dist = MAX_DISTANCE
    n_near = -(-(tq + far_dist) // tk)

    lower_bounds = jnp.cumsum(jax.nn.softmax(lb_logits.astype(F32), axis=0), axis=0)
    bias_tiles = _near_bias_tiles(rel_bias, tq, tk, n_near)

    o_cq = 0
    o_k = o_cq + Q_LORA_RANK
    o_v = o_k + KV_WIDTH
    o_ki = o_v + KV_WIDTH
    o_wi = o_ki + IDX_HEAD_DIM
    o_rq = o_wi + N_IDX_HEADS
    o_rf = o_rq + REC_WIDTH
    o_ri = o_rf + REC_WIDTH
    o_rg = o_ri + REC_WIDTH
    o_gt = o_rg + REC_WIDTH

    for layer in range(depth):
        mod = _ada(c, w_ada[layer], b_ada[layer]).reshape(bsz, 6, d)
        wl = w_in[layer]
        pad_a = LANES - IDX_HEAD_DIM - N_IDX_HEADS
        wa = jnp.concatenate([wl[:, o_cq:o_rq], jnp.zeros((d, pad_a), wl.dtype)], axis=1).astype(BF16)
        wr = jnp.concatenate([wl[:, o_rq:o_rf], wl[:, o_ri:o_rg], wl[:, o_rg:o_gt],
                              wl[:, o_rf:o_ri]], axis=1).astype(BF16)
        wg = wl[:, o_gt:].astype(BF16)
        wuq = (w_uq[layer] * (ATTN_HEAD_DIM ** -0.5)).astype(BF16)
        wui = (w_uq_idx[layer] * (IDX_HEAD_DIM ** -0.5)).astype(BF16)

        q, qi, kv, kiw, rec3, rf, gates = _in_proj(
            x, mod, norm1_w[layer].reshape(1, d), wa, wr, wg,
            q_norm_w[layer].reshape(1, Q_LORA_RANK), wuq, wui, tm)

        kit = kiw[:, :, :IDX_HEAD_DIM].astype(BF16).transpose(0, 2, 1)
        kt = kv[:, :, :KV_WIDTH].transpose(0, 2, 1)
        v = kv[:, :, KV_WIDTH:].reshape(bsz, s, N_KV_GROUPS, ATTN_HEAD_DIM).transpose(0, 2, 1, 3)
        vaug = jnp.concatenate([v, jnp.ones_like(v)], axis=-1)

        attn = _dsa(q, qi, kiw, kit, kt, vaug, bias_tiles, tq=tq, tk=tk, k_sel=k_sel)
        rec = _hgrn2(rec3, rf, lower_bounds[layer], rec_norm_w[layer], chunk=chunk)

        x = _out_ffn(x, attn, rec, gates, mod,
                     w_branch_attn[layer].astype(BF16), w_branch_rec[layer].astype(BF16),
                     w_out[layer].astype(BF16), norm2_w[layer].reshape(1, d),
                     w_ffn_gate[layer].astype(BF16), w_ffn_up[layer].astype(BF16),
                     w_ffn_down[layer].astype(BF16),
                     final_norm_w.reshape(1, d), tm, layer == depth - 1)
    return x
```

```python
import functools
import math

import jax
import jax.numpy as jnp
from jax import lax
from jax.experimental import pallas as pl
from jax.experimental.pallas import tpu as pltpu

F32 = jnp.float32
BF16 = jnp.bfloat16

N_ATTN_HEADS = 8
ATTN_HEAD_DIM = 64
N_KV_GROUPS = 2
HEADS_PER_GROUP = N_ATTN_HEADS // N_KV_GROUPS
Q_LORA_RANK = 256
N_IDX_HEADS = 8
IDX_HEAD_DIM = 64
TOPK_MAX = 256
N_BUCKETS = 32
MAX_DISTANCE = 128
N_REC_HEADS = 4
REC_HEAD_DIM = 128
EPS = 1e-6

ATTN_WIDTH = N_ATTN_HEADS * ATTN_HEAD_DIM
KV_WIDTH = N_KV_GROUPS * ATTN_HEAD_DIM
REC_WIDTH = N_REC_HEADS * REC_HEAD_DIM

LANES = 128
SUBLANES = 8
BF16_SUBLANES = 16
VMEM_LIMIT_BYTES = 56 * 1024 * 1024

TOKEN_TILE = 512
Q_TILE = 256
K_TILE = 256
REC_CHUNK = 128
NEG_BIG = -1e30
BISECT_CAP = 400
COUNT_TILES = 2
FOLD_CHAINS = 4
LOG2_E = math.log2(math.e)
V_AUG_ROWS = ATTN_HEAD_DIM + BF16_SUBLANES


def _rms(x, w):
    return x * lax.rsqrt(jnp.mean(x * x, axis=-1, keepdims=True) + EPS) * w


def _sigmoid(x):
    return 1.0 / (1.0 + jnp.exp(-x))


def _silu(x):
    return x * _sigmoid(x)


def _params(*sem):
    return pltpu.CompilerParams(dimension_semantics=sem, vmem_limit_bytes=VMEM_LIMIT_BYTES)


def _const_spec(shape):
    nd = len(shape)
    return pl.BlockSpec(shape, lambda *_: (0,) * nd, pipeline_mode=pl.Buffered(1))


def _ada_body(c_ref, w_ref, b_ref, o_ref):
    ca = _silu(c_ref[...])
    o_ref[...] = jnp.dot(ca, w_ref[...], precision=lax.Precision.HIGHEST,
                         preferred_element_type=F32) + b_ref[...]


def _ada(c, w, b):
    bsz, d = c.shape
    n = w.shape[1]
    tn = 1024
    return pl.pallas_call(
        _ada_body,
        grid=(n // tn,),
        in_specs=[pl.BlockSpec((bsz, d), lambda j: (0, 0)),
                  pl.BlockSpec((d, tn), lambda j: (0, j)),
                  pl.BlockSpec((1, tn), lambda j: (0, j))],
        out_specs=pl.BlockSpec((bsz, tn), lambda j: (0, j)),
        out_shape=jax.ShapeDtypeStruct((bsz, n), F32),
        compiler_params=_params("arbitrary"),
        name="ada",
    )(c, w, b.reshape(1, n))


def _in_proj_body(x_ref, mod_ref, n1_ref, wa_ref, wr_ref, wg_ref, qn_ref, wuq_ref, wui_ref,
                  qt_ref, qit_ref, kv_ref, kiw_ref, rec_ref, rf_ref, gates_ref):
    tm = x_ref.shape[1]
    x = x_ref[0]
    sh1 = mod_ref[0, 0:1, :]
    sc1 = mod_ref[0, 1:2, :]
    h = (_rms(x, n1_ref[...]) * (1.0 + sc1) + sh1).astype(BF16)

    pa = jnp.dot(h, wa_ref[...], preferred_element_type=F32)
    kv_ref[0] = pa[:, Q_LORA_RANK:Q_LORA_RANK + 2 * KV_WIDTH].astype(BF16)
    kiw_ref[0] = pa[:, Q_LORA_RANK + 2 * KV_WIDTH:]
    cqn = _rms(pa[:, :Q_LORA_RANK], qn_ref[...]).astype(BF16)
    nt = (((1,), (1,)), ((), ()))
    qf = lax.dot_general(wuq_ref[...], cqn, nt, preferred_element_type=F32)
    qt_ref[0] = qf.reshape(N_ATTN_HEADS, ATTN_HEAD_DIM, tm).astype(BF16)
    qif = lax.dot_general(wui_ref[...], cqn, nt, preferred_element_type=F32)
    qit_ref[0] = qif.reshape(N_IDX_HEADS, IDX_HEAD_DIM, tm).astype(BF16)

    pr = jnp.dot(h, wr_ref[...], preferred_element_type=F32)
    rec_ref[0] = pr[:, :3 * REC_WIDTH].astype(BF16)
    rf_ref[0] = pr[:, 3 * REC_WIDTH:]
    gates_ref[0] = jnp.dot(h, wg_ref[...], preferred_element_type=F32).astype(BF16)


def _in_proj(x, mod, n1w, wa, wr, wg, qnw, wuq, wui, tm):
    bsz, s, d = x.shape
    wa_n = wa.shape[1]
    grid = (bsz, s // tm)
    tok = lambda w: pl.BlockSpec((1, tm, w), lambda b, i: (b, i, 0))
    head = lambda nh, hd: pl.BlockSpec((1, nh, hd, tm), lambda b, i: (b, 0, 0, i))
    out_shape = (
        jax.ShapeDtypeStruct((bsz, N_ATTN_HEADS, ATTN_HEAD_DIM, s), BF16),
        jax.ShapeDtypeStruct((bsz, N_IDX_HEADS, IDX_HEAD_DIM, s), BF16),
        jax.ShapeDtypeStruct((bsz, s, 2 * KV_WIDTH), BF16),
        jax.ShapeDtypeStruct((bsz, s, LANES), F32),
        jax.ShapeDtypeStruct((bsz, s, 3 * REC_WIDTH), BF16),
        jax.ShapeDtypeStruct((bsz, s, REC_WIDTH), F32),
        jax.ShapeDtypeStruct((bsz, s, 2 * d), BF16),
    )
    out_specs = (head(N_ATTN_HEADS, ATTN_HEAD_DIM), head(N_IDX_HEADS, IDX_HEAD_DIM),
                 tok(2 * KV_WIDTH), tok(LANES), tok(3 * REC_WIDTH), tok(REC_WIDTH), tok(2 * d))
    return pl.pallas_call(
        _in_proj_body,
        grid=grid,
        in_specs=[tok(d),
                  pl.BlockSpec((1, 6, d), lambda b, i: (b, 0, 0)),
                  _const_spec((1, d)),
                  _const_spec((d, wa_n)),
                  _const_spec((d, 4 * REC_WIDTH)),
                  _const_spec((d, 2 * d)),
                  _const_spec((1, Q_LORA_RANK)),
                  _const_spec((ATTN_WIDTH, Q_LORA_RANK)),
                  _const_spec((N_IDX_HEADS * IDX_HEAD_DIM, Q_LORA_RANK))],
        out_specs=out_specs,
        out_shape=out_shape,
        compiler_params=_params("parallel", "parallel"),
        name="in_proj",
    )(x, mod, n1w, wa, wr, wg, qnw, wuq, wui)


def _near_bias(rel_bias, tq, tk, n_near):
    u_rows = n_near * tk
    shift = u_rows - tq
    period = u_rows + tq
    e = jnp.arange(period, dtype=jnp.int32)
    dist = jnp.where(e < tq, e, e - period) + shift
    n = jnp.maximum(dist, 0)
    max_exact = N_BUCKETS // 2
    nf = jnp.maximum(n, 1).astype(F32)
    large = max_exact + (jnp.log(nf / max_exact) / math.log(MAX_DISTANCE / max_exact)
                         * (N_BUCKETS - max_exact)).astype(jnp.int32)
    bucket = jnp.where(n < max_exact, n, jnp.minimum(large, N_BUCKETS - 1))
    onehot = (bucket[:, None] == jnp.arange(N_BUCKETS, dtype=jnp.int32)[None, :]).astype(F32)
    rb = rel_bias.astype(F32)
    wv = jnp.dot(onehot, (rb - rb[N_BUCKETS - 1][None, :]) * LOG2_E,
                 precision=lax.Precision.HIGHEST)
    wv = jnp.where((dist >= 0)[:, None], wv, 0.0).T
    m = jnp.tile(wv, (1, u_rows))[:, :u_rows * (period - 1)].reshape(-1, u_rows, period - 1)
    return m[:, :, :tq]


def _dsa_body(qit_ref, w_ref, qt_ref, ki_ref, k_ref, vt_ref, bias_ref, o_ref,
              sc_ref, acc_ref, m_ref, s_scr, p_scr, *, tq, tk, k_sel, n_near):
    i = pl.program_id(1)
    row0 = i * tq
    n_kt = (i + 1) * (tq // tk)
    krow = lax.broadcasted_iota(jnp.int32, (tk, tq), 0)
    qcol = lax.broadcasted_iota(jnp.int32, (tk, tq), 1)
    rel = krow - qcol
    kf = float(k_sel)

    def fold(a, op):
        n = a.shape[0] // SUBLANES
        a = a.reshape(n // FOLD_CHAINS, FOLD_CHAINS, SUBLANES, tq)
        r = a[0]
        for t in range(1, n // FOLD_CHAINS):
            r = op(r, a[t])
        while r.shape[0] > 1:
            half = r.shape[0] // 2
            r = op(r[:half], r[half:])
        return r[0]

    w = w_ref[0] * (N_IDX_HEADS ** -0.5)

    def score_tile(j, carry):
        mx, mn = carry
        c0 = pl.multiple_of(j * tk, tk)
        kin = ki_ref[0, pl.ds(c0, tk), :]
        sc = None
        for h in range(N_IDX_HEADS):
            lg = jnp.dot(kin, qit_ref[0, h], preferred_element_type=F32)
            term = jnp.maximum(lg, 0.0) * w[h:h + 1, :]
            sc = term if sc is None else sc + term
        causal = rel <= (row0 - c0)
        sc_ref[pl.ds(c0, tk), :] = jnp.where(causal, sc, NEG_BIG)
        mx = jnp.maximum(mx, fold(jnp.where(causal, sc, NEG_BIG), jnp.maximum))
        mn = jnp.minimum(mn, fold(jnp.where(causal, sc, -NEG_BIG), jnp.minimum))
        return mx, mn

    mx8, mn8 = lax.fori_loop(0, n_kt, score_tile,
                             (jnp.full((SUBLANES, tq), NEG_BIG, F32),
                              jnp.full((SUBLANES, tq), -NEG_BIG, F32)))
    mx = jnp.max(mx8, axis=0, keepdims=True)
    mn = jnp.min(mn8, axis=0, keepdims=True)

    ck = COUNT_TILES * tk
    n_ct = lax.div(n_kt + (COUNT_TILES - 1), COUNT_TILES)
    for extra in range(COUNT_TILES - 1):
        @pl.when(n_kt + extra < n_ct * COUNT_TILES)
        def _():
            sc_ref[pl.ds(pl.multiple_of((n_kt + extra) * tk, tk), tk), :] = jnp.full(
                (tk, tq), NEG_BIG, F32)
    krow_c = lax.broadcasted_iota(jnp.int32, (ck, tq), 0)

    def count(pred):
        def body(j, acc):
            c0 = pl.multiple_of(j * ck, ck)
            hit = pred(sc_ref[pl.ds(c0, ck), :], c0)
            return acc + fold(jnp.where(hit, 1.0, 0.0), jnp.add)
        acc = lax.fori_loop(0, n_ct, body, jnp.zeros((SUBLANES, tq), F32))
        return jnp.sum(acc, axis=0, keepdims=True)

    n_valid = (row0 + 1 + lax.broadcasted_iota(jnp.int32, (1, tq), 1)).astype(F32)
    take_all = n_valid <= kf
    cnt_top = count(lambda s, c0: s >= mx)
    at_top = jnp.logical_and(jnp.logical_not(take_all), cnt_top >= kf)
    lo0 = jnp.where(take_all, NEG_BIG, jnp.where(at_top, mx, mn))
    done0 = jnp.logical_or(take_all, at_top)

    def bis_cond(st):
        it, lo, hi, done_f = st
        return jnp.logical_and(it < BISECT_CAP, jnp.min(done_f) < 0.5)

    def bis_step(st):
        it, lo, hi, done_f = st
        done = done_f > 0.5
        mid = lo + (hi - lo) * 0.5
        stuck = jnp.logical_or(mid <= lo, mid >= hi)
        cnt = count(lambda s, c0: s >= mid)
        ge = cnt >= kf
        upd = jnp.logical_not(jnp.logical_or(done, stuck))
        lo = jnp.where(jnp.logical_and(upd, ge), mid, lo)
        hi = jnp.where(jnp.logical_and(upd, jnp.logical_not(ge)), mid, hi)
        done = jnp.logical_or(jnp.logical_or(done, stuck), jnp.logical_and(ge, cnt == kf))
        return it + 1, lo, hi, done.astype(F32)

    _, thr, _, _ = lax.while_loop(bis_cond, bis_step,
                                  (jnp.int32(0), lo0, mx, done0.astype(F32)))

    c_gt = count(lambda s, c0: s > thr)
    c_eq = count(lambda s, c0: s == thr)
    need = kf - c_gt
    excess = jnp.logical_and(jnp.logical_not(take_all), c_eq > need)
    any_excess = jnp.max(excess.astype(F32)) > 0.5
    big_j = jnp.full((1, tq), 1, jnp.int32) * (row0 + tq)

    def tie_cut(_):
        def step(_, st):
            lo_j, hi_j = st
            mid_j = lo_j + lax.shift_right_arithmetic(hi_j - lo_j, 1)
            c_mid = count(lambda s, c0: jnp.logical_and(s == thr, (krow_c + c0) <= mid_j))
            ok = c_mid >= need
            return jnp.where(ok, lo_j, mid_j), jnp.where(ok, mid_j, hi_j)

        n_steps = 1 + int(math.ceil(math.log2(sc_ref.shape[0] + 2)))
        _, hi_j = lax.fori_loop(0, n_steps, step, (jnp.full((1, tq), -1, jnp.int32), big_j))
        return hi_j

    jcut = lax.cond(any_excess, tie_cut, lambda _: big_j, 0)
    jcut = jnp.where(excess, jcut, big_j)

    acc_ref[...] = jnp.zeros(acc_ref.shape, F32)
    m_ref[...] = jnp.full(m_ref.shape, NEG_BIG, F32)

    def attend(j, near_u):
        c0 = pl.multiple_of(j * tk, tk)
        s_idx = sc_ref[pl.ds(c0, tk), :]
        keep = jnp.logical_or(s_idx > thr,
                              jnp.logical_and(s_idx == thr, (krow + c0) <= jcut))
        if near_u is not None:
            keep = jnp.logical_and(keep, rel <= (row0 - c0))
        tile_max = []
        for h in range(N_ATTN_HEADS):
            g = h // HEADS_PER_GROUP
            s = jnp.dot(k_ref[0, g, pl.ds(c0, tk), :], qt_ref[0, h],
                        preferred_element_type=F32)
            if near_u is not None:
                s = s + bias_ref[h, near_u * tk:(near_u + 1) * tk, :]
            s = jnp.where(keep, s, NEG_BIG)
            s_scr[h] = s
            tile_max.append(jnp.max(fold(s, jnp.maximum), axis=0, keepdims=True))
        alphas = []
        for h in range(N_ATTN_HEADS):
            m_old = m_ref[h]
            m_new = jnp.maximum(m_old, tile_max[h])
            alphas.append(jnp.exp2(m_old - m_new))
            p_scr[h] = jnp.exp2(s_scr[h] - m_new).astype(BF16)
            m_ref[h] = m_new
        for h in range(N_ATTN_HEADS):
            g = h // HEADS_PER_GROUP
            pv = jnp.dot(vt_ref[0, g, :, pl.ds(c0, tk)], p_scr[h], preferred_element_type=F32)
            acc_ref[h] = acc_ref[h] * alphas[h] + pv

    n_far = jnp.maximum(n_kt - n_near, 0)

    def far_step(j, c):
        attend(j, None)
        return c

    lax.fori_loop(0, n_far, far_step, 0)
    for u in range(n_near):
        j_u = n_kt - n_near + u

        @pl.when(j_u >= 0)
        def _():
            attend(j_u, u)

    outs = []
    for h in range(N_ATTN_HEADS):
        a = acc_ref[h]
        outs.append(a[:ATTN_HEAD_DIM, :] / a[ATTN_HEAD_DIM:ATTN_HEAD_DIM + 1, :])
    o_ref[0] = jnp.concatenate(outs, axis=0).T.astype(BF16)


def _dsa(qt, qit, wt, ki, k, vt, bias, *, tq, tk, k_sel, n_near):
    bsz, _, _, s = qt.shape
    body = functools.partial(_dsa_body, tq=tq, tk=tk, k_sel=k_sel, n_near=n_near)
    return pl.pallas_call(
        body,
        grid=(bsz, s // tq),
        in_specs=[pl.BlockSpec((1, N_IDX_HEADS, IDX_HEAD_DIM, tq), lambda b, i: (b, 0, 0, i)),
                  pl.BlockSpec((1, N_IDX_HEADS, tq), lambda b, i: (b, 0, i)),
                  pl.BlockSpec((1, N_ATTN_HEADS, ATTN_HEAD_DIM, tq), lambda b, i: (b, 0, 0, i)),
                  pl.BlockSpec((1, s, IDX_HEAD_DIM), lambda b, i: (b, 0, 0)),
                  pl.BlockSpec((1, N_KV_GROUPS, s, ATTN_HEAD_DIM), lambda b, i: (b, 0, 0, 0)),
                  pl.BlockSpec((1, N_KV_GROUPS, V_AUG_ROWS, s), lambda b, i: (b, 0, 0, 0)),
                  _const_spec(bias.shape)],
        out_specs=pl.BlockSpec((1, tq, ATTN_WIDTH), lambda b, i: (b, i, 0)),
        out_shape=jax.ShapeDtypeStruct((bsz, s, ATTN_WIDTH), BF16),
        scratch_shapes=[pltpu.VMEM((s, tq), F32),
                        pltpu.VMEM((N_ATTN_HEADS, V_AUG_ROWS, tq), F32),
                        pltpu.VMEM((N_ATTN_HEADS, 1, tq), F32),
                        pltpu.VMEM((N_ATTN_HEADS, tk, tq), F32),
                        pltpu.VMEM((N_ATTN_HEADS, tk, tq), BF16)],
        compiler_params=_params("parallel", "arbitrary"),
        name="dsa",
    )(qit, wt, qt, ki, k, vt, bias)


def _split3(a):
    hi = a.astype(BF16)
    r1 = a - hi.astype(F32)
    mid = r1.astype(BF16)
    lo = (r1 - mid.astype(F32)).astype(BF16)
    return hi, mid, lo


def _hgrn2_body(rq_ref, ri_ref, rg_ref, rf_ref, lb_ref, nw_ref, o_ref,
                a_scr, k_scr, v_scr, *, chunk):
    s_len = rq_ref.shape[1]
    c = chunk
    d = REC_HEAD_DIM
    pad = SUBLANES
    lb = lb_ref[0]
    nw = nw_ref[...]
    ti = lax.broadcasted_iota(jnp.int32, (c, c), 0)
    si = lax.broadcasted_iota(jnp.int32, (c, c), 1)
    tril = (si <= ti).astype(BF16)
    eye = (lax.broadcasted_iota(jnp.int32, (d, d), 0)
           == lax.broadcasted_iota(jnp.int32, (d, d), 1)).astype(F32)
    sub = lax.broadcasted_iota(jnp.int32, (c, d), 0) & (SUBLANES - 1)
    levels = []
    b = SUBLANES
    while b < c:
        blk = 2 * b
        sh = blk.bit_length() - 1
        m = jnp.logical_and(lax.shift_right_logical(ti, sh) == lax.shift_right_logical(si, sh),
                            jnp.logical_and((ti & (blk - 1)) >= b, (si & (blk - 1)) < b))
        levels.append((b, m))
        b *= 2

    zpad = jnp.zeros((pad, d), F32)
    a_scr[0:pad, :] = zpad
    k_scr[0:pad, :] = zpad
    v_scr[0:pad, :] = zpad

    def step(n, state):
        t0 = pl.multiple_of(n * c, c)
        f = lb + (1.0 - lb) * _sigmoid(rf_ref[0, pl.ds(t0, c), :])
        g = jnp.log(f)
        kk = 1.0 - f
        qq = _silu(rq_ref[0, pl.ds(t0, c), :].astype(F32))
        vb = ri_ref[0, pl.ds(t0, c), :]
        vv = vb.astype(F32)

        g3 = _split3(g)
        a = (jnp.dot(tril, g3[0], preferred_element_type=F32)
             + jnp.dot(tril, g3[1], preferred_element_type=F32)
             + jnp.dot(tril, g3[2], preferred_element_type=F32))
        a_last = a[c - 1:c, :]

        o = jnp.dot((qq * jnp.exp(a)).astype(BF16), state.astype(BF16),
                    preferred_element_type=F32)

        p = jnp.zeros((c, c), F32)
        for b, m in levels:
            blk = 2 * b
            r = jnp.broadcast_to(a.reshape(c // blk, blk, d)[:, b - 1:b, :],
                                 (c // blk, blk, d)).reshape(c, d)
            qt = (qq * jnp.exp(jnp.minimum(a - r, 0.0))).astype(BF16)
            kt = (kk * jnp.exp(jnp.minimum(r - a, 0.0))).astype(BF16)
            pb = lax.dot_general(qt, kt, (((1,), (1,)), ((), ())), preferred_element_type=F32)
            p = p + jnp.where(m, pb, 0.0)
        o = o + jnp.dot(p.astype(BF16), vb, preferred_element_type=F32)

        a_scr[pad:pad + c, :] = a
        k_scr[pad:pad + c, :] = kk
        v_scr[pad:pad + c, :] = vv
        for dl in range(SUBLANES):
            a_s = a_scr[pad - dl:pad - dl + c, :]
            k_s = k_scr[pad - dl:pad - dl + c, :]
            v_s = v_scr[pad - dl:pad - dl + c, :]
            e = jnp.exp(jnp.where(sub >= dl, a - a_s, NEG_BIG))
            pd = jnp.sum(qq * k_s * e, axis=1, keepdims=True)
            o = o + pd * v_s

        kd = (kk * jnp.exp(a_last - a)).astype(BF16)
        upd = lax.dot_general(kd, vb, (((0,), (0,)), ((), ())), preferred_element_type=F32)
        e_col = jnp.sum(eye * jnp.exp(a_last), axis=1, keepdims=True)
        state = e_col * state + upd

        y = _rms(o, nw) * _silu(rg_ref[0, pl.ds(t0, c), :].astype(F32))
        o_ref[0, pl.ds(t0, c), :] = y.astype(BF16)
        return state

    lax.fori_loop(0, s_len // c, step, jnp.zeros((d, d), F32))


def _hgrn2(rec3, rf, lb, nw, *, chunk):
    bsz, s, _ = rec3.shape
    d = REC_HEAD_DIM
    h = N_REC_HEADS
    body = functools.partial(_hgrn2_body, chunk=chunk)
    col = lambda off: pl.BlockSpec((1, s, d), lambda b, hh: (b, 0, off + hh))
    return pl.pallas_call(
        body,
        grid=(bsz, h),
        in_specs=[col(0), col(h), col(2 * h),
                  pl.BlockSpec((1, s, d), lambda b, hh: (b, 0, hh)),
                  pl.BlockSpec((1, 1, d), lambda b, hh: (hh, 0, 0)),
                  pl.BlockSpec((1, d), lambda b, hh: (0, 0))],
        out_specs=pl.BlockSpec((1, s, d), lambda b, hh: (b, 0, hh)),
        out_shape=jax.ShapeDtypeStruct((bsz, s, REC_WIDTH), BF16),
        scratch_shapes=[pltpu.VMEM((chunk + SUBLANES, d), F32)] * 3,
        compiler_params=_params("parallel", "parallel"),
        name="hgrn2",
    )(rec3, rec3, rec3, rf, lb.reshape(h, 1, d), nw.reshape(1, d))


def _out_ffn_body(x_ref, attn_ref, rec_ref, gates_ref, mod_ref, wba_ref, wbr_ref, wo_ref,
                  n2_ref, wg_ref, wu_ref, wd_ref, fn_ref, o_ref, *, final):
    d = x_ref.shape[2]
    x = x_ref[0]
    g1 = mod_ref[0, 2:3, :]
    sh2 = mod_ref[0, 3:4, :]
    sc2 = mod_ref[0, 4:5, :]
    g2 = mod_ref[0, 5:6, :]
    ya = jnp.dot(attn_ref[0], wba_ref[...], preferred_element_type=F32)
    yr = jnp.dot(rec_ref[0], wbr_ref[...], preferred_element_type=F32)
    gates = gates_ref[0].astype(F32)
    mix = (_sigmoid(gates[:, :d]) * ya + _sigmoid(gates[:, d:]) * yr).astype(BF16)
    x1 = x + g1 * jnp.dot(mix, wo_ref[...], preferred_element_type=F32)
    h2 = (_rms(x1, n2_ref[...]) * (1.0 + sc2) + sh2).astype(BF16)
    gate = jnp.dot(h2, wg_ref[...], preferred_element_type=F32)
    up = jnp.dot(h2, wu_ref[...], preferred_element_type=F32)
    act = (_silu(gate) * up).astype(BF16)
    x2 = x1 + g2 * jnp.dot(act, wd_ref[...], preferred_element_type=F32)
    o_ref[0] = _rms(x2, fn_ref[...]) if final else x2


def _out_ffn(x, attn, rec, gates, mod, wba, wbr, wo, n2w, wg, wu, wd, fnw, tm, final):
    bsz, s, d = x.shape
    hid = wg.shape[1]
    tok = lambda w: pl.BlockSpec((1, tm, w), lambda b, i: (b, i, 0))
    return pl.pallas_call(
        functools.partial(_out_ffn_body, final=final),
        grid=(bsz, s // tm),
        in_specs=[tok(d), tok(ATTN_WIDTH), tok(REC_WIDTH), tok(2 * d),
                  pl.BlockSpec((1, 6, d), lambda b, i: (b, 0, 0)),
                  _const_spec((ATTN_WIDTH, d)), _const_spec((REC_WIDTH, d)), _const_spec((d, d)),
                  _const_spec((1, d)), _const_spec((d, hid)), _const_spec((d, hid)),
                  _const_spec((hid, d)), _const_spec((1, d))],
        out_specs=tok(d),
        out_shape=jax.ShapeDtypeStruct((bsz, s, d), F32),
        compiler_params=_params("parallel", "parallel"),
        name="out_ffn",
    )(x, attn, rec, gates, mod, wba, wbr, wo, n2w, wg, wu, wd, fnw)


def _pick_tile(n, want):
    t = min(want, n)
    while n % t:
        t //= 2
    return t


def _dsa_inputs(kv, kiw):
    bsz, s, _ = kv.shape
    ki = kiw[:, :, :IDX_HEAD_DIM].astype(BF16)
    wt = kiw[:, :, IDX_HEAD_DIM:IDX_HEAD_DIM + N_IDX_HEADS].transpose(0, 2, 1)
    k = kv[:, :, :KV_WIDTH].reshape(bsz, s, N_KV_GROUPS, ATTN_HEAD_DIM).transpose(0, 2, 1, 3)
    vt = kv[:, :, KV_WIDTH:].reshape(bsz, s, N_KV_GROUPS, ATTN_HEAD_DIM).transpose(0, 2, 3, 1)
    ones = jnp.ones((bsz, N_KV_GROUPS, V_AUG_ROWS - ATTN_HEAD_DIM, s), BF16)
    return ki, wt, k, jnp.concatenate([vt, ones], axis=2)


def kernel(x, c, w_ada, b_ada, norm1_w, w_in, q_norm_w, w_uq, w_uq_idx, rel_bias, lb_logits,
           rec_norm_w, w_branch_attn, w_branch_rec, w_out, norm2_w, w_ffn_gate, w_ffn_up,
           w_ffn_down, final_norm_w):
    bsz, s, d = x.shape
    depth = w_in.shape[0]
    k_sel = min(TOPK_MAX, s // 4)
    tm = _pick_tile(s, TOKEN_TILE)
    tq = _pick_tile(s, Q_TILE)
    tk = _pick_tile(tq, K_TILE)
    chunk = _pick_tile(s, REC_CHUNK)
    n_near = -(-(tq + MAX_DISTANCE) // tk)

    lower_bounds = jnp.cumsum(jax.nn.softmax(lb_logits.astype(F32), axis=0), axis=0)
    bias = _near_bias(rel_bias, tq, tk, n_near)

    o_cq = 0
    o_k = o_cq + Q_LORA_RANK
    o_v = o_k + KV_WIDTH
    o_ki = o_v + KV_WIDTH
    o_wi = o_ki + IDX_HEAD_DIM
    o_rq = o_wi + N_IDX_HEADS
    o_rf = o_rq + REC_WIDTH
    o_ri = o_rf + REC_WIDTH
    o_rg = o_ri + REC_WIDTH
    o_gt = o_rg + REC_WIDTH

    for layer in range(depth):
        mod = _ada(c, w_ada[layer], b_ada[layer]).reshape(bsz, 6, d)
        wl = w_in[layer]
        pad_a = LANES - IDX_HEAD_DIM - N_IDX_HEADS
        wa = jnp.concatenate([wl[:, o_cq:o_rq], jnp.zeros((d, pad_a), wl.dtype)], axis=1).astype(BF16)
        wr = jnp.concatenate([wl[:, o_rq:o_rf], wl[:, o_ri:o_rg], wl[:, o_rg:o_gt],
                              wl[:, o_rf:o_ri]], axis=1).astype(BF16)
        wg = wl[:, o_gt:].astype(BF16)
        wuq = (w_uq[layer] * (ATTN_HEAD_DIM ** -0.5 * LOG2_E)).astype(BF16).T
        wui = (w_uq_idx[layer] * (IDX_HEAD_DIM ** -0.5)).astype(BF16).T

        qt, qit, kv, kiw, rec3, rf, gates = _in_proj(
            x, mod, norm1_w[layer].reshape(1, d), wa, wr, wg,
            q_norm_w[layer].reshape(1, Q_LORA_RANK), wuq, wui, tm)

        ki, wt, k, vt = _dsa_inputs(kv, kiw)
        attn = _dsa(qt, qit, wt, ki, k, vt, bias, tq=tq, tk=tk, k_sel=k_sel, n_near=n_near)
        rec = _hgrn2(rec3, rf, lower_bounds[layer], rec_norm_w[layer], chunk=chunk)

        x = _out_ffn(x, attn, rec, gates, mod,
                     w_branch_attn[layer].astype(BF16), w_branch_rec[layer].astype(BF16),
                     w_out[layer].astype(BF16), norm2_w[layer].reshape(1, d),
                     w_ffn_gate[layer].astype(BF16), w_ffn_up[layer].astype(BF16),
                     w_ffn_down[layer].astype(BF16),
                     final_norm_w.reshape(1, d), tm, layer == depth - 1)
    return x
```

```python
import functools
import math

import jax
import jax.numpy as jnp
from jax import lax
from jax.experimental import pallas as pl
from jax.experimental.pallas import tpu as pltpu

F32 = jnp.float32
BF16 = jnp.bfloat16

N_ATTN_HEADS = 8
ATTN_HEAD_DIM = 64
N_KV_GROUPS = 2
HEADS_PER_GROUP = N_ATTN_HEADS // N_KV_GROUPS
Q_LORA_RANK = 256
N_IDX_HEADS = 8
IDX_HEAD_DIM = 64
TOPK_MAX = 256
N_BUCKETS = 32
MAX_DISTANCE = 128
N_REC_HEADS = 4
REC_HEAD_DIM = 128
EPS = 1e-6

ATTN_WIDTH = N_ATTN_HEADS * ATTN_HEAD_DIM
KV_WIDTH = N_KV_GROUPS * ATTN_HEAD_DIM
REC_WIDTH = N_REC_HEADS * REC_HEAD_DIM

LANES = 128
SUBLANES = 8
BF16_SUBLANES = 16
VMEM_LIMIT_BYTES = 56 * 1024 * 1024

TOKEN_TILE = 512
Q_TILE = 256
K_TILE = 256
REC_CHUNK = 128
NEG_BIG = -1e30
BISECT_CAP = 700
SNAP_AFTER = 24
COUNT_TILES = 2
FOLD_CHAINS = 4
LOG2_E = math.log2(math.e)
V_AUG_ROWS = ATTN_HEAD_DIM + BF16_SUBLANES


def _rms(x, w):
    return x * lax.rsqrt(jnp.mean(x * x, axis=-1, keepdims=True) + EPS) * w


def _sigmoid(x):
    return 1.0 / (1.0 + jnp.exp(-x))


def _silu(x):
    return x * _sigmoid(x)


def _params(*sem):
    return pltpu.CompilerParams(dimension_semantics=sem, vmem_limit_bytes=VMEM_LIMIT_BYTES)


def _const_spec(shape):
    nd = len(shape)
    return pl.BlockSpec(shape, lambda *_: (0,) * nd, pipeline_mode=pl.Buffered(1))


def _ada_body(c_ref, w_ref, b_ref, o_ref):
    ca = _silu(c_ref[...])
    o_ref[...] = jnp.dot(ca, w_ref[...], precision=lax.Precision.HIGHEST,
                         preferred_element_type=F32) + b_ref[...]


def _ada(c, w, b):
    bsz, d = c.shape
    n = w.shape[1]
    tn = 1024
    return pl.pallas_call(
        _ada_body,
        grid=(n // tn,),
        in_specs=[pl.BlockSpec((bsz, d), lambda j: (0, 0)),
                  pl.BlockSpec((d, tn), lambda j: (0, j)),
                  pl.BlockSpec((1, tn), lambda j: (0, j))],
        out_specs=pl.BlockSpec((bsz, tn), lambda j: (0, j)),
        out_shape=jax.ShapeDtypeStruct((bsz, n), F32),
        compiler_params=_params("arbitrary"),
        name="ada",
    )(c, w, b.reshape(1, n))


def _in_proj_body(x_ref, mod_ref, n1_ref, wa_ref, wr_ref, wg_ref, qn_ref, wuq_ref, wui_ref,
                  qt_ref, qit_ref, kv_ref, kiw_ref, rec_ref, rf_ref, gates_ref):
    tm = x_ref.shape[1]
    x = x_ref[0]
    sh1 = mod_ref[0, 0:1, :]
    sc1 = mod_ref[0, 1:2, :]
    h = (_rms(x, n1_ref[...]) * (1.0 + sc1) + sh1).astype(BF16)

    pa = jnp.dot(h, wa_ref[...], preferred_element_type=F32)
    kv_ref[0] = pa[:, Q_LORA_RANK:Q_LORA_RANK + 2 * KV_WIDTH].astype(BF16)
    kiw_ref[0] = pa[:, Q_LORA_RANK + 2 * KV_WIDTH:]
    cqn = _rms(pa[:, :Q_LORA_RANK], qn_ref[...]).astype(BF16)
    nt = (((1,), (1,)), ((), ()))
    qf = lax.dot_general(wuq_ref[...], cqn, nt, preferred_element_type=F32)
    qt_ref[0] = qf.reshape(N_ATTN_HEADS, ATTN_HEAD_DIM, tm).astype(BF16)
    qif = lax.dot_general(wui_ref[...], cqn, nt, preferred_element_type=F32)
    qit_ref[0] = qif.reshape(N_IDX_HEADS, IDX_HEAD_DIM, tm).astype(BF16)

    pr = jnp.dot(h, wr_ref[...], preferred_element_type=F32)
    rec_ref[0] = pr[:, :3 * REC_WIDTH].astype(BF16)
    rf_ref[0] = pr[:, 3 * REC_WIDTH:]
    gates_ref[0] = jnp.dot(h, wg_ref[...], preferred_element_type=F32).astype(BF16)


def _in_proj(x, mod, n1w, wa, wr, wg, qnw, wuq, wui, tm):
    bsz, s, d = x.shape
    wa_n = wa.shape[1]
    grid = (bsz, s // tm)
    tok = lambda w: pl.BlockSpec((1, tm, w), lambda b, i: (b, i, 0))
    head = lambda nh, hd: pl.BlockSpec((1, nh, hd, tm), lambda b, i: (b, 0, 0, i))
    out_shape = (
        jax.ShapeDtypeStruct((bsz, N_ATTN_HEADS, ATTN_HEAD_DIM, s), BF16),
        jax.ShapeDtypeStruct((bsz, N_IDX_HEADS, IDX_HEAD_DIM, s), BF16),
        jax.ShapeDtypeStruct((bsz, s, 2 * KV_WIDTH), BF16),
        jax.ShapeDtypeStruct((bsz, s, LANES), F32),
        jax.ShapeDtypeStruct((bsz, s, 3 * REC_WIDTH), BF16),
        jax.ShapeDtypeStruct((bsz, s, REC_WIDTH), F32),
        jax.ShapeDtypeStruct((bsz, s, 2 * d), BF16),
    )
    out_specs = (head(N_ATTN_HEADS, ATTN_HEAD_DIM), head(N_IDX_HEADS, IDX_HEAD_DIM),
                 tok(2 * KV_WIDTH), tok(LANES), tok(3 * REC_WIDTH), tok(REC_WIDTH), tok(2 * d))
    return pl.pallas_call(
        _in_proj_body,
        grid=grid,
        in_specs=[tok(d),
                  pl.BlockSpec((1, 6, d), lambda b, i: (b, 0, 0)),
                  _const_spec((1, d)),
                  _const_spec((d, wa_n)),
                  _const_spec((d, 4 * REC_WIDTH)),
                  _const_spec((d, 2 * d)),
                  _const_spec((1, Q_LORA_RANK)),
                  _const_spec((ATTN_WIDTH, Q_LORA_RANK)),
                  _const_spec((N_IDX_HEADS * IDX_HEAD_DIM, Q_LORA_RANK))],
        out_specs=out_specs,
        out_shape=out_shape,
        compiler_params=_params("parallel", "parallel"),
        name="in_proj",
    )(x, mod, n1w, wa, wr, wg, qnw, wuq, wui)


def _near_bias(rel_bias, tq, tk, n_near):
    u_rows = n_near * tk
    shift = u_rows - tq
    period = u_rows + tq
    e = jnp.arange(period, dtype=jnp.int32)
    dist = jnp.where(e < tq, e, e - period) + shift
    n = jnp.maximum(dist, 0)
    max_exact = N_BUCKETS // 2
    nf = jnp.maximum(n, 1).astype(F32)
    large = max_exact + (jnp.log(nf / max_exact) / math.log(MAX_DISTANCE / max_exact)
                         * (N_BUCKETS - max_exact)).astype(jnp.int32)
    bucket = jnp.where(n < max_exact, n, jnp.minimum(large, N_BUCKETS - 1))
    onehot = (bucket[:, None] == jnp.arange(N_BUCKETS, dtype=jnp.int32)[None, :]).astype(F32)
    rb = rel_bias.astype(F32)
    wv = jnp.dot(onehot, (rb - rb[N_BUCKETS - 1][None, :]) * LOG2_E,
                 precision=lax.Precision.HIGHEST)
    wv = jnp.where((dist >= 0)[:, None], wv, 0.0).T
    m = jnp.tile(wv, (1, u_rows))[:, :u_rows * (period - 1)].reshape(-1, u_rows, period - 1)
    return m[:, :, :tq]


def _dsa_body(qit_ref, w_ref, qt_ref, ki_ref, k_ref, vt_ref, bias_ref, o_ref,
              sc_ref, acc_ref, m_ref, s_scr, p_scr, *, tq, tk, k_sel, n_near):
    i = pl.program_id(1)
    row0 = i * tq
    n_kt = (i + 1) * (tq // tk)
    krow = lax.broadcasted_iota(jnp.int32, (tk, tq), 0)
    qcol = lax.broadcasted_iota(jnp.int32, (tk, tq), 1)
    rel = krow - qcol
    kf = float(k_sel)

    def fold(a, op):
        n = a.shape[0] // SUBLANES
        a = a.reshape(n // FOLD_CHAINS, FOLD_CHAINS, SUBLANES, tq)
        r = a[0]
        for t in range(1, n // FOLD_CHAINS):
            r = op(r, a[t])
        while r.shape[0] > 1:
            half = r.shape[0] // 2
            r = op(r[:half], r[half:])
        return r[0]

    w = w_ref[0] * (N_IDX_HEADS ** -0.5)

    def score_tile(j, carry):
        mx, mn = carry
        c0 = pl.multiple_of(j * tk, tk)
        kin = ki_ref[0, pl.ds(c0, tk), :]
        sc = None
        for h in range(N_IDX_HEADS):
            lg = jnp.dot(kin, qit_ref[0, h], preferred_element_type=F32)
            term = jnp.maximum(lg, 0.0) * w[h:h + 1, :]
            sc = term if sc is None else sc + term
        causal = rel <= (row0 - c0)
        sc_ref[pl.ds(c0, tk), :] = jnp.where(causal, sc, NEG_BIG)
        mx = jnp.maximum(mx, fold(jnp.where(causal, sc, NEG_BIG), jnp.maximum))
        mn = jnp.minimum(mn, fold(jnp.where(causal, sc, -NEG_BIG), jnp.minimum))
        return mx, mn

    mx8, mn8 = lax.fori_loop(0, n_kt, score_tile,
                             (jnp.full((SUBLANES, tq), NEG_BIG, F32),
                              jnp.full((SUBLANES, tq), -NEG_BIG, F32)))
    mx = jnp.max(mx8, axis=0, keepdims=True)
    mn = jnp.min(mn8, axis=0, keepdims=True)

    ck = COUNT_TILES * tk
    n_ct = lax.div(n_kt + (COUNT_TILES - 1), COUNT_TILES)
    for extra in range(COUNT_TILES - 1):
        @pl.when(n_kt + extra < n_ct * COUNT_TILES)
        def _():
            sc_ref[pl.ds(pl.multiple_of((n_kt + extra) * tk, tk), tk), :] = jnp.full(
                (tk, tq), NEG_BIG, F32)
    krow_c = lax.broadcasted_iota(jnp.int32, (ck, tq), 0)

    def count(pred):
        def body(j, acc):
            c0 = pl.multiple_of(j * ck, ck)
            hit = pred(sc_ref[pl.ds(c0, ck), :], c0)
            return acc + fold(jnp.where(hit, 1.0, 0.0), jnp.add)
        acc = lax.fori_loop(0, n_ct, body, jnp.zeros((SUBLANES, tq), F32))
        return jnp.sum(acc, axis=0, keepdims=True)

    n_valid = (row0 + 1 + lax.broadcasted_iota(jnp.int32, (1, tq), 1)).astype(F32)
    take_all = n_valid <= kf
    cnt_top = count(lambda s, c0: s >= mx)
    at_top = jnp.logical_and(jnp.logical_not(take_all), cnt_top >= kf)
    lo0 = jnp.where(take_all, NEG_BIG, jnp.where(at_top, mx, mn))
    done0 = jnp.logical_or(take_all, at_top)

    def bis_cond(st):
        it, lo, hi, done_f = st
        return jnp.logical_and(it < BISECT_CAP, jnp.min(done_f) < 0.5)

    def halve(st):
        lo, hi, done_f = st
        done = done_f > 0.5
        mid = lo + (hi - lo) * 0.5
        stuck = jnp.logical_or(mid <= lo, mid >= hi)
        cnt = count(lambda s, c0: s >= mid)
        ge = cnt >= kf
        upd = jnp.logical_not(jnp.logical_or(done, stuck))
        lo = jnp.where(jnp.logical_and(upd, ge), mid, lo)
        hi = jnp.where(jnp.logical_and(upd, jnp.logical_not(ge)), mid, hi)
        done = jnp.logical_or(jnp.logical_or(done, stuck), jnp.logical_and(ge, cnt == kf))
        return lo, hi, done.astype(F32)

    def snap(st):
        lo, hi, done_f = st
        done = done_f > 0.5

        def body(j, acc):
            s = sc_ref[pl.ds(pl.multiple_of(j * ck, ck), ck), :]
            return jnp.maximum(acc, fold(jnp.where(s < hi, s, NEG_BIG), jnp.maximum))
        below = jnp.max(lax.fori_loop(0, n_ct, body, jnp.full((SUBLANES, tq), NEG_BIG, F32)),
                        axis=0, keepdims=True)
        found = count(lambda s, c0: s >= below) >= kf
        live = jnp.logical_not(done)
        lo = jnp.where(jnp.logical_and(live, found), below, lo)
        hi = jnp.where(jnp.logical_and(live, jnp.logical_not(found)), below, hi)
        return lo, hi, jnp.logical_or(done, found).astype(F32)

    def bis_step(st):
        it, lo, hi, done_f = st
        use_snap = jnp.logical_and(it >= SNAP_AFTER, (it & 1) == 0)
        lo, hi, done_f = lax.cond(use_snap, snap, halve, (lo, hi, done_f))
        return it + 1, lo, hi, done_f

    _, thr, _, _ = lax.while_loop(bis_cond, bis_step,
                                  (jnp.int32(0), lo0, mx, done0.astype(F32)))

    c_gt = count(lambda s, c0: s > thr)
    c_eq = count(lambda s, c0: s == thr)
    need = kf - c_gt
    excess = jnp.logical_and(jnp.logical_not(take_all), c_eq > need)
    any_excess = jnp.max(excess.astype(F32)) > 0.5
    big_j = jnp.full((1, tq), 1, jnp.int32) * (row0 + tq)

    def tie_cut(_):
        def step(_, st):
            lo_j, hi_j = st
            mid_j = lo_j + lax.shift_right_arithmetic(hi_j - lo_j, 1)
            c_mid = count(lambda s, c0: jnp.logical_and(s == thr, (krow_c + c0) <= mid_j))
            ok = c_mid >= need
            return jnp.where(ok, lo_j, mid_j), jnp.where(ok, mid_j, hi_j)

        n_steps = 1 + int(math.ceil(math.log2(sc_ref.shape[0] + 2)))
        _, hi_j = lax.fori_loop(0, n_steps, step, (jnp.full((1, tq), -1, jnp.int32), big_j))
        return hi_j

    jcut = lax.cond(any_excess, tie_cut, lambda _: big_j, 0)
    jcut = jnp.where(excess, jcut, big_j)

    acc_ref[...] = jnp.zeros(acc_ref.shape, F32)
    m_ref[...] = jnp.full(m_ref.shape, NEG_BIG, F32)

    def attend(j, near_u):
        c0 = pl.multiple_of(j * tk, tk)
        s_idx = sc_ref[pl.ds(c0, tk), :]
        keep = jnp.logical_or(s_idx > thr,
                              jnp.logical_and(s_idx == thr, (krow + c0) <= jcut))
        if near_u is not None:
            keep = jnp.logical_and(keep, rel <= (row0 - c0))
        tile_max = []
        for h in range(N_ATTN_HEADS):
            g = h // HEADS_PER_GROUP
            s = jnp.dot(k_ref[0, g, pl.ds(c0, tk), :], qt_ref[0, h],
                        preferred_element_type=F32)
            if near_u is not None:
                s = s + bias_ref[h, near_u * tk:(near_u + 1) * tk, :]
            s = jnp.where(keep, s, NEG_BIG)
            s_scr[h] = s
            tile_max.append(jnp.max(fold(s, jnp.maximum), axis=0, keepdims=True))
        alphas = []
        for h in range(N_ATTN_HEADS):
            m_old = m_ref[h]
            m_new = jnp.maximum(m_old, tile_max[h])
            alphas.append(jnp.exp2(m_old - m_new))
            p_scr[h] = jnp.exp2(s_scr[h] - m_new).astype(BF16)
            m_ref[h] = m_new
        for h in range(N_ATTN_HEADS):
            g = h // HEADS_PER_GROUP
            pv = jnp.dot(vt_ref[0, g, :, pl.ds(c0, tk)], p_scr[h], preferred_element_type=F32)
            acc_ref[h] = acc_ref[h] * alphas[h] + pv

    n_far = jnp.maximum(n_kt - n_near, 0)

    def far_step(j, c):
        attend(j, None)
        return c

    lax.fori_loop(0, n_far, far_step, 0)
    for u in range(n_near):
        j_u = n_kt - n_near + u

        @pl.when(j_u >= 0)
        def _():
            attend(j_u, u)

    outs = []
    for h in range(N_ATTN_HEADS):
        a = acc_ref[h]
        outs.append(a[:ATTN_HEAD_DIM, :] / a[ATTN_HEAD_DIM:ATTN_HEAD_DIM + 1, :])
    o_ref[0] = jnp.concatenate(outs, axis=0).T.astype(BF16)


def _dsa(qt, qit, wt, ki, k, vt, bias, *, tq, tk, k_sel, n_near):
    bsz, _, _, s = qt.shape
    body = functools.partial(_dsa_body, tq=tq, tk=tk, k_sel=k_sel, n_near=n_near)
    return pl.pallas_call(
        body,
        grid=(bsz, s // tq),
        in_specs=[pl.BlockSpec((1, N_IDX_HEADS, IDX_HEAD_DIM, tq), lambda b, i: (b, 0, 0, i)),
                  pl.BlockSpec((1, N_IDX_HEADS, tq), lambda b, i: (b, 0, i)),
                  pl.BlockSpec((1, N_ATTN_HEADS, ATTN_HEAD_DIM, tq), lambda b, i: (b, 0, 0, i)),
                  pl.BlockSpec((1, s, IDX_HEAD_DIM), lambda b, i: (b, 0, 0)),
                  pl.BlockSpec((1, N_KV_GROUPS, s, ATTN_HEAD_DIM), lambda b, i: (b, 0, 0, 0)),
                  pl.BlockSpec((1, N_KV_GROUPS, V_AUG_ROWS, s), lambda b, i: (b, 0, 0, 0)),
                  _const_spec(bias.shape)],
        out_specs=pl.BlockSpec((1, tq, ATTN_WIDTH), lambda b, i: (b, i, 0)),
        out_shape=jax.ShapeDtypeStruct((bsz, s, ATTN_WIDTH), BF16),
        scratch_shapes=[pltpu.VMEM((s, tq), F32),
                        pltpu.VMEM((N_ATTN_HEADS, V_AUG_ROWS, tq), F32),
                        pltpu.VMEM((N_ATTN_HEADS, 1, tq), F32),
                        pltpu.VMEM((N_ATTN_HEADS, tk, tq), F32),
                        pltpu.VMEM((N_ATTN_HEADS, tk, tq), BF16)],
        compiler_params=_params("parallel", "arbitrary"),
        name="dsa",
    )(qit, wt, qt, ki, k, vt, bias)


def _split3(a):
    hi = a.astype(BF16)
    r1 = a - hi.astype(F32)
    mid = r1.astype(BF16)
    lo = (r1 - mid.astype(F32)).astype(BF16)
    return hi, mid, lo


def _hgrn2_body(rq_ref, ri_ref, rg_ref, rf_ref, lb_ref, nw_ref, o_ref,
                a_scr, k_scr, v_scr, *, chunk):
    s_len = rq_ref.shape[1]
    c = chunk
    d = REC_HEAD_DIM
    pad = SUBLANES
    lb = lb_ref[0]
    nw = nw_ref[...]
    ti = lax.broadcasted_iota(jnp.int32, (c, c), 0)
    si = lax.broadcasted_iota(jnp.int32, (c, c), 1)
    tril = (si <= ti).astype(BF16)
    eye = (lax.broadcasted_iota(jnp.int32, (d, d), 0)
           == lax.broadcasted_iota(jnp.int32, (d, d), 1)).astype(F32)
    sub = lax.broadcasted_iota(jnp.int32, (c, d), 0) & (SUBLANES - 1)
    levels = []
    b = SUBLANES
    while b < c:
        blk = 2 * b
        sh = blk.bit_length() - 1
        m = jnp.logical_and(lax.shift_right_logical(ti, sh) == lax.shift_right_logical(si, sh),
                            jnp.logical_and((ti & (blk - 1)) >= b, (si & (blk - 1)) < b))
        levels.append((b, m))
        b *= 2

    zpad = jnp.zeros((pad, d), F32)
    a_scr[0:pad, :] = zpad
    k_scr[0:pad, :] = zpad
    v_scr[0:pad, :] = zpad

    def step(n, state):
        t0 = pl.multiple_of(n * c, c)
        f = lb + (1.0 - lb) * _sigmoid(rf_ref[0, pl.ds(t0, c), :])
        g = jnp.log(f)
        kk = 1.0 - f
        qq = _silu(rq_ref[0, pl.ds(t0, c), :].astype(F32))
        vb = ri_ref[0, pl.ds(t0, c), :]
        vv = vb.astype(F32)

        g3 = _split3(g)
        a = (jnp.dot(tril, g3[0], preferred_element_type=F32)
             + jnp.dot(tril, g3[1], preferred_element_type=F32)
             + jnp.dot(tril, g3[2], preferred_element_type=F32))
        a_last = a[c - 1:c, :]

        o = jnp.dot((qq * jnp.exp(a)).astype(BF16), state.astype(BF16),
                    preferred_element_type=F32)

        p = jnp.zeros((c, c), F32)
        for b, m in levels:
            blk = 2 * b
            r = jnp.broadcast_to(a.reshape(c // blk, blk, d)[:, b - 1:b, :],
                                 (c // blk, blk, d)).reshape(c, d)
            qt = (qq * jnp.exp(jnp.minimum(a - r, 0.0))).astype(BF16)
            kt = (kk * jnp.exp(jnp.minimum(r - a, 0.0))).astype(BF16)
            pb = lax.dot_general(qt, kt, (((1,), (1,)), ((), ())), preferred_element_type=F32)
            p = p + jnp.where(m, pb, 0.0)
        o = o + jnp.dot(p.astype(BF16), vb, preferred_element_type=F32)

        a_scr[pad:pad + c, :] = a
        k_scr[pad:pad + c, :] = kk
        v_scr[pad:pad + c, :] = vv
        for dl in range(SUBLANES):
            a_s = a_scr[pad - dl:pad - dl + c, :]
            k_s = k_scr[pad - dl:pad - dl + c, :]
            v_s = v_scr[pad - dl:pad - dl + c, :]
            e = jnp.exp(jnp.where(sub >= dl, a - a_s, NEG_BIG))
            pd = jnp.sum(qq * k_s * e, axis=1, keepdims=True)
            o = o + pd * v_s

        kd = (kk * jnp.exp(a_last - a)).astype(BF16)
        upd = lax.dot_general(kd, vb, (((0,), (0,)), ((), ())), preferred_element_type=F32)
        e_col = jnp.sum(eye * jnp.exp(a_last), axis=1, keepdims=True)
        state = e_col * state + upd

        y = _rms(o, nw) * _silu(rg_ref[0, pl.ds(t0, c), :].astype(F32))
        o_ref[0, pl.ds(t0, c), :] = y.astype(BF16)
        return state

    lax.fori_loop(0, s_len // c, step, jnp.zeros((d, d), F32))


def _hgrn2(rec3, rf, lb, nw, *, chunk):
    bsz, s, _ = rec3.shape
    d = REC_HEAD_DIM
    h = N_REC_HEADS
    body = functools.partial(_hgrn2_body, chunk=chunk)
    col = lambda off: pl.BlockSpec((1, s, d), lambda b, hh: (b, 0, off + hh))
    return pl.pallas_call(
        body,
        grid=(bsz, h),
        in_specs=[col(0), col(h), col(2 * h),
                  pl.BlockSpec((1, s, d), lambda b, hh: (b, 0, hh)),
                  pl.BlockSpec((1, 1, d), lambda b, hh: (hh, 0, 0)),
                  pl.BlockSpec((1, d), lambda b, hh: (0, 0))],
        out_specs=pl.BlockSpec((1, s, d), lambda b, hh: (b, 0, hh)),
        out_shape=jax.ShapeDtypeStruct((bsz, s, REC_WIDTH), BF16),
        scratch_shapes=[pltpu.VMEM((chunk + SUBLANES, d), F32)] * 3,
        compiler_params=_params("parallel", "parallel"),
        name="hgrn2",
    )(rec3, rec3, rec3, rf, lb.reshape(h, 1, d), nw.reshape(1, d))


def _out_ffn_body(x_ref, attn_ref, rec_ref, gates_ref, mod_ref, wba_ref, wbr_ref, wo_ref,
                  n2_ref, wg_ref, wu_ref, wd_ref, fn_ref, o_ref, *, final):
    d = x_ref.shape[2]
    x = x_ref[0]
    g1 = mod_ref[0, 2:3, :]
    sh2 = mod_ref[0, 3:4, :]
    sc2 = mod_ref[0, 4:5, :]
    g2 = mod_ref[0, 5:6, :]
    ya = jnp.dot(attn_ref[0], wba_ref[...], preferred_element_type=F32)
    yr = jnp.dot(rec_ref[0], wbr_ref[...], preferred_element_type=F32)
    gates = gates_ref[0].astype(F32)
    mix = (_sigmoid(gates[:, :d]) * ya + _sigmoid(gates[:, d:]) * yr).astype(BF16)
    x1 = x + g1 * jnp.dot(mix, wo_ref[...], preferred_element_type=F32)
    h2 = (_rms(x1, n2_ref[...]) * (1.0 + sc2) + sh2).astype(BF16)
    gate = jnp.dot(h2, wg_ref[...], preferred_element_type=F32)
    up = jnp.dot(h2, wu_ref[...], preferred_element_type=F32)
    act = (_silu(gate) * up).astype(BF16)
    x2 = x1 + g2 * jnp.dot(act, wd_ref[...], preferred_element_type=F32)
    o_ref[0] = _rms(x2, fn_ref[...]) if final else x2


def _out_ffn(x, attn, rec, gates, mod, wba, wbr, wo, n2w, wg, wu, wd, fnw, tm, final):
    bsz, s, d = x.shape
    hid = wg.shape[1]
    tok = lambda w: pl.BlockSpec((1, tm, w), lambda b, i: (b, i, 0))
    return pl.pallas_call(
        functools.partial(_out_ffn_body, final=final),
        grid=(bsz, s // tm),
        in_specs=[tok(d), tok(ATTN_WIDTH), tok(REC_WIDTH), tok(2 * d),
                  pl.BlockSpec((1, 6, d), lambda b, i: (b, 0, 0)),
                  _const_spec((ATTN_WIDTH, d)), _const_spec((REC_WIDTH, d)), _const_spec((d, d)),
                  _const_spec((1, d)), _const_spec((d, hid)), _const_spec((d, hid)),
                  _const_spec((hid, d)), _const_spec((1, d))],
        out_specs=tok(d),
        out_shape=jax.ShapeDtypeStruct((bsz, s, d), F32),
        compiler_params=_params("parallel", "parallel"),
        name="out_ffn",
    )(x, attn, rec, gates, mod, wba, wbr, wo, n2w, wg, wu, wd, fnw)


def _pick_tile(n, want):
    t = min(want, n)
    while n % t:
        t //= 2
    return t


def _dsa_inputs(kv, kiw):
    bsz, s, _ = kv.shape
    ki = kiw[:, :, :IDX_HEAD_DIM].astype(BF16)
    wt = kiw[:, :, IDX_HEAD_DIM:IDX_HEAD_DIM + N_IDX_HEADS].transpose(0, 2, 1)
    k = kv[:, :, :KV_WIDTH].reshape(bsz, s, N_KV_GROUPS, ATTN_HEAD_DIM).transpose(0, 2, 1, 3)
    vt = kv[:, :, KV_WIDTH:].reshape(bsz, s, N_KV_GROUPS, ATTN_HEAD_DIM).transpose(0, 2, 3, 1)
    ones = jnp.ones((bsz, N_KV_GROUPS, V_AUG_ROWS - ATTN_HEAD_DIM, s), BF16)
    return ki, wt, k, jnp.concatenate([vt, ones], axis=2)


def kernel(x, c, w_ada, b_ada, norm1_w, w_in, q_norm_w, w_uq, w_uq_idx, rel_bias, lb_logits,
           rec_norm_w, w_branch_attn, w_branch_rec, w_out, norm2_w, w_ffn_gate, w_ffn_up,
           w_ffn_down, final_norm_w):
    bsz, s, d = x.shape
    depth = w_in.shape[0]
    k_sel = min(TOPK_MAX, s // 4)
    tm = _pick_tile(s, TOKEN_TILE)
    tq = _pick_tile(s, Q_TILE)
    tk = _pick_tile(tq, K_TILE)
    chunk = _pick_tile(s, REC_CHUNK)
    n_near = -(-(tq + MAX_DISTANCE) // tk)

    lower_bounds = jnp.cumsum(jax.nn.softmax(lb_logits.astype(F32), axis=0), axis=0)
    bias = _near_bias(rel_bias, tq, tk, n_near)

    o_cq = 0
    o_k = o_cq + Q_LORA_RANK
    o_v = o_k + KV_WIDTH
    o_ki = o_v + KV_WIDTH
    o_wi = o_ki + IDX_HEAD_DIM
    o_rq = o_wi + N_IDX_HEADS
    o_rf = o_rq + REC_WIDTH
    o_ri = o_rf + REC_WIDTH
    o_rg = o_ri + REC_WIDTH
    o_gt = o_rg + REC_WIDTH

    for layer in range(depth):
        mod = _ada(c, w_ada[layer], b_ada[layer]).reshape(bsz, 6, d)
        wl = w_in[layer]
        pad_a = LANES - IDX_HEAD_DIM - N_IDX_HEADS
        wa = jnp.concatenate([wl[:, o_cq:o_rq], jnp.zeros((d, pad_a), wl.dtype)], axis=1).astype(BF16)
        wr = jnp.concatenate([wl[:, o_rq:o_rf], wl[:, o_ri:o_rg], wl[:, o_rg:o_gt],
                              wl[:, o_rf:o_ri]], axis=1).astype(BF16)
        wg = wl[:, o_gt:].astype(BF16)
        wuq = (w_uq[layer] * (ATTN_HEAD_DIM ** -0.5 * LOG2_E)).astype(BF16).T
        wui = (w_uq_idx[layer] * (IDX_HEAD_DIM ** -0.5)).astype(BF16).T

        qt, qit, kv, kiw, rec3, rf, gates = _in_proj(
            x, mod, norm1_w[layer].reshape(1, d), wa, wr, wg,
            q_norm_w[layer].reshape(1, Q_LORA_RANK), wuq, wui, tm)

        ki, wt, k, vt = _dsa_inputs(kv, kiw)
        attn = _dsa(qt, qit, wt, ki, k, vt, bias, tq=tq, tk=tk, k_sel=k_sel, n_near=n_near)
        rec = _hgrn2(rec3, rf, lower_bounds[layer], rec_norm_w[layer], chunk=chunk)

        x = _out_ffn(x, attn, rec, gates, mod,
                     w_branch_attn[layer].astype(BF16), w_branch_rec[layer].astype(BF16),
                     w_out[layer].astype(BF16), norm2_w[layer].reshape(1, d),
                     w_ffn_gate[layer].astype(BF16), w_ffn_up[layer].astype(BF16),
                     w_ffn_down[layer].astype(BF16),
                     final_norm_w.reshape(1, d), tm, layer == depth - 1)
    return x
```

```python
import functools
import math

import jax
import jax.numpy as jnp
from jax import lax
from jax.experimental import pallas as pl
from jax.experimental.pallas import tpu as pltpu

F32 = jnp.float32
BF16 = jnp.bfloat16

N_ATTN_HEADS = 8
ATTN_HEAD_DIM = 64
N_KV_GROUPS = 2
HEADS_PER_GROUP = N_ATTN_HEADS // N_KV_GROUPS
Q_LORA_RANK = 256
N_IDX_HEADS = 8
IDX_HEAD_DIM = 64
TOPK_MAX = 256
N_BUCKETS = 32
MAX_DISTANCE = 128
N_REC_HEADS = 4
REC_HEAD_DIM = 128
EPS = 1e-6

ATTN_WIDTH = N_ATTN_HEADS * ATTN_HEAD_DIM
KV_WIDTH = N_KV_GROUPS * ATTN_HEAD_DIM
REC_WIDTH = N_REC_HEADS * REC_HEAD_DIM

LANES = 128
SUBLANES = 8
BF16_SUBLANES = 16
VMEM_LIMIT_BYTES = 56 * 1024 * 1024

TOKEN_TILE = 512
Q_TILE = 256
K_TILE = 256
REC_CHUNK = 128
NEG_BIG = -1e30
BISECT_CAP = 700
SNAP_AFTER = 24
COUNT_TILES = 2
FOLD_CHAINS = 4
LOG2_E = math.log2(math.e)
V_AUG_ROWS = ATTN_HEAD_DIM + BF16_SUBLANES


def _rms(x, w):
    return x * lax.rsqrt(jnp.mean(x * x, axis=-1, keepdims=True) + EPS) * w


def _sigmoid(x):
    return 1.0 / (1.0 + jnp.exp(-x))


def _silu(x):
    return x * _sigmoid(x)


def _params(*sem):
    return pltpu.CompilerParams(dimension_semantics=sem, vmem_limit_bytes=VMEM_LIMIT_BYTES)


def _const_spec(shape):
    nd = len(shape)
    return pl.BlockSpec(shape, lambda *_: (0,) * nd, pipeline_mode=pl.Buffered(1))


def _ada_body(c_ref, w_ref, b_ref, o_ref):
    ca = _silu(c_ref[...])
    o_ref[...] = jnp.dot(ca, w_ref[...], precision=lax.Precision.HIGHEST,
                         preferred_element_type=F32) + b_ref[...]


def _ada(c, w, b):
    bsz, d = c.shape
    n = w.shape[1]
    tn = 1024
    return pl.pallas_call(
        _ada_body,
        grid=(n // tn,),
        in_specs=[pl.BlockSpec((bsz, d), lambda j: (0, 0)),
                  pl.BlockSpec((d, tn), lambda j: (0, j)),
                  pl.BlockSpec((1, tn), lambda j: (0, j))],
        out_specs=pl.BlockSpec((bsz, tn), lambda j: (0, j)),
        out_shape=jax.ShapeDtypeStruct((bsz, n), F32),
        compiler_params=_params("arbitrary"),
        name="ada",
    )(c, w, b.reshape(1, n))


def _in_proj_body(x_ref, mod_ref, n1_ref, wa_ref, wr_ref, wg_ref, qn_ref, wuq_ref, wui_ref,
                  qt_ref, qit_ref, kv_ref, kiw_ref, rec_ref, rf_ref, gates_ref):
    tm = x_ref.shape[1]
    x = x_ref[0]
    sh1 = mod_ref[0, 0:1, :]
    sc1 = mod_ref[0, 1:2, :]
    h = (_rms(x, n1_ref[...]) * (1.0 + sc1) + sh1).astype(BF16)

    pa = jnp.dot(h, wa_ref[...], preferred_element_type=F32)
    kv_ref[0] = pa[:, Q_LORA_RANK:Q_LORA_RANK + 2 * KV_WIDTH].astype(BF16)
    kiw_ref[0] = pa[:, Q_LORA_RANK + 2 * KV_WIDTH:]
    cqn = _rms(pa[:, :Q_LORA_RANK], qn_ref[...]).astype(BF16)
    nt = (((1,), (1,)), ((), ()))
    qf = lax.dot_general(wuq_ref[...], cqn, nt, preferred_element_type=F32)
    qt_ref[0] = qf.reshape(N_ATTN_HEADS, ATTN_HEAD_DIM, tm).astype(BF16)
    qif = lax.dot_general(wui_ref[...], cqn, nt, preferred_element_type=F32)
    qit_ref[0] = qif.reshape(N_IDX_HEADS, IDX_HEAD_DIM, tm).astype(BF16)

    pr = jnp.dot(h, wr_ref[...], preferred_element_type=F32)
    rec_ref[0] = pr[:, :3 * REC_WIDTH].astype(BF16)
    rf_ref[0] = pr[:, 3 * REC_WIDTH:]
    gates_ref[0] = jnp.dot(h, wg_ref[...], preferred_element_type=F32).astype(BF16)


def _in_proj(x, mod, n1w, wa, wr, wg, qnw, wuq, wui, tm):
    bsz, s, d = x.shape
    wa_n = wa.shape[1]
    grid = (bsz, s // tm)
    tok = lambda w: pl.BlockSpec((1, tm, w), lambda b, i: (b, i, 0))
    head = lambda nh, hd: pl.BlockSpec((1, nh, hd, tm), lambda b, i: (b, 0, 0, i))
    out_shape = (
        jax.ShapeDtypeStruct((bsz, N_ATTN_HEADS, ATTN_HEAD_DIM, s), BF16),
        jax.ShapeDtypeStruct((bsz, N_IDX_HEADS, IDX_HEAD_DIM, s), BF16),
        jax.ShapeDtypeStruct((bsz, s, 2 * KV_WIDTH), BF16),
        jax.ShapeDtypeStruct((bsz, s, LANES), F32),
        jax.ShapeDtypeStruct((bsz, s, 3 * REC_WIDTH), BF16),
        jax.ShapeDtypeStruct((bsz, s, REC_WIDTH), F32),
        jax.ShapeDtypeStruct((bsz, s, 2 * d), BF16),
    )
    out_specs = (head(N_ATTN_HEADS, ATTN_HEAD_DIM), head(N_IDX_HEADS, IDX_HEAD_DIM),
                 tok(2 * KV_WIDTH), tok(LANES), tok(3 * REC_WIDTH), tok(REC_WIDTH), tok(2 * d))
    return pl.pallas_call(
        _in_proj_body,
        grid=grid,
        in_specs=[tok(d),
                  pl.BlockSpec((1, 6, d), lambda b, i: (b, 0, 0)),
                  _const_spec((1, d)),
                  _const_spec((d, wa_n)),
                  _const_spec((d, 4 * REC_WIDTH)),
                  _const_spec((d, 2 * d)),
                  _const_spec((1, Q_LORA_RANK)),
                  _const_spec((ATTN_WIDTH, Q_LORA_RANK)),
                  _const_spec((N_IDX_HEADS * IDX_HEAD_DIM, Q_LORA_RANK))],
        out_specs=out_specs,
        out_shape=out_shape,
        compiler_params=_params("parallel", "parallel"),
        name="in_proj",
    )(x, mod, n1w, wa, wr, wg, qnw, wuq, wui)


def _near_bias(rel_bias, tq, tk, n_near):
    u_rows = n_near * tk
    shift = u_rows - tq
    period = u_rows + tq
    e = jnp.arange(period, dtype=jnp.int32)
    dist = jnp.where(e < tq, e, e - period) + shift
    n = jnp.maximum(dist, 0)
    max_exact = N_BUCKETS // 2
    nf = jnp.maximum(n, 1).astype(F32)
    large = max_exact + (jnp.log(nf / max_exact) / math.log(MAX_DISTANCE / max_exact)
                         * (N_BUCKETS - max_exact)).astype(jnp.int32)
    bucket = jnp.where(n < max_exact, n, jnp.minimum(large, N_BUCKETS - 1))
    onehot = (bucket[:, None] == jnp.arange(N_BUCKETS, dtype=jnp.int32)[None, :]).astype(F32)
    rb = rel_bias.astype(F32)
    wv = jnp.dot(onehot, (rb - rb[N_BUCKETS - 1][None, :]) * LOG2_E,
                 precision=lax.Precision.HIGHEST)
    wv = jnp.where((dist >= 0)[:, None], wv, 0.0).T
    m = jnp.tile(wv, (1, u_rows))[:, :u_rows * (period - 1)].reshape(-1, u_rows, period - 1)
    return m[:, :, :tq]


def _dsa_body(qit_ref, w_ref, qt_ref, ki_ref, k_ref, vt_ref, bias_ref, o_ref,
              sc_ref, acc_ref, m_ref, s_scr, p_scr, *, tq, tk, k_sel, n_near):
    i = pl.program_id(1)
    row0 = i * tq
    n_kt = (i + 1) * (tq // tk)
    krow = lax.broadcasted_iota(jnp.int32, (tk, tq), 0)
    qcol = lax.broadcasted_iota(jnp.int32, (tk, tq), 1)
    rel = krow - qcol
    kf = float(k_sel)

    def fold(a, op):
        n = a.shape[0] // SUBLANES
        a = a.reshape(n // FOLD_CHAINS, FOLD_CHAINS, SUBLANES, tq)
        r = a[0]
        for t in range(1, n // FOLD_CHAINS):
            r = op(r, a[t])
        while r.shape[0] > 1:
            half = r.shape[0] // 2
            r = op(r[:half], r[half:])
        return r[0]

    w = w_ref[0] * (N_IDX_HEADS ** -0.5)

    def score_tile(j, carry):
        mx, mn = carry
        c0 = pl.multiple_of(j * tk, tk)
        kin = ki_ref[0, pl.ds(c0, tk), :]
        sc = None
        for h in range(N_IDX_HEADS):
            lg = jnp.dot(kin, qit_ref[0, h], preferred_element_type=F32)
            term = jnp.maximum(lg, 0.0) * w[h:h + 1, :]
            sc = term if sc is None else sc + term
        causal = rel <= (row0 - c0)
        sc_ref[pl.ds(c0, tk), :] = jnp.where(causal, sc, NEG_BIG)
        mx = jnp.maximum(mx, fold(jnp.where(causal, sc, NEG_BIG), jnp.maximum))
        mn = jnp.minimum(mn, fold(jnp.where(causal, sc, -NEG_BIG), jnp.minimum))
        return mx, mn

    mx8, mn8 = lax.fori_loop(0, n_kt, score_tile,
                             (jnp.full((SUBLANES, tq), NEG_BIG, F32),
                              jnp.full((SUBLANES, tq), -NEG_BIG, F32)))
    mx = jnp.max(mx8, axis=0, keepdims=True)
    mn = jnp.min(mn8, axis=0, keepdims=True)

    ck = COUNT_TILES * tk
    n_ct = lax.div(n_kt + (COUNT_TILES - 1), COUNT_TILES)
    for extra in range(COUNT_TILES - 1):
        @pl.when(n_kt + extra < n_ct * COUNT_TILES)
        def _():
            sc_ref[pl.ds(pl.multiple_of((n_kt + extra) * tk, tk), tk), :] = jnp.full(
                (tk, tq), NEG_BIG, F32)

    def count(pred):
        def body(j, acc):
            c0 = pl.multiple_of(j * ck, ck)
            hit = pred(sc_ref[pl.ds(c0, ck), :], c0)
            return acc + fold(jnp.where(hit, 1.0, 0.0), jnp.add)
        acc = lax.fori_loop(0, n_ct, body, jnp.zeros((SUBLANES, tq), F32))
        return jnp.sum(acc, axis=0, keepdims=True)

    n_valid = (row0 + 1 + lax.broadcasted_iota(jnp.int32, (1, tq), 1)).astype(F32)
    take_all = n_valid <= kf
    cnt_top = count(lambda s, c0: s >= mx)
    at_top = jnp.logical_and(jnp.logical_not(take_all), cnt_top >= kf)
    lo0 = jnp.where(take_all, NEG_BIG, jnp.where(at_top, mx, mn))
    done0 = jnp.logical_or(take_all, at_top)

    def bis_cond(st):
        it, lo, hi, c_lo, done_f = st
        return jnp.logical_and(it < BISECT_CAP, jnp.min(done_f) < 0.5)

    def halve(st):
        lo, hi, c_lo, done_f = st
        done = done_f > 0.5
        mid = lo + (hi - lo) * 0.5
        stuck = jnp.logical_or(mid <= lo, mid >= hi)
        cnt = count(lambda s, c0: s >= mid)
        ge = cnt >= kf
        upd = jnp.logical_not(jnp.logical_or(done, stuck))
        raise_lo = jnp.logical_and(upd, ge)
        lo = jnp.where(raise_lo, mid, lo)
        c_lo = jnp.where(raise_lo, cnt, c_lo)
        hi = jnp.where(jnp.logical_and(upd, jnp.logical_not(ge)), mid, hi)
        done = jnp.logical_or(jnp.logical_or(done, stuck), jnp.logical_and(ge, cnt == kf))
        return lo, hi, c_lo, done.astype(F32)

    def snap(st):
        lo, hi, c_lo, done_f = st
        done = done_f > 0.5

        def body(j, acc):
            s = sc_ref[pl.ds(pl.multiple_of(j * ck, ck), ck), :]
            return jnp.maximum(acc, fold(jnp.where(s < hi, s, NEG_BIG), jnp.maximum))
        below = jnp.max(lax.fori_loop(0, n_ct, body, jnp.full((SUBLANES, tq), NEG_BIG, F32)),
                        axis=0, keepdims=True)
        cnt = count(lambda s, c0: s >= below)
        found = jnp.logical_and(jnp.logical_not(done), cnt >= kf)
        lower_hi = jnp.logical_and(jnp.logical_not(done), cnt < kf)
        lo = jnp.where(found, below, lo)
        c_lo = jnp.where(found, cnt, c_lo)
        hi = jnp.where(lower_hi, below, hi)
        return lo, hi, c_lo, jnp.logical_or(done, found).astype(F32)

    def bis_step(st):
        it, lo, hi, c_lo, done_f = st
        use_snap = jnp.logical_and(it >= SNAP_AFTER, (it & 1) == 0)
        lo, hi, c_lo, done_f = lax.cond(use_snap, snap, halve, (lo, hi, c_lo, done_f))
        return it + 1, lo, hi, c_lo, done_f

    c_lo0 = jnp.where(at_top, cnt_top, n_valid)
    _, thr, _, c_thr, _ = lax.while_loop(
        bis_cond, bis_step, (jnp.int32(0), lo0, mx, c_lo0, done0.astype(F32)))

    excess = jnp.logical_and(jnp.logical_not(take_all), c_thr > kf)

    @pl.when(jnp.max(excess.astype(F32)) > 0.5)
    def _():
        need = jnp.where(excess, kf - count(lambda s, c0: s > thr), float(sc_ref.shape[0]))
        tril = (lax.broadcasted_iota(jnp.int32, (tk, tk), 1)
                <= lax.broadcasted_iota(jnp.int32, (tk, tk), 0)).astype(BF16)

        def drop_tile(j, seen):
            c0 = pl.multiple_of(j * tk, tk)
            s = sc_ref[pl.ds(c0, tk), :]
            eq = s == thr
            rank = seen + jnp.dot(tril, jnp.where(eq, 1.0, 0.0).astype(BF16),
                                  preferred_element_type=F32)
            sc_ref[pl.ds(c0, tk), :] = jnp.where(jnp.logical_and(eq, rank > need), NEG_BIG, s)
            return rank[tk - 1:tk, :]

        lax.fori_loop(0, n_kt, drop_tile, jnp.zeros((1, tq), F32))

    acc_ref[...] = jnp.zeros(acc_ref.shape, F32)
    m_ref[...] = jnp.full(m_ref.shape, NEG_BIG, F32)

    def attend(j, near_u):
        c0 = pl.multiple_of(j * tk, tk)
        keep = sc_ref[pl.ds(c0, tk), :] >= thr
        if near_u is not None:
            keep = jnp.logical_and(keep, rel <= (row0 - c0))
        tile_max = []
        for h in range(N_ATTN_HEADS):
            g = h // HEADS_PER_GROUP
            s = jnp.dot(k_ref[0, g, pl.ds(c0, tk), :], qt_ref[0, h],
                        preferred_element_type=F32)
            if near_u is not None:
                s = s + bias_ref[h, near_u * tk:(near_u + 1) * tk, :]
            s = jnp.where(keep, s, NEG_BIG)
            s_scr[h] = s
            tile_max.append(jnp.max(fold(s, jnp.maximum), axis=0, keepdims=True))
        alphas = []
        for h in range(N_ATTN_HEADS):
            m_old = m_ref[h]
            m_new = jnp.maximum(m_old, tile_max[h])
            alphas.append(jnp.exp2(m_old - m_new))
            p_scr[h] = jnp.exp2(s_scr[h] - m_new).astype(BF16)
            m_ref[h] = m_new
        for h in range(N_ATTN_HEADS):
            g = h // HEADS_PER_GROUP
            pv = jnp.dot(vt_ref[0, g, :, pl.ds(c0, tk)], p_scr[h], preferred_element_type=F32)
            acc_ref[h] = acc_ref[h] * alphas[h] + pv

    n_far = jnp.maximum(n_kt - n_near, 0)

    def far_step(j, c):
        attend(j, None)
        return c

    lax.fori_loop(0, n_far, far_step, 0)
    for u in range(n_near):
        j_u = n_kt - n_near + u

        @pl.when(j_u >= 0)
        def _():
            attend(j_u, u)

    outs = []
    for h in range(N_ATTN_HEADS):
        a = acc_ref[h]
        outs.append(a[:ATTN_HEAD_DIM, :] / a[ATTN_HEAD_DIM:ATTN_HEAD_DIM + 1, :])
    o_ref[0] = jnp.concatenate(outs, axis=0).T.astype(BF16)


def _dsa(qt, qit, wt, ki, k, vt, bias, *, tq, tk, k_sel, n_near):
    bsz, _, _, s = qt.shape
    body = functools.partial(_dsa_body, tq=tq, tk=tk, k_sel=k_sel, n_near=n_near)
    return pl.pallas_call(
        body,
        grid=(bsz, s // tq),
        in_specs=[pl.BlockSpec((1, N_IDX_HEADS, IDX_HEAD_DIM, tq), lambda b, i: (b, 0, 0, i)),
                  pl.BlockSpec((1, N_IDX_HEADS, tq), lambda b, i: (b, 0, i)),
                  pl.BlockSpec((1, N_ATTN_HEADS, ATTN_HEAD_DIM, tq), lambda b, i: (b, 0, 0, i)),
                  pl.BlockSpec((1, s, IDX_HEAD_DIM), lambda b, i: (b, 0, 0)),
                  pl.BlockSpec((1, N_KV_GROUPS, s, ATTN_HEAD_DIM), lambda b, i: (b, 0, 0, 0)),
                  pl.BlockSpec((1, N_KV_GROUPS, V_AUG_ROWS, s), lambda b, i: (b, 0, 0, 0)),
                  _const_spec(bias.shape)],
        out_specs=pl.BlockSpec((1, tq, ATTN_WIDTH), lambda b, i: (b, i, 0)),
        out_shape=jax.ShapeDtypeStruct((bsz, s, ATTN_WIDTH), BF16),
        scratch_shapes=[pltpu.VMEM((s, tq), F32),
                        pltpu.VMEM((N_ATTN_HEADS, V_AUG_ROWS, tq), F32),
                        pltpu.VMEM((N_ATTN_HEADS, 1, tq), F32),
                        pltpu.VMEM((N_ATTN_HEADS, tk, tq), F32),
                        pltpu.VMEM((N_ATTN_HEADS, tk, tq), BF16)],
        compiler_params=_params("parallel", "arbitrary"),
        name="dsa",
    )(qit, wt, qt, ki, k, vt, bias)


def _split3(a):
    hi = a.astype(BF16)
    r1 = a - hi.astype(F32)
    mid = r1.astype(BF16)
    lo = (r1 - mid.astype(F32)).astype(BF16)
    return hi, mid, lo


def _hgrn2_body(rq_ref, ri_ref, rg_ref, rf_ref, lb_ref, nw_ref, o_ref,
                a_scr, k_scr, v_scr, *, chunk):
    s_len = rq_ref.shape[1]
    c = chunk
    d = REC_HEAD_DIM
    pad = SUBLANES
    lb = lb_ref[0]
    nw = nw_ref[...]
    ti = lax.broadcasted_iota(jnp.int32, (c, c), 0)
    si = lax.broadcasted_iota(jnp.int32, (c, c), 1)
    tril = (si <= ti).astype(BF16)
    eye = (lax.broadcasted_iota(jnp.int32, (d, d), 0)
           == lax.broadcasted_iota(jnp.int32, (d, d), 1)).astype(F32)
    sub = lax.broadcasted_iota(jnp.int32, (c, d), 0) & (SUBLANES - 1)
    levels = []
    b = SUBLANES
    while b < c:
        blk = 2 * b
        sh = blk.bit_length() - 1
        m = jnp.logical_and(lax.shift_right_logical(ti, sh) == lax.shift_right_logical(si, sh),
                            jnp.logical_and((ti & (blk - 1)) >= b, (si & (blk - 1)) < b))
        levels.append((b, m))
        b *= 2

    zpad = jnp.zeros((pad, d), F32)
    a_scr[0:pad, :] = zpad
    k_scr[0:pad, :] = zpad
    v_scr[0:pad, :] = zpad

    def step(n, state):
        t0 = pl.multiple_of(n * c, c)
        f = lb + (1.0 - lb) * _sigmoid(rf_ref[0, pl.ds(t0, c), :])
        g = jnp.log(f)
        kk = 1.0 - f
        qq = _silu(rq_ref[0, pl.ds(t0, c), :].astype(F32))
        vb = ri_ref[0, pl.ds(t0, c), :]
        vv = vb.astype(F32)

        g3 = _split3(g)
        a = (jnp.dot(tril, g3[0], preferred_element_type=F32)
             + jnp.dot(tril, g3[1], preferred_element_type=F32)
             + jnp.dot(tril, g3[2], preferred_element_type=F32))
        a_last = a[c - 1:c, :]

        o = jnp.dot((qq * jnp.exp(a)).astype(BF16), state.astype(BF16),
                    preferred_element_type=F32)

        p = jnp.zeros((c, c), F32)
        for b, m in levels:
            blk = 2 * b
            r = jnp.broadcast_to(a.reshape(c // blk, blk, d)[:, b - 1:b, :],
                                 (c // blk, blk, d)).reshape(c, d)
            qt = (qq * jnp.exp(jnp.minimum(a - r, 0.0))).astype(BF16)
            kt = (kk * jnp.exp(jnp.minimum(r - a, 0.0))).astype(BF16)
            pb = lax.dot_general(qt, kt, (((1,), (1,)), ((), ())), preferred_element_type=F32)
            p = p + jnp.where(m, pb, 0.0)
        o = o + jnp.dot(p.astype(BF16), vb, preferred_element_type=F32)

        a_scr[pad:pad + c, :] = a
        k_scr[pad:pad + c, :] = kk
        v_scr[pad:pad + c, :] = vv
        for dl in range(SUBLANES):
            a_s = a_scr[pad - dl:pad - dl + c, :]
            k_s = k_scr[pad - dl:pad - dl + c, :]
            v_s = v_scr[pad - dl:pad - dl + c, :]
            e = jnp.exp(jnp.where(sub >= dl, a - a_s, NEG_BIG))
            pd = jnp.sum(qq * k_s * e, axis=1, keepdims=True)
            o = o + pd * v_s

        kd = (kk * jnp.exp(a_last - a)).astype(BF16)
        upd = lax.dot_general(kd, vb, (((0,), (0,)), ((), ())), preferred_element_type=F32)
        e_col = jnp.sum(eye * jnp.exp(a_last), axis=1, keepdims=True)
        state = e_col * state + upd

        y = _rms(o, nw) * _silu(rg_ref[0, pl.ds(t0, c), :].astype(F32))
        o_ref[0, pl.ds(t0, c), :] = y.astype(BF16)
        return state

    lax.fori_loop(0, s_len // c, step, jnp.zeros((d, d), F32))


def _hgrn2(rec3, rf, lb, nw, *, chunk):
    bsz, s, _ = rec3.shape
    d = REC_HEAD_DIM
    h = N_REC_HEADS
    body = functools.partial(_hgrn2_body, chunk=chunk)
    col = lambda off: pl.BlockSpec((1, s, d), lambda b, hh: (b, 0, off + hh))
    return pl.pallas_call(
        body,
        grid=(bsz, h),
        in_specs=[col(0), col(h), col(2 * h),
                  pl.BlockSpec((1, s, d), lambda b, hh: (b, 0, hh)),
                  pl.BlockSpec((1, 1, d), lambda b, hh: (hh, 0, 0)),
                  pl.BlockSpec((1, d), lambda b, hh: (0, 0))],
        out_specs=pl.BlockSpec((1, s, d), lambda b, hh: (b, 0, hh)),
        out_shape=jax.ShapeDtypeStruct((bsz, s, REC_WIDTH), BF16),
        scratch_shapes=[pltpu.VMEM((chunk + SUBLANES, d), F32)] * 3,
        compiler_params=_params("parallel", "parallel"),
        name="hgrn2",
    )(rec3, rec3, rec3, rf, lb.reshape(h, 1, d), nw.reshape(1, d))


def _out_ffn_body(x_ref, attn_ref, rec_ref, gates_ref, mod_ref, wba_ref, wbr_ref, wo_ref,
                  n2_ref, wg_ref, wu_ref, wd_ref, fn_ref, o_ref, *, final):
    d = x_ref.shape[2]
    x = x_ref[0]
    g1 = mod_ref[0, 2:3, :]
    sh2 = mod_ref[0, 3:4, :]
    sc2 = mod_ref[0, 4:5, :]
    g2 = mod_ref[0, 5:6, :]
    ya = jnp.dot(attn_ref[0], wba_ref[...], preferred_element_type=F32)
    yr = jnp.dot(rec_ref[0], wbr_ref[...], preferred_element_type=F32)
    gates = gates_ref[0].astype(F32)
    mix = (_sigmoid(gates[:, :d]) * ya + _sigmoid(gates[:, d:]) * yr).astype(BF16)
    x1 = x + g1 * jnp.dot(mix, wo_ref[...], preferred_element_type=F32)
    h2 = (_rms(x1, n2_ref[...]) * (1.0 + sc2) + sh2).astype(BF16)
    gate = jnp.dot(h2, wg_ref[...], preferred_element_type=F32)
    up = jnp.dot(h2, wu_ref[...], preferred_element_type=F32)
    act = (_silu(gate) * up).astype(BF16)
    x2 = x1 + g2 * jnp.dot(act, wd_ref[...], preferred_element_type=F32)
    o_ref[0] = _rms(x2, fn_ref[...]) if final else x2


def _out_ffn(x, attn, rec, gates, mod, wba, wbr, wo, n2w, wg, wu, wd, fnw, tm, final):
    bsz, s, d = x.shape
    hid = wg.shape[1]
    tok = lambda w: pl.BlockSpec((1, tm, w), lambda b, i: (b, i, 0))
    return pl.pallas_call(
        functools.partial(_out_ffn_body, final=final),
        grid=(bsz, s // tm),
        in_specs=[tok(d), tok(ATTN_WIDTH), tok(REC_WIDTH), tok(2 * d),
                  pl.BlockSpec((1, 6, d), lambda b, i: (b, 0, 0)),
                  _const_spec((ATTN_WIDTH, d)), _const_spec((REC_WIDTH, d)), _const_spec((d, d)),
                  _const_spec((1, d)), _const_spec((d, hid)), _const_spec((d, hid)),
                  _const_spec((hid, d)), _const_spec((1, d))],
        out_specs=tok(d),
        out_shape=jax.ShapeDtypeStruct((bsz, s, d), F32),
        compiler_params=_params("parallel", "parallel"),
        name="out_ffn",
    )(x, attn, rec, gates, mod, wba, wbr, wo, n2w, wg, wu, wd, fnw)


def _pick_tile(n, want):
    t = min(want, n)
    while n % t:
        t //= 2
    return t


def _dsa_inputs(kv, kiw):
    bsz, s, _ = kv.shape
    ki = kiw[:, :, :IDX_HEAD_DIM].astype(BF16)
    wt = kiw[:, :, IDX_HEAD_DIM:IDX_HEAD_DIM + N_IDX_HEADS].transpose(0, 2, 1)
    k = kv[:, :, :KV_WIDTH].reshape(bsz, s, N_KV_GROUPS, ATTN_HEAD_DIM).transpose(0, 2, 1, 3)
    vt = kv[:, :, KV_WIDTH:].reshape(bsz, s, N_KV_GROUPS, ATTN_HEAD_DIM).transpose(0, 2, 3, 1)
    ones = jnp.ones((bsz, N_KV_GROUPS, V_AUG_ROWS - ATTN_HEAD_DIM, s), BF16)
    return ki, wt, k, jnp.concatenate([vt, ones], axis=2)


def kernel(x, c, w_ada, b_ada, norm1_w, w_in, q_norm_w, w_uq, w_uq_idx, rel_bias, lb_logits,
           rec_norm_w, w_branch_attn, w_branch_rec, w_out, norm2_w, w_ffn_gate, w_ffn_up,
           w_ffn_down, final_norm_w):
    bsz, s, d = x.shape
    depth = w_in.shape[0]
    k_sel = min(TOPK_MAX, s // 4)
    tm = _pick_tile(s, TOKEN_TILE)
    tq = _pick_tile(s, Q_TILE)
    tk = _pick_tile(tq, K_TILE)
    chunk = _pick_tile(s, REC_CHUNK)
    n_near = -(-(tq + MAX_DISTANCE) // tk)

    lower_bounds = jnp.cumsum(jax.nn.softmax(lb_logits.astype(F32), axis=0), axis=0)
    bias = _near_bias(rel_bias, tq, tk, n_near)

    o_cq = 0
    o_k = o_cq + Q_LORA_RANK
    o_v = o_k + KV_WIDTH
    o_ki = o_v + KV_WIDTH
    o_wi = o_ki + IDX_HEAD_DIM
    o_rq = o_wi + N_IDX_HEADS
    o_rf = o_rq + REC_WIDTH
    o_ri = o_rf + REC_WIDTH
    o_rg = o_ri + REC_WIDTH
    o_gt = o_rg + REC_WIDTH

    for layer in range(depth):
        mod = _ada(c, w_ada[layer], b_ada[layer]).reshape(bsz, 6, d)
        wl = w_in[layer]
        pad_a = LANES - IDX_HEAD_DIM - N_IDX_HEADS
        wa = jnp.concatenate([wl[:, o_cq:o_rq], jnp.zeros((d, pad_a), wl.dtype)], axis=1).astype(BF16)
        wr = jnp.concatenate([wl[:, o_rq:o_rf], wl[:, o_ri:o_rg], wl[:, o_rg:o_gt],
                              wl[:, o_rf:o_ri]], axis=1).astype(BF16)
        wg = wl[:, o_gt:].astype(BF16)
        wuq = (w_uq[layer] * (ATTN_HEAD_DIM ** -0.5 * LOG2_E)).astype(BF16).T
        wui = (w_uq_idx[layer] * (IDX_HEAD_DIM ** -0.5)).astype(BF16).T

        qt, qit, kv, kiw, rec3, rf, gates = _in_proj(
            x, mod, norm1_w[layer].reshape(1, d), wa, wr, wg,
            q_norm_w[layer].reshape(1, Q_LORA_RANK), wuq, wui, tm)

        ki, wt, k, vt = _dsa_inputs(kv, kiw)
        attn = _dsa(qt, qit, wt, ki, k, vt, bias, tq=tq, tk=tk, k_sel=k_sel, n_near=n_near)
        rec = _hgrn2(rec3, rf, lower_bounds[layer], rec_norm_w[layer], chunk=chunk)

        x = _out_ffn(x, attn, rec, gates, mod,
                     w_branch_attn[layer].astype(BF16), w_branch_rec[layer].astype(BF16),
                     w_out[layer].astype(BF16), norm2_w[layer].reshape(1, d),
                     w_ffn_gate[layer].astype(BF16), w_ffn_up[layer].astype(BF16),
                     w_ffn_down[layer].astype(BF16),
                     final_norm_w.reshape(1, d), tm, layer == depth - 1)
    return x
```

```python
import functools
import math

import jax
import jax.numpy as jnp
from jax import lax
from jax.experimental import pallas as pl
from jax.experimental.pallas import tpu as pltpu

F32 = jnp.float32
BF16 = jnp.bfloat16

N_ATTN_HEADS = 8
ATTN_HEAD_DIM = 64
N_KV_GROUPS = 2
HEADS_PER_GROUP = N_ATTN_HEADS // N_KV_GROUPS
Q_LORA_RANK = 256
N_IDX_HEADS = 8
IDX_HEAD_DIM = 64
TOPK_MAX = 256
N_BUCKETS = 32
MAX_DISTANCE = 128
N_REC_HEADS = 4
REC_HEAD_DIM = 128
EPS = 1e-6

ATTN_WIDTH = N_ATTN_HEADS * ATTN_HEAD_DIM
KV_WIDTH = N_KV_GROUPS * ATTN_HEAD_DIM
REC_WIDTH = N_REC_HEADS * REC_HEAD_DIM

LANES = 128
SUBLANES = 8
BF16_SUBLANES = 16
VMEM_LIMIT_BYTES = 56 * 1024 * 1024

TOKEN_TILE = 512
Q_TILE = 256
K_TILE = 512
REC_CHUNK = 128
NEG_BIG = -1e30
BISECT_CAP = 700
SNAP_AFTER = 24
FOLD_CHAINS = 4
LOG2_E = math.log2(math.e)
V_AUG_ROWS = ATTN_HEAD_DIM + BF16_SUBLANES


def _rms(x, w):
    return x * lax.rsqrt(jnp.mean(x * x, axis=-1, keepdims=True) + EPS) * w


def _sigmoid(x):
    return 1.0 / (1.0 + jnp.exp(-x))


def _silu(x):
    return x * _sigmoid(x)


def _params(*sem):
    return pltpu.CompilerParams(dimension_semantics=sem, vmem_limit_bytes=VMEM_LIMIT_BYTES)


def _const_spec(shape):
    nd = len(shape)
    return pl.BlockSpec(shape, lambda *_: (0,) * nd, pipeline_mode=pl.Buffered(1))


def _ada_body(c_ref, w_ref, b_ref, o_ref):
    ca = _silu(c_ref[...])
    o_ref[...] = jnp.dot(ca, w_ref[...], precision=lax.Precision.HIGHEST,
                         preferred_element_type=F32) + b_ref[...]


def _ada(c, w, b):
    bsz, d = c.shape
    n = w.shape[1]
    tn = 1024
    return pl.pallas_call(
        _ada_body,
        grid=(n // tn,),
        in_specs=[pl.BlockSpec((bsz, d), lambda j: (0, 0)),
                  pl.BlockSpec((d, tn), lambda j: (0, j)),
                  pl.BlockSpec((1, tn), lambda j: (0, j))],
        out_specs=pl.BlockSpec((bsz, tn), lambda j: (0, j)),
        out_shape=jax.ShapeDtypeStruct((bsz, n), F32),
        compiler_params=_params("arbitrary"),
        name="ada",
    )(c, w, b.reshape(1, n))


def _in_proj_body(x_ref, mod_ref, n1_ref, wa_ref, wr_ref, wg_ref, qn_ref, wuq_ref, wui_ref,
                  qt_ref, qit_ref, kv_ref, kiw_ref, rec_ref, rf_ref, gates_ref):
    tm = x_ref.shape[1]
    x = x_ref[0]
    sh1 = mod_ref[0, 0:1, :]
    sc1 = mod_ref[0, 1:2, :]
    h = (_rms(x, n1_ref[...]) * (1.0 + sc1) + sh1).astype(BF16)

    pa = jnp.dot(h, wa_ref[...], preferred_element_type=F32)
    kv_ref[0] = pa[:, Q_LORA_RANK:Q_LORA_RANK + 2 * KV_WIDTH].astype(BF16)
    kiw_ref[0] = pa[:, Q_LORA_RANK + 2 * KV_WIDTH:]
    cqn = _rms(pa[:, :Q_LORA_RANK], qn_ref[...]).astype(BF16)
    nt = (((1,), (1,)), ((), ()))
    qf = lax.dot_general(wuq_ref[...], cqn, nt, preferred_element_type=F32)
    qt_ref[0] = qf.reshape(N_ATTN_HEADS, ATTN_HEAD_DIM, tm).astype(BF16)
    qif = lax.dot_general(wui_ref[...], cqn, nt, preferred_element_type=F32)
    qit_ref[0] = qif.reshape(N_IDX_HEADS, IDX_HEAD_DIM, tm).astype(BF16)

    pr = jnp.dot(h, wr_ref[...], preferred_element_type=F32)
    rec_ref[0] = pr[:, :3 * REC_WIDTH].astype(BF16)
    rf_ref[0] = pr[:, 3 * REC_WIDTH:]
    gates_ref[0] = jnp.dot(h, wg_ref[...], preferred_element_type=F32).astype(BF16)


def _in_proj(x, mod, n1w, wa, wr, wg, qnw, wuq, wui, tm):
    bsz, s, d = x.shape
    wa_n = wa.shape[1]
    grid = (bsz, s // tm)
    tok = lambda w: pl.BlockSpec((1, tm, w), lambda b, i: (b, i, 0))
    head = lambda nh, hd: pl.BlockSpec((1, nh, hd, tm), lambda b, i: (b, 0, 0, i))
    out_shape = (
        jax.ShapeDtypeStruct((bsz, N_ATTN_HEADS, ATTN_HEAD_DIM, s), BF16),
        jax.ShapeDtypeStruct((bsz, N_IDX_HEADS, IDX_HEAD_DIM, s), BF16),
        jax.ShapeDtypeStruct((bsz, s, 2 * KV_WIDTH), BF16),
        jax.ShapeDtypeStruct((bsz, s, LANES), F32),
        jax.ShapeDtypeStruct((bsz, s, 3 * REC_WIDTH), BF16),
        jax.ShapeDtypeStruct((bsz, s, REC_WIDTH), F32),
        jax.ShapeDtypeStruct((bsz, s, 2 * d), BF16),
    )
    out_specs = (head(N_ATTN_HEADS, ATTN_HEAD_DIM), head(N_IDX_HEADS, IDX_HEAD_DIM),
                 tok(2 * KV_WIDTH), tok(LANES), tok(3 * REC_WIDTH), tok(REC_WIDTH), tok(2 * d))
    return pl.pallas_call(
        _in_proj_body,
        grid=grid,
        in_specs=[tok(d),
                  pl.BlockSpec((1, 6, d), lambda b, i: (b, 0, 0)),
                  _const_spec((1, d)),
                  _const_spec((d, wa_n)),
                  _const_spec((d, 4 * REC_WIDTH)),
                  _const_spec((d, 2 * d)),
                  _const_spec((1, Q_LORA_RANK)),
                  _const_spec((ATTN_WIDTH, Q_LORA_RANK)),
                  _const_spec((N_IDX_HEADS * IDX_HEAD_DIM, Q_LORA_RANK))],
        out_specs=out_specs,
        out_shape=out_shape,
        compiler_params=_params("parallel", "parallel"),
        name="in_proj",
    )(x, mod, n1w, wa, wr, wg, qnw, wuq, wui)


def _near_bias(rel_bias, tq, tk):
    u_rows = 2 * tk
    shift = tk
    period = u_rows + tq
    e = jnp.arange(period, dtype=jnp.int32)
    dist = jnp.where(e < tq, e, e - period) + shift
    n = jnp.maximum(dist, 0)
    max_exact = N_BUCKETS // 2
    nf = jnp.maximum(n, 1).astype(F32)
    large = max_exact + (jnp.log(nf / max_exact) / math.log(MAX_DISTANCE / max_exact)
                         * (N_BUCKETS - max_exact)).astype(jnp.int32)
    bucket = jnp.where(n < max_exact, n, jnp.minimum(large, N_BUCKETS - 1))
    onehot = (bucket[:, None] == jnp.arange(N_BUCKETS, dtype=jnp.int32)[None, :]).astype(F32)
    rb = rel_bias.astype(F32)
    wv = jnp.dot(onehot, (rb - rb[N_BUCKETS - 1][None, :]) * LOG2_E,
                 precision=lax.Precision.HIGHEST)
    wv = jnp.where((dist >= 0)[:, None], wv, 0.0).T
    m = jnp.tile(wv, (1, u_rows))[:, :u_rows * (period - 1)].reshape(-1, u_rows, period - 1)
    return m[:, :, :tq]


def _dsa_body(qit_ref, w_ref, qt_ref, ki_ref, k_ref, vt_ref, bias_ref, o_ref,
              sc_ref, acc_ref, m_ref, s_scr, p_scr, mask_scr, *, tq, tk, k_sel):
    i = pl.program_id(1)
    row0 = i * tq
    n_kt = lax.div(row0 + tq + (tk - 1), tk)
    j_near = lax.div(jnp.maximum(row0 - (MAX_DISTANCE - 1), 0), tk)
    krow = lax.broadcasted_iota(jnp.int32, (tk, tq), 0)
    qcol = lax.broadcasted_iota(jnp.int32, (tk, tq), 1)
    rel = krow - qcol
    kf = float(k_sel)

    def fold(a, op):
        n = a.shape[0] // SUBLANES
        a = a.reshape(n // FOLD_CHAINS, FOLD_CHAINS, SUBLANES, tq)
        r = a[0]
        for t in range(1, n // FOLD_CHAINS):
            r = op(r, a[t])
        while r.shape[0] > 1:
            half = r.shape[0] // 2
            r = op(r[:half], r[half:])
        return r[0]

    w = w_ref[0] * (N_IDX_HEADS ** -0.5)

    def score_tile(j, carry):
        mx, mn = carry
        c0 = pl.multiple_of(j * tk, tk)
        kin = ki_ref[0, pl.ds(c0, tk), :]
        sc = None
        for h in range(N_IDX_HEADS):
            lg = jnp.dot(kin, qit_ref[0, h], preferred_element_type=F32)
            term = jnp.maximum(lg, 0.0) * w[h:h + 1, :]
            sc = term if sc is None else sc + term
        causal = rel <= (row0 - c0)
        sc_ref[pl.ds(c0, tk), :] = jnp.where(causal, sc, NEG_BIG)
        mx = jnp.maximum(mx, fold(jnp.where(causal, sc, NEG_BIG), jnp.maximum))
        mn = jnp.minimum(mn, fold(jnp.where(causal, sc, -NEG_BIG), jnp.minimum))
        return mx, mn

    mx8, mn8 = lax.fori_loop(0, n_kt, score_tile,
                             (jnp.full((SUBLANES, tq), NEG_BIG, F32),
                              jnp.full((SUBLANES, tq), -NEG_BIG, F32)))
    mx = jnp.max(mx8, axis=0, keepdims=True)
    mn = jnp.min(mn8, axis=0, keepdims=True)

    ck = tk
    n_ct = n_kt

    def count(pred):
        def body(j, acc):
            c0 = pl.multiple_of(j * ck, ck)
            hit = pred(sc_ref[pl.ds(c0, ck), :], c0)
            return acc + fold(jnp.where(hit, 1.0, 0.0), jnp.add)
        acc = lax.fori_loop(0, n_ct, body, jnp.zeros((SUBLANES, tq), F32))
        return jnp.sum(acc, axis=0, keepdims=True)

    n_valid = (row0 + 1 + lax.broadcasted_iota(jnp.int32, (1, tq), 1)).astype(F32)
    take_all = n_valid <= kf
    cnt_top = count(lambda s, c0: s >= mx)
    at_top = jnp.logical_and(jnp.logical_not(take_all), cnt_top >= kf)
    lo0 = jnp.where(take_all, NEG_BIG, jnp.where(at_top, mx, mn))
    done0 = jnp.logical_or(take_all, at_top)

    def bis_cond(st):
        it, lo, hi, c_lo, done_f = st
        return jnp.logical_and(it < BISECT_CAP, jnp.min(done_f) < 0.5)

    def halve(st):
        lo, hi, c_lo, done_f = st
        done = done_f > 0.5
        mid = lo + (hi - lo) * 0.5
        stuck = jnp.logical_or(mid <= lo, mid >= hi)
        cnt = count(lambda s, c0: s >= mid)
        ge = cnt >= kf
        upd = jnp.logical_not(jnp.logical_or(done, stuck))
        raise_lo = jnp.logical_and(upd, ge)
        lo = jnp.where(raise_lo, mid, lo)
        c_lo = jnp.where(raise_lo, cnt, c_lo)
        hi = jnp.where(jnp.logical_and(upd, jnp.logical_not(ge)), mid, hi)
        done = jnp.logical_or(jnp.logical_or(done, stuck), jnp.logical_and(ge, cnt == kf))
        return lo, hi, c_lo, done.astype(F32)

    def snap(st):
        lo, hi, c_lo, done_f = st
        done = done_f > 0.5

        def body(j, acc):
            s = sc_ref[pl.ds(pl.multiple_of(j * ck, ck), ck), :]
            return jnp.maximum(acc, fold(jnp.where(s < hi, s, NEG_BIG), jnp.maximum))
        below = jnp.max(lax.fori_loop(0, n_ct, body, jnp.full((SUBLANES, tq), NEG_BIG, F32)),
                        axis=0, keepdims=True)
        cnt = count(lambda s, c0: s >= below)
        found = jnp.logical_and(jnp.logical_not(done), cnt >= kf)
        lower_hi = jnp.logical_and(jnp.logical_not(done), cnt < kf)
        lo = jnp.where(found, below, lo)
        c_lo = jnp.where(found, cnt, c_lo)
        hi = jnp.where(lower_hi, below, hi)
        return lo, hi, c_lo, jnp.logical_or(done, found).astype(F32)

    def bis_step(st):
        it, lo, hi, c_lo, done_f = st
        use_snap = jnp.logical_and(it >= SNAP_AFTER, (it & 1) == 0)
        lo, hi, c_lo, done_f = lax.cond(use_snap, snap, halve, (lo, hi, c_lo, done_f))
        return it + 1, lo, hi, c_lo, done_f

    c_lo0 = jnp.where(at_top, cnt_top, n_valid)
    _, thr, _, c_thr, _ = lax.while_loop(
        bis_cond, bis_step, (jnp.int32(0), lo0, mx, c_lo0, done0.astype(F32)))

    excess = jnp.logical_and(jnp.logical_not(take_all), c_thr > kf)

    @pl.when(jnp.max(excess.astype(F32)) > 0.5)
    def _():
        need = jnp.where(excess, kf - count(lambda s, c0: s > thr), float(sc_ref.shape[0]))
        tril = (lax.broadcasted_iota(jnp.int32, (tk, tk), 1)
                <= lax.broadcasted_iota(jnp.int32, (tk, tk), 0)).astype(BF16)

        def drop_tile(j, seen):
            c0 = pl.multiple_of(j * tk, tk)
            s = sc_ref[pl.ds(c0, tk), :]
            eq = s == thr
            rank = seen + jnp.dot(tril, jnp.where(eq, 1.0, 0.0).astype(BF16),
                                  preferred_element_type=F32)
            sc_ref[pl.ds(c0, tk), :] = jnp.where(jnp.logical_and(eq, rank > need), NEG_BIG, s)
            return rank[tk - 1:tk, :]

        lax.fori_loop(0, n_kt, drop_tile, jnp.zeros((1, tq), F32))

    acc_ref[...] = jnp.zeros(acc_ref.shape, F32)
    m_ref[...] = jnp.full(m_ref.shape, NEG_BIG, F32)

    def attend(j, near):
        c0 = pl.multiple_of(j * tk, tk)
        keep = sc_ref[pl.ds(c0, tk), :] >= thr
        if near:
            keep = jnp.logical_and(keep, rel <= (row0 - c0))
            bias_row = pl.multiple_of(c0 - row0 + tk, tq)
        mask_scr[...] = jnp.where(keep, 0.0, NEG_BIG)
        tile_max = []
        for h in range(N_ATTN_HEADS):
            g = h // HEADS_PER_GROUP
            s = jnp.dot(k_ref[0, g, pl.ds(c0, tk), :], qt_ref[0, h],
                        preferred_element_type=F32)
            if near:
                s = s + bias_ref[h, pl.ds(bias_row, tk), :]
            s = s + mask_scr[...]
            s_scr[h] = s
            tile_max.append(jnp.max(fold(s, jnp.maximum), axis=0, keepdims=True))
        alphas = []
        for h in range(N_ATTN_HEADS):
            m_old = m_ref[h]
            m_new = jnp.maximum(m_old, tile_max[h])
            alphas.append(jnp.exp2(m_old - m_new))
            p_scr[h] = jnp.exp2(s_scr[h] - m_new).astype(BF16)
            m_ref[h] = m_new
        for h in range(N_ATTN_HEADS):
            g = h // HEADS_PER_GROUP
            pv = jnp.dot(vt_ref[0, g, :, pl.ds(c0, tk)], p_scr[h], preferred_element_type=F32)
            acc_ref[h] = acc_ref[h] * alphas[h] + pv

    def far_step(j, c):
        attend(j, False)
        return c

    lax.fori_loop(0, j_near, far_step, 0)
    for u in range(2):
        @pl.when(j_near + u < n_kt)
        def _():
            attend(j_near + u, True)

    outs = []
    for h in range(N_ATTN_HEADS):
        a = acc_ref[h]
        outs.append(a[:ATTN_HEAD_DIM, :] / a[ATTN_HEAD_DIM:ATTN_HEAD_DIM + 1, :])
    o_ref[0] = jnp.concatenate(outs, axis=0).T.astype(BF16)


def _dsa(qt, qit, wt, ki, k, vt, bias, *, tq, tk, k_sel):
    bsz, _, _, s = qt.shape
    assert tq >= MAX_DISTANCE and tk % tq == 0 and s % tk == 0
    body = functools.partial(_dsa_body, tq=tq, tk=tk, k_sel=k_sel)
    return pl.pallas_call(
        body,
        grid=(bsz, s // tq),
        in_specs=[pl.BlockSpec((1, N_IDX_HEADS, IDX_HEAD_DIM, tq), lambda b, i: (b, 0, 0, i)),
                  pl.BlockSpec((1, N_IDX_HEADS, tq), lambda b, i: (b, 0, i)),
                  pl.BlockSpec((1, N_ATTN_HEADS, ATTN_HEAD_DIM, tq), lambda b, i: (b, 0, 0, i)),
                  pl.BlockSpec((1, s, IDX_HEAD_DIM), lambda b, i: (b, 0, 0)),
                  pl.BlockSpec((1, N_KV_GROUPS, s, ATTN_HEAD_DIM), lambda b, i: (b, 0, 0, 0)),
                  pl.BlockSpec((1, N_KV_GROUPS, V_AUG_ROWS, s), lambda b, i: (b, 0, 0, 0)),
                  _const_spec(bias.shape)],
        out_specs=pl.BlockSpec((1, tq, ATTN_WIDTH), lambda b, i: (b, i, 0)),
        out_shape=jax.ShapeDtypeStruct((bsz, s, ATTN_WIDTH), BF16),
        scratch_shapes=[pltpu.VMEM((s, tq), F32),
                        pltpu.VMEM((N_ATTN_HEADS, V_AUG_ROWS, tq), F32),
                        pltpu.VMEM((N_ATTN_HEADS, 1, tq), F32),
                        pltpu.VMEM((N_ATTN_HEADS, tk, tq), F32),
                        pltpu.VMEM((N_ATTN_HEADS, tk, tq), BF16),
                        pltpu.VMEM((tk, tq), F32)],
        compiler_params=_params("parallel", "arbitrary"),
        name="dsa",
    )(qit, wt, qt, ki, k, vt, bias)


def _split3(a):
    hi = a.astype(BF16)
    r1 = a - hi.astype(F32)
    mid = r1.astype(BF16)
    lo = (r1 - mid.astype(F32)).astype(BF16)
    return hi, mid, lo


def _hgrn2_body(rq_ref, ri_ref, rg_ref, rf_ref, lb_ref, nw_ref, o_ref,
                a_scr, k_scr, v_scr, *, chunk):
    s_len = rq_ref.shape[1]
    c = chunk
    d = REC_HEAD_DIM
    pad = SUBLANES
    lb = lb_ref[0]
    nw = nw_ref[...]
    ti = lax.broadcasted_iota(jnp.int32, (c, c), 0)
    si = lax.broadcasted_iota(jnp.int32, (c, c), 1)
    tril = (si <= ti).astype(BF16)
    eye = (lax.broadcasted_iota(jnp.int32, (d, d), 0)
           == lax.broadcasted_iota(jnp.int32, (d, d), 1)).astype(F32)
    sub = lax.broadcasted_iota(jnp.int32, (c, d), 0) & (SUBLANES - 1)
    levels = []
    b = SUBLANES
    while b < c:
        blk = 2 * b
        sh = blk.bit_length() - 1
        m = jnp.logical_and(lax.shift_right_logical(ti, sh) == lax.shift_right_logical(si, sh),
                            jnp.logical_and((ti & (blk - 1)) >= b, (si & (blk - 1)) < b))
        levels.append((b, m))
        b *= 2

    zpad = jnp.zeros((pad, d), F32)
    a_scr[0:pad, :] = zpad
    k_scr[0:pad, :] = zpad
    v_scr[0:pad, :] = zpad

    def step(n, state):
        t0 = pl.multiple_of(n * c, c)
        f = lb + (1.0 - lb) * _sigmoid(rf_ref[0, pl.ds(t0, c), :])
        g = jnp.log(f)
        kk = 1.0 - f
        qq = _silu(rq_ref[0, pl.ds(t0, c), :].astype(F32))
        vb = ri_ref[0, pl.ds(t0, c), :]
        vv = vb.astype(F32)

        g3 = _split3(g)
        a = (jnp.dot(tril, g3[0], preferred_element_type=F32)
             + jnp.dot(tril, g3[1], preferred_element_type=F32)
             + jnp.dot(tril, g3[2], preferred_element_type=F32))
        a_last = a[c - 1:c, :]

        o = jnp.dot((qq * jnp.exp(a)).astype(BF16), state.astype(BF16),
                    preferred_element_type=F32)

        p = jnp.zeros((c, c), F32)
        for b, m in levels:
            blk = 2 * b
            r = jnp.broadcast_to(a.reshape(c // blk, blk, d)[:, b - 1:b, :],
                                 (c // blk, blk, d)).reshape(c, d)
            qt = (qq * jnp.exp(jnp.minimum(a - r, 0.0))).astype(BF16)
            kt = (kk * jnp.exp(jnp.minimum(r - a, 0.0))).astype(BF16)
            pb = lax.dot_general(qt, kt, (((1,), (1,)), ((), ())), preferred_element_type=F32)
            p = p + jnp.where(m, pb, 0.0)
        o = o + jnp.dot(p.astype(BF16), vb, preferred_element_type=F32)

        a_scr[pad:pad + c, :] = a
        k_scr[pad:pad + c, :] = kk
        v_scr[pad:pad + c, :] = vv
        for dl in range(SUBLANES):
            a_s = a_scr[pad - dl:pad - dl + c, :]
            k_s = k_scr[pad - dl:pad - dl + c, :]
            v_s = v_scr[pad - dl:pad - dl + c, :]
            e = jnp.exp(jnp.where(sub >= dl, a - a_s, NEG_BIG))
            pd = jnp.sum(qq * k_s * e, axis=1, keepdims=True)
            o = o + pd * v_s

        kd = (kk * jnp.exp(a_last - a)).astype(BF16)
        upd = lax.dot_general(kd, vb, (((0,), (0,)), ((), ())), preferred_element_type=F32)
        e_col = jnp.sum(eye * jnp.exp(a_last), axis=1, keepdims=True)
        state = e_col * state + upd

        y = _rms(o, nw) * _silu(rg_ref[0, pl.ds(t0, c), :].astype(F32))
        o_ref[0, pl.ds(t0, c), :] = y.astype(BF16)
        return state

    lax.fori_loop(0, s_len // c, step, jnp.zeros((d, d), F32))


def _hgrn2(rec3, rf, lb, nw, *, chunk):
    bsz, s, _ = rec3.shape
    d = REC_HEAD_DIM
    h = N_REC_HEADS
    body = functools.partial(_hgrn2_body, chunk=chunk)
    col = lambda off: pl.BlockSpec((1, s, d), lambda b, hh: (b, 0, off + hh))
    return pl.pallas_call(
        body,
        grid=(bsz, h),
        in_specs=[col(0), col(h), col(2 * h),
                  pl.BlockSpec((1, s, d), lambda b, hh: (b, 0, hh)),
                  pl.BlockSpec((1, 1, d), lambda b, hh: (hh, 0, 0)),
                  pl.BlockSpec((1, d), lambda b, hh: (0, 0))],
        out_specs=pl.BlockSpec((1, s, d), lambda b, hh: (b, 0, hh)),
        out_shape=jax.ShapeDtypeStruct((bsz, s, REC_WIDTH), BF16),
        scratch_shapes=[pltpu.VMEM((chunk + SUBLANES, d), F32)] * 3,
        compiler_params=_params("parallel", "parallel"),
        name="hgrn2",
    )(rec3, rec3, rec3, rf, lb.reshape(h, 1, d), nw.reshape(1, d))


def _out_ffn_body(x_ref, attn_ref, rec_ref, gates_ref, mod_ref, wba_ref, wbr_ref, wo_ref,
                  n2_ref, wg_ref, wu_ref, wd_ref, fn_ref, o_ref, *, final):
    d = x_ref.shape[2]
    x = x_ref[0]
    g1 = mod_ref[0, 2:3, :]
    sh2 = mod_ref[0, 3:4, :]
    sc2 = mod_ref[0, 4:5, :]
    g2 = mod_ref[0, 5:6, :]
    ya = jnp.dot(attn_ref[0], wba_ref[...], preferred_element_type=F32)
    yr = jnp.dot(rec_ref[0], wbr_ref[...], preferred_element_type=F32)
    gates = gates_ref[0].astype(F32)
    mix = (_sigmoid(gates[:, :d]) * ya + _sigmoid(gates[:, d:]) * yr).astype(BF16)
    x1 = x + g1 * jnp.dot(mix, wo_ref[...], preferred_element_type=F32)
    h2 = (_rms(x1, n2_ref[...]) * (1.0 + sc2) + sh2).astype(BF16)
    gate = jnp.dot(h2, wg_ref[...], preferred_element_type=F32)
    up = jnp.dot(h2, wu_ref[...], preferred_element_type=F32)
    act = (_silu(gate) * up).astype(BF16)
    x2 = x1 + g2 * jnp.dot(act, wd_ref[...], preferred_element_type=F32)
    o_ref[0] = _rms(x2, fn_ref[...]) if final else x2


def _out_ffn(x, attn, rec, gates, mod, wba, wbr, wo, n2w, wg, wu, wd, fnw, tm, final):
    bsz, s, d = x.shape
    hid = wg.shape[1]
    tok = lambda w: pl.BlockSpec((1, tm, w), lambda b, i: (b, i, 0))
    return pl.pallas_call(
        functools.partial(_out_ffn_body, final=final),
        grid=(bsz, s // tm),
        in_specs=[tok(d), tok(ATTN_WIDTH), tok(REC_WIDTH), tok(2 * d),
                  pl.BlockSpec((1, 6, d), lambda b, i: (b, 0, 0)),
                  _const_spec((ATTN_WIDTH, d)), _const_spec((REC_WIDTH, d)), _const_spec((d, d)),
                  _const_spec((1, d)), _const_spec((d, hid)), _const_spec((d, hid)),
                  _const_spec((hid, d)), _const_spec((1, d))],
        out_specs=tok(d),
        out_shape=jax.ShapeDtypeStruct((bsz, s, d), F32),
        compiler_params=_params("parallel", "parallel"),
        name="out_ffn",
    )(x, attn, rec, gates, mod, wba, wbr, wo, n2w, wg, wu, wd, fnw)


def _pick_tile(n, want):
    t = min(want, n)
    while n % t:
        t //= 2
    return t


def _dsa_inputs(kv, kiw):
    bsz, s, _ = kv.shape
    ki = kiw[:, :, :IDX_HEAD_DIM].astype(BF16)
    wt = kiw[:, :, IDX_HEAD_DIM:IDX_HEAD_DIM + N_IDX_HEADS].transpose(0, 2, 1)
    k = kv[:, :, :KV_WIDTH].reshape(bsz, s, N_KV_GROUPS, ATTN_HEAD_DIM).transpose(0, 2, 1, 3)
    vt = kv[:, :, KV_WIDTH:].reshape(bsz, s, N_KV_GROUPS, ATTN_HEAD_DIM).transpose(0, 2, 3, 1)
    ones = jnp.ones((bsz, N_KV_GROUPS, V_AUG_ROWS - ATTN_HEAD_DIM, s), BF16)
    return ki, wt, k, jnp.concatenate([vt, ones], axis=2)


def kernel(x, c, w_ada, b_ada, norm1_w, w_in, q_norm_w, w_uq, w_uq_idx, rel_bias, lb_logits,
           rec_norm_w, w_branch_attn, w_branch_rec, w_out, norm2_w, w_ffn_gate, w_ffn_up,
           w_ffn_down, final_norm_w):
    bsz, s, d = x.shape
    depth = w_in.shape[0]
    k_sel = min(TOPK_MAX, s // 4)
    tm = _pick_tile(s, TOKEN_TILE)
    tq = _pick_tile(s, Q_TILE)
    tk = max(tq, _pick_tile(s, K_TILE))
    chunk = _pick_tile(s, REC_CHUNK)
    lower_bounds = jnp.cumsum(jax.nn.softmax(lb_logits.astype(F32), axis=0), axis=0)
    bias = _near_bias(rel_bias, tq, tk)

    o_cq = 0
    o_k = o_cq + Q_LORA_RANK
    o_v = o_k + KV_WIDTH
    o_ki = o_v + KV_WIDTH
    o_wi = o_ki + IDX_HEAD_DIM
    o_rq = o_wi + N_IDX_HEADS
    o_rf = o_rq + REC_WIDTH
    o_ri = o_rf + REC_WIDTH
    o_rg = o_ri + REC_WIDTH
    o_gt = o_rg + REC_WIDTH

    for layer in range(depth):
        mod = _ada(c, w_ada[layer], b_ada[layer]).reshape(bsz, 6, d)
        wl = w_in[layer]
        pad_a = LANES - IDX_HEAD_DIM - N_IDX_HEADS
        wa = jnp.concatenate([wl[:, o_cq:o_rq], jnp.zeros((d, pad_a), wl.dtype)], axis=1).astype(BF16)
        wr = jnp.concatenate([wl[:, o_rq:o_rf], wl[:, o_ri:o_rg], wl[:, o_rg:o_gt],
                              wl[:, o_rf:o_ri]], axis=1).astype(BF16)
        wg = wl[:, o_gt:].astype(BF16)
        wuq = (w_uq[layer] * (ATTN_HEAD_DIM ** -0.5 * LOG2_E)).astype(BF16).T
        wui = (w_uq_idx[layer] * (IDX_HEAD_DIM ** -0.5)).astype(BF16).T

        qt, qit, kv, kiw, rec3, rf, gates = _in_proj(
            x, mod, norm1_w[layer].reshape(1, d), wa, wr, wg,
            q_norm_w[layer].reshape(1, Q_LORA_RANK), wuq, wui, tm)

        ki, wt, k, vt = _dsa_inputs(kv, kiw)
        attn = _dsa(qt, qit, wt, ki, k, vt, bias, tq=tq, tk=tk, k_sel=k_sel)
        rec = _hgrn2(rec3, rf, lower_bounds[layer], rec_norm_w[layer], chunk=chunk)

        x = _out_ffn(x, attn, rec, gates, mod,
                     w_branch_attn[layer].astype(BF16), w_branch_rec[layer].astype(BF16),
                     w_out[layer].astype(BF16), norm2_w[layer].reshape(1, d),
                     w_ffn_gate[layer].astype(BF16), w_ffn_up[layer].astype(BF16),
                     w_ffn_down[layer].astype(BF16),
                     final_norm_w.reshape(1, d), tm, layer == depth - 1)
    return x
```

```python
import functools
import math

import jax
import jax.numpy as jnp
from jax import lax
from jax.experimental import pallas as pl
from jax.experimental.pallas import tpu as pltpu

F32 = jnp.float32
BF16 = jnp.bfloat16

N_ATTN_HEADS = 8
ATTN_HEAD_DIM = 64
N_KV_GROUPS = 2
HEADS_PER_GROUP = N_ATTN_HEADS // N_KV_GROUPS
Q_LORA_RANK = 256
N_IDX_HEADS = 8
IDX_HEAD_DIM = 64
TOPK_MAX = 256
N_BUCKETS = 32
MAX_DISTANCE = 128
N_REC_HEADS = 4
REC_HEAD_DIM = 128
EPS = 1e-6

ATTN_WIDTH = N_ATTN_HEADS * ATTN_HEAD_DIM
KV_WIDTH = N_KV_GROUPS * ATTN_HEAD_DIM
REC_WIDTH = N_REC_HEADS * REC_HEAD_DIM

LANES = 128
SUBLANES = 8
BF16_SUBLANES = 16
VMEM_LIMIT_BYTES = 56 * 1024 * 1024

TOKEN_TILE = 512
Q_TILE = 256
K_TILE = 512
REC_CHUNK = 128
NEG_BIG = -1e30
BISECT_CAP = 700
SNAP_AFTER = 14
FOLD_CHAINS = 4
PASS_ROWS = 128
LOG2_E = math.log2(math.e)
V_AUG_ROWS = ATTN_HEAD_DIM + BF16_SUBLANES


def _rms(x, w):
    return x * lax.rsqrt(jnp.mean(x * x, axis=-1, keepdims=True) + EPS) * w


def _sigmoid(x):
    return 1.0 / (1.0 + jnp.exp(-x))


def _silu(x):
    return x * _sigmoid(x)


def _params(*sem):
    return pltpu.CompilerParams(dimension_semantics=sem, vmem_limit_bytes=VMEM_LIMIT_BYTES)


def _const_spec(shape):
    nd = len(shape)
    return pl.BlockSpec(shape, lambda *_: (0,) * nd, pipeline_mode=pl.Buffered(1))


def _ada_body(c_ref, w_ref, b_ref, o_ref):
    ca = _silu(c_ref[...])
    o_ref[...] = jnp.dot(ca, w_ref[...], precision=lax.Precision.HIGHEST,
                         preferred_element_type=F32) + b_ref[...]


def _ada(c, w, b):
    bsz, d = c.shape
    n = w.shape[1]
    tn = 1024
    return pl.pallas_call(
        _ada_body,
        grid=(n // tn,),
        in_specs=[pl.BlockSpec((bsz, d), lambda j: (0, 0)),
                  pl.BlockSpec((d, tn), lambda j: (0, j)),
                  pl.BlockSpec((1, tn), lambda j: (0, j))],
        out_specs=pl.BlockSpec((bsz, tn), lambda j: (0, j)),
        out_shape=jax.ShapeDtypeStruct((bsz, n), F32),
        compiler_params=_params("arbitrary"),
        name="ada",
    )(c, w, b.reshape(1, n))


def _in_proj_body(x_ref, mod_ref, n1_ref, wa_ref, wr_ref, wg_ref, qn_ref, wuq_ref, wui_ref,
                  qt_ref, qit_ref, kv_ref, kiw_ref, rec_ref, rf_ref, gates_ref):
    tm = x_ref.shape[1]
    x = x_ref[0]
    sh1 = mod_ref[0, 0:1, :]
    sc1 = mod_ref[0, 1:2, :]
    h = (_rms(x, n1_ref[...]) * (1.0 + sc1) + sh1).astype(BF16)

    pa = jnp.dot(h, wa_ref[...], preferred_element_type=F32)
    kv_ref[0] = pa[:, Q_LORA_RANK:Q_LORA_RANK + 2 * KV_WIDTH].astype(BF16)
    kiw_ref[0] = pa[:, Q_LORA_RANK + 2 * KV_WIDTH:]
    cqn = _rms(pa[:, :Q_LORA_RANK], qn_ref[...]).astype(BF16)
    nt = (((1,), (1,)), ((), ()))
    qf = lax.dot_general(wuq_ref[...], cqn, nt, preferred_element_type=F32)
    qt_ref[0] = qf.reshape(N_ATTN_HEADS, ATTN_HEAD_DIM, tm).astype(BF16)
    qif = lax.dot_general(wui_ref[...], cqn, nt, preferred_element_type=F32)
    qit_ref[0] = qif.reshape(N_IDX_HEADS, IDX_HEAD_DIM, tm).astype(BF16)

    pr = jnp.dot(h, wr_ref[...], preferred_element_type=F32)
    rec_ref[0] = pr[:, :3 * REC_WIDTH].astype(BF16)
    rf_ref[0] = pr[:, 3 * REC_WIDTH:]
    gates_ref[0] = jnp.dot(h, wg_ref[...], preferred_element_type=F32).astype(BF16)


def _in_proj(x, mod, n1w, wa, wr, wg, qnw, wuq, wui, tm):
    bsz, s, d = x.shape
    wa_n = wa.shape[1]
    grid = (bsz, s // tm)
    tok = lambda w: pl.BlockSpec((1, tm, w), lambda b, i: (b, i, 0))
    head = lambda nh, hd: pl.BlockSpec((1, nh, hd, tm), lambda b, i: (b, 0, 0, i))
    out_shape = (
        jax.ShapeDtypeStruct((bsz, N_ATTN_HEADS, ATTN_HEAD_DIM, s), BF16),
        jax.ShapeDtypeStruct((bsz, N_IDX_HEADS, IDX_HEAD_DIM, s), BF16),
        jax.ShapeDtypeStruct((bsz, s, 2 * KV_WIDTH), BF16),
        jax.ShapeDtypeStruct((bsz, s, LANES), F32),
        jax.ShapeDtypeStruct((bsz, s, 3 * REC_WIDTH), BF16),
        jax.ShapeDtypeStruct((bsz, s, REC_WIDTH), F32),
        jax.ShapeDtypeStruct((bsz, s, 2 * d), BF16),
    )
    out_specs = (head(N_ATTN_HEADS, ATTN_HEAD_DIM), head(N_IDX_HEADS, IDX_HEAD_DIM),
                 tok(2 * KV_WIDTH), tok(LANES), tok(3 * REC_WIDTH), tok(REC_WIDTH), tok(2 * d))
    return pl.pallas_call(
        _in_proj_body,
        grid=grid,
        in_specs=[tok(d),
                  pl.BlockSpec((1, 6, d), lambda b, i: (b, 0, 0)),
                  _const_spec((1, d)),
                  _const_spec((d, wa_n)),
                  _const_spec((d, 4 * REC_WIDTH)),
                  _const_spec((d, 2 * d)),
                  _const_spec((1, Q_LORA_RANK)),
                  _const_spec((ATTN_WIDTH, Q_LORA_RANK)),
                  _const_spec((N_IDX_HEADS * IDX_HEAD_DIM, Q_LORA_RANK))],
        out_specs=out_specs,
        out_shape=out_shape,
        compiler_params=_params("parallel", "parallel"),
        name="in_proj",
    )(x, mod, n1w, wa, wr, wg, qnw, wuq, wui)


def _near_bias(rel_bias, tq, tk):
    n = jnp.arange(MAX_DISTANCE + 1, dtype=jnp.int32)
    max_exact = N_BUCKETS // 2
    nf = jnp.maximum(n, 1).astype(F32)
    large = max_exact + (jnp.log(nf / max_exact) / math.log(MAX_DISTANCE / max_exact)
                         * (N_BUCKETS - max_exact)).astype(jnp.int32)
    bucket = jnp.where(n < max_exact, n, jnp.minimum(large, N_BUCKETS - 1))
    bucket = bucket.at[MAX_DISTANCE].set(N_BUCKETS - 1)
    ids = jnp.arange(N_BUCKETS, dtype=jnp.int32)
    start = jnp.sum((bucket[None, :] < ids[:, None]).astype(jnp.int32), axis=1)
    rb = (rel_bias.astype(F32) - rel_bias[N_BUCKETS - 1].astype(F32)[None, :]) * LOG2_E
    step = rb - jnp.concatenate([jnp.zeros_like(rb[:1]), rb[:-1]], axis=0)
    dist = (jnp.arange(tq, dtype=jnp.int32)[None, :]
            - jnp.arange(2 * tk, dtype=jnp.int32)[:, None] + tk)
    reached = (dist[None, :, :] >= start[:, None, None]).astype(F32)
    return jnp.einsum("bh,but->hut", step, reached, precision=lax.Precision.HIGHEST)


def _dsa_body(qit_ref, w_ref, qt_ref, ki_ref, k_ref, vt_ref, bias_ref, o_ref,
              sc_ref, acc_ref, m_ref, s_scr, p_scr, mask_scr, *, tq, tk, k_sel):
    i = pl.program_id(1)
    row0 = i * tq
    n_kt = lax.div(row0 + tq + (tk - 1), tk)
    j_near = lax.div(jnp.maximum(row0 - (MAX_DISTANCE - 1), 0), tk)
    krow = lax.broadcasted_iota(jnp.int32, (tk, tq), 0)
    qcol = lax.broadcasted_iota(jnp.int32, (tk, tq), 1)
    rel = krow - qcol
    kf = float(k_sel)

    def fold(a, op):
        n = a.shape[0] // SUBLANES
        a = a.reshape(n // FOLD_CHAINS, FOLD_CHAINS, SUBLANES, tq)
        r = a[0]
        for t in range(1, n // FOLD_CHAINS):
            r = op(r, a[t])
        while r.shape[0] > 1:
            half = r.shape[0] // 2
            r = op(r[:half], r[half:])
        return r[0]

    w = w_ref[0] * (N_IDX_HEADS ** -0.5)

    def score_tile(j, carry):
        mx, mn = carry
        c0 = pl.multiple_of(j * tk, tk)
        kin = ki_ref[0, pl.ds(c0, tk), :]
        sc = None
        for h in range(N_IDX_HEADS):
            lg = jnp.dot(kin, qit_ref[0, h], preferred_element_type=F32)
            term = jnp.maximum(lg, 0.0) * w[h:h + 1, :]
            sc = term if sc is None else sc + term
        causal = rel <= (row0 - c0)
        sc_ref[pl.ds(c0, tk), :] = jnp.where(causal, sc, NEG_BIG)
        mx = jnp.maximum(mx, fold(jnp.where(causal, sc, NEG_BIG), jnp.maximum))
        mn = jnp.minimum(mn, fold(jnp.where(causal, sc, -NEG_BIG), jnp.minimum))
        return mx, mn

    mx8, mn8 = lax.fori_loop(0, n_kt, score_tile,
                             (jnp.full((SUBLANES, tq), NEG_BIG, F32),
                              jnp.full((SUBLANES, tq), -NEG_BIG, F32)))
    mx = jnp.max(mx8, axis=0, keepdims=True)
    mn = jnp.min(mn8, axis=0, keepdims=True)

    def key_pass(fn, op, init):
        def body(j, acc):
            c0 = pl.multiple_of(j * tk, tk)
            for u in range(tk // PASS_ROWS):
                s = sc_ref[pl.ds(c0 + u * PASS_ROWS, PASS_ROWS), :]
                acc = op(acc, fold(fn(s), op))
            return acc
        return lax.fori_loop(0, n_kt, body, jnp.full((SUBLANES, tq), init, F32))

    def count(pred):
        part = key_pass(lambda s: jnp.where(pred(s), 1.0, 0.0), jnp.add, 0.0)
        return jnp.sum(part, axis=0, keepdims=True)

    n_valid = (row0 + 1 + lax.broadcasted_iota(jnp.int32, (1, tq), 1)).astype(F32)
    take_all = n_valid <= kf
    cnt_top = count(lambda s: s >= mx)
    at_top = jnp.logical_and(jnp.logical_not(take_all), cnt_top >= kf)
    lo0 = jnp.where(take_all, NEG_BIG, jnp.where(at_top, mx, mn))
    done0 = jnp.logical_or(take_all, at_top)

    def bis_cond(st):
        it, lo, hi, c_lo, done_f = st
        return jnp.logical_and(it < BISECT_CAP, jnp.min(done_f) < 0.5)

    def halve(st):
        lo, hi, c_lo, done_f = st
        done = done_f > 0.5
        mid = lo + (hi - lo) * 0.5
        stuck = jnp.logical_or(mid <= lo, mid >= hi)
        cnt = count(lambda s: s >= mid)
        ge = cnt >= kf
        upd = jnp.logical_not(jnp.logical_or(done, stuck))
        raise_lo = jnp.logical_and(upd, ge)
        lo = jnp.where(raise_lo, mid, lo)
        c_lo = jnp.where(raise_lo, cnt, c_lo)
        hi = jnp.where(jnp.logical_and(upd, jnp.logical_not(ge)), mid, hi)
        done = jnp.logical_or(jnp.logical_or(done, stuck), jnp.logical_and(ge, cnt == kf))
        return lo, hi, c_lo, done.astype(F32)

    def snap(st):
        lo, hi, c_lo, done_f = st
        done = done_f > 0.5

        below = jnp.max(key_pass(lambda s: jnp.where(s < hi, s, NEG_BIG), jnp.maximum, NEG_BIG),
                        axis=0, keepdims=True)
        cnt = count(lambda s: s >= below)
        found = jnp.logical_and(jnp.logical_not(done), cnt >= kf)
        lower_hi = jnp.logical_and(jnp.logical_not(done), cnt < kf)
        lo = jnp.where(found, below, lo)
        c_lo = jnp.where(found, cnt, c_lo)
        hi = jnp.where(lower_hi, below, hi)
        return lo, hi, c_lo, jnp.logical_or(done, found).astype(F32)

    def bis_step(st):
        it, lo, hi, c_lo, done_f = st
        use_snap = jnp.logical_and(it >= SNAP_AFTER, (it & 1) == 0)
        lo, hi, c_lo, done_f = lax.cond(use_snap, snap, halve, (lo, hi, c_lo, done_f))
        return it + 1, lo, hi, c_lo, done_f

    c_lo0 = jnp.where(at_top, cnt_top, n_valid)
    _, thr, _, c_thr, _ = lax.while_loop(
        bis_cond, bis_step, (jnp.int32(0), lo0, mx, c_lo0, done0.astype(F32)))

    excess = jnp.logical_and(jnp.logical_not(take_all), c_thr > kf)

    @pl.when(jnp.max(excess.astype(F32)) > 0.5)
    def _():
        need = jnp.where(excess, kf - count(lambda s: s > thr), float(sc_ref.shape[0]))
        tril = (lax.broadcasted_iota(jnp.int32, (tk, tk), 1)
                <= lax.broadcasted_iota(jnp.int32, (tk, tk), 0)).astype(BF16)

        def drop_tile(j, seen):
            c0 = pl.multiple_of(j * tk, tk)
            s = sc_ref[pl.ds(c0, tk), :]
            eq = s == thr
            rank = seen + jnp.dot(tril, jnp.where(eq, 1.0, 0.0).astype(BF16),
                                  preferred_element_type=F32)
            sc_ref[pl.ds(c0, tk), :] = jnp.where(jnp.logical_and(eq, rank > need), NEG_BIG, s)
            return rank[tk - 1:tk, :]

        lax.fori_loop(0, n_kt, drop_tile, jnp.zeros((1, tq), F32))

    acc_ref[...] = jnp.zeros(acc_ref.shape, F32)
    m_ref[...] = jnp.full(m_ref.shape, NEG_BIG, F32)

    def attend(j, near):
        c0 = pl.multiple_of(j * tk, tk)
        keep = sc_ref[pl.ds(c0, tk), :] >= thr
        if near:
            keep = jnp.logical_and(keep, rel <= (row0 - c0))
            bias_row = pl.multiple_of(c0 - row0 + tk, tq)
        mask_scr[...] = jnp.where(keep, 0.0, NEG_BIG)
        tile_max = []
        for h in range(N_ATTN_HEADS):
            g = h // HEADS_PER_GROUP
            s = jnp.dot(k_ref[0, g, pl.ds(c0, tk), :], qt_ref[0, h],
                        preferred_element_type=F32)
            if near:
                s = s + bias_ref[h, pl.ds(bias_row, tk), :]
            s = s + mask_scr[...]
            s_scr[h] = s
            tile_max.append(jnp.max(fold(s, jnp.maximum), axis=0, keepdims=True))
        alphas = []
        for h in range(N_ATTN_HEADS):
            m_old = m_ref[h]
            m_new = jnp.maximum(m_old, tile_max[h])
            alphas.append(jnp.exp2(m_old - m_new))
            p_scr[h] = jnp.exp2(s_scr[h] - m_new).astype(BF16)
            m_ref[h] = m_new
        for h in range(N_ATTN_HEADS):
            g = h // HEADS_PER_GROUP
            pv = jnp.dot(vt_ref[0, g, :, pl.ds(c0, tk)], p_scr[h], preferred_element_type=F32)
            acc_ref[h] = acc_ref[h] * alphas[h] + pv

    def far_step(j, c):
        attend(j, False)
        return c

    lax.fori_loop(0, j_near, far_step, 0)
    for u in range(2):
        @pl.when(j_near + u < n_kt)
        def _():
            attend(j_near + u, True)

    outs = []
    for h in range(N_ATTN_HEADS):
        a = acc_ref[h]
        outs.append(a[:ATTN_HEAD_DIM, :] / a[ATTN_HEAD_DIM:ATTN_HEAD_DIM + 1, :])
    o_ref[0] = jnp.concatenate(outs, axis=0).T.astype(BF16)


def _dsa(qt, qit, wt, ki, k, vt, bias, *, tq, tk, k_sel):
    bsz, _, _, s = qt.shape
    assert tq >= MAX_DISTANCE and tk % tq == 0 and s % tk == 0
    body = functools.partial(_dsa_body, tq=tq, tk=tk, k_sel=k_sel)
    return pl.pallas_call(
        body,
        grid=(bsz, s // tq),
        in_specs=[pl.BlockSpec((1, N_IDX_HEADS, IDX_HEAD_DIM, tq), lambda b, i: (b, 0, 0, i)),
                  pl.BlockSpec((1, N_IDX_HEADS, tq), lambda b, i: (b, 0, i)),
                  pl.BlockSpec((1, N_ATTN_HEADS, ATTN_HEAD_DIM, tq), lambda b, i: (b, 0, 0, i)),
                  pl.BlockSpec((1, s, IDX_HEAD_DIM), lambda b, i: (b, 0, 0)),
                  pl.BlockSpec((1, N_KV_GROUPS, s, ATTN_HEAD_DIM), lambda b, i: (b, 0, 0, 0)),
                  pl.BlockSpec((1, N_KV_GROUPS, V_AUG_ROWS, s), lambda b, i: (b, 0, 0, 0)),
                  _const_spec(bias.shape)],
        out_specs=pl.BlockSpec((1, tq, ATTN_WIDTH), lambda b, i: (b, i, 0)),
        out_shape=jax.ShapeDtypeStruct((bsz, s, ATTN_WIDTH), BF16),
        scratch_shapes=[pltpu.VMEM((s, tq), F32),
                        pltpu.VMEM((N_ATTN_HEADS, V_AUG_ROWS, tq), F32),
                        pltpu.VMEM((N_ATTN_HEADS, 1, tq), F32),
                        pltpu.VMEM((N_ATTN_HEADS, tk, tq), F32),
                        pltpu.VMEM((N_ATTN_HEADS, tk, tq), BF16),
                        pltpu.VMEM((tk, tq), F32)],
        compiler_params=_params("parallel", "arbitrary"),
        name="dsa",
    )(qit, wt, qt, ki, k, vt, bias)


def _split3(a):
    hi = a.astype(BF16)
    r1 = a - hi.astype(F32)
    mid = r1.astype(BF16)
    lo = (r1 - mid.astype(F32)).astype(BF16)
    return hi, mid, lo


def _hgrn2_body(rq_ref, ri_ref, rg_ref, rf_ref, lb_ref, nw_ref, o_ref,
                a_scr, k_scr, v_scr, *, chunk):
    s_len = rq_ref.shape[1]
    c = chunk
    d = REC_HEAD_DIM
    pad = SUBLANES
    lb = lb_ref[0]
    nw = nw_ref[...]
    ti = lax.broadcasted_iota(jnp.int32, (c, c), 0)
    si = lax.broadcasted_iota(jnp.int32, (c, c), 1)
    tril = (si <= ti).astype(BF16)
    eye = (lax.broadcasted_iota(jnp.int32, (d, d), 0)
           == lax.broadcasted_iota(jnp.int32, (d, d), 1)).astype(F32)
    sub = lax.broadcasted_iota(jnp.int32, (c, d), 0) & (SUBLANES - 1)
    levels = []
    b = SUBLANES
    while b < c:
        blk = 2 * b
        sh = blk.bit_length() - 1
        m = jnp.logical_and(lax.shift_right_logical(ti, sh) == lax.shift_right_logical(si, sh),
                            jnp.logical_and((ti & (blk - 1)) >= b, (si & (blk - 1)) < b))
        levels.append((b, m))
        b *= 2

    zpad = jnp.zeros((pad, d), F32)
    a_scr[0:pad, :] = zpad
    k_scr[0:pad, :] = zpad
    v_scr[0:pad, :] = zpad

    def step(n, state):
        t0 = pl.multiple_of(n * c, c)
        f = lb + (1.0 - lb) * _sigmoid(rf_ref[0, pl.ds(t0, c), :])
        g = jnp.log(f)
        kk = 1.0 - f
        qq = _silu(rq_ref[0, pl.ds(t0, c), :].astype(F32))
        vb = ri_ref[0, pl.ds(t0, c), :]
        vv = vb.astype(F32)

        g3 = _split3(g)
        a = (jnp.dot(tril, g3[0], preferred_element_type=F32)
             + jnp.dot(tril, g3[1], preferred_element_type=F32)
             + jnp.dot(tril, g3[2], preferred_element_type=F32))
        a_last = a[c - 1:c, :]

        o = jnp.dot((qq * jnp.exp(a)).astype(BF16), state.astype(BF16),
                    preferred_element_type=F32)

        p = jnp.zeros((c, c), F32)
        for b, m in levels:
            blk = 2 * b
            r = jnp.broadcast_to(a.reshape(c // blk, blk, d)[:, b - 1:b, :],
                                 (c // blk, blk, d)).reshape(c, d)
            qt = (qq * jnp.exp(jnp.minimum(a - r, 0.0))).astype(BF16)
            kt = (kk * jnp.exp(jnp.minimum(r - a, 0.0))).astype(BF16)
            pb = lax.dot_general(qt, kt, (((1,), (1,)), ((), ())), preferred_element_type=F32)
            p = p + jnp.where(m, pb, 0.0)
        o = o + jnp.dot(p.astype(BF16), vb, preferred_element_type=F32)

        a_scr[pad:pad + c, :] = a
        k_scr[pad:pad + c, :] = kk
        v_scr[pad:pad + c, :] = vv
        for dl in range(SUBLANES):
            a_s = a_scr[pad - dl:pad - dl + c, :]
            k_s = k_scr[pad - dl:pad - dl + c, :]
            v_s = v_scr[pad - dl:pad - dl + c, :]
            e = jnp.exp(jnp.where(sub >= dl, a - a_s, NEG_BIG))
            pd = jnp.sum(qq * k_s * e, axis=1, keepdims=True)
            o = o + pd * v_s

        kd = (kk * jnp.exp(a_last - a)).astype(BF16)
        upd = lax.dot_general(kd, vb, (((0,), (0,)), ((), ())), preferred_element_type=F32)
        e_col = jnp.sum(eye * jnp.exp(a_last), axis=1, keepdims=True)
        state = e_col * state + upd

        y = _rms(o, nw) * _silu(rg_ref[0, pl.ds(t0, c), :].astype(F32))
        o_ref[0, pl.ds(t0, c), :] = y.astype(BF16)
        return state

    lax.fori_loop(0, s_len // c, step, jnp.zeros((d, d), F32))


def _hgrn2(rec3, rf, lb, nw, *, chunk):
    bsz, s, _ = rec3.shape
    d = REC_HEAD_DIM
    h = N_REC_HEADS
    body = functools.partial(_hgrn2_body, chunk=chunk)
    col = lambda off: pl.BlockSpec((1, s, d), lambda b, hh: (b, 0, off + hh))
    return pl.pallas_call(
        body,
        grid=(bsz, h),
        in_specs=[col(0), col(h), col(2 * h),
                  pl.BlockSpec((1, s, d), lambda b, hh: (b, 0, hh)),
                  pl.BlockSpec((1, 1, d), lambda b, hh: (hh, 0, 0)),
                  pl.BlockSpec((1, d), lambda b, hh: (0, 0))],
        out_specs=pl.BlockSpec((1, s, d), lambda b, hh: (b, 0, hh)),
        out_shape=jax.ShapeDtypeStruct((bsz, s, REC_WIDTH), BF16),
        scratch_shapes=[pltpu.VMEM((chunk + SUBLANES, d), F32)] * 3,
        compiler_params=_params("parallel", "parallel"),
        name="hgrn2",
    )(rec3, rec3, rec3, rf, lb.reshape(h, 1, d), nw.reshape(1, d))


def _out_ffn_body(x_ref, attn_ref, rec_ref, gates_ref, mod_ref, wba_ref, wbr_ref, wo_ref,
                  n2_ref, wg_ref, wu_ref, wd_ref, fn_ref, o_ref, *, final):
    d = x_ref.shape[2]
    x = x_ref[0]
    g1 = mod_ref[0, 2:3, :]
    sh2 = mod_ref[0, 3:4, :]
    sc2 = mod_ref[0, 4:5, :]
    g2 = mod_ref[0, 5:6, :]
    ya = jnp.dot(attn_ref[0], wba_ref[...], preferred_element_type=F32)
    yr = jnp.dot(rec_ref[0], wbr_ref[...], preferred_element_type=F32)
    gates = gates_ref[0].astype(F32)
    mix = (_sigmoid(gates[:, :d]) * ya + _sigmoid(gates[:, d:]) * yr).astype(BF16)
    x1 = x + g1 * jnp.dot(mix, wo_ref[...], preferred_element_type=F32)
    h2 = (_rms(x1, n2_ref[...]) * (1.0 + sc2) + sh2).astype(BF16)
    gate = jnp.dot(h2, wg_ref[...], preferred_element_type=F32)
    up = jnp.dot(h2, wu_ref[...], preferred_element_type=F32)
    act = (_silu(gate) * up).astype(BF16)
    x2 = x1 + g2 * jnp.dot(act, wd_ref[...], preferred_element_type=F32)
    o_ref[0] = _rms(x2, fn_ref[...]) if final else x2


def _out_ffn(x, attn, rec, gates, mod, wba, wbr, wo, n2w, wg, wu, wd, fnw, tm, final):
    bsz, s, d = x.shape
    hid = wg.shape[1]
    tok = lambda w: pl.BlockSpec((1, tm, w), lambda b, i: (b, i, 0))
    return pl.pallas_call(
        functools.partial(_out_ffn_body, final=final),
        grid=(bsz, s // tm),
        in_specs=[tok(d), tok(ATTN_WIDTH), tok(REC_WIDTH), tok(2 * d),
                  pl.BlockSpec((1, 6, d), lambda b, i: (b, 0, 0)),
                  _const_spec((ATTN_WIDTH, d)), _const_spec((REC_WIDTH, d)), _const_spec((d, d)),
                  _const_spec((1, d)), _const_spec((d, hid)), _const_spec((d, hid)),
                  _const_spec((hid, d)), _const_spec((1, d))],
        out_specs=tok(d),
        out_shape=jax.ShapeDtypeStruct((bsz, s, d), F32),
        compiler_params=_params("parallel", "parallel"),
        name="out_ffn",
    )(x, attn, rec, gates, mod, wba, wbr, wo, n2w, wg, wu, wd, fnw)


def _pick_tile(n, want):
    t = min(want, n)
    while n % t:
        t //= 2
    return t


def _dsa_inputs(kv, kiw):
    bsz, s, _ = kv.shape
    ki = kiw[:, :, :IDX_HEAD_DIM].astype(BF16)
    wt = kiw[:, :, IDX_HEAD_DIM:IDX_HEAD_DIM + N_IDX_HEADS].transpose(0, 2, 1)
    k = kv[:, :, :KV_WIDTH].reshape(bsz, s, N_KV_GROUPS, ATTN_HEAD_DIM).transpose(0, 2, 1, 3)
    vt = kv[:, :, KV_WIDTH:].reshape(bsz, s, N_KV_GROUPS, ATTN_HEAD_DIM).transpose(0, 2, 3, 1)
    ones = jnp.ones((bsz, N_KV_GROUPS, V_AUG_ROWS - ATTN_HEAD_DIM, s), BF16)
    return ki, wt, k, jnp.concatenate([vt, ones], axis=2)


def kernel(x, c, w_ada, b_ada, norm1_w, w_in, q_norm_w, w_uq, w_uq_idx, rel_bias, lb_logits,
           rec_norm_w, w_branch_attn, w_branch_rec, w_out, norm2_w, w_ffn_gate, w_ffn_up,
           w_ffn_down, final_norm_w):
    bsz, s, d = x.shape
    depth = w_in.shape[0]
    k_sel = min(TOPK_MAX, s // 4)
    tm = _pick_tile(s, TOKEN_TILE)
    tq = _pick_tile(s, Q_TILE)
    tk = max(tq, _pick_tile(s, K_TILE))
    chunk = _pick_tile(s, REC_CHUNK)
    lower_bounds = jnp.cumsum(jax.nn.softmax(lb_logits.astype(F32), axis=0), axis=0)
    bias = _near_bias(rel_bias, tq, tk)

    o_cq = 0
    o_k = o_cq + Q_LORA_RANK
    o_v = o_k + KV_WIDTH
    o_ki = o_v + KV_WIDTH
    o_wi = o_ki + IDX_HEAD_DIM
    o_rq = o_wi + N_IDX_HEADS
    o_rf = o_rq + REC_WIDTH
    o_ri = o_rf + REC_WIDTH
    o_rg = o_ri + REC_WIDTH
    o_gt = o_rg + REC_WIDTH

    for layer in range(depth):
        mod = _ada(c, w_ada[layer], b_ada[layer]).reshape(bsz, 6, d)
        wl = w_in[layer]
        pad_a = LANES - IDX_HEAD_DIM - N_IDX_HEADS
        wa = jnp.concatenate([wl[:, o_cq:o_rq], jnp.zeros((d, pad_a), wl.dtype)], axis=1).astype(BF16)
        wr = jnp.concatenate([wl[:, o_rq:o_rf], wl[:, o_ri:o_rg], wl[:, o_rg:o_gt],
                              wl[:, o_rf:o_ri]], axis=1).astype(BF16)
        wg = wl[:, o_gt:].astype(BF16)
        wuq = (w_uq[layer] * (ATTN_HEAD_DIM ** -0.5 * LOG2_E)).astype(BF16).T
        wui = (w_uq_idx[layer] * (IDX_HEAD_DIM ** -0.5)).astype(BF16).T

        qt, qit, kv, kiw, rec3, rf, gates = _in_proj(
            x, mod, norm1_w[layer].reshape(1, d), wa, wr, wg,
            q_norm_w[layer].reshape(1, Q_LORA_RANK), wuq, wui, tm)

        ki, wt, k, vt = _dsa_inputs(kv, kiw)
        attn = _dsa(qt, qit, wt, ki, k, vt, bias, tq=tq, tk=tk, k_sel=k_sel)
        rec = _hgrn2(rec3, rf, lower_bounds[layer], rec_norm_w[layer], chunk=chunk)

        x = _out_ffn(x, attn, rec, gates, mod,
                     w_branch_attn[layer].astype(BF16), w_branch_rec[layer].astype(BF16),
                     w_out[layer].astype(BF16), norm2_w[layer].reshape(1, d),
                     w_ffn_gate[layer].astype(BF16), w_ffn_up[layer].astype(BF16),
                     w_ffn_down[layer].astype(BF16),
                     final_norm_w.reshape(1, d), tm, layer == depth - 1)
    return x
```

```python
import functools
import math

import jax
import jax.numpy as jnp
from jax import lax
from jax.experimental import pallas as pl
from jax.experimental.pallas import tpu as pltpu

F32 = jnp.float32
BF16 = jnp.bfloat16

N_ATTN_HEADS = 8
ATTN_HEAD_DIM = 64
N_KV_GROUPS = 2
HEADS_PER_GROUP = N_ATTN_HEADS // N_KV_GROUPS
Q_LORA_RANK = 256
N_IDX_HEADS = 8
IDX_HEAD_DIM = 64
TOPK_MAX = 256
N_BUCKETS = 32
MAX_DISTANCE = 128
N_REC_HEADS = 4
REC_HEAD_DIM = 128
EPS = 1e-6

ATTN_WIDTH = N_ATTN_HEADS * ATTN_HEAD_DIM
KV_WIDTH = N_KV_GROUPS * ATTN_HEAD_DIM
REC_WIDTH = N_REC_HEADS * REC_HEAD_DIM

LANES = 128
SUBLANES = 8
BF16_SUBLANES = 16
VMEM_LIMIT_BYTES = 56 * 1024 * 1024

TOKEN_TILE = 512
Q_TILE = 256
K_TILE = 512
REC_CHUNK = 128
REC_TIME_TILE = 1024
REC_BAND = 4
NEG_BIG = -1e30
BISECT_CAP = 700
SNAP_AFTER = 14
FOLD_CHAINS = 4
PASS_ROWS = 128
LOG2_E = math.log2(math.e)
V_AUG_ROWS = ATTN_HEAD_DIM + BF16_SUBLANES


def _rms(x, w):
    return x * lax.rsqrt(jnp.mean(x * x, axis=-1, keepdims=True) + EPS) * w


def _sigmoid(x):
    return 0.5 * jnp.tanh(0.5 * x) + 0.5


def _sigmoid_rel(x):
    return 1.0 / (1.0 + jnp.exp(-x))


def _silu(x):
    return x * _sigmoid(x)


def _params(*sem):
    return pltpu.CompilerParams(dimension_semantics=sem, vmem_limit_bytes=VMEM_LIMIT_BYTES)


def _const_spec(shape):
    nd = len(shape)
    return pl.BlockSpec(shape, lambda *_: (0,) * nd, pipeline_mode=pl.Buffered(1))


def _ada_body(c_ref, w_ref, b_ref, o_ref):
    ca = _silu(c_ref[...])
    o_ref[...] = jnp.dot(ca, w_ref[...], precision=lax.Precision.HIGHEST,
                         preferred_element_type=F32) + b_ref[...]


def _ada(c, w, b):
    bsz, d = c.shape
    n = w.shape[1]
    tn = 1024
    return pl.pallas_call(
        _ada_body,
        grid=(n // tn,),
        in_specs=[pl.BlockSpec((bsz, d), lambda j: (0, 0)),
                  pl.BlockSpec((d, tn), lambda j: (0, j)),
                  pl.BlockSpec((1, tn), lambda j: (0, j))],
        out_specs=pl.BlockSpec((bsz, tn), lambda j: (0, j)),
        out_shape=jax.ShapeDtypeStruct((bsz, n), F32),
        compiler_params=_params("arbitrary"),
        name="ada",
    )(c, w, b.reshape(1, n))


def _in_proj_body(x_ref, mod_ref, n1_ref, wa_ref, wr_ref, wg_ref, qn_ref, wuq_ref, wui_ref,
                  qt_ref, qit_ref, kv_ref, kiw_ref, rec_ref, rf_ref, gates_ref):
    tm = x_ref.shape[1]
    x = x_ref[0]
    sh1 = mod_ref[0, 0:1, :]
    sc1 = mod_ref[0, 1:2, :]
    h = (_rms(x, n1_ref[...]) * (1.0 + sc1) + sh1).astype(BF16)

    pa = jnp.dot(h, wa_ref[...], preferred_element_type=F32)
    kv_ref[0] = pa[:, Q_LORA_RANK:Q_LORA_RANK + 2 * KV_WIDTH].astype(BF16)
    kiw_ref[0] = pa[:, Q_LORA_RANK + 2 * KV_WIDTH:]
    cqn = _rms(pa[:, :Q_LORA_RANK], qn_ref[...]).astype(BF16)
    nt = (((1,), (1,)), ((), ()))
    qf = lax.dot_general(wuq_ref[...], cqn, nt, preferred_element_type=F32)
    qt_ref[0] = qf.reshape(N_ATTN_HEADS, ATTN_HEAD_DIM, tm).astype(BF16)
    qif = lax.dot_general(wui_ref[...], cqn, nt, preferred_element_type=F32)
    qit_ref[0] = qif.reshape(N_IDX_HEADS, IDX_HEAD_DIM, tm).astype(BF16)

    pr = jnp.dot(h, wr_ref[...], preferred_element_type=F32)
    rec_ref[0] = pr[:, :3 * REC_WIDTH].astype(BF16)
    rf_ref[0] = pr[:, 3 * REC_WIDTH:]
    gates_ref[0] = jnp.dot(h, wg_ref[...], preferred_element_type=F32).astype(BF16)


def _in_proj(x, mod, n1w, wa, wr, wg, qnw, wuq, wui, tm):
    bsz, s, d = x.shape
    wa_n = wa.shape[1]
    grid = (bsz, s // tm)
    tok = lambda w: pl.BlockSpec((1, tm, w), lambda b, i: (b, i, 0))
    head = lambda nh, hd: pl.BlockSpec((1, nh, hd, tm), lambda b, i: (b, 0, 0, i))
    out_shape = (
        jax.ShapeDtypeStruct((bsz, N_ATTN_HEADS, ATTN_HEAD_DIM, s), BF16),
        jax.ShapeDtypeStruct((bsz, N_IDX_HEADS, IDX_HEAD_DIM, s), BF16),
        jax.ShapeDtypeStruct((bsz, s, 2 * KV_WIDTH), BF16),
        jax.ShapeDtypeStruct((bsz, s, LANES), F32),
        jax.ShapeDtypeStruct((bsz, s, 3 * REC_WIDTH), BF16),
        jax.ShapeDtypeStruct((bsz, s, REC_WIDTH), F32),
        jax.ShapeDtypeStruct((bsz, s, 2 * d), BF16),
    )
    out_specs = (head(N_ATTN_HEADS, ATTN_HEAD_DIM), head(N_IDX_HEADS, IDX_HEAD_DIM),
                 tok(2 * KV_WIDTH), tok(LANES), tok(3 * REC_WIDTH), tok(REC_WIDTH), tok(2 * d))
    return pl.pallas_call(
        _in_proj_body,
        grid=grid,
        in_specs=[tok(d),
                  pl.BlockSpec((1, 6, d), lambda b, i: (b, 0, 0)),
                  _const_spec((1, d)),
                  _const_spec((d, wa_n)),
                  _const_spec((d, 4 * REC_WIDTH)),
                  _const_spec((d, 2 * d)),
                  _const_spec((1, Q_LORA_RANK)),
                  _const_spec((ATTN_WIDTH, Q_LORA_RANK)),
                  _const_spec((N_IDX_HEADS * IDX_HEAD_DIM, Q_LORA_RANK))],
        out_specs=out_specs,
        out_shape=out_shape,
        compiler_params=_params("parallel", "parallel"),
        name="in_proj",
    )(x, mod, n1w, wa, wr, wg, qnw, wuq, wui)


def _near_bias(rel_bias, tq, tk):
    n = jnp.arange(MAX_DISTANCE + 1, dtype=jnp.int32)
    max_exact = N_BUCKETS // 2
    nf = jnp.maximum(n, 1).astype(F32)
    large = max_exact + (jnp.log(nf / max_exact) / math.log(MAX_DISTANCE / max_exact)
                         * (N_BUCKETS - max_exact)).astype(jnp.int32)
    bucket = jnp.where(n < max_exact, n, jnp.minimum(large, N_BUCKETS - 1))
    bucket = bucket.at[MAX_DISTANCE].set(N_BUCKETS - 1)
    ids = jnp.arange(N_BUCKETS, dtype=jnp.int32)
    start = jnp.sum((bucket[None, :] < ids[:, None]).astype(jnp.int32), axis=1)
    rb = (rel_bias.astype(F32) - rel_bias[N_BUCKETS - 1].astype(F32)[None, :]) * LOG2_E
    step = rb - jnp.concatenate([jnp.zeros_like(rb[:1]), rb[:-1]], axis=0)
    dist = (jnp.arange(tq, dtype=jnp.int32)[None, :]
            - jnp.arange(2 * tk, dtype=jnp.int32)[:, None] + tk)
    reached = (dist[None, :, :] >= start[:, None, None]).astype(F32)
    return jnp.einsum("bh,but->hut", step, reached, precision=lax.Precision.HIGHEST)


def _dsa_body(qit_ref, w_ref, qt_ref, ki_ref, k_ref, vt_ref, bias_ref, o_ref,
              sc_ref, acc_ref, m_ref, s_scr, p_scr, mask_scr, *, tq, tk, k_sel):
    i = pl.program_id(1)
    row0 = i * tq
    n_kt = lax.div(row0 + tq + (tk - 1), tk)
    j_near = lax.div(jnp.maximum(row0 - (MAX_DISTANCE - 1), 0), tk)
    krow = lax.broadcasted_iota(jnp.int32, (tk, tq), 0)
    qcol = lax.broadcasted_iota(jnp.int32, (tk, tq), 1)
    rel = krow - qcol
    kf = float(k_sel)

    def fold(a, op):
        n = a.shape[0] // SUBLANES
        a = a.reshape(n // FOLD_CHAINS, FOLD_CHAINS, SUBLANES, tq)
        r = a[0]
        for t in range(1, n // FOLD_CHAINS):
            r = op(r, a[t])
        while r.shape[0] > 1:
            half = r.shape[0] // 2
            r = op(r[:half], r[half:])
        return r[0]

    w = w_ref[0] * (N_IDX_HEADS ** -0.5)

    def score_tile(j, carry):
        mx, mn = carry
        c0 = pl.multiple_of(j * tk, tk)
        kin = ki_ref[0, pl.ds(c0, tk), :]
        sc = None
        for h in range(N_IDX_HEADS):
            lg = jnp.dot(kin, qit_ref[0, h], preferred_element_type=F32)
            term = jnp.maximum(lg, 0.0) * w[h:h + 1, :]
            sc = term if sc is None else sc + term
        causal = rel <= (row0 - c0)
        sc_ref[pl.ds(c0, tk), :] = jnp.where(causal, sc, NEG_BIG)
        mx = jnp.maximum(mx, fold(jnp.where(causal, sc, NEG_BIG), jnp.maximum))
        mn = jnp.minimum(mn, fold(jnp.where(causal, sc, -NEG_BIG), jnp.minimum))
        return mx, mn

    mx8, mn8 = lax.fori_loop(0, n_kt, score_tile,
                             (jnp.full((SUBLANES, tq), NEG_BIG, F32),
                              jnp.full((SUBLANES, tq), -NEG_BIG, F32)))
    mx = jnp.max(mx8, axis=0, keepdims=True)
    mn = jnp.min(mn8, axis=0, keepdims=True)

    def key_pass(fn, op, init):
        def body(j, acc):
            c0 = pl.multiple_of(j * tk, tk)
            for u in range(tk // PASS_ROWS):
                s = sc_ref[pl.ds(c0 + u * PASS_ROWS, PASS_ROWS), :]
                acc = op(acc, fold(fn(s), op))
            return acc
        return lax.fori_loop(0, n_kt, body, jnp.full((SUBLANES, tq), init, F32))

    def count(pred):
        part = key_pass(lambda s: jnp.where(pred(s), 1.0, 0.0), jnp.add, 0.0)
        return jnp.sum(part, axis=0, keepdims=True)

    n_valid = (row0 + 1 + lax.broadcasted_iota(jnp.int32, (1, tq), 1)).astype(F32)
    take_all = n_valid <= kf
    cnt_top = count(lambda s: s >= mx)
    at_top = jnp.logical_and(jnp.logical_not(take_all), cnt_top >= kf)
    lo0 = jnp.where(take_all, NEG_BIG, jnp.where(at_top, mx, mn))
    done0 = jnp.logical_or(take_all, at_top)

    def bis_cond(st):
        it, lo, hi, c_lo, done_f = st
        return jnp.logical_and(it < BISECT_CAP, jnp.min(done_f) < 0.5)

    def halve(st):
        lo, hi, c_lo, done_f = st
        done = done_f > 0.5
        mid = lo + (hi - lo) * 0.5
        stuck = jnp.logical_or(mid <= lo, mid >= hi)
        cnt = count(lambda s: s >= mid)
        ge = cnt >= kf
        upd = jnp.logical_not(jnp.logical_or(done, stuck))
        raise_lo = jnp.logical_and(upd, ge)
        lo = jnp.where(raise_lo, mid, lo)
        c_lo = jnp.where(raise_lo, cnt, c_lo)
        hi = jnp.where(jnp.logical_and(upd, jnp.logical_not(ge)), mid, hi)
        done = jnp.logical_or(jnp.logical_or(done, stuck), jnp.logical_and(ge, cnt == kf))
        return lo, hi, c_lo, done.astype(F32)

    def snap(st):
        lo, hi, c_lo, done_f = st
        done = done_f > 0.5

        below = jnp.max(key_pass(lambda s: jnp.where(s < hi, s, NEG_BIG), jnp.maximum, NEG_BIG),
                        axis=0, keepdims=True)
        cnt = count(lambda s: s >= below)
        found = jnp.logical_and(jnp.logical_not(done), cnt >= kf)
        lower_hi = jnp.logical_and(jnp.logical_not(done), cnt < kf)
        lo = jnp.where(found, below, lo)
        c_lo = jnp.where(found, cnt, c_lo)
        hi = jnp.where(lower_hi, below, hi)
        return lo, hi, c_lo, jnp.logical_or(done, found).astype(F32)

    def bis_step(st):
        it, lo, hi, c_lo, done_f = st
        use_snap = jnp.logical_and(it >= SNAP_AFTER, (it & 1) == 0)
        lo, hi, c_lo, done_f = lax.cond(use_snap, snap, halve, (lo, hi, c_lo, done_f))
        return it + 1, lo, hi, c_lo, done_f

    c_lo0 = jnp.where(at_top, cnt_top, n_valid)
    _, thr, _, c_thr, _ = lax.while_loop(
        bis_cond, bis_step, (jnp.int32(0), lo0, mx, c_lo0, done0.astype(F32)))

    excess = jnp.logical_and(jnp.logical_not(take_all), c_thr > kf)

    @pl.when(jnp.max(excess.astype(F32)) > 0.5)
    def _():
        need = jnp.where(excess, kf - count(lambda s: s > thr), float(sc_ref.shape[0]))
        tril = (lax.broadcasted_iota(jnp.int32, (tk, tk), 1)
                <= lax.broadcasted_iota(jnp.int32, (tk, tk), 0)).astype(BF16)

        def drop_tile(j, seen):
            c0 = pl.multiple_of(j * tk, tk)
            s = sc_ref[pl.ds(c0, tk), :]
            eq = s == thr
            rank = seen + jnp.dot(tril, jnp.where(eq, 1.0, 0.0).astype(BF16),
                                  preferred_element_type=F32)
            sc_ref[pl.ds(c0, tk), :] = jnp.where(jnp.logical_and(eq, rank > need), NEG_BIG, s)
            return rank[tk - 1:tk, :]

        lax.fori_loop(0, n_kt, drop_tile, jnp.zeros((1, tq), F32))

    acc_ref[...] = jnp.zeros(acc_ref.shape, F32)
    m_ref[...] = jnp.full(m_ref.shape, NEG_BIG, F32)

    def attend(j, near):
        c0 = pl.multiple_of(j * tk, tk)
        keep = sc_ref[pl.ds(c0, tk), :] >= thr
        if near:
            keep = jnp.logical_and(keep, rel <= (row0 - c0))
            bias_row = pl.multiple_of(c0 - row0 + tk, tq)
        mask_scr[...] = jnp.where(keep, 0.0, NEG_BIG)
        tile_max = []
        for h in range(N_ATTN_HEADS):
            g = h // HEADS_PER_GROUP
            s = jnp.dot(k_ref[0, g, pl.ds(c0, tk), :], qt_ref[0, h],
                        preferred_element_type=F32)
            if near:
                s = s + bias_ref[h, pl.ds(bias_row, tk), :]
            s = s + mask_scr[...]
            s_scr[h] = s
            tile_max.append(jnp.max(fold(s, jnp.maximum), axis=0, keepdims=True))
        alphas = []
        for h in range(N_ATTN_HEADS):
            m_old = m_ref[h]
            m_new = jnp.maximum(m_old, tile_max[h])
            alphas.append(jnp.exp2(m_old - m_new))
            p_scr[h] = jnp.exp2(s_scr[h] - m_new).astype(BF16)
            m_ref[h] = m_new
        for h in range(N_ATTN_HEADS):
            g = h // HEADS_PER_GROUP
            pv = jnp.dot(vt_ref[0, g, :, pl.ds(c0, tk)], p_scr[h], preferred_element_type=F32)
            acc_ref[h] = acc_ref[h] * alphas[h] + pv

    def far_step(j, c):
        attend(j, False)
        return c

    lax.fori_loop(0, j_near, far_step, 0)
    for u in range(2):
        @pl.when(j_near + u < n_kt)
        def _():
            attend(j_near + u, True)

    outs = []
    for h in range(N_ATTN_HEADS):
        a = acc_ref[h]
        outs.append(a[:ATTN_HEAD_DIM, :] / a[ATTN_HEAD_DIM:ATTN_HEAD_DIM + 1, :])
    o_ref[0] = jnp.concatenate(outs, axis=0).T.astype(BF16)


def _dsa(qt, qit, wt, ki, k, vt, bias, *, tq, tk, k_sel):
    bsz, _, _, s = qt.shape
    assert tq >= MAX_DISTANCE and tk % tq == 0 and s % tk == 0
    body = functools.partial(_dsa_body, tq=tq, tk=tk, k_sel=k_sel)
    return pl.pallas_call(
        body,
        grid=(bsz, s // tq),
        in_specs=[pl.BlockSpec((1, N_IDX_HEADS, IDX_HEAD_DIM, tq), lambda b, i: (b, 0, 0, i)),
                  pl.BlockSpec((1, N_IDX_HEADS, tq), lambda b, i: (b, 0, i)),
                  pl.BlockSpec((1, N_ATTN_HEADS, ATTN_HEAD_DIM, tq), lambda b, i: (b, 0, 0, i)),
                  pl.BlockSpec((1, s, IDX_HEAD_DIM), lambda b, i: (b, 0, 0)),
                  pl.BlockSpec((1, N_KV_GROUPS, s, ATTN_HEAD_DIM), lambda b, i: (b, 0, 0, 0)),
                  pl.BlockSpec((1, N_KV_GROUPS, V_AUG_ROWS, s), lambda b, i: (b, 0, 0, 0)),
                  _const_spec(bias.shape)],
        out_specs=pl.BlockSpec((1, tq, ATTN_WIDTH), lambda b, i: (b, i, 0)),
        out_shape=jax.ShapeDtypeStruct((bsz, s, ATTN_WIDTH), BF16),
        scratch_shapes=[pltpu.VMEM((s, tq), F32),
                        pltpu.VMEM((N_ATTN_HEADS, V_AUG_ROWS, tq), F32),
                        pltpu.VMEM((N_ATTN_HEADS, 1, tq), F32),
                        pltpu.VMEM((N_ATTN_HEADS, tk, tq), F32),
                        pltpu.VMEM((N_ATTN_HEADS, tk, tq), BF16),
                        pltpu.VMEM((tk, tq), F32)],
        compiler_params=_params("parallel", "arbitrary"),
        name="dsa",
    )(qit, wt, qt, ki, k, vt, bias)


def _split3(a):
    hi = a.astype(BF16)
    r1 = a - hi.astype(F32)
    mid = r1.astype(BF16)
    lo = (r1 - mid.astype(F32)).astype(BF16)
    return hi, mid, lo


def _hgrn2_body(rq_ref, ri_ref, rg_ref, rf_ref, lb_ref, nw_ref, o_ref,
                state_scr, a_scr, k_scr, v_scr, tril_scr, lvl_scr, *, chunk):
    t_len = rq_ref.shape[1]
    c = chunk
    d = REC_HEAD_DIM
    pad = SUBLANES
    nw = nw_ref[...]
    eye = (lax.broadcasted_iota(jnp.int32, (d, d), 0)
           == lax.broadcasted_iota(jnp.int32, (d, d), 1)).astype(F32)
    sub = lax.broadcasted_iota(jnp.int32, (c, d), 0) & (REC_BAND - 1)
    level_sizes = []
    b = REC_BAND
    while b < c:
        level_sizes.append(b)
        b *= 2

    @pl.when(pl.program_id(1) == 0)
    def _():
        state_scr[...] = jnp.zeros(state_scr.shape, F32)
        zpad = jnp.zeros((pad, a_scr.shape[1]), F32)
        a_scr[0:pad, :] = zpad
        k_scr[0:pad, :] = zpad
        v_scr[0:pad, :] = zpad
        ti = lax.broadcasted_iota(jnp.int32, (c, c), 0)
        si = lax.broadcasted_iota(jnp.int32, (c, c), 1)
        tril_scr[...] = (si <= ti).astype(BF16)
        for li, b in enumerate(level_sizes):
            blk = 2 * b
            sh = blk.bit_length() - 1
            m = jnp.logical_and(
                lax.shift_right_logical(ti, sh) == lax.shift_right_logical(si, sh),
                jnp.logical_and((ti & (blk - 1)) >= b, (si & (blk - 1)) < b))
            lvl_scr[li] = m.astype(F32)

    def head_step(hh, t0):
        cols = slice(hh * d, (hh + 1) * d)
        lb = lb_ref[:, cols]
        state = state_scr[hh]
        f = lb + (1.0 - lb) * _sigmoid_rel(rf_ref[0, pl.ds(t0, c), cols])
        g = jnp.log(f)
        kk = 1.0 - f
        qq = _silu(rq_ref[0, pl.ds(t0, c), cols].astype(F32))
        vb = ri_ref[0, pl.ds(t0, c), cols]
        vv = vb.astype(F32)

        g3 = _split3(g)
        tril = tril_scr[...]
        a = (jnp.dot(tril, g3[0], preferred_element_type=F32)
             + jnp.dot(tril, g3[1], preferred_element_type=F32)
             + jnp.dot(tril, g3[2], preferred_element_type=F32))
        a_last = a[c - 1:c, :]

        o = jnp.dot((qq * jnp.exp(a)).astype(BF16), state.astype(BF16),
                    preferred_element_type=F32)

        p = jnp.zeros((c, c), F32)
        for li, b in enumerate(level_sizes):
            blk = 2 * b
            r = jnp.broadcast_to(a.reshape(c // blk, blk, d)[:, b - 1:b, :],
                                 (c // blk, blk, d)).reshape(c, d)
            qt = (qq * jnp.exp(jnp.minimum(a - r, 0.0))).astype(BF16)
            kt = (kk * jnp.exp(jnp.minimum(r - a, 0.0))).astype(BF16)
            pb = lax.dot_general(qt, kt, (((1,), (1,)), ((), ())), preferred_element_type=F32)
            p = p + pb * lvl_scr[li]
        o = o + jnp.dot(p.astype(BF16), vb, preferred_element_type=F32)

        a_scr[pad:pad + c, cols] = a
        k_scr[pad:pad + c, cols] = kk
        v_scr[pad:pad + c, cols] = vv
        for dl in range(REC_BAND):
            a_s = a_scr[pad - dl:pad - dl + c, cols]
            k_s = k_scr[pad - dl:pad - dl + c, cols]
            v_s = v_scr[pad - dl:pad - dl + c, cols]
            e = jnp.exp(jnp.where(sub >= dl, a - a_s, NEG_BIG))
            pd = jnp.sum(qq * k_s * e, axis=1, keepdims=True)
            o = o + pd * v_s

        kd = (kk * jnp.exp(a_last - a)).astype(BF16)
        upd = lax.dot_general(kd, vb, (((0,), (0,)), ((), ())), preferred_element_type=F32)
        e_col = jnp.sum(eye * jnp.exp(a_last), axis=1, keepdims=True)
        state_scr[hh] = e_col * state + upd

        y = _rms(o, nw) * _silu(rg_ref[0, pl.ds(t0, c), cols].astype(F32))
        o_ref[0, pl.ds(t0, c), cols] = y.astype(BF16)

    def step(n, carry):
        t0 = pl.multiple_of(n * c, c)
        for hh in range(N_REC_HEADS):
            head_step(hh, t0)
        return carry

    lax.fori_loop(0, t_len // c, step, 0)


def _hgrn2(rec3, rf, lb, nw, *, chunk):
    bsz, s, _ = rec3.shape
    d = REC_HEAD_DIM
    h = N_REC_HEADS
    body = functools.partial(_hgrn2_body, chunk=chunk)
    tt = _pick_tile(s, REC_TIME_TILE)
    col = lambda part: pl.BlockSpec((1, tt, REC_WIDTH), lambda b, t: (b, t, part))
    return pl.pallas_call(
        body,
        grid=(bsz, s // tt),
        in_specs=[col(0), col(1), col(2), col(0),
                  pl.BlockSpec((1, REC_WIDTH), lambda b, t: (0, 0)),
                  pl.BlockSpec((1, d), lambda b, t: (0, 0))],
        out_specs=col(0),
        out_shape=jax.ShapeDtypeStruct((bsz, s, REC_WIDTH), BF16),
        scratch_shapes=[pltpu.VMEM((h, d, d), F32)]
        + [pltpu.VMEM((chunk + SUBLANES, REC_WIDTH), F32)] * 3
        + [pltpu.VMEM((chunk, chunk), BF16),
           pltpu.VMEM(((chunk // REC_BAND).bit_length() - 1, chunk, chunk), F32)],
        compiler_params=_params("parallel", "arbitrary"),
        name="hgrn2",
    )(rec3, rec3, rec3, rf, lb.reshape(1, REC_WIDTH), nw.reshape(1, d))


def _out_ffn_body(x_ref, attn_ref, rec_ref, gates_ref, mod_ref, wba_ref, wbr_ref, wo_ref,
                  n2_ref, wg_ref, wu_ref, wd_ref, fn_ref, o_ref, *, final):
    d = x_ref.shape[2]
    x = x_ref[0]
    g1 = mod_ref[0, 2:3, :]
    sh2 = mod_ref[0, 3:4, :]
    sc2 = mod_ref[0, 4:5, :]
    g2 = mod_ref[0, 5:6, :]
    ya = jnp.dot(attn_ref[0], wba_ref[...], preferred_element_type=F32)
    yr = jnp.dot(rec_ref[0], wbr_ref[...], preferred_element_type=F32)
    gates = gates_ref[0].astype(F32)
    mix = (_sigmoid(gates[:, :d]) * ya + _sigmoid(gates[:, d:]) * yr).astype(BF16)
    x1 = x + g1 * jnp.dot(mix, wo_ref[...], preferred_element_type=F32)
    h2 = (_rms(x1, n2_ref[...]) * (1.0 + sc2) + sh2).astype(BF16)
    gate = jnp.dot(h2, wg_ref[...], preferred_element_type=F32)
    up = jnp.dot(h2, wu_ref[...], preferred_element_type=F32)
    act = (_silu(gate) * up).astype(BF16)
    x2 = x1 + g2 * jnp.dot(act, wd_ref[...], preferred_element_type=F32)
    o_ref[0] = _rms(x2, fn_ref[...]) if final else x2


def _out_ffn(x, attn, rec, gates, mod, wba, wbr, wo, n2w, wg, wu, wd, fnw, tm, final):
    bsz, s, d = x.shape
    hid = wg.shape[1]
    tok = lambda w: pl.BlockSpec((1, tm, w), lambda b, i: (b, i, 0))
    return pl.pallas_call(
        functools.partial(_out_ffn_body, final=final),
        grid=(bsz, s // tm),
        in_specs=[tok(d), tok(ATTN_WIDTH), tok(REC_WIDTH), tok(2 * d),
                  pl.BlockSpec((1, 6, d), lambda b, i: (b, 0, 0)),
                  _const_spec((ATTN_WIDTH, d)), _const_spec((REC_WIDTH, d)), _const_spec((d, d)),
                  _const_spec((1, d)), _const_spec((d, hid)), _const_spec((d, hid)),
                  _const_spec((hid, d)), _const_spec((1, d))],
        out_specs=tok(d),
        out_shape=jax.ShapeDtypeStruct((bsz, s, d), F32),
        compiler_params=_params("parallel", "parallel"),
        name="out_ffn",
    )(x, attn, rec, gates, mod, wba, wbr, wo, n2w, wg, wu, wd, fnw)


def _pick_tile(n, want):
    t = min(want, n)
    while n % t:
        t //= 2
    return t


def _dsa_inputs(kv, kiw):
    bsz, s, _ = kv.shape
    ki = kiw[:, :, :IDX_HEAD_DIM].astype(BF16)
    wt = kiw[:, :, IDX_HEAD_DIM:IDX_HEAD_DIM + N_IDX_HEADS].transpose(0, 2, 1)
    k = kv[:, :, :KV_WIDTH].reshape(bsz, s, N_KV_GROUPS, ATTN_HEAD_DIM).transpose(0, 2, 1, 3)
    vt = kv[:, :, KV_WIDTH:].reshape(bsz, s, N_KV_GROUPS, ATTN_HEAD_DIM).transpose(0, 2, 3, 1)
    ones = jnp.ones((bsz, N_KV_GROUPS, V_AUG_ROWS - ATTN_HEAD_DIM, s), BF16)
    return ki, wt, k, jnp.concatenate([vt, ones], axis=2)


def kernel(x, c, w_ada, b_ada, norm1_w, w_in, q_norm_w, w_uq, w_uq_idx, rel_bias, lb_logits,
           rec_norm_w, w_branch_attn, w_branch_rec, w_out, norm2_w, w_ffn_gate, w_ffn_up,
           w_ffn_down, final_norm_w):
    bsz, s, d = x.shape
    depth = w_in.shape[0]
    k_sel = min(TOPK_MAX, s // 4)
    tm = _pick_tile(s, TOKEN_TILE)
    tq = _pick_tile(s, Q_TILE)
    tk = max(tq, _pick_tile(s, K_TILE))
    chunk = _pick_tile(s, REC_CHUNK)
    lower_bounds = jnp.cumsum(jax.nn.softmax(lb_logits.astype(F32), axis=0), axis=0)
    bias = _near_bias(rel_bias, tq, tk)

    o_cq = 0
    o_k = o_cq + Q_LORA_RANK
    o_v = o_k + KV_WIDTH
    o_ki = o_v + KV_WIDTH
    o_wi = o_ki + IDX_HEAD_DIM
    o_rq = o_wi + N_IDX_HEADS
    o_rf = o_rq + REC_WIDTH
    o_ri = o_rf + REC_WIDTH
    o_rg = o_ri + REC_WIDTH
    o_gt = o_rg + REC_WIDTH

    for layer in range(depth):
        mod = _ada(c, w_ada[layer], b_ada[layer]).reshape(bsz, 6, d)
        wl = w_in[layer]
        pad_a = LANES - IDX_HEAD_DIM - N_IDX_HEADS
        wa = jnp.concatenate([wl[:, o_cq:o_rq], jnp.zeros((d, pad_a), wl.dtype)], axis=1).astype(BF16)
        wr = jnp.concatenate([wl[:, o_rq:o_rf], wl[:, o_ri:o_rg], wl[:, o_rg:o_gt],
                              wl[:, o_rf:o_ri]], axis=1).astype(BF16)
        wg = wl[:, o_gt:].astype(BF16)
        wuq = (w_uq[layer] * (ATTN_HEAD_DIM ** -0.5 * LOG2_E)).astype(BF16).T
        wui = (w_uq_idx[layer] * (IDX_HEAD_DIM ** -0.5)).astype(BF16).T

        qt, qit, kv, kiw, rec3, rf, gates = _in_proj(
            x, mod, norm1_w[layer].reshape(1, d), wa, wr, wg,
            q_norm_w[layer].reshape(1, Q_LORA_RANK), wuq, wui, tm)

        ki, wt, k, vt = _dsa_inputs(kv, kiw)
        attn = _dsa(qt, qit, wt, ki, k, vt, bias, tq=tq, tk=tk, k_sel=k_sel)
        rec = _hgrn2(rec3, rf, lower_bounds[layer], rec_norm_w[layer], chunk=chunk)

        x = _out_ffn(x, attn, rec, gates, mod,
                     w_branch_attn[layer].astype(BF16), w_branch_rec[layer].astype(BF16),
                     w_out[layer].astype(BF16), norm2_w[layer].reshape(1, d),
                     w_ffn_gate[layer].astype(BF16), w_ffn_up[layer].astype(BF16),
                     w_ffn_down[layer].astype(BF16),
                     final_norm_w.reshape(1, d), tm, layer == depth - 1)
    return x
```

```python
import functools
import math

import jax
import jax.numpy as jnp
from jax import lax
from jax.experimental import pallas as pl
from jax.experimental.pallas import tpu as pltpu

F32 = jnp.float32
BF16 = jnp.bfloat16

N_ATTN_HEADS = 8
ATTN_HEAD_DIM = 64
N_KV_GROUPS = 2
HEADS_PER_GROUP = N_ATTN_HEADS // N_KV_GROUPS
Q_LORA_RANK = 256
N_IDX_HEADS = 8
IDX_HEAD_DIM = 64
TOPK_MAX = 256
N_BUCKETS = 32
MAX_DISTANCE = 128
N_REC_HEADS = 4
REC_HEAD_DIM = 128
EPS = 1e-6

ATTN_WIDTH = N_ATTN_HEADS * ATTN_HEAD_DIM
KV_WIDTH = N_KV_GROUPS * ATTN_HEAD_DIM
REC_WIDTH = N_REC_HEADS * REC_HEAD_DIM

LANES = 128
SUBLANES = 8
BF16_SUBLANES = 16
VMEM_LIMIT_BYTES = 56 * 1024 * 1024

TOKEN_TILE = 512
Q_TILE = 256
K_TILE = 512
REC_CHUNK = 128
REC_TIME_TILE = 1024
REC_BAND = 4
NEG_BIG = -1e30
BISECT_CAP = 700
SNAP_AFTER = 14
FOLD_CHAINS = 4
PASS_ROWS = 128
LOG2_E = math.log2(math.e)
V_AUG_ROWS = ATTN_HEAD_DIM + BF16_SUBLANES


def _rms(x, w):
    return x * lax.rsqrt(jnp.mean(x * x, axis=-1, keepdims=True) + EPS) * w


def _sigmoid(x):
    return 0.5 * jnp.tanh(0.5 * x) + 0.5


def _sigmoid_rel(x):
    return 1.0 / (1.0 + jnp.exp(-x))


def _silu(x):
    return x * _sigmoid(x)


def _params(*sem):
    return pltpu.CompilerParams(dimension_semantics=sem, vmem_limit_bytes=VMEM_LIMIT_BYTES)


def _const_spec(shape):
    nd = len(shape)
    return pl.BlockSpec(shape, lambda *_: (0,) * nd, pipeline_mode=pl.Buffered(1))


def _ada_body(c_ref, w_ref, b_ref, o_ref):
    ca = _silu(c_ref[...])
    o_ref[...] = jnp.dot(ca, w_ref[...], precision=lax.Precision.HIGHEST,
                         preferred_element_type=F32) + b_ref[...]


def _ada(c, w, b):
    bsz, d = c.shape
    n = w.shape[1]
    tn = 1024
    return pl.pallas_call(
        _ada_body,
        grid=(n // tn,),
        in_specs=[pl.BlockSpec((bsz, d), lambda j: (0, 0)),
                  pl.BlockSpec((d, tn), lambda j: (0, j)),
                  pl.BlockSpec((1, tn), lambda j: (0, j))],
        out_specs=pl.BlockSpec((bsz, tn), lambda j: (0, j)),
        out_shape=jax.ShapeDtypeStruct((bsz, n), F32),
        compiler_params=_params("arbitrary"),
        name="ada",
    )(c, w, b.reshape(1, n))


def _in_proj_body(x_ref, mod_ref, n1_ref, wa_ref, wr_ref, wg_ref, qn_ref, wuq_ref, wui_ref,
                  qt_ref, qit_ref, kv_ref, kiw_ref, rec_ref, rf_ref, gates_ref):
    tm = x_ref.shape[1]
    x = x_ref[0]
    sh1 = mod_ref[0, 0:1, :]
    sc1 = mod_ref[0, 1:2, :]
    h = (_rms(x, n1_ref[...]) * (1.0 + sc1) + sh1).astype(BF16)

    pa = jnp.dot(h, wa_ref[...], preferred_element_type=F32)
    kv_ref[0] = pa[:, Q_LORA_RANK:Q_LORA_RANK + 2 * KV_WIDTH].astype(BF16)
    kiw_ref[0] = pa[:, Q_LORA_RANK + 2 * KV_WIDTH:]
    cqn = _rms(pa[:, :Q_LORA_RANK], qn_ref[...]).astype(BF16)
    nt = (((1,), (1,)), ((), ()))
    qf = lax.dot_general(wuq_ref[...], cqn, nt, preferred_element_type=F32)
    qt_ref[0] = qf.reshape(N_ATTN_HEADS, ATTN_HEAD_DIM, tm).astype(BF16)
    qif = lax.dot_general(wui_ref[...], cqn, nt, preferred_element_type=F32)
    qit_ref[0] = qif.reshape(N_IDX_HEADS, IDX_HEAD_DIM, tm).astype(BF16)

    pr = jnp.dot(h, wr_ref[...], preferred_element_type=F32)
    rec_ref[0] = pr[:, :3 * REC_WIDTH].astype(BF16)
    rf_ref[0] = pr[:, 3 * REC_WIDTH:]
    gates_ref[0] = jnp.dot(h, wg_ref[...], preferred_element_type=F32).astype(BF16)


def _in_proj(x, mod, n1w, wa, wr, wg, qnw, wuq, wui, tm):
    bsz, s, d = x.shape
    wa_n = wa.shape[1]
    grid = (bsz, s // tm)
    tok = lambda w: pl.BlockSpec((1, tm, w), lambda b, i: (b, i, 0))
    head = lambda nh, hd: pl.BlockSpec((1, nh, hd, tm), lambda b, i: (b, 0, 0, i))
    out_shape = (
        jax.ShapeDtypeStruct((bsz, N_ATTN_HEADS, ATTN_HEAD_DIM, s), BF16),
        jax.ShapeDtypeStruct((bsz, N_IDX_HEADS, IDX_HEAD_DIM, s), BF16),
        jax.ShapeDtypeStruct((bsz, s, 2 * KV_WIDTH), BF16),
        jax.ShapeDtypeStruct((bsz, s, LANES), F32),
        jax.ShapeDtypeStruct((bsz, s, 3 * REC_WIDTH), BF16),
        jax.ShapeDtypeStruct((bsz, s, REC_WIDTH), F32),
        jax.ShapeDtypeStruct((bsz, s, 2 * d), BF16),
    )
    out_specs = (head(N_ATTN_HEADS, ATTN_HEAD_DIM), head(N_IDX_HEADS, IDX_HEAD_DIM),
                 tok(2 * KV_WIDTH), tok(LANES), tok(3 * REC_WIDTH), tok(REC_WIDTH), tok(2 * d))
    return pl.pallas_call(
        _in_proj_body,
        grid=grid,
        in_specs=[tok(d),
                  pl.BlockSpec((1, 6, d), lambda b, i: (b, 0, 0)),
                  _const_spec((1, d)),
                  _const_spec((d, wa_n)),
                  _const_spec((d, 4 * REC_WIDTH)),
                  _const_spec((d, 2 * d)),
                  _const_spec((1, Q_LORA_RANK)),
                  _const_spec((ATTN_WIDTH, Q_LORA_RANK)),
                  _const_spec((N_IDX_HEADS * IDX_HEAD_DIM, Q_LORA_RANK))],
        out_specs=out_specs,
        out_shape=out_shape,
        compiler_params=_params("parallel", "parallel"),
        name="in_proj",
    )(x, mod, n1w, wa, wr, wg, qnw, wuq, wui)


def _near_bias(rel_bias, tq, tk):
    n = jnp.arange(MAX_DISTANCE + 1, dtype=jnp.int32)
    max_exact = N_BUCKETS // 2
    nf = jnp.maximum(n, 1).astype(F32)
    large = max_exact + (jnp.log(nf / max_exact) / math.log(MAX_DISTANCE / max_exact)
                         * (N_BUCKETS - max_exact)).astype(jnp.int32)
    bucket = jnp.where(n < max_exact, n, jnp.minimum(large, N_BUCKETS - 1))
    bucket = bucket.at[MAX_DISTANCE].set(N_BUCKETS - 1)
    ids = jnp.arange(N_BUCKETS, dtype=jnp.int32)
    start = jnp.sum((bucket[None, :] < ids[:, None]).astype(jnp.int32), axis=1)
    rb = (rel_bias.astype(F32) - rel_bias[N_BUCKETS - 1].astype(F32)[None, :]) * LOG2_E
    step = rb - jnp.concatenate([jnp.zeros_like(rb[:1]), rb[:-1]], axis=0)
    dist = (jnp.arange(tq, dtype=jnp.int32)[None, :]
            - jnp.arange(2 * tk, dtype=jnp.int32)[:, None] + tk)
    reached = (dist[None, :, :] >= start[:, None, None]).astype(F32)
    return jnp.einsum("bh,but->hut", step, reached, precision=lax.Precision.HIGHEST)


def _dsa_body(qit_ref, w_ref, qt_ref, ki_ref, k_ref, vt_ref, bias_ref, o_ref,
              sc_ref, acc_ref, m_ref, s_scr, p_scr, mask_scr, *, tq, tk, k_sel):
    i = pl.program_id(1)
    row0 = i * tq
    n_kt = lax.div(row0 + tq + (tk - 1), tk)
    j_near = lax.div(jnp.maximum(row0 - (MAX_DISTANCE - 1), 0), tk)
    krow = lax.broadcasted_iota(jnp.int32, (tk, tq), 0)
    qcol = lax.broadcasted_iota(jnp.int32, (tk, tq), 1)
    rel = krow - qcol
    kf = float(k_sel)

    def fold(a, op):
        n = a.shape[0] // SUBLANES
        a = a.reshape(n // FOLD_CHAINS, FOLD_CHAINS, SUBLANES, tq)
        r = a[0]
        for t in range(1, n // FOLD_CHAINS):
            r = op(r, a[t])
        while r.shape[0] > 1:
            half = r.shape[0] // 2
            r = op(r[:half], r[half:])
        return r[0]

    w = w_ref[0] * (N_IDX_HEADS ** -0.5)

    def score_tile(j, carry):
        mx, mn = carry
        c0 = pl.multiple_of(j * tk, tk)
        kin = ki_ref[0, pl.ds(c0, tk), :]
        sc = None
        for h in range(N_IDX_HEADS):
            lg = jnp.dot(kin, qit_ref[0, h], preferred_element_type=F32)
            term = jnp.maximum(lg, 0.0) * w[h:h + 1, :]
            sc = term if sc is None else sc + term
        causal = rel <= (row0 - c0)
        sc_ref[pl.ds(c0, tk), :] = jnp.where(causal, sc, NEG_BIG)
        mx = jnp.maximum(mx, fold(jnp.where(causal, sc, NEG_BIG), jnp.maximum))
        mn = jnp.minimum(mn, fold(jnp.where(causal, sc, -NEG_BIG), jnp.minimum))
        return mx, mn

    def score_pair(t, carry):
        carry = score_tile(2 * t, carry)
        return score_tile(jnp.minimum(2 * t + 1, n_kt - 1), carry)

    mx8, mn8 = lax.fori_loop(0, lax.div(n_kt + 1, 2), score_pair,
                             (jnp.full((SUBLANES, tq), NEG_BIG, F32),
                              jnp.full((SUBLANES, tq), -NEG_BIG, F32)))
    mx = jnp.max(mx8, axis=0, keepdims=True)
    mn = jnp.min(mn8, axis=0, keepdims=True)

    def key_pass(fn, op, init):
        def body(j, acc):
            c0 = pl.multiple_of(j * tk, tk)
            for u in range(tk // PASS_ROWS):
                s = sc_ref[pl.ds(c0 + u * PASS_ROWS, PASS_ROWS), :]
                acc = op(acc, fold(fn(s), op))
            return acc
        return lax.fori_loop(0, n_kt, body, jnp.full((SUBLANES, tq), init, F32))

    def count(pred):
        part = key_pass(lambda s: jnp.where(pred(s), 1.0, 0.0), jnp.add, 0.0)
        return jnp.sum(part, axis=0, keepdims=True)

    n_valid = (row0 + 1 + lax.broadcasted_iota(jnp.int32, (1, tq), 1)).astype(F32)
    take_all = n_valid <= kf
    cnt_top = count(lambda s: s >= mx)
    at_top = jnp.logical_and(jnp.logical_not(take_all), cnt_top >= kf)
    lo0 = jnp.where(take_all, NEG_BIG, jnp.where(at_top, mx, mn))
    done0 = jnp.logical_or(take_all, at_top)

    def bis_cond(st):
        it, lo, hi, c_lo, done_f = st
        return jnp.logical_and(it < BISECT_CAP, jnp.min(done_f) < 0.5)

    def halve(st):
        lo, hi, c_lo, done_f = st
        done = done_f > 0.5
        mid = lo + (hi - lo) * 0.5
        stuck = jnp.logical_or(mid <= lo, mid >= hi)
        cnt = count(lambda s: s >= mid)
        ge = cnt >= kf
        upd = jnp.logical_not(jnp.logical_or(done, stuck))
        raise_lo = jnp.logical_and(upd, ge)
        lo = jnp.where(raise_lo, mid, lo)
        c_lo = jnp.where(raise_lo, cnt, c_lo)
        hi = jnp.where(jnp.logical_and(upd, jnp.logical_not(ge)), mid, hi)
        done = jnp.logical_or(jnp.logical_or(done, stuck), jnp.logical_and(ge, cnt == kf))
        return lo, hi, c_lo, done.astype(F32)

    def snap(st):
        lo, hi, c_lo, done_f = st
        done = done_f > 0.5

        below = jnp.max(key_pass(lambda s: jnp.where(s < hi, s, NEG_BIG), jnp.maximum, NEG_BIG),
                        axis=0, keepdims=True)
        cnt = count(lambda s: s >= below)
        found = jnp.logical_and(jnp.logical_not(done), cnt >= kf)
        lower_hi = jnp.logical_and(jnp.logical_not(done), cnt < kf)
        lo = jnp.where(found, below, lo)
        c_lo = jnp.where(found, cnt, c_lo)
        hi = jnp.where(lower_hi, below, hi)
        return lo, hi, c_lo, jnp.logical_or(done, found).astype(F32)

    def bis_step(st):
        it, lo, hi, c_lo, done_f = st
        lo, hi, c_lo, done_f = lax.cond((it & 1) == 0, snap, halve, (lo, hi, c_lo, done_f))
        return it + 1, lo, hi, c_lo, done_f

    c_lo0 = jnp.where(at_top, cnt_top, n_valid)
    st = lax.fori_loop(0, SNAP_AFTER, lambda _, s: halve(s),
                       (lo0, mx, c_lo0, done0.astype(F32)))
    _, thr, _, c_thr, _ = lax.while_loop(bis_cond, bis_step, (jnp.int32(0),) + st)

    excess = jnp.logical_and(jnp.logical_not(take_all), c_thr > kf)

    @pl.when(jnp.max(excess.astype(F32)) > 0.5)
    def _():
        need = jnp.where(excess, kf - count(lambda s: s > thr), float(sc_ref.shape[0]))
        tril = (lax.broadcasted_iota(jnp.int32, (tk, tk), 1)
                <= lax.broadcasted_iota(jnp.int32, (tk, tk), 0)).astype(BF16)

        def drop_tile(j, seen):
            c0 = pl.multiple_of(j * tk, tk)
            s = sc_ref[pl.ds(c0, tk), :]
            eq = s == thr
            rank = seen + jnp.dot(tril, jnp.where(eq, 1.0, 0.0).astype(BF16),
                                  preferred_element_type=F32)
            sc_ref[pl.ds(c0, tk), :] = jnp.where(jnp.logical_and(eq, rank > need), NEG_BIG, s)
            return rank[tk - 1:tk, :]

        lax.fori_loop(0, n_kt, drop_tile, jnp.zeros((1, tq), F32))

    acc_ref[...] = jnp.zeros(acc_ref.shape, F32)
    m_ref[...] = jnp.full(m_ref.shape, NEG_BIG, F32)

    def attend(j, near):
        c0 = pl.multiple_of(j * tk, tk)
        keep = sc_ref[pl.ds(c0, tk), :] >= thr
        if near:
            keep = jnp.logical_and(keep, rel <= (row0 - c0))
            bias_row = pl.multiple_of(c0 - row0 + tk, tq)
        mask_scr[...] = jnp.where(keep, 0.0, NEG_BIG)
        tile_max = []
        for h in range(N_ATTN_HEADS):
            g = h // HEADS_PER_GROUP
            s = jnp.dot(k_ref[0, g, pl.ds(c0, tk), :], qt_ref[0, h],
                        preferred_element_type=F32)
            if near:
                s = s + bias_ref[h, pl.ds(bias_row, tk), :]
            s = s + mask_scr[...]
            s_scr[h] = s
            tile_max.append(jnp.max(fold(s, jnp.maximum), axis=0, keepdims=True))
        for h in range(N_ATTN_HEADS):
            g = h // HEADS_PER_GROUP
            m_old = m_ref[h]
            m_new = jnp.maximum(m_old, tile_max[h])
            alpha = jnp.exp2(m_old - m_new)
            p_scr[h] = jnp.exp2(s_scr[h] - m_new).astype(BF16)
            m_ref[h] = m_new
            pv = jnp.dot(vt_ref[0, g, :, pl.ds(c0, tk)], p_scr[h], preferred_element_type=F32)
            acc_ref[h] = acc_ref[h] * alpha + pv

    def far_step(j, c):
        attend(j, False)
        return c

    lax.fori_loop(0, j_near, far_step, 0)
    for u in range(2):
        @pl.when(j_near + u < n_kt)
        def _():
            attend(j_near + u, True)

    outs = []
    for h in range(N_ATTN_HEADS):
        a = acc_ref[h]
        outs.append(a[:ATTN_HEAD_DIM, :] / a[ATTN_HEAD_DIM:ATTN_HEAD_DIM + 1, :])
    o_ref[0] = jnp.concatenate(outs, axis=0).T.astype(BF16)


def _dsa(qt, qit, wt, ki, k, vt, bias, *, tq, tk, k_sel):
    bsz, _, _, s = qt.shape
    assert tq >= MAX_DISTANCE and tk % tq == 0 and s % tk == 0
    body = functools.partial(_dsa_body, tq=tq, tk=tk, k_sel=k_sel)
    return pl.pallas_call(
        body,
        grid=(bsz, s // tq),
        in_specs=[pl.BlockSpec((1, N_IDX_HEADS, IDX_HEAD_DIM, tq), lambda b, i: (b, 0, 0, i)),
                  pl.BlockSpec((1, N_IDX_HEADS, tq), lambda b, i: (b, 0, i)),
                  pl.BlockSpec((1, N_ATTN_HEADS, ATTN_HEAD_DIM, tq), lambda b, i: (b, 0, 0, i)),
                  pl.BlockSpec((1, s, IDX_HEAD_DIM), lambda b, i: (b, 0, 0)),
                  pl.BlockSpec((1, N_KV_GROUPS, s, ATTN_HEAD_DIM), lambda b, i: (b, 0, 0, 0)),
                  pl.BlockSpec((1, N_KV_GROUPS, V_AUG_ROWS, s), lambda b, i: (b, 0, 0, 0)),
                  _const_spec(bias.shape)],
        out_specs=pl.BlockSpec((1, tq, ATTN_WIDTH), lambda b, i: (b, i, 0)),
        out_shape=jax.ShapeDtypeStruct((bsz, s, ATTN_WIDTH), BF16),
        scratch_shapes=[pltpu.VMEM((s, tq), F32),
                        pltpu.VMEM((N_ATTN_HEADS, V_AUG_ROWS, tq), F32),
                        pltpu.VMEM((N_ATTN_HEADS, 1, tq), F32),
                        pltpu.VMEM((N_ATTN_HEADS, tk, tq), F32),
                        pltpu.VMEM((N_ATTN_HEADS, tk, tq), BF16),
                        pltpu.VMEM((tk, tq), F32)],
        compiler_params=_params("parallel", "arbitrary"),
        name="dsa",
    )(qit, wt, qt, ki, k, vt, bias)


def _split3(a):
    hi = a.astype(BF16)
    r1 = a - hi.astype(F32)
    mid = r1.astype(BF16)
    lo = (r1 - mid.astype(F32)).astype(BF16)
    return hi, mid, lo


def _hgrn2_body(rq_ref, ri_ref, rg_ref, rf_ref, lb_ref, nw_ref, o_ref,
                state_scr, a_scr, k_scr, v_scr, tril_scr, lvl_scr, *, chunk):
    t_len = rq_ref.shape[1]
    c = chunk
    d = REC_HEAD_DIM
    pad = SUBLANES
    nw = nw_ref[...]
    eye = (lax.broadcasted_iota(jnp.int32, (d, d), 0)
           == lax.broadcasted_iota(jnp.int32, (d, d), 1)).astype(F32)
    sub = lax.broadcasted_iota(jnp.int32, (c, d), 0) & (REC_BAND - 1)
    level_sizes = []
    b = REC_BAND
    while b < c:
        level_sizes.append(b)
        b *= 2

    @pl.when(pl.program_id(1) == 0)
    def _():
        state_scr[...] = jnp.zeros(state_scr.shape, F32)
        zpad = jnp.zeros((pad, a_scr.shape[1]), F32)
        a_scr[0:pad, :] = zpad
        k_scr[0:pad, :] = zpad
        v_scr[0:pad, :] = zpad
        ti = lax.broadcasted_iota(jnp.int32, (c, c), 0)
        si = lax.broadcasted_iota(jnp.int32, (c, c), 1)
        tril_scr[...] = (si <= ti).astype(BF16)
        for li, b in enumerate(level_sizes):
            blk = 2 * b
            sh = blk.bit_length() - 1
            m = jnp.logical_and(
                lax.shift_right_logical(ti, sh) == lax.shift_right_logical(si, sh),
                jnp.logical_and((ti & (blk - 1)) >= b, (si & (blk - 1)) < b))
            lvl_scr[li] = m.astype(F32)

    def head_step(hh, t0):
        cols = slice(hh * d, (hh + 1) * d)
        lb = lb_ref[:, cols]
        state = state_scr[hh]
        f = lb + (1.0 - lb) * _sigmoid_rel(rf_ref[0, pl.ds(t0, c), cols])
        g = jnp.log(f)
        kk = 1.0 - f
        qq = _silu(rq_ref[0, pl.ds(t0, c), cols].astype(F32))
        vb = ri_ref[0, pl.ds(t0, c), cols]
        vv = vb.astype(F32)

        g3 = _split3(g)
        tril = tril_scr[...]
        a = (jnp.dot(tril, g3[0], preferred_element_type=F32)
             + jnp.dot(tril, g3[1], preferred_element_type=F32)
             + jnp.dot(tril, g3[2], preferred_element_type=F32))
        a_last = a[c - 1:c, :]

        o = jnp.dot((qq * jnp.exp(a)).astype(BF16), state.astype(BF16),
                    preferred_element_type=F32)

        p = jnp.zeros((c, c), F32)
        for li, b in enumerate(level_sizes):
            blk = 2 * b
            r = jnp.broadcast_to(a.reshape(c // blk, blk, d)[:, b - 1:b, :],
                                 (c // blk, blk, d)).reshape(c, d)
            qt = (qq * jnp.exp(jnp.minimum(a - r, 0.0))).astype(BF16)
            kt = (kk * jnp.exp(jnp.minimum(r - a, 0.0))).astype(BF16)
            pb = lax.dot_general(qt, kt, (((1,), (1,)), ((), ())), preferred_element_type=F32)
            p = p + pb * lvl_scr[li]
        o = o + jnp.dot(p.astype(BF16), vb, preferred_element_type=F32)

        a_scr[pad:pad + c, cols] = a
        k_scr[pad:pad + c, cols] = kk
        v_scr[pad:pad + c, cols] = vv
        for dl in range(REC_BAND):
            a_s = a_scr[pad - dl:pad - dl + c, cols]
            k_s = k_scr[pad - dl:pad - dl + c, cols]
            v_s = v_scr[pad - dl:pad - dl + c, cols]
            e = jnp.exp(jnp.where(sub >= dl, a - a_s, NEG_BIG))
            pd = jnp.sum(qq * k_s * e, axis=1, keepdims=True)
            o = o + pd * v_s

        kd = (kk * jnp.exp(a_last - a)).astype(BF16)
        upd = lax.dot_general(kd, vb, (((0,), (0,)), ((), ())), preferred_element_type=F32)
        e_col = jnp.sum(eye * jnp.exp(a_last), axis=1, keepdims=True)
        state_scr[hh] = e_col * state + upd

        y = _rms(o, nw) * _silu(rg_ref[0, pl.ds(t0, c), cols].astype(F32))
        o_ref[0, pl.ds(t0, c), cols] = y.astype(BF16)

    def step(n, carry):
        t0 = pl.multiple_of(n * c, c)
        for hh in range(N_REC_HEADS):
            head_step(hh, t0)
        return carry

    lax.fori_loop(0, t_len // c, step, 0)


def _hgrn2(rec3, rf, lb, nw, *, chunk):
    bsz, s, _ = rec3.shape
    d = REC_HEAD_DIM
    h = N_REC_HEADS
    body = functools.partial(_hgrn2_body, chunk=chunk)
    tt = _pick_tile(s, REC_TIME_TILE)
    col = lambda part: pl.BlockSpec((1, tt, REC_WIDTH), lambda b, t: (b, t, part))
    return pl.pallas_call(
        body,
        grid=(bsz, s // tt),
        in_specs=[col(0), col(1), col(2), col(0),
                  pl.BlockSpec((1, REC_WIDTH), lambda b, t: (0, 0)),
                  pl.BlockSpec((1, d), lambda b, t: (0, 0))],
        out_specs=col(0),
        out_shape=jax.ShapeDtypeStruct((bsz, s, REC_WIDTH), BF16),
        scratch_shapes=[pltpu.VMEM((h, d, d), F32)]
        + [pltpu.VMEM((chunk + SUBLANES, REC_WIDTH), F32)] * 3
        + [pltpu.VMEM((chunk, chunk), BF16),
           pltpu.VMEM(((chunk // REC_BAND).bit_length() - 1, chunk, chunk), F32)],
        compiler_params=_params("parallel", "arbitrary"),
        name="hgrn2",
    )(rec3, rec3, rec3, rf, lb.reshape(1, REC_WIDTH), nw.reshape(1, d))


def _out_ffn_body(x_ref, attn_ref, rec_ref, gates_ref, mod_ref, wba_ref, wbr_ref, wo_ref,
                  n2_ref, wg_ref, wu_ref, wd_ref, fn_ref, o_ref, *, final):
    d = x_ref.shape[2]
    x = x_ref[0]
    g1 = mod_ref[0, 2:3, :]
    sh2 = mod_ref[0, 3:4, :]
    sc2 = mod_ref[0, 4:5, :]
    g2 = mod_ref[0, 5:6, :]
    ya = jnp.dot(attn_ref[0], wba_ref[...], preferred_element_type=F32)
    yr = jnp.dot(rec_ref[0], wbr_ref[...], preferred_element_type=F32)
    gates = gates_ref[0].astype(F32)
    mix = (_sigmoid(gates[:, :d]) * ya + _sigmoid(gates[:, d:]) * yr).astype(BF16)
    x1 = x + g1 * jnp.dot(mix, wo_ref[...], preferred_element_type=F32)
    h2 = (_rms(x1, n2_ref[...]) * (1.0 + sc2) + sh2).astype(BF16)
    gate = jnp.dot(h2, wg_ref[...], preferred_element_type=F32)
    up = jnp.dot(h2, wu_ref[...], preferred_element_type=F32)
    act = (_silu(gate) * up).astype(BF16)
    x2 = x1 + g2 * jnp.dot(act, wd_ref[...], preferred_element_type=F32)
    o_ref[0] = _rms(x2, fn_ref[...]) if final else x2


def _out_ffn(x, attn, rec, gates, mod, wba, wbr, wo, n2w, wg, wu, wd, fnw, tm, final):
    bsz, s, d = x.shape
    hid = wg.shape[1]
    tok = lambda w: pl.BlockSpec((1, tm, w), lambda b, i: (b, i, 0))
    return pl.pallas_call(
        functools.partial(_out_ffn_body, final=final),
        grid=(bsz, s // tm),
        in_specs=[tok(d), tok(ATTN_WIDTH), tok(REC_WIDTH), tok(2 * d),
                  pl.BlockSpec((1, 6, d), lambda b, i: (b, 0, 0)),
                  _const_spec((ATTN_WIDTH, d)), _const_spec((REC_WIDTH, d)), _const_spec((d, d)),
                  _const_spec((1, d)), _const_spec((d, hid)), _const_spec((d, hid)),
                  _const_spec((hid, d)), _const_spec((1, d))],
        out_specs=tok(d),
        out_shape=jax.ShapeDtypeStruct((bsz, s, d), F32),
        compiler_params=_params("parallel", "parallel"),
        name="out_ffn",
    )(x, attn, rec, gates, mod, wba, wbr, wo, n2w, wg, wu, wd, fnw)


def _pick_tile(n, want):
    t = min(want, n)
    while n % t:
        t //= 2
    return t


def _dsa_inputs(kv, kiw):
    bsz, s, _ = kv.shape
    ki = kiw[:, :, :IDX_HEAD_DIM].astype(BF16)
    wt = kiw[:, :, IDX_HEAD_DIM:IDX_HEAD_DIM + N_IDX_HEADS].transpose(0, 2, 1)
    k = kv[:, :, :KV_WIDTH].reshape(bsz, s, N_KV_GROUPS, ATTN_HEAD_DIM).transpose(0, 2, 1, 3)
    vt = kv[:, :, KV_WIDTH:].reshape(bsz, s, N_KV_GROUPS, ATTN_HEAD_DIM).transpose(0, 2, 3, 1)
    ones = jnp.ones((bsz, N_KV_GROUPS, V_AUG_ROWS - ATTN_HEAD_DIM, s), BF16)
    return ki, wt, k, jnp.concatenate([vt, ones], axis=2)


def kernel(x, c, w_ada, b_ada, norm1_w, w_in, q_norm_w, w_uq, w_uq_idx, rel_bias, lb_logits,
           rec_norm_w, w_branch_attn, w_branch_rec, w_out, norm2_w, w_ffn_gate, w_ffn_up,
           w_ffn_down, final_norm_w):
    bsz, s, d = x.shape
    depth = w_in.shape[0]
    k_sel = min(TOPK_MAX, s // 4)
    tm = _pick_tile(s, TOKEN_TILE)
    tq = _pick_tile(s, Q_TILE)
    tk = max(tq, _pick_tile(s, K_TILE))
    chunk = _pick_tile(s, REC_CHUNK)
    lower_bounds = jnp.cumsum(jax.nn.softmax(lb_logits.astype(F32), axis=0), axis=0)
    bias = _near_bias(rel_bias, tq, tk)

    o_cq = 0
    o_k = o_cq + Q_LORA_RANK
    o_v = o_k + KV_WIDTH
    o_ki = o_v + KV_WIDTH
    o_wi = o_ki + IDX_HEAD_DIM
    o_rq = o_wi + N_IDX_HEADS
    o_rf = o_rq + REC_WIDTH
    o_ri = o_rf + REC_WIDTH
    o_rg = o_ri + REC_WIDTH
    o_gt = o_rg + REC_WIDTH

    for layer in range(depth):
        mod = _ada(c, w_ada[layer], b_ada[layer]).reshape(bsz, 6, d)
        wl = w_in[layer]
        pad_a = LANES - IDX_HEAD_DIM - N_IDX_HEADS
        wa = jnp.concatenate([wl[:, o_cq:o_rq], jnp.zeros((d, pad_a), wl.dtype)], axis=1).astype(BF16)
        wr = jnp.concatenate([wl[:, o_rq:o_rf], wl[:, o_ri:o_rg], wl[:, o_rg:o_gt],
                              wl[:, o_rf:o_ri]], axis=1).astype(BF16)
        wg = wl[:, o_gt:].astype(BF16)
        wuq = (w_uq[layer] * (ATTN_HEAD_DIM ** -0.5 * LOG2_E)).astype(BF16).T
        wui = (w_uq_idx[layer] * (IDX_HEAD_DIM ** -0.5)).astype(BF16).T

        qt, qit, kv, kiw, rec3, rf, gates = _in_proj(
            x, mod, norm1_w[layer].reshape(1, d), wa, wr, wg,
            q_norm_w[layer].reshape(1, Q_LORA_RANK), wuq, wui, tm)

        ki, wt, k, vt = _dsa_inputs(kv, kiw)
        attn = _dsa(qt, qit, wt, ki, k, vt, bias, tq=tq, tk=tk, k_sel=k_sel)
        rec = _hgrn2(rec3, rf, lower_bounds[layer], rec_norm_w[layer], chunk=chunk)

        x = _out_ffn(x, attn, rec, gates, mod,
                     w_branch_attn[layer].astype(BF16), w_branch_rec[layer].astype(BF16),
                     w_out[layer].astype(BF16), norm2_w[layer].reshape(1, d),
                     w_ffn_gate[layer].astype(BF16), w_ffn_up[layer].astype(BF16),
                     w_ffn_down[layer].astype(BF16),
                     final_norm_w.reshape(1, d), tm, layer == depth - 1)
    return x
```

```python
import functools
import math

import jax
import jax.numpy as jnp
from jax import lax
from jax.experimental import pallas as pl
from jax.experimental.pallas import tpu as pltpu

F32 = jnp.float32
BF16 = jnp.bfloat16

N_ATTN_HEADS = 8
ATTN_HEAD_DIM = 64
N_KV_GROUPS = 2
HEADS_PER_GROUP = N_ATTN_HEADS // N_KV_GROUPS
Q_LORA_RANK = 256
N_IDX_HEADS = 8
IDX_HEAD_DIM = 64
TOPK_MAX = 256
N_BUCKETS = 32
MAX_DISTANCE = 128
N_REC_HEADS = 4
REC_HEAD_DIM = 128
EPS = 1e-6

ATTN_WIDTH = N_ATTN_HEADS * ATTN_HEAD_DIM
KV_WIDTH = N_KV_GROUPS * ATTN_HEAD_DIM
REC_WIDTH = N_REC_HEADS * REC_HEAD_DIM

LANES = 128
SUBLANES = 8
BF16_SUBLANES = 16
VMEM_LIMIT_BYTES = 56 * 1024 * 1024

TOKEN_TILE = 512
Q_TILE = 256
K_TILE = 512
REC_CHUNK = 128
REC_TIME_TILE = 1024
REC_BAND = 4
NEG_BIG = -1e30
BISECT_CAP = 700
SNAP_AFTER = 14
FOLD_CHAINS = 4
PASS_ROWS = 128
LOG2_E = math.log2(math.e)
SPEC_MAX_RISE = 100.0
V_AUG_ROWS = ATTN_HEAD_DIM + BF16_SUBLANES


def _rms(x, w):
    return x * lax.rsqrt(jnp.mean(x * x, axis=-1, keepdims=True) + EPS) * w


def _sigmoid(x):
    return 0.5 * jnp.tanh(0.5 * x) + 0.5


def _sigmoid_rel(x):
    return 1.0 / (1.0 + jnp.exp(-x))


def _silu(x):
    return x * _sigmoid(x)


def _params(*sem):
    return pltpu.CompilerParams(dimension_semantics=sem, vmem_limit_bytes=VMEM_LIMIT_BYTES)


def _const_spec(shape):
    nd = len(shape)
    return pl.BlockSpec(shape, lambda *_: (0,) * nd, pipeline_mode=pl.Buffered(1))


def _ada_body(c_ref, w_ref, b_ref, o_ref):
    ca = _silu(c_ref[...])
    o_ref[...] = jnp.dot(ca, w_ref[...], precision=lax.Precision.HIGHEST,
                         preferred_element_type=F32) + b_ref[...]


def _ada(c, w, b):
    bsz, d = c.shape
    n = w.shape[1]
    tn = 1024
    return pl.pallas_call(
        _ada_body,
        grid=(n // tn,),
        in_specs=[pl.BlockSpec((bsz, d), lambda j: (0, 0)),
                  pl.BlockSpec((d, tn), lambda j: (0, j)),
                  pl.BlockSpec((1, tn), lambda j: (0, j))],
        out_specs=pl.BlockSpec((bsz, tn), lambda j: (0, j)),
        out_shape=jax.ShapeDtypeStruct((bsz, n), F32),
        compiler_params=_params("arbitrary"),
        name="ada",
    )(c, w, b.reshape(1, n))


def _in_proj_body(x_ref, mod_ref, n1_ref, wa_ref, wr_ref, wg_ref, qn_ref, wuq_ref, wui_ref,
                  qt_ref, qit_ref, kv_ref, kiw_ref, rec_ref, rf_ref, gates_ref):
    tm = x_ref.shape[1]
    x = x_ref[0]
    sh1 = mod_ref[0, 0:1, :]
    sc1 = mod_ref[0, 1:2, :]
    h = (_rms(x, n1_ref[...]) * (1.0 + sc1) + sh1).astype(BF16)

    pa = jnp.dot(h, wa_ref[...], preferred_element_type=F32)
    kv_ref[0] = pa[:, Q_LORA_RANK:Q_LORA_RANK + 2 * KV_WIDTH].astype(BF16)
    kiw_ref[0] = pa[:, Q_LORA_RANK + 2 * KV_WIDTH:]
    cqn = _rms(pa[:, :Q_LORA_RANK], qn_ref[...]).astype(BF16)
    nt = (((1,), (1,)), ((), ()))
    qf = lax.dot_general(wuq_ref[...], cqn, nt, preferred_element_type=F32)
    qt_ref[0] = qf.reshape(N_ATTN_HEADS, ATTN_HEAD_DIM, tm).astype(BF16)
    qif = lax.dot_general(wui_ref[...], cqn, nt, preferred_element_type=F32)
    qit_ref[0] = qif.reshape(N_IDX_HEADS, IDX_HEAD_DIM, tm).astype(BF16)

    pr = jnp.dot(h, wr_ref[...], preferred_element_type=F32)
    rec_ref[0] = pr[:, :3 * REC_WIDTH].astype(BF16)
    rf_ref[0] = pr[:, 3 * REC_WIDTH:]
    gates_ref[0] = jnp.dot(h, wg_ref[...], preferred_element_type=F32).astype(BF16)


def _in_proj(x, mod, n1w, wa, wr, wg, qnw, wuq, wui, tm):
    bsz, s, d = x.shape
    wa_n = wa.shape[1]
    grid = (bsz, s // tm)
    tok = lambda w: pl.BlockSpec((1, tm, w), lambda b, i: (b, i, 0))
    head = lambda nh, hd: pl.BlockSpec((1, nh, hd, tm), lambda b, i: (b, 0, 0, i))
    out_shape = (
        jax.ShapeDtypeStruct((bsz, N_ATTN_HEADS, ATTN_HEAD_DIM, s), BF16),
        jax.ShapeDtypeStruct((bsz, N_IDX_HEADS, IDX_HEAD_DIM, s), BF16),
        jax.ShapeDtypeStruct((bsz, s, 2 * KV_WIDTH), BF16),
        jax.ShapeDtypeStruct((bsz, s, LANES), F32),
        jax.ShapeDtypeStruct((bsz, s, 3 * REC_WIDTH), BF16),
        jax.ShapeDtypeStruct((bsz, s, REC_WIDTH), F32),
        jax.ShapeDtypeStruct((bsz, s, 2 * d), BF16),
    )
    out_specs = (head(N_ATTN_HEADS, ATTN_HEAD_DIM), head(N_IDX_HEADS, IDX_HEAD_DIM),
                 tok(2 * KV_WIDTH), tok(LANES), tok(3 * REC_WIDTH), tok(REC_WIDTH), tok(2 * d))
    return pl.pallas_call(
        _in_proj_body,
        grid=grid,
        in_specs=[tok(d),
                  pl.BlockSpec((1, 6, d), lambda b, i: (b, 0, 0)),
                  _const_spec((1, d)),
                  _const_spec((d, wa_n)),
                  _const_spec((d, 4 * REC_WIDTH)),
                  _const_spec((d, 2 * d)),
                  _const_spec((1, Q_LORA_RANK)),
                  _const_spec((ATTN_WIDTH, Q_LORA_RANK)),
                  _const_spec((N_IDX_HEADS * IDX_HEAD_DIM, Q_LORA_RANK))],
        out_specs=out_specs,
        out_shape=out_shape,
        compiler_params=_params("parallel", "parallel"),
        name="in_proj",
    )(x, mod, n1w, wa, wr, wg, qnw, wuq, wui)


def _near_bias(rel_bias, tq, tk):
    n = jnp.arange(MAX_DISTANCE + 1, dtype=jnp.int32)
    max_exact = N_BUCKETS // 2
    nf = jnp.maximum(n, 1).astype(F32)
    large = max_exact + (jnp.log(nf / max_exact) / math.log(MAX_DISTANCE / max_exact)
                         * (N_BUCKETS - max_exact)).astype(jnp.int32)
    bucket = jnp.where(n < max_exact, n, jnp.minimum(large, N_BUCKETS - 1))
    bucket = bucket.at[MAX_DISTANCE].set(N_BUCKETS - 1)
    ids = jnp.arange(N_BUCKETS, dtype=jnp.int32)
    start = jnp.sum((bucket[None, :] < ids[:, None]).astype(jnp.int32), axis=1)
    rb = (rel_bias.astype(F32) - rel_bias[N_BUCKETS - 1].astype(F32)[None, :]) * LOG2_E
    step = rb - jnp.concatenate([jnp.zeros_like(rb[:1]), rb[:-1]], axis=0)
    dist = (jnp.arange(tq, dtype=jnp.int32)[None, :]
            - jnp.arange(2 * tk, dtype=jnp.int32)[:, None] + tk)
    reached = (dist[None, :, :] >= start[:, None, None]).astype(F32)
    return jnp.einsum("bh,but->hut", step, reached, precision=lax.Precision.HIGHEST)


def _dsa_body(qit_ref, w_ref, qt_ref, ki_ref, k_ref, vt_ref, bias_ref, o_ref,
              sc_ref, acc_ref, m_ref, s_scr, p_scr, mask_scr, rise_scr, *, tq, tk, k_sel):
    i = pl.program_id(1)
    row0 = i * tq
    n_kt = lax.div(row0 + tq + (tk - 1), tk)
    j_near = lax.div(jnp.maximum(row0 - (MAX_DISTANCE - 1), 0), tk)
    krow = lax.broadcasted_iota(jnp.int32, (tk, tq), 0)
    qcol = lax.broadcasted_iota(jnp.int32, (tk, tq), 1)
    rel = krow - qcol
    kf = float(k_sel)

    def fold(a, op):
        n = a.shape[0] // SUBLANES
        a = a.reshape(n // FOLD_CHAINS, FOLD_CHAINS, SUBLANES, tq)
        r = a[0]
        for t in range(1, n // FOLD_CHAINS):
            r = op(r, a[t])
        while r.shape[0] > 1:
            half = r.shape[0] // 2
            r = op(r[:half], r[half:])
        return r[0]

    w = w_ref[0] * (N_IDX_HEADS ** -0.5)

    def score_tile(j, carry):
        mx, mn = carry
        c0 = pl.multiple_of(j * tk, tk)
        kin = ki_ref[0, pl.ds(c0, tk), :]
        sc = None
        for h in range(N_IDX_HEADS):
            lg = jnp.dot(kin, qit_ref[0, h], preferred_element_type=F32)
            term = jnp.maximum(lg, 0.0) * w[h:h + 1, :]
            sc = term if sc is None else sc + term
        causal = rel <= (row0 - c0)
        sc_ref[pl.ds(c0, tk), :] = jnp.where(causal, sc, NEG_BIG)
        mx = jnp.maximum(mx, fold(jnp.where(causal, sc, NEG_BIG), jnp.maximum))
        mn = jnp.minimum(mn, fold(jnp.where(causal, sc, -NEG_BIG), jnp.minimum))
        return mx, mn

    def score_pair(t, carry):
        carry = score_tile(2 * t, carry)
        return score_tile(jnp.minimum(2 * t + 1, n_kt - 1), carry)

    mx8, mn8 = lax.fori_loop(0, lax.div(n_kt + 1, 2), score_pair,
                             (jnp.full((SUBLANES, tq), NEG_BIG, F32),
                              jnp.full((SUBLANES, tq), -NEG_BIG, F32)))
    mx = jnp.max(mx8, axis=0, keepdims=True)
    mn = jnp.min(mn8, axis=0, keepdims=True)

    def key_pass(fn, op, init):
        def body(j, acc):
            c0 = pl.multiple_of(j * tk, tk)
            for u in range(tk // PASS_ROWS):
                s = sc_ref[pl.ds(c0 + u * PASS_ROWS, PASS_ROWS), :]
                acc = op(acc, fold(fn(s), op))
            return acc
        return lax.fori_loop(0, n_kt, body, jnp.full((SUBLANES, tq), init, F32))

    def count(pred):
        part = key_pass(lambda s: jnp.where(pred(s), 1.0, 0.0), jnp.add, 0.0)
        return jnp.sum(part, axis=0, keepdims=True)

    n_valid = (row0 + 1 + lax.broadcasted_iota(jnp.int32, (1, tq), 1)).astype(F32)
    take_all = n_valid <= kf
    cnt_top = count(lambda s: s >= mx)
    at_top = jnp.logical_and(jnp.logical_not(take_all), cnt_top >= kf)
    lo0 = jnp.where(take_all, NEG_BIG, jnp.where(at_top, mx, mn))
    done0 = jnp.logical_or(take_all, at_top)

    def bis_cond(st):
        it, lo, hi, c_lo, done_f = st
        return jnp.logical_and(it < BISECT_CAP, jnp.min(done_f) < 0.5)

    def halve(st):
        lo, hi, c_lo, done_f = st
        done = done_f > 0.5
        mid = lo + (hi - lo) * 0.5
        stuck = jnp.logical_or(mid <= lo, mid >= hi)
        cnt = count(lambda s: s >= mid)
        ge = cnt >= kf
        upd = jnp.logical_not(jnp.logical_or(done, stuck))
        raise_lo = jnp.logical_and(upd, ge)
        lo = jnp.where(raise_lo, mid, lo)
        c_lo = jnp.where(raise_lo, cnt, c_lo)
        hi = jnp.where(jnp.logical_and(upd, jnp.logical_not(ge)), mid, hi)
        done = jnp.logical_or(jnp.logical_or(done, stuck), jnp.logical_and(ge, cnt == kf))
        return lo, hi, c_lo, done.astype(F32)

    def snap(st):
        lo, hi, c_lo, done_f = st
        done = done_f > 0.5

        below = jnp.max(key_pass(lambda s: jnp.where(s < hi, s, NEG_BIG), jnp.maximum, NEG_BIG),
                        axis=0, keepdims=True)
        cnt = count(lambda s: s >= below)
        found = jnp.logical_and(jnp.logical_not(done), cnt >= kf)
        lower_hi = jnp.logical_and(jnp.logical_not(done), cnt < kf)
        lo = jnp.where(found, below, lo)
        c_lo = jnp.where(found, cnt, c_lo)
        hi = jnp.where(lower_hi, below, hi)
        return lo, hi, c_lo, jnp.logical_or(done, found).astype(F32)

    def bis_step(st):
        it, lo, hi, c_lo, done_f = st
        lo, hi, c_lo, done_f = lax.cond((it & 1) == 0, snap, halve, (lo, hi, c_lo, done_f))
        return it + 1, lo, hi, c_lo, done_f

    c_lo0 = jnp.where(at_top, cnt_top, n_valid)
    st = lax.fori_loop(0, SNAP_AFTER, lambda _, s: halve(s),
                       (lo0, mx, c_lo0, done0.astype(F32)))
    _, thr, _, c_thr, _ = lax.while_loop(bis_cond, bis_step, (jnp.int32(0),) + st)

    excess = jnp.logical_and(jnp.logical_not(take_all), c_thr > kf)

    @pl.when(jnp.max(excess.astype(F32)) > 0.5)
    def _():
        need = jnp.where(excess, kf - count(lambda s: s > thr), float(sc_ref.shape[0]))
        tril = (lax.broadcasted_iota(jnp.int32, (tk, tk), 1)
                <= lax.broadcasted_iota(jnp.int32, (tk, tk), 0)).astype(BF16)

        def drop_tile(j, seen):
            c0 = pl.multiple_of(j * tk, tk)
            s = sc_ref[pl.ds(c0, tk), :]
            eq = s == thr
            rank = seen + jnp.dot(tril, jnp.where(eq, 1.0, 0.0).astype(BF16),
                                  preferred_element_type=F32)
            sc_ref[pl.ds(c0, tk), :] = jnp.where(jnp.logical_and(eq, rank > need), NEG_BIG, s)
            return rank[tk - 1:tk, :]

        lax.fori_loop(0, n_kt, drop_tile, jnp.zeros((1, tq), F32))

    def logits(j, h, near):
        c0 = pl.multiple_of(j * tk, tk)
        g = h // HEADS_PER_GROUP
        s = jnp.dot(k_ref[0, g, pl.ds(c0, tk), :], qt_ref[0, h],
                    preferred_element_type=F32)
        if near:
            bias_row = pl.multiple_of(c0 - row0 + tk, tq)
            s = s + bias_ref[h, pl.ds(bias_row, tk), :]
        return s + mask_scr[...]

    def set_mask(j, near):
        c0 = pl.multiple_of(j * tk, tk)
        keep = sc_ref[pl.ds(c0, tk), :] >= thr
        if near:
            keep = jnp.logical_and(keep, rel <= (row0 - c0))
        mask_scr[...] = jnp.where(keep, 0.0, NEG_BIG)

    def attend_exact(j, near):
        c0 = pl.multiple_of(j * tk, tk)
        set_mask(j, near)
        tile_max = []
        for h in range(N_ATTN_HEADS):
            s = logits(j, h, near)
            s_scr[h] = s
            tile_max.append(jnp.max(fold(s, jnp.maximum), axis=0, keepdims=True))
        for h in range(N_ATTN_HEADS):
            g = h // HEADS_PER_GROUP
            m_old = m_ref[h]
            m_new = jnp.maximum(m_old, tile_max[h])
            alpha = jnp.exp2(m_old - m_new)
            p_scr[h] = jnp.exp2(s_scr[h] - m_new).astype(BF16)
            m_ref[h] = m_new
            pv = jnp.dot(vt_ref[0, g, :, pl.ds(c0, tk)], p_scr[h], preferred_element_type=F32)
            acc_ref[h] = acc_ref[h] * alpha + pv

    def attend_fast(j, near):
        c0 = pl.multiple_of(j * tk, tk)
        set_mask(j, near)
        rise = rise_scr[...]
        tile_max = []
        for h in range(N_ATTN_HEADS):
            s = logits(j, h, near)
            p_scr[h] = jnp.exp2(s - m_ref[h]).astype(BF16)
            tile_max.append(jnp.max(fold(s, jnp.maximum), axis=0, keepdims=True))
        for h in range(N_ATTN_HEADS):
            g = h // HEADS_PER_GROUP
            pv = jnp.dot(vt_ref[0, g, :, pl.ds(c0, tk)], p_scr[h], preferred_element_type=F32)
            m_old = m_ref[h]
            m_new = jnp.maximum(m_old, tile_max[h])
            acc_ref[h] = (acc_ref[h] + pv) * jnp.exp2(m_old - m_new)
            m_ref[h] = m_new
            rise = jnp.maximum(rise, m_new - m_old)
        rise_scr[...] = rise

    def run_tiles(first_fast):
        acc_ref[...] = jnp.zeros(acc_ref.shape, F32)
        m_ref[...] = jnp.full(m_ref.shape, NEG_BIG, F32)
        rise_scr[...] = jnp.zeros(rise_scr.shape, F32)

        def tile(j, near):
            if first_fast is None:
                attend_exact(j, near)
            else:
                lax.cond(j >= first_fast, lambda: attend_fast(j, near),
                         lambda: attend_exact(j, near))

        def far_step(j, c):
            tile(j, False)
            return c

        lax.fori_loop(0, j_near, far_step, 0)
        for u in range(2):
            @pl.when(j_near + u < n_kt)
            def _():
                tile(j_near + u, True)

    run_tiles(1)

    @pl.when(jnp.max(rise_scr[...]) > SPEC_MAX_RISE)
    def _():
        run_tiles(None)

    outs = []
    for h in range(N_ATTN_HEADS):
        a = acc_ref[h]
        outs.append(a[:ATTN_HEAD_DIM, :] / a[ATTN_HEAD_DIM:ATTN_HEAD_DIM + 1, :])
    o_ref[0] = jnp.concatenate(outs, axis=0).T.astype(BF16)


def _dsa(qt, qit, wt, ki, k, vt, bias, *, tq, tk, k_sel):
    bsz, _, _, s = qt.shape
    assert tq >= MAX_DISTANCE and tk % tq == 0 and s % tk == 0
    body = functools.partial(_dsa_body, tq=tq, tk=tk, k_sel=k_sel)
    return pl.pallas_call(
        body,
        grid=(bsz, s // tq),
        in_specs=[pl.BlockSpec((1, N_IDX_HEADS, IDX_HEAD_DIM, tq), lambda b, i: (b, 0, 0, i)),
                  pl.BlockSpec((1, N_IDX_HEADS, tq), lambda b, i: (b, 0, i)),
                  pl.BlockSpec((1, N_ATTN_HEADS, ATTN_HEAD_DIM, tq), lambda b, i: (b, 0, 0, i)),
                  pl.BlockSpec((1, s, IDX_HEAD_DIM), lambda b, i: (b, 0, 0)),
                  pl.BlockSpec((1, N_KV_GROUPS, s, ATTN_HEAD_DIM), lambda b, i: (b, 0, 0, 0)),
                  pl.BlockSpec((1, N_KV_GROUPS, V_AUG_ROWS, s), lambda b, i: (b, 0, 0, 0)),
                  _const_spec(bias.shape)],
        out_specs=pl.BlockSpec((1, tq, ATTN_WIDTH), lambda b, i: (b, i, 0)),
        out_shape=jax.ShapeDtypeStruct((bsz, s, ATTN_WIDTH), BF16),
        scratch_shapes=[pltpu.VMEM((s, tq), F32),
                        pltpu.VMEM((N_ATTN_HEADS, V_AUG_ROWS, tq), F32),
                        pltpu.VMEM((N_ATTN_HEADS, 1, tq), F32),
                        pltpu.VMEM((N_ATTN_HEADS, tk, tq), F32),
                        pltpu.VMEM((N_ATTN_HEADS, tk, tq), BF16),
                        pltpu.VMEM((tk, tq), F32),
                        pltpu.VMEM((1, tq), F32)],
        compiler_params=_params("parallel", "arbitrary"),
        name="dsa",
    )(qit, wt, qt, ki, k, vt, bias)


def _split3(a):
    hi = a.astype(BF16)
    r1 = a - hi.astype(F32)
    mid = r1.astype(BF16)
    lo = (r1 - mid.astype(F32)).astype(BF16)
    return hi, mid, lo


def _hgrn2_body(rq_ref, ri_ref, rg_ref, rf_ref, lb_ref, nw_ref, o_ref,
                state_scr, a_scr, k_scr, v_scr, tril_scr, lvl_scr, *, chunk):
    t_len = rq_ref.shape[1]
    c = chunk
    d = REC_HEAD_DIM
    pad = SUBLANES
    nw = nw_ref[...]
    eye = (lax.broadcasted_iota(jnp.int32, (d, d), 0)
           == lax.broadcasted_iota(jnp.int32, (d, d), 1)).astype(F32)
    sub = lax.broadcasted_iota(jnp.int32, (c, d), 0) & (REC_BAND - 1)
    level_sizes = []
    b = REC_BAND
    while b < c:
        level_sizes.append(b)
        b *= 2

    @pl.when(pl.program_id(1) == 0)
    def _():
        state_scr[...] = jnp.zeros(state_scr.shape, F32)
        zpad = jnp.zeros((pad, a_scr.shape[1]), F32)
        a_scr[0:pad, :] = zpad
        k_scr[0:pad, :] = zpad
        v_scr[0:pad, :] = zpad
        ti = lax.broadcasted_iota(jnp.int32, (c, c), 0)
        si = lax.broadcasted_iota(jnp.int32, (c, c), 1)
        tril_scr[...] = (si <= ti).astype(BF16)
        for li, b in enumerate(level_sizes):
            blk = 2 * b
            sh = blk.bit_length() - 1
            m = jnp.logical_and(
                lax.shift_right_logical(ti, sh) == lax.shift_right_logical(si, sh),
                jnp.logical_and((ti & (blk - 1)) >= b, (si & (blk - 1)) < b))
            lvl_scr[li] = m.astype(F32)

    def head_step(hh, t0):
        cols = slice(hh * d, (hh + 1) * d)
        lb = lb_ref[:, cols]
        state = state_scr[hh]
        f = lb + (1.0 - lb) * _sigmoid_rel(rf_ref[0, pl.ds(t0, c), cols])
        g = jnp.log(f)
        kk = 1.0 - f
        qq = _silu(rq_ref[0, pl.ds(t0, c), cols].astype(F32))
        vb = ri_ref[0, pl.ds(t0, c), cols]
        vv = vb.astype(F32)

        g3 = _split3(g)
        tril = tril_scr[...]
        a = (jnp.dot(tril, g3[0], preferred_element_type=F32)
             + jnp.dot(tril, g3[1], preferred_element_type=F32)
             + jnp.dot(tril, g3[2], preferred_element_type=F32))
        a_last = a[c - 1:c, :]

        o = jnp.dot((qq * jnp.exp(a)).astype(BF16), state.astype(BF16),
                    preferred_element_type=F32)

        p = jnp.zeros((c, c), F32)
        for li, b in enumerate(level_sizes):
            blk = 2 * b
            r = jnp.broadcast_to(a.reshape(c // blk, blk, d)[:, b - 1:b, :],
                                 (c // blk, blk, d)).reshape(c, d)
            qt = (qq * jnp.exp(jnp.minimum(a - r, 0.0))).astype(BF16)
            kt = (kk * jnp.exp(jnp.minimum(r - a, 0.0))).astype(BF16)
            pb = lax.dot_general(qt, kt, (((1,), (1,)), ((), ())), preferred_element_type=F32)
            p = p + pb * lvl_scr[li]
        o = o + jnp.dot(p.astype(BF16), vb, preferred_element_type=F32)

        a_scr[pad:pad + c, cols] = a
        k_scr[pad:pad + c, cols] = kk
        v_scr[pad:pad + c, cols] = vv
        for dl in range(REC_BAND):
            a_s = a_scr[pad - dl:pad - dl + c, cols]
            k_s = k_scr[pad - dl:pad - dl + c, cols]
            v_s = v_scr[pad - dl:pad - dl + c, cols]
            e = jnp.exp(jnp.where(sub >= dl, a - a_s, NEG_BIG))
            pd = jnp.sum(qq * k_s * e, axis=1, keepdims=True)
            o = o + pd * v_s

        kd = (kk * jnp.exp(a_last - a)).astype(BF16)
        upd = lax.dot_general(kd, vb, (((0,), (0,)), ((), ())), preferred_element_type=F32)
        e_col = jnp.sum(eye * jnp.exp(a_last), axis=1, keepdims=True)
        state_scr[hh] = e_col * state + upd

        y = _rms(o, nw) * _silu(rg_ref[0, pl.ds(t0, c), cols].astype(F32))
        o_ref[0, pl.ds(t0, c), cols] = y.astype(BF16)

    def step(n, carry):
        t0 = pl.multiple_of(n * c, c)
        for hh in range(N_REC_HEADS):
            head_step(hh, t0)
        return carry

    lax.fori_loop(0, t_len // c, step, 0)


def _hgrn2(rec3, rf, lb, nw, *, chunk):
    bsz, s, _ = rec3.shape
    d = REC_HEAD_DIM
    h = N_REC_HEADS
    body = functools.partial(_hgrn2_body, chunk=chunk)
    tt = _pick_tile(s, REC_TIME_TILE)
    col = lambda part: pl.BlockSpec((1, tt, REC_WIDTH), lambda b, t: (b, t, part))
    return pl.pallas_call(
        body,
        grid=(bsz, s // tt),
        in_specs=[col(0), col(1), col(2), col(0),
                  pl.BlockSpec((1, REC_WIDTH), lambda b, t: (0, 0)),
                  pl.BlockSpec((1, d), lambda b, t: (0, 0))],
        out_specs=col(0),
        out_shape=jax.ShapeDtypeStruct((bsz, s, REC_WIDTH), BF16),
        scratch_shapes=[pltpu.VMEM((h, d, d), F32)]
        + [pltpu.VMEM((chunk + SUBLANES, REC_WIDTH), F32)] * 3
        + [pltpu.VMEM((chunk, chunk), BF16),
           pltpu.VMEM(((chunk // REC_BAND).bit_length() - 1, chunk, chunk), F32)],
        compiler_params=_params("parallel", "arbitrary"),
        name="hgrn2",
    )(rec3, rec3, rec3, rf, lb.reshape(1, REC_WIDTH), nw.reshape(1, d))


def _out_ffn_body(x_ref, attn_ref, rec_ref, gates_ref, mod_ref, wba_ref, wbr_ref, wo_ref,
                  n2_ref, wg_ref, wu_ref, wd_ref, fn_ref, o_ref, *, final):
    d = x_ref.shape[2]
    x = x_ref[0]
    g1 = mod_ref[0, 2:3, :]
    sh2 = mod_ref[0, 3:4, :]
    sc2 = mod_ref[0, 4:5, :]
    g2 = mod_ref[0, 5:6, :]
    ya = jnp.dot(attn_ref[0], wba_ref[...], preferred_element_type=F32)
    yr = jnp.dot(rec_ref[0], wbr_ref[...], preferred_element_type=F32)
    gates = gates_ref[0].astype(F32)
    mix = (_sigmoid(gates[:, :d]) * ya + _sigmoid(gates[:, d:]) * yr).astype(BF16)
    x1 = x + g1 * jnp.dot(mix, wo_ref[...], preferred_element_type=F32)
    h2 = (_rms(x1, n2_ref[...]) * (1.0 + sc2) + sh2).astype(BF16)
    gate = jnp.dot(h2, wg_ref[...], preferred_element_type=F32)
    up = jnp.dot(h2, wu_ref[...], preferred_element_type=F32)
    act = (_silu(gate) * up).astype(BF16)
    x2 = x1 + g2 * jnp.dot(act, wd_ref[...], preferred_element_type=F32)
    o_ref[0] = _rms(x2, fn_ref[...]) if final else x2


def _out_ffn(x, attn, rec, gates, mod, wba, wbr, wo, n2w, wg, wu, wd, fnw, tm, final):
    bsz, s, d = x.shape
    hid = wg.shape[1]
    tok = lambda w: pl.BlockSpec((1, tm, w), lambda b, i: (b, i, 0))
    return pl.pallas_call(
        functools.partial(_out_ffn_body, final=final),
        grid=(bsz, s // tm),
        in_specs=[tok(d), tok(ATTN_WIDTH), tok(REC_WIDTH), tok(2 * d),
                  pl.BlockSpec((1, 6, d), lambda b, i: (b, 0, 0)),
                  _const_spec((ATTN_WIDTH, d)), _const_spec((REC_WIDTH, d)), _const_spec((d, d)),
                  _const_spec((1, d)), _const_spec((d, hid)), _const_spec((d, hid)),
                  _const_spec((hid, d)), _const_spec((1, d))],
        out_specs=tok(d),
        out_shape=jax.ShapeDtypeStruct((bsz, s, d), F32),
        compiler_params=_params("parallel", "parallel"),
        name="out_ffn",
    )(x, attn, rec, gates, mod, wba, wbr, wo, n2w, wg, wu, wd, fnw)


def _pick_tile(n, want):
    t = min(want, n)
    while n % t:
        t //= 2
    return t


def _dsa_inputs(kv, kiw):
    bsz, s, _ = kv.shape
    ki = kiw[:, :, :IDX_HEAD_DIM].astype(BF16)
    wt = kiw[:, :, IDX_HEAD_DIM:IDX_HEAD_DIM + N_IDX_HEADS].transpose(0, 2, 1)
    k = kv[:, :, :KV_WIDTH].reshape(bsz, s, N_KV_GROUPS, ATTN_HEAD_DIM).transpose(0, 2, 1, 3)
    vt = kv[:, :, KV_WIDTH:].reshape(bsz, s, N_KV_GROUPS, ATTN_HEAD_DIM).transpose(0, 2, 3, 1)
    ones = jnp.ones((bsz, N_KV_GROUPS, V_AUG_ROWS - ATTN_HEAD_DIM, s), BF16)
    return ki, wt, k, jnp.concatenate([vt, ones], axis=2)


def kernel(x, c, w_ada, b_ada, norm1_w, w_in, q_norm_w, w_uq, w_uq_idx, rel_bias, lb_logits,
           rec_norm_w, w_branch_attn, w_branch_rec, w_out, norm2_w, w_ffn_gate, w_ffn_up,
           w_ffn_down, final_norm_w):
    bsz, s, d = x.shape
    depth = w_in.shape[0]
    k_sel = min(TOPK_MAX, s // 4)
    tm = _pick_tile(s, TOKEN_TILE)
    tq = _pick_tile(s, Q_TILE)
    tk = max(tq, _pick_tile(s, K_TILE))
    chunk = _pick_tile(s, REC_CHUNK)
    lower_bounds = jnp.cumsum(jax.nn.softmax(lb_logits.astype(F32), axis=0), axis=0)
    bias = _near_bias(rel_bias, tq, tk)

    o_cq = 0
    o_k = o_cq + Q_LORA_RANK
    o_v = o_k + KV_WIDTH
    o_ki = o_v + KV_WIDTH
    o_wi = o_ki + IDX_HEAD_DIM
    o_rq = o_wi + N_IDX_HEADS
    o_rf = o_rq + REC_WIDTH
    o_ri = o_rf + REC_WIDTH
    o_rg = o_ri + REC_WIDTH
    o_gt = o_rg + REC_WIDTH

    for layer in range(depth):
        mod = _ada(c, w_ada[layer], b_ada[layer]).reshape(bsz, 6, d)
        wl = w_in[layer]
        pad_a = LANES - IDX_HEAD_DIM - N_IDX_HEADS
        wa = jnp.concatenate([wl[:, o_cq:o_rq], jnp.zeros((d, pad_a), wl.dtype)], axis=1).astype(BF16)
        wr = jnp.concatenate([wl[:, o_rq:o_rf], wl[:, o_ri:o_rg], wl[:, o_rg:o_gt],
                              wl[:, o_rf:o_ri]], axis=1).astype(BF16)
        wg = wl[:, o_gt:].astype(BF16)
        wuq = (w_uq[layer] * (ATTN_HEAD_DIM ** -0.5 * LOG2_E)).astype(BF16).T
        wui = (w_uq_idx[layer] * (IDX_HEAD_DIM ** -0.5)).astype(BF16).T

        qt, qit, kv, kiw, rec3, rf, gates = _in_proj(
            x, mod, norm1_w[layer].reshape(1, d), wa, wr, wg,
            q_norm_w[layer].reshape(1, Q_LORA_RANK), wuq, wui, tm)

        ki, wt, k, vt = _dsa_inputs(kv, kiw)
        attn = _dsa(qt, qit, wt, ki, k, vt, bias, tq=tq, tk=tk, k_sel=k_sel)
        rec = _hgrn2(rec3, rf, lower_bounds[layer], rec_norm_w[layer], chunk=chunk)

        x = _out_ffn(x, attn, rec, gates, mod,
                     w_branch_attn[layer].astype(BF16), w_branch_rec[layer].astype(BF16),
                     w_out[layer].astype(BF16), norm2_w[layer].reshape(1, d),
                     w_ffn_gate[layer].astype(BF16), w_ffn_up[layer].astype(BF16),
                     w_ffn_down[layer].astype(BF16),
                     final_norm_w.reshape(1, d), tm, layer == depth - 1)
    return x
```

```python
import functools
import math

import jax
import jax.numpy as jnp
from jax import lax
from jax.experimental import pallas as pl
from jax.experimental.pallas import tpu as pltpu

F32 = jnp.float32
BF16 = jnp.bfloat16

N_ATTN_HEADS = 8
ATTN_HEAD_DIM = 64
N_KV_GROUPS = 2
HEADS_PER_GROUP = N_ATTN_HEADS // N_KV_GROUPS
Q_LORA_RANK = 256
N_IDX_HEADS = 8
IDX_HEAD_DIM = 64
TOPK_MAX = 256
N_BUCKETS = 32
MAX_DISTANCE = 128
N_REC_HEADS = 4
REC_HEAD_DIM = 128
EPS = 1e-6

ATTN_WIDTH = N_ATTN_HEADS * ATTN_HEAD_DIM
KV_WIDTH = N_KV_GROUPS * ATTN_HEAD_DIM
REC_WIDTH = N_REC_HEADS * REC_HEAD_DIM

LANES = 128
SUBLANES = 8
BF16_SUBLANES = 16
VMEM_LIMIT_BYTES = 56 * 1024 * 1024

ADA_TILE = 1024
TOKEN_TILE = 512
Q_TILE = 256
K_TILE = 512
REC_CHUNK = 128
REC_TIME_TILE = 1024
REC_BAND = 4
NEG_BIG = -1e30
BISECT_CAP = 700
SNAP_AFTER = 14
FOLD_CHAINS = 4
PASS_ROWS = 128
LOG2_E = math.log2(math.e)
SPEC_MAX_RISE = 100.0
V_AUG_ROWS = ATTN_HEAD_DIM + BF16_SUBLANES


def _rms(x, w):
    return x * lax.rsqrt(jnp.mean(x * x, axis=-1, keepdims=True) + EPS) * w


def _sigmoid(x):
    return 0.5 * jnp.tanh(0.5 * x) + 0.5


def _sigmoid_rel(x):
    return 1.0 / (1.0 + jnp.exp(-x))


def _silu(x):
    return x * _sigmoid(x)


def _params(*sem):
    return pltpu.CompilerParams(dimension_semantics=sem, vmem_limit_bytes=VMEM_LIMIT_BYTES)


def _const_spec(shape):
    nd = len(shape)
    return pl.BlockSpec(shape, lambda *_: (0,) * nd, pipeline_mode=pl.Buffered(1))


def _ada_body(c_ref, w_ref, b_ref, o_ref):
    ca = _silu(c_ref[...])
    o_ref[...] = jnp.dot(ca, w_ref[...], precision=lax.Precision.HIGHEST,
                         preferred_element_type=F32) + b_ref[...]


def _ada(c, w, b):
    bsz, d = c.shape
    n = w.shape[1]
    tn = _pick_tile(n, ADA_TILE)
    return pl.pallas_call(
        _ada_body,
        grid=(n // tn,),
        in_specs=[pl.BlockSpec((bsz, d), lambda j: (0, 0)),
                  pl.BlockSpec((d, tn), lambda j: (0, j)),
                  pl.BlockSpec((1, tn), lambda j: (0, j))],
        out_specs=pl.BlockSpec((bsz, tn), lambda j: (0, j)),
        out_shape=jax.ShapeDtypeStruct((bsz, n), F32),
        compiler_params=_params("arbitrary"),
        name="ada",
    )(c, w, b.reshape(1, n))


def _in_proj_body(x_ref, mod_ref, n1_ref, wa_ref, wr_ref, wg_ref, qn_ref, wuq_ref, wui_ref,
                  qt_ref, qit_ref, kv_ref, kiw_ref, rec_ref, rf_ref, gates_ref):
    tm = x_ref.shape[1]
    x = x_ref[0]
    sh1 = mod_ref[0, 0:1, :]
    sc1 = mod_ref[0, 1:2, :]
    h = (_rms(x, n1_ref[...]) * (1.0 + sc1) + sh1).astype(BF16)

    pa = jnp.dot(h, wa_ref[...], preferred_element_type=F32)
    kv_ref[0] = pa[:, Q_LORA_RANK:Q_LORA_RANK + 2 * KV_WIDTH].astype(BF16)
    kiw_ref[0] = pa[:, Q_LORA_RANK + 2 * KV_WIDTH:]
    cqn = _rms(pa[:, :Q_LORA_RANK], qn_ref[...]).astype(BF16)
    nt = (((1,), (1,)), ((), ()))
    qf = lax.dot_general(wuq_ref[...], cqn, nt, preferred_element_type=F32)
    qt_ref[0] = qf.reshape(N_ATTN_HEADS, ATTN_HEAD_DIM, tm).astype(BF16)
    qif = lax.dot_general(wui_ref[...], cqn, nt, preferred_element_type=F32)
    qit_ref[0] = qif.reshape(N_IDX_HEADS, IDX_HEAD_DIM, tm).astype(BF16)

    pr = jnp.dot(h, wr_ref[...], preferred_element_type=F32)
    rec_ref[0] = pr[:, :3 * REC_WIDTH].astype(BF16)
    rf_ref[0] = pr[:, 3 * REC_WIDTH:]
    gates_ref[0] = jnp.dot(h, wg_ref[...], preferred_element_type=F32).astype(BF16)


def _in_proj(x, mod, n1w, wa, wr, wg, qnw, wuq, wui, tm):
    bsz, s, d = x.shape
    wa_n = wa.shape[1]
    grid = (bsz, s // tm)
    tok = lambda w: pl.BlockSpec((1, tm, w), lambda b, i: (b, i, 0))
    head = lambda nh, hd: pl.BlockSpec((1, nh, hd, tm), lambda b, i: (b, 0, 0, i))
    out_shape = (
        jax.ShapeDtypeStruct((bsz, N_ATTN_HEADS, ATTN_HEAD_DIM, s), BF16),
        jax.ShapeDtypeStruct((bsz, N_IDX_HEADS, IDX_HEAD_DIM, s), BF16),
        jax.ShapeDtypeStruct((bsz, s, 2 * KV_WIDTH), BF16),
        jax.ShapeDtypeStruct((bsz, s, LANES), F32),
        jax.ShapeDtypeStruct((bsz, s, 3 * REC_WIDTH), BF16),
        jax.ShapeDtypeStruct((bsz, s, REC_WIDTH), F32),
        jax.ShapeDtypeStruct((bsz, s, 2 * d), BF16),
    )
    out_specs = (head(N_ATTN_HEADS, ATTN_HEAD_DIM), head(N_IDX_HEADS, IDX_HEAD_DIM),
                 tok(2 * KV_WIDTH), tok(LANES), tok(3 * REC_WIDTH), tok(REC_WIDTH), tok(2 * d))
    return pl.pallas_call(
        _in_proj_body,
        grid=grid,
        in_specs=[tok(d),
                  pl.BlockSpec((1, 6, d), lambda b, i: (b, 0, 0)),
                  _const_spec((1, d)),
                  _const_spec((d, wa_n)),
                  _const_spec((d, 4 * REC_WIDTH)),
                  _const_spec((d, 2 * d)),
                  _const_spec((1, Q_LORA_RANK)),
                  _const_spec((ATTN_WIDTH, Q_LORA_RANK)),
                  _const_spec((N_IDX_HEADS * IDX_HEAD_DIM, Q_LORA_RANK))],
        out_specs=out_specs,
        out_shape=out_shape,
        compiler_params=_params("parallel", "parallel"),
        name="in_proj",
    )(x, mod, n1w, wa, wr, wg, qnw, wuq, wui)


def _near_bias(rel_bias, tq, tk):
    n = jnp.arange(MAX_DISTANCE + 1, dtype=jnp.int32)
    max_exact = N_BUCKETS // 2
    nf = jnp.maximum(n, 1).astype(F32)
    large = max_exact + (jnp.log(nf / max_exact) / math.log(MAX_DISTANCE / max_exact)
                         * (N_BUCKETS - max_exact)).astype(jnp.int32)
    bucket = jnp.where(n < max_exact, n, jnp.minimum(large, N_BUCKETS - 1))
    bucket = bucket.at[MAX_DISTANCE].set(N_BUCKETS - 1)
    ids = jnp.arange(N_BUCKETS, dtype=jnp.int32)
    start = jnp.sum((bucket[None, :] < ids[:, None]).astype(jnp.int32), axis=1)
    rb = (rel_bias.astype(F32) - rel_bias[N_BUCKETS - 1].astype(F32)[None, :]) * LOG2_E
    step = rb - jnp.concatenate([jnp.zeros_like(rb[:1]), rb[:-1]], axis=0)
    dist = (jnp.arange(tq, dtype=jnp.int32)[None, :]
            - jnp.arange(2 * tk, dtype=jnp.int32)[:, None] + tk)
    reached = (dist[None, :, :] >= start[:, None, None]).astype(F32)
    return jnp.einsum("bh,but->hut", step, reached, precision=lax.Precision.HIGHEST)


def _dsa_body(qit_ref, w_ref, qt_ref, ki_ref, k_ref, vt_ref, bias_ref, o_ref,
              sc_ref, acc_ref, m_ref, s_scr, p_scr, mask_scr, rise_scr, *, tq, tk, k_sel):
    i = pl.program_id(1)
    row0 = i * tq
    n_kt = lax.div(row0 + tq + (tk - 1), tk)
    j_near = lax.div(jnp.maximum(row0 - (MAX_DISTANCE - 1), 0), tk)
    krow = lax.broadcasted_iota(jnp.int32, (tk, tq), 0)
    qcol = lax.broadcasted_iota(jnp.int32, (tk, tq), 1)
    rel = krow - qcol
    kf = float(k_sel)

    def fold(a, op, group=SUBLANES):
        n = a.shape[0] // group
        a = a.reshape(n // FOLD_CHAINS, FOLD_CHAINS, group, tq)
        r = a[0]
        for t in range(1, n // FOLD_CHAINS):
            r = op(r, a[t])
        while r.shape[0] > 1:
            half = r.shape[0] // 2
            r = op(r[:half], r[half:])
        return r[0]

    w = w_ref[0] * (N_IDX_HEADS ** -0.5)

    def score_tile(j, carry):
        mx, mn = carry
        c0 = pl.multiple_of(j * tk, tk)
        kin = ki_ref[0, pl.ds(c0, tk), :]
        sc = None
        for h in range(N_IDX_HEADS):
            lg = jnp.dot(kin, qit_ref[0, h], preferred_element_type=F32)
            term = jnp.maximum(lg, 0.0) * w[h:h + 1, :]
            sc = term if sc is None else sc + term
        causal = rel <= (row0 - c0)
        scm = jnp.where(causal, sc, NEG_BIG)
        sc_ref[pl.ds(c0, tk), :] = scm
        mx = jnp.maximum(mx, fold(scm, jnp.maximum))
        mn = jnp.minimum(mn, fold(jnp.where(causal, sc, -NEG_BIG), jnp.minimum))
        return mx, mn

    def score_pair(t, carry):
        carry = score_tile(2 * t, carry)
        return score_tile(jnp.minimum(2 * t + 1, n_kt - 1), carry)

    mx8, mn8 = lax.fori_loop(0, lax.div(n_kt + 1, 2), score_pair,
                             (jnp.full((SUBLANES, tq), NEG_BIG, F32),
                              jnp.full((SUBLANES, tq), -NEG_BIG, F32)))
    mx = jnp.max(mx8, axis=0, keepdims=True)
    mn = jnp.min(mn8, axis=0, keepdims=True)

    def key_pass(fn, op, init):
        def body(j, acc):
            c0 = pl.multiple_of(j * tk, tk)
            for u in range(tk // PASS_ROWS):
                s = sc_ref[pl.ds(c0 + u * PASS_ROWS, PASS_ROWS), :]
                acc = op(acc, fold(fn(s), op))
            return acc
        return lax.fori_loop(0, n_kt, body, jnp.full((SUBLANES, tq), init, F32))

    def count(pred):
        part = key_pass(lambda s: jnp.where(pred(s), 1.0, 0.0), jnp.add, 0.0)
        return jnp.sum(part, axis=0, keepdims=True)

    n_valid = (row0 + 1 + lax.broadcasted_iota(jnp.int32, (1, tq), 1)).astype(F32)
    take_all = n_valid <= kf
    cnt_top = count(lambda s: s >= mx)
    at_top = jnp.logical_and(jnp.logical_not(take_all), cnt_top >= kf)
    lo0 = jnp.where(take_all, NEG_BIG, jnp.where(at_top, mx, mn))
    done0 = jnp.logical_or(take_all, at_top)

    def bis_cond(st):
        it, lo, hi, c_lo, done_f = st
        return jnp.logical_and(it < BISECT_CAP, jnp.min(done_f) < 0.5)

    def halve(st):
        lo, hi, c_lo, done_f = st
        done = done_f > 0.5
        mid = lo + (hi - lo) * 0.5
        stuck = jnp.logical_or(mid <= lo, mid >= hi)
        cnt = count(lambda s: s >= mid)
        ge = cnt >= kf
        upd = jnp.logical_not(jnp.logical_or(done, stuck))
        raise_lo = jnp.logical_and(upd, ge)
        lo = jnp.where(raise_lo, mid, lo)
        c_lo = jnp.where(raise_lo, cnt, c_lo)
        hi = jnp.where(jnp.logical_and(upd, jnp.logical_not(ge)), mid, hi)
        done = jnp.logical_or(jnp.logical_or(done, stuck), jnp.logical_and(ge, cnt == kf))
        return lo, hi, c_lo, done.astype(F32)

    def snap(st):
        lo, hi, c_lo, done_f = st
        done = done_f > 0.5

        below = jnp.max(key_pass(lambda s: jnp.where(s < hi, s, NEG_BIG), jnp.maximum, NEG_BIG),
                        axis=0, keepdims=True)
        cnt = count(lambda s: s >= below)
        found = jnp.logical_and(jnp.logical_not(done), cnt >= kf)
        lower_hi = jnp.logical_and(jnp.logical_not(done), cnt < kf)
        lo = jnp.where(found, below, lo)
        c_lo = jnp.where(found, cnt, c_lo)
        hi = jnp.where(lower_hi, below, hi)
        return lo, hi, c_lo, jnp.logical_or(done, found).astype(F32)

    def bis_step(st):
        it, lo, hi, c_lo, done_f = st
        lo, hi, c_lo, done_f = lax.cond((it & 1) == 0, snap, halve, (lo, hi, c_lo, done_f))
        return it + 1, lo, hi, c_lo, done_f

    c_lo0 = jnp.where(at_top, cnt_top, n_valid)
    st = lax.fori_loop(0, SNAP_AFTER, lambda _, s: halve(s),
                       (lo0, mx, c_lo0, done0.astype(F32)))
    _, thr, _, c_thr, _ = lax.while_loop(bis_cond, bis_step, (jnp.int32(0),) + st)

    excess = jnp.logical_and(jnp.logical_not(take_all), c_thr > kf)

    @pl.when(jnp.max(excess.astype(F32)) > 0.5)
    def _():
        need = jnp.where(excess, kf - count(lambda s: s > thr), float(sc_ref.shape[0]))
        tril = (lax.broadcasted_iota(jnp.int32, (tk, tk), 1)
                <= lax.broadcasted_iota(jnp.int32, (tk, tk), 0)).astype(BF16)

        def drop_tile(j, seen):
            c0 = pl.multiple_of(j * tk, tk)
            s = sc_ref[pl.ds(c0, tk), :]
            eq = s == thr
            rank = seen + jnp.dot(tril, jnp.where(eq, 1.0, 0.0).astype(BF16),
                                  preferred_element_type=F32)
            sc_ref[pl.ds(c0, tk), :] = jnp.where(jnp.logical_and(eq, rank > need), NEG_BIG, s)
            return rank[tk - 1:tk, :]

        lax.fori_loop(0, n_kt, drop_tile, jnp.zeros((1, tq), F32))

    def logits(j, h, near):
        c0 = pl.multiple_of(j * tk, tk)
        g = h // HEADS_PER_GROUP
        s = jnp.dot(k_ref[0, g, pl.ds(c0, tk), :], qt_ref[0, h],
                    preferred_element_type=F32)
        if near:
            bias_row = pl.multiple_of(c0 - row0 + tk, tq)
            s = s + bias_ref[h, pl.ds(bias_row, tk), :]
        return s

    def selected(j, near):
        c0 = pl.multiple_of(j * tk, tk)
        keep = sc_ref[pl.ds(c0, tk), :] >= thr
        if near:
            keep = jnp.logical_and(keep, rel <= (row0 - c0))
        return keep

    def attend_exact(j, near):
        c0 = pl.multiple_of(j * tk, tk)
        mask_scr[...] = jnp.where(selected(j, near), 0.0, NEG_BIG)
        tile_max = []
        for h in range(N_ATTN_HEADS):
            s = logits(j, h, near) + mask_scr[...]
            s_scr[h] = s
            tile_max.append(jnp.max(fold(s, jnp.maximum), axis=0, keepdims=True))
        for h in range(N_ATTN_HEADS):
            g = h // HEADS_PER_GROUP
            m_old = m_ref[h]
            m_new = jnp.maximum(m_old, tile_max[h])
            alpha = jnp.exp2(m_old - m_new)
            p_scr[h] = jnp.exp2(s_scr[h] - m_new).astype(BF16)
            m_ref[h] = m_new
            pv = jnp.dot(vt_ref[0, g, :, pl.ds(c0, tk)], p_scr[h], preferred_element_type=F32)
            acc_ref[h] = acc_ref[h] * alpha + pv

    def attend_fast(j, near):
        c0 = pl.multiple_of(j * tk, tk)
        mask_scr[...] = jnp.where(selected(j, near), 0.0, NEG_BIG)
        rise = rise_scr[...]
        tile_max = []
        for h in range(N_ATTN_HEADS):
            s = logits(j, h, near) + mask_scr[...]
            p_scr[h] = jnp.exp2(s - m_ref[h]).astype(BF16)
            tile_max.append(jnp.max(fold(s, jnp.maximum), axis=0, keepdims=True))
        for h in range(N_ATTN_HEADS):
            g = h // HEADS_PER_GROUP
            pv = jnp.dot(vt_ref[0, g, :, pl.ds(c0, tk)], p_scr[h], preferred_element_type=F32)
            m_old = m_ref[h]
            m_new = jnp.maximum(m_old, tile_max[h])
            acc_ref[h] = (acc_ref[h] + pv) * jnp.exp2(m_old - m_new)
            m_ref[h] = m_new
            rise = jnp.maximum(rise, m_new - m_old)
        rise_scr[...] = rise

    def run_tiles(first_fast):
        acc_ref[...] = jnp.zeros(acc_ref.shape, F32)
        m_ref[...] = jnp.full(m_ref.shape, NEG_BIG, F32)
        rise_scr[...] = jnp.zeros(rise_scr.shape, F32)

        def tile(j, near):
            if first_fast is None:
                attend_exact(j, near)
            else:
                lax.cond(j >= first_fast, lambda: attend_fast(j, near),
                         lambda: attend_exact(j, near))

        def far_step(j, c):
            tile(j, False)
            return c

        lax.fori_loop(0, j_near, far_step, 0)
        for u in range(2):
            @pl.when(j_near + u < n_kt)
            def _():
                tile(j_near + u, True)

    run_tiles(1)

    @pl.when(jnp.max(rise_scr[...]) > SPEC_MAX_RISE)
    def _():
        run_tiles(None)

    outs = []
    for h in range(N_ATTN_HEADS):
        a = acc_ref[h]
        outs.append(a[:ATTN_HEAD_DIM, :] / a[ATTN_HEAD_DIM:ATTN_HEAD_DIM + 1, :])
    o_ref[0] = jnp.concatenate(outs, axis=0).T.astype(BF16)


def _dsa(qt, qit, wt, ki, k, vt, bias, *, tq, tk, k_sel):
    bsz, _, _, s = qt.shape
    assert tq >= MAX_DISTANCE and tk % tq == 0 and s % tk == 0
    body = functools.partial(_dsa_body, tq=tq, tk=tk, k_sel=k_sel)
    return pl.pallas_call(
        body,
        grid=(bsz, s // tq),
        in_specs=[pl.BlockSpec((1, N_IDX_HEADS, IDX_HEAD_DIM, tq), lambda b, i: (b, 0, 0, i)),
                  pl.BlockSpec((1, N_IDX_HEADS, tq), lambda b, i: (b, 0, i)),
                  pl.BlockSpec((1, N_ATTN_HEADS, ATTN_HEAD_DIM, tq), lambda b, i: (b, 0, 0, i)),
                  pl.BlockSpec((1, s, IDX_HEAD_DIM), lambda b, i: (b, 0, 0)),
                  pl.BlockSpec((1, N_KV_GROUPS, s, ATTN_HEAD_DIM), lambda b, i: (b, 0, 0, 0)),
                  pl.BlockSpec((1, N_KV_GROUPS, V_AUG_ROWS, s), lambda b, i: (b, 0, 0, 0)),
                  _const_spec(bias.shape)],
        out_specs=pl.BlockSpec((1, tq, ATTN_WIDTH), lambda b, i: (b, i, 0)),
        out_shape=jax.ShapeDtypeStruct((bsz, s, ATTN_WIDTH), BF16),
        scratch_shapes=[pltpu.VMEM((s, tq), F32),
                        pltpu.VMEM((N_ATTN_HEADS, V_AUG_ROWS, tq), F32),
                        pltpu.VMEM((N_ATTN_HEADS, 1, tq), F32),
                        pltpu.VMEM((N_ATTN_HEADS, tk, tq), F32),
                        pltpu.VMEM((N_ATTN_HEADS, tk, tq), BF16),
                        pltpu.VMEM((tk, tq), F32),
                        pltpu.VMEM((1, tq), F32)],
        compiler_params=_params("parallel", "arbitrary"),
        name="dsa",
    )(qit, wt, qt, ki, k, vt, bias)


def _split3(a):
    hi = a.astype(BF16)
    r1 = a - hi.astype(F32)
    mid = r1.astype(BF16)
    lo = (r1 - mid.astype(F32)).astype(BF16)
    return hi, mid, lo


def _hgrn2_body(rq_ref, ri_ref, rg_ref, rf_ref, lb_ref, nw_ref, o_ref,
                state_scr, tril_scr, lvl_scr, *, chunk):
    t_len = rq_ref.shape[1]
    c = chunk
    d = REC_HEAD_DIM
    nw = nw_ref[...]
    eye = (lax.broadcasted_iota(jnp.int32, (d, d), 0)
           == lax.broadcasted_iota(jnp.int32, (d, d), 1)).astype(F32)
    sub = lax.broadcasted_iota(jnp.int32, (c, d), 0) & (REC_BAND - 1)
    level_sizes = []
    b = REC_BAND
    while b < c:
        level_sizes.append(b)
        b *= 2

    @pl.when(pl.program_id(1) == 0)
    def _():
        state_scr[...] = jnp.zeros(state_scr.shape, F32)
        ti = lax.broadcasted_iota(jnp.int32, (c, c), 0)
        si = lax.broadcasted_iota(jnp.int32, (c, c), 1)
        tril_scr[...] = (si <= ti).astype(BF16)
        for li, b in enumerate(level_sizes):
            blk = 2 * b
            sh = blk.bit_length() - 1
            m = jnp.logical_and(
                lax.shift_right_logical(ti, sh) == lax.shift_right_logical(si, sh),
                jnp.logical_and((ti & (blk - 1)) >= b, (si & (blk - 1)) < b))
            lvl_scr[li] = m.astype(F32)

    def head_step(hh, t0):
        cols = slice(hh * d, (hh + 1) * d)
        lb = lb_ref[:, cols]
        state = state_scr[hh]
        f = lb + (1.0 - lb) * _sigmoid_rel(rf_ref[0, pl.ds(t0, c), cols])
        g = jnp.log(f)
        kk = 1.0 - f
        qq = _silu(rq_ref[0, pl.ds(t0, c), cols].astype(F32))
        vb = ri_ref[0, pl.ds(t0, c), cols]
        vv = vb.astype(F32)

        g3 = _split3(g)
        tril = tril_scr[...]
        a = (jnp.dot(tril, g3[0], preferred_element_type=F32)
             + jnp.dot(tril, g3[1], preferred_element_type=F32)
             + jnp.dot(tril, g3[2], preferred_element_type=F32))
        a_last = a[c - 1:c, :]

        o = jnp.dot((qq * jnp.exp(a)).astype(BF16), state.astype(BF16),
                    preferred_element_type=F32)

        p = jnp.zeros((c, c), F32)
        for li, b in enumerate(level_sizes):
            blk = 2 * b
            r = jnp.broadcast_to(a.reshape(c // blk, blk, d)[:, b - 1:b, :],
                                 (c // blk, blk, d)).reshape(c, d)
            qt = (qq * jnp.exp(jnp.minimum(a - r, 0.0))).astype(BF16)
            kt = (kk * jnp.exp(jnp.minimum(r - a, 0.0))).astype(BF16)
            pb = lax.dot_general(qt, kt, (((1,), (1,)), ((), ())), preferred_element_type=F32)
            p = p + pb * lvl_scr[li]
        o = o + jnp.dot(p.astype(BF16), vb, preferred_element_type=F32)

        def back(x, dl):
            if dl == 0:
                return x
            return pltpu.roll(x.reshape(c // SUBLANES, SUBLANES, d), dl, axis=1).reshape(c, d)

        for dl in range(REC_BAND):
            a_s = back(a, dl)
            k_s = back(kk, dl)
            v_s = back(vv, dl)
            e = jnp.exp(jnp.where(sub >= dl, a - a_s, NEG_BIG))
            pd = jnp.sum(qq * k_s * e, axis=1, keepdims=True)
            o = o + pd * v_s

        kd = (kk * jnp.exp(a_last - a)).astype(BF16)
        upd = lax.dot_general(kd, vb, (((0,), (0,)), ((), ())), preferred_element_type=F32)
        e_col = jnp.sum(eye * jnp.exp(a_last), axis=1, keepdims=True)
        state_scr[hh] = e_col * state + upd

        y = _rms(o, nw) * _silu(rg_ref[0, pl.ds(t0, c), cols].astype(F32))
        o_ref[0, pl.ds(t0, c), cols] = y.astype(BF16)

    def step(n, carry):
        t0 = pl.multiple_of(n * c, c)
        for hh in range(N_REC_HEADS):
            head_step(hh, t0)
        return carry

    lax.fori_loop(0, t_len // c, step, 0)


def _hgrn2(rec3, rf, lb, nw, *, chunk):
    bsz, s, _ = rec3.shape
    d = REC_HEAD_DIM
    h = N_REC_HEADS
    body = functools.partial(_hgrn2_body, chunk=chunk)
    tt = _pick_tile(s, REC_TIME_TILE)
    col = lambda part: pl.BlockSpec((1, tt, REC_WIDTH), lambda b, t: (b, t, part))
    return pl.pallas_call(
        body,
        grid=(bsz, s // tt),
        in_specs=[col(0), col(1), col(2), col(0),
                  pl.BlockSpec((1, REC_WIDTH), lambda b, t: (0, 0)),
                  pl.BlockSpec((1, d), lambda b, t: (0, 0))],
        out_specs=col(0),
        out_shape=jax.ShapeDtypeStruct((bsz, s, REC_WIDTH), BF16),
        scratch_shapes=[pltpu.VMEM((h, d, d), F32)]
        + [pltpu.VMEM((chunk, chunk), BF16),
           pltpu.VMEM(((chunk // REC_BAND).bit_length() - 1, chunk, chunk), F32)],
        compiler_params=_params("parallel", "arbitrary"),
        name="hgrn2",
    )(rec3, rec3, rec3, rf, lb.reshape(1, REC_WIDTH), nw.reshape(1, d))


def _out_ffn_body(x_ref, attn_ref, rec_ref, gates_ref, mod_ref, wba_ref, wbr_ref, wo_ref,
                  n2_ref, wg_ref, wu_ref, wd_ref, fn_ref, o_ref, *, final):
    d = x_ref.shape[2]
    x = x_ref[0]
    g1 = mod_ref[0, 2:3, :]
    sh2 = mod_ref[0, 3:4, :]
    sc2 = mod_ref[0, 4:5, :]
    g2 = mod_ref[0, 5:6, :]
    ya = jnp.dot(attn_ref[0], wba_ref[...], preferred_element_type=F32)
    yr = jnp.dot(rec_ref[0], wbr_ref[...], preferred_element_type=F32)
    gates = gates_ref[0].astype(F32)
    mix = (_sigmoid(gates[:, :d]) * ya + _sigmoid(gates[:, d:]) * yr).astype(BF16)
    x1 = x + g1 * jnp.dot(mix, wo_ref[...], preferred_element_type=F32)
    h2 = (_rms(x1, n2_ref[...]) * (1.0 + sc2) + sh2).astype(BF16)
    gate = jnp.dot(h2, wg_ref[...], preferred_element_type=F32)
    up = jnp.dot(h2, wu_ref[...], preferred_element_type=F32)
    act = (_silu(gate) * up).astype(BF16)
    x2 = x1 + g2 * jnp.dot(act, wd_ref[...], preferred_element_type=F32)
    o_ref[0] = _rms(x2, fn_ref[...]) if final else x2


def _out_ffn(x, attn, rec, gates, mod, wba, wbr, wo, n2w, wg, wu, wd, fnw, tm, final):
    bsz, s, d = x.shape
    hid = wg.shape[1]
    tok = lambda w: pl.BlockSpec((1, tm, w), lambda b, i: (b, i, 0))
    return pl.pallas_call(
        functools.partial(_out_ffn_body, final=final),
        grid=(bsz, s // tm),
        in_specs=[tok(d), tok(ATTN_WIDTH), tok(REC_WIDTH), tok(2 * d),
                  pl.BlockSpec((1, 6, d), lambda b, i: (b, 0, 0)),
                  _const_spec((ATTN_WIDTH, d)), _const_spec((REC_WIDTH, d)), _const_spec((d, d)),
                  _const_spec((1, d)), _const_spec((d, hid)), _const_spec((d, hid)),
                  _const_spec((hid, d)), _const_spec((1, d))],
        out_specs=tok(d),
        out_shape=jax.ShapeDtypeStruct((bsz, s, d), F32),
        compiler_params=_params("parallel", "parallel"),
        name="out_ffn",
    )(x, attn, rec, gates, mod, wba, wbr, wo, n2w, wg, wu, wd, fnw)


def _pick_tile(n, want):
    t = min(want, n)
    while n % t:
        t //= 2
    return t


def _dsa_inputs(kv, kiw):
    bsz, s, _ = kv.shape
    ki = kiw[:, :, :IDX_HEAD_DIM].astype(BF16)
    wt = kiw[:, :, IDX_HEAD_DIM:IDX_HEAD_DIM + N_IDX_HEADS].transpose(0, 2, 1)
    k = kv[:, :, :KV_WIDTH].reshape(bsz, s, N_KV_GROUPS, ATTN_HEAD_DIM).transpose(0, 2, 1, 3)
    vt = kv[:, :, KV_WIDTH:].reshape(bsz, s, N_KV_GROUPS, ATTN_HEAD_DIM).transpose(0, 2, 3, 1)
    ones = jnp.ones((bsz, N_KV_GROUPS, V_AUG_ROWS - ATTN_HEAD_DIM, s), BF16)
    return ki, wt, k, jnp.concatenate([vt, ones], axis=2)


def kernel(x, c, w_ada, b_ada, norm1_w, w_in, q_norm_w, w_uq, w_uq_idx, rel_bias, lb_logits,
           rec_norm_w, w_branch_attn, w_branch_rec, w_out, norm2_w, w_ffn_gate, w_ffn_up,
           w_ffn_down, final_norm_w):
    bsz, s, d = x.shape
    depth = w_in.shape[0]
    k_sel = min(TOPK_MAX, s // 4)
    tm = _pick_tile(s, TOKEN_TILE)
    tq = _pick_tile(s, Q_TILE)
    tk = max(tq, _pick_tile(s, K_TILE))
    chunk = _pick_tile(s, REC_CHUNK)
    lower_bounds = jnp.cumsum(jax.nn.softmax(lb_logits.astype(F32), axis=0), axis=0)
    bias = _near_bias(rel_bias, tq, tk)

    o_cq = 0
    o_k = o_cq + Q_LORA_RANK
    o_v = o_k + KV_WIDTH
    o_ki = o_v + KV_WIDTH
    o_wi = o_ki + IDX_HEAD_DIM
    o_rq = o_wi + N_IDX_HEADS
    o_rf = o_rq + REC_WIDTH
    o_ri = o_rf + REC_WIDTH
    o_rg = o_ri + REC_WIDTH
    o_gt = o_rg + REC_WIDTH

    for layer in range(depth):
        mod = _ada(c, w_ada[layer], b_ada[layer]).reshape(bsz, 6, d)
        wl = w_in[layer]
        pad_a = LANES - IDX_HEAD_DIM - N_IDX_HEADS
        wa = jnp.concatenate([wl[:, o_cq:o_rq], jnp.zeros((d, pad_a), wl.dtype)], axis=1).astype(BF16)
        wr = jnp.concatenate([wl[:, o_rq:o_rf], wl[:, o_ri:o_rg], wl[:, o_rg:o_gt],
                              wl[:, o_rf:o_ri]], axis=1).astype(BF16)
        wg = wl[:, o_gt:].astype(BF16)
        wuq = (w_uq[layer] * (ATTN_HEAD_DIM ** -0.5 * LOG2_E)).astype(BF16).T
        wui = (w_uq_idx[layer] * (IDX_HEAD_DIM ** -0.5)).astype(BF16).T

        qt, qit, kv, kiw, rec3, rf, gates = _in_proj(
            x, mod, norm1_w[layer].reshape(1, d), wa, wr, wg,
            q_norm_w[layer].reshape(1, Q_LORA_RANK), wuq, wui, tm)

        ki, wt, k, vt = _dsa_inputs(kv, kiw)
        attn = _dsa(qt, qit, wt, ki, k, vt, bias, tq=tq, tk=tk, k_sel=k_sel)
        rec = _hgrn2(rec3, rf, lower_bounds[layer], rec_norm_w[layer], chunk=chunk)

        x = _out_ffn(x, attn, rec, gates, mod,
                     w_branch_attn[layer].astype(BF16), w_branch_rec[layer].astype(BF16),
                     w_out[layer].astype(BF16), norm2_w[layer].reshape(1, d),
                     w_ffn_gate[layer].astype(BF16), w_ffn_up[layer].astype(BF16),
                     w_ffn_down[layer].astype(BF16),
                     final_norm_w.reshape(1, d), tm, layer == depth - 1)
    return x
```

```python
import functools
import math

import jax
import jax.numpy as jnp
from jax import lax
from jax.experimental import pallas as pl
from jax.experimental.pallas import tpu as pltpu

F32 = jnp.float32
BF16 = jnp.bfloat16

N_ATTN_HEADS = 8
ATTN_HEAD_DIM = 64
N_KV_GROUPS = 2
HEADS_PER_GROUP = N_ATTN_HEADS // N_KV_GROUPS
Q_LORA_RANK = 256
N_IDX_HEADS = 8
IDX_HEAD_DIM = 64
TOPK_MAX = 256
N_BUCKETS = 32
MAX_DISTANCE = 128
N_REC_HEADS = 4
REC_HEAD_DIM = 128
EPS = 1e-6

ATTN_WIDTH = N_ATTN_HEADS * ATTN_HEAD_DIM
KV_WIDTH = N_KV_GROUPS * ATTN_HEAD_DIM
REC_WIDTH = N_REC_HEADS * REC_HEAD_DIM

LANES = 128
SUBLANES = 8
BF16_SUBLANES = 16
VMEM_LIMIT_BYTES = 56 * 1024 * 1024

ADA_TILE = 1024
TOKEN_TILE = 512
Q_TILE = 256
K_TILE = 512
REC_CHUNK = 128
REC_TIME_TILE = 1024
REC_BAND = 4
NEG_BIG = -1e30
BISECT_CAP = 700
SNAP_AFTER = 14
FOLD_CHAINS = 4
PASS_ROWS = 128
LOG2_E = math.log2(math.e)
MAX_SANE_DENOM = 1e30
V_AUG_ROWS = ATTN_HEAD_DIM + BF16_SUBLANES


def _rms(x, w):
    return x * lax.rsqrt(jnp.mean(x * x, axis=-1, keepdims=True) + EPS) * w


def _sigmoid(x):
    return 0.5 * jnp.tanh(0.5 * x) + 0.5


def _sigmoid_rel(x):
    return 1.0 / (1.0 + jnp.exp(-x))


def _silu(x):
    return x * _sigmoid(x)


def _params(*sem):
    return pltpu.CompilerParams(dimension_semantics=sem, vmem_limit_bytes=VMEM_LIMIT_BYTES)


def _const_spec(shape):
    nd = len(shape)
    return pl.BlockSpec(shape, lambda *_: (0,) * nd, pipeline_mode=pl.Buffered(1))


def _ada_body(c_ref, w_ref, b_ref, o_ref):
    ca = _silu(c_ref[...])
    o_ref[...] = jnp.dot(ca, w_ref[...], precision=lax.Precision.HIGHEST,
                         preferred_element_type=F32) + b_ref[...]


def _ada(c, w, b):
    bsz, d = c.shape
    n = w.shape[1]
    tn = _pick_tile(n, ADA_TILE)
    return pl.pallas_call(
        _ada_body,
        grid=(n // tn,),
        in_specs=[pl.BlockSpec((bsz, d), lambda j: (0, 0)),
                  pl.BlockSpec((d, tn), lambda j: (0, j)),
                  pl.BlockSpec((1, tn), lambda j: (0, j))],
        out_specs=pl.BlockSpec((bsz, tn), lambda j: (0, j)),
        out_shape=jax.ShapeDtypeStruct((bsz, n), F32),
        compiler_params=_params("arbitrary"),
        name="ada",
    )(c, w, b.reshape(1, n))


def _in_proj_body(x_ref, mod_ref, n1_ref, wa_ref, wr_ref, wg_ref, qn_ref, wuq_ref, wui_ref,
                  qt_ref, qit_ref, kv_ref, kiw_ref, rec_ref, rf_ref, gates_ref):
    tm = x_ref.shape[1]
    x = x_ref[0]
    sh1 = mod_ref[0, 0:1, :]
    sc1 = mod_ref[0, 1:2, :]
    h = (_rms(x, n1_ref[...]) * (1.0 + sc1) + sh1).astype(BF16)

    pa = jnp.dot(h, wa_ref[...], preferred_element_type=F32)
    kv_ref[0] = pa[:, Q_LORA_RANK:Q_LORA_RANK + 2 * KV_WIDTH].astype(BF16)
    kiw_ref[0] = pa[:, Q_LORA_RANK + 2 * KV_WIDTH:]
    cqn = _rms(pa[:, :Q_LORA_RANK], qn_ref[...]).astype(BF16)
    nt = (((1,), (1,)), ((), ()))
    qf = lax.dot_general(wuq_ref[...], cqn, nt, preferred_element_type=F32)
    qt_ref[0] = qf.reshape(N_ATTN_HEADS, ATTN_HEAD_DIM, tm).astype(BF16)
    qif = lax.dot_general(wui_ref[...], cqn, nt, preferred_element_type=F32)
    qit_ref[0] = qif.reshape(N_IDX_HEADS, IDX_HEAD_DIM, tm).astype(BF16)

    pr = jnp.dot(h, wr_ref[...], preferred_element_type=F32)
    rec_ref[0] = pr[:, :3 * REC_WIDTH].astype(BF16)
    rf_ref[0] = pr[:, 3 * REC_WIDTH:]
    gates_ref[0] = jnp.dot(h, wg_ref[...], preferred_element_type=F32).astype(BF16)


def _in_proj(x, mod, n1w, wa, wr, wg, qnw, wuq, wui, tm):
    bsz, s, d = x.shape
    wa_n = wa.shape[1]
    grid = (bsz, s // tm)
    tok = lambda w: pl.BlockSpec((1, tm, w), lambda b, i: (b, i, 0))
    head = lambda nh, hd: pl.BlockSpec((1, nh, hd, tm), lambda b, i: (b, 0, 0, i))
    out_shape = (
        jax.ShapeDtypeStruct((bsz, N_ATTN_HEADS, ATTN_HEAD_DIM, s), BF16),
        jax.ShapeDtypeStruct((bsz, N_IDX_HEADS, IDX_HEAD_DIM, s), BF16),
        jax.ShapeDtypeStruct((bsz, s, 2 * KV_WIDTH), BF16),
        jax.ShapeDtypeStruct((bsz, s, LANES), F32),
        jax.ShapeDtypeStruct((bsz, s, 3 * REC_WIDTH), BF16),
        jax.ShapeDtypeStruct((bsz, s, REC_WIDTH), F32),
        jax.ShapeDtypeStruct((bsz, s, 2 * d), BF16),
    )
    out_specs = (head(N_ATTN_HEADS, ATTN_HEAD_DIM), head(N_IDX_HEADS, IDX_HEAD_DIM),
                 tok(2 * KV_WIDTH), tok(LANES), tok(3 * REC_WIDTH), tok(REC_WIDTH), tok(2 * d))
    return pl.pallas_call(
        _in_proj_body,
        grid=grid,
        in_specs=[tok(d),
                  pl.BlockSpec((1, 6, d), lambda b, i: (b, 0, 0)),
                  _const_spec((1, d)),
                  _const_spec((d, wa_n)),
                  _const_spec((d, 4 * REC_WIDTH)),
                  _const_spec((d, 2 * d)),
                  _const_spec((1, Q_LORA_RANK)),
                  _const_spec((ATTN_WIDTH, Q_LORA_RANK)),
                  _const_spec((N_IDX_HEADS * IDX_HEAD_DIM, Q_LORA_RANK))],
        out_specs=out_specs,
        out_shape=out_shape,
        compiler_params=_params("parallel", "parallel"),
        name="in_proj",
    )(x, mod, n1w, wa, wr, wg, qnw, wuq, wui)


def _near_bias(rel_bias, tq, tk):
    n = jnp.arange(MAX_DISTANCE + 1, dtype=jnp.int32)
    max_exact = N_BUCKETS // 2
    nf = jnp.maximum(n, 1).astype(F32)
    large = max_exact + (jnp.log(nf / max_exact) / math.log(MAX_DISTANCE / max_exact)
                         * (N_BUCKETS - max_exact)).astype(jnp.int32)
    bucket = jnp.where(n < max_exact, n, jnp.minimum(large, N_BUCKETS - 1))
    bucket = bucket.at[MAX_DISTANCE].set(N_BUCKETS - 1)
    ids = jnp.arange(N_BUCKETS, dtype=jnp.int32)
    start = jnp.sum((bucket[None, :] < ids[:, None]).astype(jnp.int32), axis=1)
    rb = (rel_bias.astype(F32) - rel_bias[N_BUCKETS - 1].astype(F32)[None, :]) * LOG2_E
    step = rb - jnp.concatenate([jnp.zeros_like(rb[:1]), rb[:-1]], axis=0)
    dist = (jnp.arange(tq, dtype=jnp.int32)[None, :]
            - jnp.arange(2 * tk, dtype=jnp.int32)[:, None] + tk)
    reached = (dist[None, :, :] >= start[:, None, None]).astype(F32)
    return jnp.einsum("bh,but->hut", step, reached, precision=lax.Precision.HIGHEST)


def _dsa_body(qit_ref, w_ref, qt_ref, ki_ref, k_ref, vt_ref, bias_ref, o_ref,
              sc_ref, acc_ref, m_ref, s_scr, p_scr, mask_scr, *, tq, tk, k_sel):
    i = pl.program_id(1)
    row0 = i * tq
    n_kt = lax.div(row0 + tq + (tk - 1), tk)
    j_near = lax.div(jnp.maximum(row0 - (MAX_DISTANCE - 1), 0), tk)
    krow = lax.broadcasted_iota(jnp.int32, (tk, tq), 0)
    qcol = lax.broadcasted_iota(jnp.int32, (tk, tq), 1)
    rel = krow - qcol
    kf = float(k_sel)

    def fold(a, op, group=SUBLANES):
        n = a.shape[0] // group
        a = a.reshape(n // FOLD_CHAINS, FOLD_CHAINS, group, tq)
        r = a[0]
        for t in range(1, n // FOLD_CHAINS):
            r = op(r, a[t])
        while r.shape[0] > 1:
            half = r.shape[0] // 2
            r = op(r[:half], r[half:])
        return r[0]

    w = w_ref[0] * (N_IDX_HEADS ** -0.5)

    def score_tile(j, carry):
        mx, mn = carry
        c0 = pl.multiple_of(j * tk, tk)
        kin = ki_ref[0, pl.ds(c0, tk), :]
        sc = None
        for h in range(N_IDX_HEADS):
            lg = jnp.dot(kin, qit_ref[0, h], preferred_element_type=F32)
            term = jnp.maximum(lg, 0.0) * w[h:h + 1, :]
            sc = term if sc is None else sc + term
        causal = rel <= (row0 - c0)
        scm = jnp.where(causal, sc, NEG_BIG)
        sc_ref[pl.ds(c0, tk), :] = scm
        mx = jnp.maximum(mx, fold(scm, jnp.maximum))
        mn = jnp.minimum(mn, fold(jnp.where(causal, sc, -NEG_BIG), jnp.minimum))
        return mx, mn

    def score_pair(t, carry):
        carry = score_tile(2 * t, carry)
        return score_tile(jnp.minimum(2 * t + 1, n_kt - 1), carry)

    mx8, mn8 = lax.fori_loop(0, lax.div(n_kt + 1, 2), score_pair,
                             (jnp.full((SUBLANES, tq), NEG_BIG, F32),
                              jnp.full((SUBLANES, tq), -NEG_BIG, F32)))
    mx = jnp.max(mx8, axis=0, keepdims=True)
    mn = jnp.min(mn8, axis=0, keepdims=True)

    def key_pass(fn, op, init):
        def body(j, acc):
            c0 = pl.multiple_of(j * tk, tk)
            for u in range(tk // PASS_ROWS):
                s = sc_ref[pl.ds(c0 + u * PASS_ROWS, PASS_ROWS), :]
                acc = op(acc, fold(fn(s), op))
            return acc
        return lax.fori_loop(0, n_kt, body, jnp.full((SUBLANES, tq), init, F32))

    def count(pred):
        part = key_pass(lambda s: jnp.where(pred(s), 1.0, 0.0), jnp.add, 0.0)
        return jnp.sum(part, axis=0, keepdims=True)

    n_valid = (row0 + 1 + lax.broadcasted_iota(jnp.int32, (1, tq), 1)).astype(F32)
    take_all = n_valid <= kf
    cnt_top = count(lambda s: s >= mx)
    at_top = jnp.logical_and(jnp.logical_not(take_all), cnt_top >= kf)
    lo0 = jnp.where(take_all, NEG_BIG, jnp.where(at_top, mx, mn))
    done0 = jnp.logical_or(take_all, at_top)

    def bis_cond(st):
        it, lo, hi, c_lo, done_f = st
        return jnp.logical_and(it < BISECT_CAP, jnp.min(done_f) < 0.5)

    def halve(st):
        lo, hi, c_lo, done_f = st
        done = done_f > 0.5
        mid = lo + (hi - lo) * 0.5
        stuck = jnp.logical_or(mid <= lo, mid >= hi)
        cnt = count(lambda s: s >= mid)
        ge = cnt >= kf
        upd = jnp.logical_not(jnp.logical_or(done, stuck))
        raise_lo = jnp.logical_and(upd, ge)
        lo = jnp.where(raise_lo, mid, lo)
        c_lo = jnp.where(raise_lo, cnt, c_lo)
        hi = jnp.where(jnp.logical_and(upd, jnp.logical_not(ge)), mid, hi)
        done = jnp.logical_or(jnp.logical_or(done, stuck), jnp.logical_and(ge, cnt == kf))
        return lo, hi, c_lo, done.astype(F32)

    def snap(st):
        lo, hi, c_lo, done_f = st
        done = done_f > 0.5

        below = jnp.max(key_pass(lambda s: jnp.where(s < hi, s, NEG_BIG), jnp.maximum, NEG_BIG),
                        axis=0, keepdims=True)
        cnt = count(lambda s: s >= below)
        found = jnp.logical_and(jnp.logical_not(done), cnt >= kf)
        lower_hi = jnp.logical_and(jnp.logical_not(done), cnt < kf)
        lo = jnp.where(found, below, lo)
        c_lo = jnp.where(found, cnt, c_lo)
        hi = jnp.where(lower_hi, below, hi)
        return lo, hi, c_lo, jnp.logical_or(done, found).astype(F32)

    def bis_step(st):
        it, lo, hi, c_lo, done_f = st
        lo, hi, c_lo, done_f = lax.cond((it & 1) == 0, snap, halve, (lo, hi, c_lo, done_f))
        return it + 1, lo, hi, c_lo, done_f

    c_lo0 = jnp.where(at_top, cnt_top, n_valid)
    st = lax.fori_loop(0, SNAP_AFTER, lambda _, s: halve(s),
                       (lo0, mx, c_lo0, done0.astype(F32)))
    _, thr, _, c_thr, _ = lax.while_loop(bis_cond, bis_step, (jnp.int32(0),) + st)

    excess = jnp.logical_and(jnp.logical_not(take_all), c_thr > kf)

    @pl.when(jnp.max(excess.astype(F32)) > 0.5)
    def _():
        need = jnp.where(excess, kf - count(lambda s: s > thr), float(sc_ref.shape[0]))
        tril = (lax.broadcasted_iota(jnp.int32, (tk, tk), 1)
                <= lax.broadcasted_iota(jnp.int32, (tk, tk), 0)).astype(BF16)

        def drop_tile(j, seen):
            c0 = pl.multiple_of(j * tk, tk)
            s = sc_ref[pl.ds(c0, tk), :]
            eq = s == thr
            rank = seen + jnp.dot(tril, jnp.where(eq, 1.0, 0.0).astype(BF16),
                                  preferred_element_type=F32)
            sc_ref[pl.ds(c0, tk), :] = jnp.where(jnp.logical_and(eq, rank > need), NEG_BIG, s)
            return rank[tk - 1:tk, :]

        lax.fori_loop(0, n_kt, drop_tile, jnp.zeros((1, tq), F32))

    def logits(j, h, near):
        c0 = pl.multiple_of(j * tk, tk)
        g = h // HEADS_PER_GROUP
        s = jnp.dot(k_ref[0, g, pl.ds(c0, tk), :], qt_ref[0, h],
                    preferred_element_type=F32)
        if near:
            bias_row = pl.multiple_of(c0 - row0 + tk, tq)
            s = s + bias_ref[h, pl.ds(bias_row, tk), :]
        return s

    def selected(j, near):
        c0 = pl.multiple_of(j * tk, tk)
        keep = sc_ref[pl.ds(c0, tk), :] >= thr
        if near:
            keep = jnp.logical_and(keep, rel <= (row0 - c0))
        return keep

    def attend_exact(j, near):
        c0 = pl.multiple_of(j * tk, tk)
        mask_scr[...] = jnp.where(selected(j, near), 0.0, NEG_BIG)
        tile_max = []
        for h in range(N_ATTN_HEADS):
            s = logits(j, h, near) + mask_scr[...]
            s_scr[h] = s
            tile_max.append(jnp.max(fold(s, jnp.maximum), axis=0, keepdims=True))
        for h in range(N_ATTN_HEADS):
            g = h // HEADS_PER_GROUP
            m_old = m_ref[h]
            m_new = jnp.maximum(m_old, tile_max[h])
            alpha = jnp.exp2(m_old - m_new)
            p_scr[h] = jnp.exp2(s_scr[h] - m_new).astype(BF16)
            m_ref[h] = m_new
            pv = jnp.dot(vt_ref[0, g, :, pl.ds(c0, tk)], p_scr[h], preferred_element_type=F32)
            acc_ref[h] = acc_ref[h] * alpha + pv

    def attend_fast(j, near):
        c0 = pl.multiple_of(j * tk, tk)
        mask_scr[...] = jnp.where(selected(j, near), 0.0, NEG_BIG)
        for h in range(N_ATTN_HEADS):
            s = logits(j, h, near) + mask_scr[...]
            p_scr[h] = jnp.exp2(s - m_ref[h]).astype(BF16)
        for h in range(N_ATTN_HEADS):
            g = h // HEADS_PER_GROUP
            acc_ref[h] += jnp.dot(vt_ref[0, g, :, pl.ds(c0, tk)], p_scr[h],
                                  preferred_element_type=F32)

    def run_tiles(first_fast):
        acc_ref[...] = jnp.zeros(acc_ref.shape, F32)
        m_ref[...] = jnp.full(m_ref.shape, NEG_BIG, F32)

        def tile(j, near):
            if first_fast is None:
                attend_exact(j, near)
            else:
                lax.cond(j >= first_fast, lambda: attend_fast(j, near),
                         lambda: attend_exact(j, near))

        def far_step(j, c):
            tile(j, False)
            return c

        lax.fori_loop(0, j_near, far_step, 0)
        for u in range(2):
            @pl.when(j_near + u < n_kt)
            def _():
                tile(j_near + u, True)

    run_tiles(1)
    denom = acc_ref[:, ATTN_HEAD_DIM:ATTN_HEAD_DIM + 1, :]
    sane = jnp.logical_and(denom > 0.0, denom < MAX_SANE_DENOM)

    @pl.when(jnp.min(sane.astype(F32)) < 0.5)
    def _():
        run_tiles(None)

    outs = []
    for h in range(N_ATTN_HEADS):
        a = acc_ref[h]
        outs.append(a[:ATTN_HEAD_DIM, :] / a[ATTN_HEAD_DIM:ATTN_HEAD_DIM + 1, :])
    o_ref[0] = jnp.concatenate(outs, axis=0).T.astype(BF16)


def _dsa(qt, qit, wt, ki, k, vt, bias, *, tq, tk, k_sel):
    bsz, _, _, s = qt.shape
    assert tq >= MAX_DISTANCE and tk % tq == 0 and s % tk == 0
    body = functools.partial(_dsa_body, tq=tq, tk=tk, k_sel=k_sel)
    return pl.pallas_call(
        body,
        grid=(bsz, s // tq),
        in_specs=[pl.BlockSpec((1, N_IDX_HEADS, IDX_HEAD_DIM, tq), lambda b, i: (b, 0, 0, i)),
                  pl.BlockSpec((1, N_IDX_HEADS, tq), lambda b, i: (b, 0, i)),
                  pl.BlockSpec((1, N_ATTN_HEADS, ATTN_HEAD_DIM, tq), lambda b, i: (b, 0, 0, i)),
                  pl.BlockSpec((1, s, IDX_HEAD_DIM), lambda b, i: (b, 0, 0)),
                  pl.BlockSpec((1, N_KV_GROUPS, s, ATTN_HEAD_DIM), lambda b, i: (b, 0, 0, 0)),
                  pl.BlockSpec((1, N_KV_GROUPS, V_AUG_ROWS, s), lambda b, i: (b, 0, 0, 0)),
                  _const_spec(bias.shape)],
        out_specs=pl.BlockSpec((1, tq, ATTN_WIDTH), lambda b, i: (b, i, 0)),
        out_shape=jax.ShapeDtypeStruct((bsz, s, ATTN_WIDTH), BF16),
        scratch_shapes=[pltpu.VMEM((s, tq), F32),
                        pltpu.VMEM((N_ATTN_HEADS, V_AUG_ROWS, tq), F32),
                        pltpu.VMEM((N_ATTN_HEADS, 1, tq), F32),
                        pltpu.VMEM((N_ATTN_HEADS, tk, tq), F32),
                        pltpu.VMEM((N_ATTN_HEADS, tk, tq), BF16),
                        pltpu.VMEM((tk, tq), F32)],
        compiler_params=_params("parallel", "arbitrary"),
        name="dsa",
    )(qit, wt, qt, ki, k, vt, bias)


def _split3(a):
    hi = a.astype(BF16)
    r1 = a - hi.astype(F32)
    mid = r1.astype(BF16)
    lo = (r1 - mid.astype(F32)).astype(BF16)
    return hi, mid, lo


def _hgrn2_body(rq_ref, ri_ref, rg_ref, rf_ref, lb_ref, nw_ref, o_ref,
                state_scr, tril_scr, lvl_scr, *, chunk):
    t_len = rq_ref.shape[1]
    c = chunk
    d = REC_HEAD_DIM
    nw = nw_ref[...]
    eye = (lax.broadcasted_iota(jnp.int32, (d, d), 0)
           == lax.broadcasted_iota(jnp.int32, (d, d), 1)).astype(F32)
    sub = lax.broadcasted_iota(jnp.int32, (c, d), 0) & (REC_BAND - 1)
    level_sizes = []
    b = REC_BAND
    while b < c:
        level_sizes.append(b)
        b *= 2

    @pl.when(pl.program_id(1) == 0)
    def _():
        state_scr[...] = jnp.zeros(state_scr.shape, F32)
        ti = lax.broadcasted_iota(jnp.int32, (c, c), 0)
        si = lax.broadcasted_iota(jnp.int32, (c, c), 1)
        tril_scr[...] = (si <= ti).astype(BF16)
        for li, b in enumerate(level_sizes):
            blk = 2 * b
            sh = blk.bit_length() - 1
            m = jnp.logical_and(
                lax.shift_right_logical(ti, sh) == lax.shift_right_logical(si, sh),
                jnp.logical_and((ti & (blk - 1)) >= b, (si & (blk - 1)) < b))
            lvl_scr[li] = m.astype(F32)

    def head_step(hh, t0):
        cols = slice(hh * d, (hh + 1) * d)
        lb = lb_ref[:, cols]
        state = state_scr[hh]
        f = lb + (1.0 - lb) * _sigmoid_rel(rf_ref[0, pl.ds(t0, c), cols])
        g = jnp.log(f)
        kk = 1.0 - f
        qq = _silu(rq_ref[0, pl.ds(t0, c), cols].astype(F32))
        vb = ri_ref[0, pl.ds(t0, c), cols]
        vv = vb.astype(F32)

        g3 = _split3(g)
        tril = tril_scr[...]
        a = (jnp.dot(tril, g3[0], preferred_element_type=F32)
             + jnp.dot(tril, g3[1], preferred_element_type=F32)
             + jnp.dot(tril, g3[2], preferred_element_type=F32))
        a_last = a[c - 1:c, :]

        o = jnp.dot((qq * jnp.exp(a)).astype(BF16), state.astype(BF16),
                    preferred_element_type=F32)

        p = jnp.zeros((c, c), F32)
        for li, b in enumerate(level_sizes):
            blk = 2 * b
            r = jnp.broadcast_to(a.reshape(c // blk, blk, d)[:, b - 1:b, :],
                                 (c // blk, blk, d)).reshape(c, d)
            qt = (qq * jnp.exp(jnp.minimum(a - r, 0.0))).astype(BF16)
            kt = (kk * jnp.exp(jnp.minimum(r - a, 0.0))).astype(BF16)
            pb = lax.dot_general(qt, kt, (((1,), (1,)), ((), ())), preferred_element_type=F32)
            p = p + pb * lvl_scr[li]
        o = o + jnp.dot(p.astype(BF16), vb, preferred_element_type=F32)

        def back(x, dl):
            if dl == 0:
                return x
            return pltpu.roll(x.reshape(c // SUBLANES, SUBLANES, d), dl, axis=1).reshape(c, d)

        for dl in range(REC_BAND):
            a_s = back(a, dl)
            k_s = back(kk, dl)
            v_s = back(vv, dl)
            e = jnp.exp(jnp.where(sub >= dl, a - a_s, NEG_BIG))
            pd = jnp.sum(qq * k_s * e, axis=1, keepdims=True)
            o = o + pd * v_s

        kd = (kk * jnp.exp(a_last - a)).astype(BF16)
        upd = lax.dot_general(kd, vb, (((0,), (0,)), ((), ())), preferred_element_type=F32)
        e_col = jnp.sum(eye * jnp.exp(a_last), axis=1, keepdims=True)
        state_scr[hh] = e_col * state + upd

        y = _rms(o, nw) * _silu(rg_ref[0, pl.ds(t0, c), cols].astype(F32))
        o_ref[0, pl.ds(t0, c), cols] = y.astype(BF16)

    def step(n, carry):
        t0 = pl.multiple_of(n * c, c)
        for hh in range(N_REC_HEADS):
            head_step(hh, t0)
        return carry

    lax.fori_loop(0, t_len // c, step, 0)


def _hgrn2(rec3, rf, lb, nw, *, chunk):
    bsz, s, _ = rec3.shape
    d = REC_HEAD_DIM
    h = N_REC_HEADS
    body = functools.partial(_hgrn2_body, chunk=chunk)
    tt = _pick_tile(s, REC_TIME_TILE)
    col = lambda part: pl.BlockSpec((1, tt, REC_WIDTH), lambda b, t: (b, t, part))
    return pl.pallas_call(
        body,
        grid=(bsz, s // tt),
        in_specs=[col(0), col(1), col(2), col(0),
                  pl.BlockSpec((1, REC_WIDTH), lambda b, t: (0, 0)),
                  pl.BlockSpec((1, d), lambda b, t: (0, 0))],
        out_specs=col(0),
        out_shape=jax.ShapeDtypeStruct((bsz, s, REC_WIDTH), BF16),
        scratch_shapes=[pltpu.VMEM((h, d, d), F32)]
        + [pltpu.VMEM((chunk, chunk), BF16),
           pltpu.VMEM(((chunk // REC_BAND).bit_length() - 1, chunk, chunk), F32)],
        compiler_params=_params("parallel", "arbitrary"),
        name="hgrn2",
    )(rec3, rec3, rec3, rf, lb.reshape(1, REC_WIDTH), nw.reshape(1, d))


def _out_ffn_body(x_ref, attn_ref, rec_ref, gates_ref, mod_ref, wba_ref, wbr_ref, wo_ref,
                  n2_ref, wg_ref, wu_ref, wd_ref, fn_ref, o_ref, *, final):
    d = x_ref.shape[2]
    x = x_ref[0]
    g1 = mod_ref[0, 2:3, :]
    sh2 = mod_ref[0, 3:4, :]
    sc2 = mod_ref[0, 4:5, :]
    g2 = mod_ref[0, 5:6, :]
    ya = jnp.dot(attn_ref[0], wba_ref[...], preferred_element_type=F32)
    yr = jnp.dot(rec_ref[0], wbr_ref[...], preferred_element_type=F32)
    gates = gates_ref[0].astype(F32)
    mix = (_sigmoid(gates[:, :d]) * ya + _sigmoid(gates[:, d:]) * yr).astype(BF16)
    x1 = x + g1 * jnp.dot(mix, wo_ref[...], preferred_element_type=F32)
    h2 = (_rms(x1, n2_ref[...]) * (1.0 + sc2) + sh2).astype(BF16)
    gate = jnp.dot(h2, wg_ref[...], preferred_element_type=F32)
    up = jnp.dot(h2, wu_ref[...], preferred_element_type=F32)
    act = (_silu(gate) * up).astype(BF16)
    x2 = x1 + g2 * jnp.dot(act, wd_ref[...], preferred_element_type=F32)
    o_ref[0] = _rms(x2, fn_ref[...]) if final else x2


def _out_ffn(x, attn, rec, gates, mod, wba, wbr, wo, n2w, wg, wu, wd, fnw, tm, final):
    bsz, s, d = x.shape
    hid = wg.shape[1]
    tok = lambda w: pl.BlockSpec((1, tm, w), lambda b, i: (b, i, 0))
    return pl.pallas_call(
        functools.partial(_out_ffn_body, final=final),
        grid=(bsz, s // tm),
        in_specs=[tok(d), tok(ATTN_WIDTH), tok(REC_WIDTH), tok(2 * d),
                  pl.BlockSpec((1, 6, d), lambda b, i: (b, 0, 0)),
                  _const_spec((ATTN_WIDTH, d)), _const_spec((REC_WIDTH, d)), _const_spec((d, d)),
                  _const_spec((1, d)), _const_spec((d, hid)), _const_spec((d, hid)),
                  _const_spec((hid, d)), _const_spec((1, d))],
        out_specs=tok(d),
        out_shape=jax.ShapeDtypeStruct((bsz, s, d), F32),
        compiler_params=_params("parallel", "parallel"),
        name="out_ffn",
    )(x, attn, rec, gates, mod, wba, wbr, wo, n2w, wg, wu, wd, fnw)


def _pick_tile(n, want):
    t = min(want, n)
    while n % t:
        t //= 2
    return t


def _dsa_inputs(kv, kiw):
    bsz, s, _ = kv.shape
    ki = kiw[:, :, :IDX_HEAD_DIM].astype(BF16)
    wt = kiw[:, :, IDX_HEAD_DIM:IDX_HEAD_DIM + N_IDX_HEADS].transpose(0, 2, 1)
    k = kv[:, :, :KV_WIDTH].reshape(bsz, s, N_KV_GROUPS, ATTN_HEAD_DIM).transpose(0, 2, 1, 3)
    vt = kv[:, :, KV_WIDTH:].reshape(bsz, s, N_KV_GROUPS, ATTN_HEAD_DIM).transpose(0, 2, 3, 1)
    ones = jnp.ones((bsz, N_KV_GROUPS, V_AUG_ROWS - ATTN_HEAD_DIM, s), BF16)
    return ki, wt, k, jnp.concatenate([vt, ones], axis=2)


def kernel(x, c, w_ada, b_ada, norm1_w, w_in, q_norm_w, w_uq, w_uq_idx, rel_bias, lb_logits,
           rec_norm_w, w_branch_attn, w_branch_rec, w_out, norm2_w, w_ffn_gate, w_ffn_up,
           w_ffn_down, final_norm_w):
    bsz, s, d = x.shape
    depth = w_in.shape[0]
    k_sel = min(TOPK_MAX, s // 4)
    tm = _pick_tile(s, TOKEN_TILE)
    tq = _pick_tile(s, Q_TILE)
    tk = max(tq, _pick_tile(s, K_TILE))
    chunk = _pick_tile(s, REC_CHUNK)
    lower_bounds = jnp.cumsum(jax.nn.softmax(lb_logits.astype(F32), axis=0), axis=0)
    bias = _near_bias(rel_bias, tq, tk)

    o_cq = 0
    o_k = o_cq + Q_LORA_RANK
    o_v = o_k + KV_WIDTH
    o_ki = o_v + KV_WIDTH
    o_wi = o_ki + IDX_HEAD_DIM
    o_rq = o_wi + N_IDX_HEADS
    o_rf = o_rq + REC_WIDTH
    o_ri = o_rf + REC_WIDTH
    o_rg = o_ri + REC_WIDTH
    o_gt = o_rg + REC_WIDTH

    for layer in range(depth):
        mod = _ada(c, w_ada[layer], b_ada[layer]).reshape(bsz, 6, d)
        wl = w_in[layer]
        pad_a = LANES - IDX_HEAD_DIM - N_IDX_HEADS
        wa = jnp.concatenate([wl[:, o_cq:o_rq], jnp.zeros((d, pad_a), wl.dtype)], axis=1).astype(BF16)
        wr = jnp.concatenate([wl[:, o_rq:o_rf], wl[:, o_ri:o_rg], wl[:, o_rg:o_gt],
                              wl[:, o_rf:o_ri]], axis=1).astype(BF16)
        wg = wl[:, o_gt:].astype(BF16)
        wuq = (w_uq[layer] * (ATTN_HEAD_DIM ** -0.5 * LOG2_E)).astype(BF16).T
        wui = (w_uq_idx[layer] * (IDX_HEAD_DIM ** -0.5)).astype(BF16).T

        qt, qit, kv, kiw, rec3, rf, gates = _in_proj(
            x, mod, norm1_w[layer].reshape(1, d), wa, wr, wg,
            q_norm_w[layer].reshape(1, Q_LORA_RANK), wuq, wui, tm)

        ki, wt, k, vt = _dsa_inputs(kv, kiw)
        attn = _dsa(qt, qit, wt, ki, k, vt, bias, tq=tq, tk=tk, k_sel=k_sel)
        rec = _hgrn2(rec3, rf, lower_bounds[layer], rec_norm_w[layer], chunk=chunk)

        x = _out_ffn(x, attn, rec, gates, mod,
                     w_branch_attn[layer].astype(BF16), w_branch_rec[layer].astype(BF16),
                     w_out[layer].astype(BF16), norm2_w[layer].reshape(1, d),
                     w_ffn_gate[layer].astype(BF16), w_ffn_up[layer].astype(BF16),
                     w_ffn_down[layer].astype(BF16),
                     final_norm_w.reshape(1, d), tm, layer == depth - 1)
    return x
```

```python
import functools
import math

import jax
import jax.numpy as jnp
from jax import lax
from jax.experimental import pallas as pl
from jax.experimental.pallas import tpu as pltpu

F32 = jnp.float32
BF16 = jnp.bfloat16

N_ATTN_HEADS = 8
ATTN_HEAD_DIM = 64
N_KV_GROUPS = 2
HEADS_PER_GROUP = N_ATTN_HEADS // N_KV_GROUPS
Q_LORA_RANK = 256
N_IDX_HEADS = 8
IDX_HEAD_DIM = 64
TOPK_MAX = 256
N_BUCKETS = 32
MAX_DISTANCE = 128
N_REC_HEADS = 4
REC_HEAD_DIM = 128
EPS = 1e-6

ATTN_WIDTH = N_ATTN_HEADS * ATTN_HEAD_DIM
KV_WIDTH = N_KV_GROUPS * ATTN_HEAD_DIM
REC_WIDTH = N_REC_HEADS * REC_HEAD_DIM

LANES = 128
SUBLANES = 8
BF16_SUBLANES = 16
VMEM_LIMIT_BYTES = 56 * 1024 * 1024

ADA_TILE = 1024
TOKEN_TILE = 512
Q_TILE = 256
K_TILE = 512
REC_CHUNK = 128
REC_TIME_TILE = 1024
REC_BAND = 4
NEG_BIG = -(2.0 ** 100)
BISECT_CAP = 700
SNAP_AFTER = 14
FOLD_CHAINS = 4
PASS_ROWS = 128
LOG2_E = math.log2(math.e)
MAX_SANE_DENOM = 1e30
V_AUG_ROWS = ATTN_HEAD_DIM + BF16_SUBLANES
K_AUG_ROWS = ATTN_HEAD_DIM + BF16_SUBLANES


def _rms(x, w):
    return x * lax.rsqrt(jnp.mean(x * x, axis=-1, keepdims=True) + EPS) * w


def _sigmoid(x):
    return 0.5 * jnp.tanh(0.5 * x) + 0.5


def _sigmoid_rel(x):
    return 1.0 / (1.0 + jnp.exp(-x))


def _silu(x):
    return x * _sigmoid(x)


def _params(*sem):
    return pltpu.CompilerParams(dimension_semantics=sem, vmem_limit_bytes=VMEM_LIMIT_BYTES)


def _const_spec(shape):
    nd = len(shape)
    return pl.BlockSpec(shape, lambda *_: (0,) * nd, pipeline_mode=pl.Buffered(1))


def _ada_body(c_ref, w_ref, b_ref, o_ref):
    ca = _silu(c_ref[...])
    o_ref[...] = jnp.dot(ca, w_ref[...], precision=lax.Precision.HIGHEST,
                         preferred_element_type=F32) + b_ref[...]


def _ada(c, w, b):
    bsz, d = c.shape
    n = w.shape[1]
    tn = _pick_tile(n, ADA_TILE)
    return pl.pallas_call(
        _ada_body,
        grid=(n // tn,),
        in_specs=[pl.BlockSpec((bsz, d), lambda j: (0, 0)),
                  pl.BlockSpec((d, tn), lambda j: (0, j)),
                  pl.BlockSpec((1, tn), lambda j: (0, j))],
        out_specs=pl.BlockSpec((bsz, tn), lambda j: (0, j)),
        out_shape=jax.ShapeDtypeStruct((bsz, n), F32),
        compiler_params=_params("arbitrary"),
        name="ada",
    )(c, w, b.reshape(1, n))


def _in_proj_body(x_ref, mod_ref, n1_ref, wa_ref, wr_ref, wg_ref, qn_ref, wuq_ref, wui_ref,
                  qt_ref, qit_ref, kv_ref, kiw_ref, rec_ref, rf_ref, gates_ref):
    tm = x_ref.shape[1]
    x = x_ref[0]
    sh1 = mod_ref[0, 0:1, :]
    sc1 = mod_ref[0, 1:2, :]
    h = (_rms(x, n1_ref[...]) * (1.0 + sc1) + sh1).astype(BF16)

    pa = jnp.dot(h, wa_ref[...], preferred_element_type=F32)
    kv_ref[0] = pa[:, Q_LORA_RANK:Q_LORA_RANK + 2 * KV_WIDTH].astype(BF16)
    kiw_ref[0] = pa[:, Q_LORA_RANK + 2 * KV_WIDTH:]
    cqn = _rms(pa[:, :Q_LORA_RANK], qn_ref[...]).astype(BF16)
    nt = (((1,), (1,)), ((), ()))
    qf = lax.dot_general(wuq_ref[...], cqn, nt, preferred_element_type=F32)
    qt_ref[0] = qf.reshape(N_ATTN_HEADS, ATTN_HEAD_DIM, tm).astype(BF16)
    qif = lax.dot_general(wui_ref[...], cqn, nt, preferred_element_type=F32)
    qit_ref[0] = qif.reshape(N_IDX_HEADS, IDX_HEAD_DIM, tm).astype(BF16)

    pr = jnp.dot(h, wr_ref[...], preferred_element_type=F32)
    rec_ref[0] = pr[:, :3 * REC_WIDTH].astype(BF16)
    rf_ref[0] = pr[:, 3 * REC_WIDTH:]
    gates_ref[0] = jnp.dot(h, wg_ref[...], preferred_element_type=F32).astype(BF16)


def _in_proj(x, mod, n1w, wa, wr, wg, qnw, wuq, wui, tm):
    bsz, s, d = x.shape
    wa_n = wa.shape[1]
    grid = (bsz, s // tm)
    tok = lambda w: pl.BlockSpec((1, tm, w), lambda b, i: (b, i, 0))
    head = lambda nh, hd: pl.BlockSpec((1, nh, hd, tm), lambda b, i: (b, 0, 0, i))
    out_shape = (
        jax.ShapeDtypeStruct((bsz, N_ATTN_HEADS, ATTN_HEAD_DIM, s), BF16),
        jax.ShapeDtypeStruct((bsz, N_IDX_HEADS, IDX_HEAD_DIM, s), BF16),
        jax.ShapeDtypeStruct((bsz, s, 2 * KV_WIDTH), BF16),
        jax.ShapeDtypeStruct((bsz, s, LANES), F32),
        jax.ShapeDtypeStruct((bsz, s, 3 * REC_WIDTH), BF16),
        jax.ShapeDtypeStruct((bsz, s, REC_WIDTH), F32),
        jax.ShapeDtypeStruct((bsz, s, 2 * d), BF16),
    )
    out_specs = (head(N_ATTN_HEADS, ATTN_HEAD_DIM), head(N_IDX_HEADS, IDX_HEAD_DIM),
                 tok(2 * KV_WIDTH), tok(LANES), tok(3 * REC_WIDTH), tok(REC_WIDTH), tok(2 * d))
    return pl.pallas_call(
        _in_proj_body,
        grid=grid,
        in_specs=[tok(d),
                  pl.BlockSpec((1, 6, d), lambda b, i: (b, 0, 0)),
                  _const_spec((1, d)),
                  _const_spec((d, wa_n)),
                  _const_spec((d, 4 * REC_WIDTH)),
                  _const_spec((d, 2 * d)),
                  _const_spec((1, Q_LORA_RANK)),
                  _const_spec((ATTN_WIDTH, Q_LORA_RANK)),
                  _const_spec((N_IDX_HEADS * IDX_HEAD_DIM, Q_LORA_RANK))],
        out_specs=out_specs,
        out_shape=out_shape,
        compiler_params=_params("parallel", "parallel"),
        name="in_proj",
    )(x, mod, n1w, wa, wr, wg, qnw, wuq, wui)


def _near_bias(rel_bias, tq, tk):
    n = jnp.arange(MAX_DISTANCE + 1, dtype=jnp.int32)
    max_exact = N_BUCKETS // 2
    nf = jnp.maximum(n, 1).astype(F32)
    large = max_exact + (jnp.log(nf / max_exact) / math.log(MAX_DISTANCE / max_exact)
                         * (N_BUCKETS - max_exact)).astype(jnp.int32)
    bucket = jnp.where(n < max_exact, n, jnp.minimum(large, N_BUCKETS - 1))
    bucket = bucket.at[MAX_DISTANCE].set(N_BUCKETS - 1)
    ids = jnp.arange(N_BUCKETS, dtype=jnp.int32)
    start = jnp.sum((bucket[None, :] < ids[:, None]).astype(jnp.int32), axis=1)
    rb = (rel_bias.astype(F32) - rel_bias[N_BUCKETS - 1].astype(F32)[None, :]) * LOG2_E
    step = rb - jnp.concatenate([jnp.zeros_like(rb[:1]), rb[:-1]], axis=0)
    dist = (jnp.arange(tq, dtype=jnp.int32)[None, :]
            - jnp.arange(2 * tk, dtype=jnp.int32)[:, None] + tk)
    reached = (dist[None, :, :] >= start[:, None, None]).astype(F32)
    return jnp.einsum("bh,but->hut", step, reached, precision=lax.Precision.HIGHEST)


def _dsa_body(qit_ref, w_ref, qt_ref, ki_ref, k_ref, vt_ref, bias_ref, o_ref,
              sc_ref, acc_ref, m_ref, s_scr, p_scr, mask_scr, qa_scr, *, tq, tk, k_sel):
    i = pl.program_id(1)
    row0 = i * tq
    n_kt = lax.div(row0 + tq + (tk - 1), tk)
    j_near = lax.div(jnp.maximum(row0 - (MAX_DISTANCE - 1), 0), tk)
    krow = lax.broadcasted_iota(jnp.int32, (tk, tq), 0)
    qcol = lax.broadcasted_iota(jnp.int32, (tk, tq), 1)
    rel = krow - qcol
    kf = float(k_sel)

    def fold(a, op, group=SUBLANES):
        n = a.shape[0] // group
        a = a.reshape(n // FOLD_CHAINS, FOLD_CHAINS, group, tq)
        r = a[0]
        for t in range(1, n // FOLD_CHAINS):
            r = op(r, a[t])
        while r.shape[0] > 1:
            half = r.shape[0] // 2
            r = op(r[:half], r[half:])
        return r[0]

    w = w_ref[0] * (N_IDX_HEADS ** -0.5)

    def score_tile(j, carry):
        mx, mn = carry
        c0 = pl.multiple_of(j * tk, tk)
        kin = ki_ref[0, pl.ds(c0, tk), :]
        sc = None
        for h in range(N_IDX_HEADS):
            lg = jnp.dot(kin, qit_ref[0, h], preferred_element_type=F32)
            term = jnp.maximum(lg, 0.0) * w[h:h + 1, :]
            sc = term if sc is None else sc + term
        causal = rel <= (row0 - c0)
        scm = jnp.where(causal, sc, NEG_BIG)
        sc_ref[pl.ds(c0, tk), :] = scm
        mx = jnp.maximum(mx, fold(scm, jnp.maximum))
        mn = jnp.minimum(mn, fold(jnp.where(causal, sc, -NEG_BIG), jnp.minimum))
        return mx, mn

    def score_pair(t, carry):
        carry = score_tile(2 * t, carry)
        return score_tile(jnp.minimum(2 * t + 1, n_kt - 1), carry)

    mx8, mn8 = lax.fori_loop(0, lax.div(n_kt + 1, 2), score_pair,
                             (jnp.full((SUBLANES, tq), NEG_BIG, F32),
                              jnp.full((SUBLANES, tq), -NEG_BIG, F32)))
    mx = jnp.max(mx8, axis=0, keepdims=True)
    mn = jnp.min(mn8, axis=0, keepdims=True)

    def key_pass(fn, op, init):
        def body(j, acc):
            c0 = pl.multiple_of(j * tk, tk)
            for u in range(tk // PASS_ROWS):
                s = sc_ref[pl.ds(c0 + u * PASS_ROWS, PASS_ROWS), :]
                acc = op(acc, fold(fn(s), op))
            return acc
        return lax.fori_loop(0, n_kt, body, jnp.full((SUBLANES, tq), init, F32))

    def count(pred):
        part = key_pass(lambda s: jnp.where(pred(s), 1.0, 0.0), jnp.add, 0.0)
        return jnp.sum(part, axis=0, keepdims=True)

    n_valid = (row0 + 1 + lax.broadcasted_iota(jnp.int32, (1, tq), 1)).astype(F32)
    take_all = n_valid <= kf
    cnt_top = count(lambda s: s >= mx)
    at_top = jnp.logical_and(jnp.logical_not(take_all), cnt_top >= kf)
    lo0 = jnp.where(take_all, NEG_BIG, jnp.where(at_top, mx, mn))
    done0 = jnp.logical_or(take_all, at_top)

    def bis_cond(st):
        it, lo, hi, c_lo, done_f = st
        return jnp.logical_and(it < BISECT_CAP, jnp.min(done_f) < 0.5)

    def halve(st):
        lo, hi, c_lo, done_f = st
        done = done_f > 0.5
        mid = lo + (hi - lo) * 0.5
        stuck = jnp.logical_or(mid <= lo, mid >= hi)
        cnt = count(lambda s: s >= mid)
        ge = cnt >= kf
        upd = jnp.logical_not(jnp.logical_or(done, stuck))
        raise_lo = jnp.logical_and(upd, ge)
        lo = jnp.where(raise_lo, mid, lo)
        c_lo = jnp.where(raise_lo, cnt, c_lo)
        hi = jnp.where(jnp.logical_and(upd, jnp.logical_not(ge)), mid, hi)
        done = jnp.logical_or(jnp.logical_or(done, stuck), jnp.logical_and(ge, cnt == kf))
        return lo, hi, c_lo, done.astype(F32)

    def snap(st):
        lo, hi, c_lo, done_f = st
        done = done_f > 0.5

        below = jnp.max(key_pass(lambda s: jnp.where(s < hi, s, NEG_BIG), jnp.maximum, NEG_BIG),
                        axis=0, keepdims=True)
        cnt = count(lambda s: s >= below)
        found = jnp.logical_and(jnp.logical_not(done), cnt >= kf)
        lower_hi = jnp.logical_and(jnp.logical_not(done), cnt < kf)
        lo = jnp.where(found, below, lo)
        c_lo = jnp.where(found, cnt, c_lo)
        hi = jnp.where(lower_hi, below, hi)
        return lo, hi, c_lo, jnp.logical_or(done, found).astype(F32)

    def bis_step(st):
        it, lo, hi, c_lo, done_f = st
        lo, hi, c_lo, done_f = lax.cond((it & 1) == 0, snap, halve, (lo, hi, c_lo, done_f))
        return it + 1, lo, hi, c_lo, done_f

    c_lo0 = jnp.where(at_top, cnt_top, n_valid)
    st = lax.fori_loop(0, SNAP_AFTER, lambda _, s: halve(s),
                       (lo0, mx, c_lo0, done0.astype(F32)))
    _, thr, _, c_thr, _ = lax.while_loop(bis_cond, bis_step, (jnp.int32(0),) + st)

    excess = jnp.logical_and(jnp.logical_not(take_all), c_thr > kf)

    @pl.when(jnp.max(excess.astype(F32)) > 0.5)
    def _():
        need = jnp.where(excess, kf - count(lambda s: s > thr), float(sc_ref.shape[0]))
        tril = (lax.broadcasted_iota(jnp.int32, (tk, tk), 1)
                <= lax.broadcasted_iota(jnp.int32, (tk, tk), 0)).astype(BF16)

        def drop_tile(j, seen):
            c0 = pl.multiple_of(j * tk, tk)
            s = sc_ref[pl.ds(c0, tk), :]
            eq = s == thr
            rank = seen + jnp.dot(tril, jnp.where(eq, 1.0, 0.0).astype(BF16),
                                  preferred_element_type=F32)
            sc_ref[pl.ds(c0, tk), :] = jnp.where(jnp.logical_and(eq, rank > need), NEG_BIG, s)
            return rank[tk - 1:tk, :]

        lax.fori_loop(0, n_kt, drop_tile, jnp.zeros((1, tq), F32))

    def logits(j, h, near):
        c0 = pl.multiple_of(j * tk, tk)
        g = h // HEADS_PER_GROUP
        s = jnp.dot(k_ref[0, g, pl.ds(c0, tk), :], qa_scr[h],
                    preferred_element_type=F32)
        if near:
            bias_row = pl.multiple_of(c0 - row0 + tk, tq)
            s = s + bias_ref[h, pl.ds(bias_row, tk), :]
        return s

    def selected(j, near):
        c0 = pl.multiple_of(j * tk, tk)
        keep = sc_ref[pl.ds(c0, tk), :] >= thr
        if near:
            keep = jnp.logical_and(keep, rel <= (row0 - c0))
        return keep

    def attend_exact(j, near):
        c0 = pl.multiple_of(j * tk, tk)
        mask_scr[...] = jnp.where(selected(j, near), 0.0, NEG_BIG)
        tile_max = []
        for h in range(N_ATTN_HEADS):
            s = logits(j, h, near) + mask_scr[...]
            s_scr[h] = s
            tile_max.append(jnp.max(fold(s, jnp.maximum), axis=0, keepdims=True))
        for h in range(N_ATTN_HEADS):
            g = h // HEADS_PER_GROUP
            m_old = m_ref[h]
            m_new = jnp.maximum(m_old, tile_max[h]).astype(BF16).astype(F32)
            alpha = jnp.exp2(m_old - m_new)
            p_scr[h] = jnp.exp2(s_scr[h] - m_new).astype(BF16)
            m_ref[h] = m_new
            pv = jnp.dot(vt_ref[0, g, :, pl.ds(c0, tk)], p_scr[h], preferred_element_type=F32)
            acc_ref[h] = acc_ref[h] * alpha + pv

    def attend_fast(j, near):
        c0 = pl.multiple_of(j * tk, tk)
        mask_scr[...] = jnp.where(selected(j, near), 0.0, NEG_BIG)
        for h in range(N_ATTN_HEADS):
            p_scr[h] = jnp.exp2(logits(j, h, near) + mask_scr[...]).astype(BF16)
        for h in range(N_ATTN_HEADS):
            g = h // HEADS_PER_GROUP
            acc_ref[h] += jnp.dot(vt_ref[0, g, :, pl.ds(c0, tk)], p_scr[h],
                                  preferred_element_type=F32)

    def run_tiles(first_fast):
        acc_ref[...] = jnp.zeros(acc_ref.shape, F32)
        m_ref[...] = jnp.full(m_ref.shape, NEG_BIG, F32)
        qa_scr[:, :ATTN_HEAD_DIM, :] = qt_ref[0]
        qa_scr[:, ATTN_HEAD_DIM:, :] = jnp.zeros((N_ATTN_HEADS, K_AUG_ROWS - ATTN_HEAD_DIM, tq), BF16)

        def exact_then_fold(j, near):
            attend_exact(j, near)
            shape = (N_ATTN_HEADS, K_AUG_ROWS - ATTN_HEAD_DIM, tq)
            first_row = lax.broadcasted_iota(jnp.int32, shape, 1) == 0
            qa_scr[:, ATTN_HEAD_DIM:, :] = jnp.where(first_row, -m_ref[...], 0.0).astype(BF16)

        def tile(j, near):
            if first_fast is None:
                attend_exact(j, near)
            else:
                lax.cond(j >= first_fast, lambda: attend_fast(j, near),
                         lambda: exact_then_fold(j, near))

        def far_step(j, c):
            tile(j, False)
            return c

        lax.fori_loop(0, j_near, far_step, 0)
        for u in range(2):
            @pl.when(j_near + u < n_kt)
            def _():
                tile(j_near + u, True)

    run_tiles(1)
    denom = acc_ref[:, ATTN_HEAD_DIM:ATTN_HEAD_DIM + 1, :]
    sane = jnp.logical_and(denom > 0.0, denom < MAX_SANE_DENOM)

    @pl.when(jnp.min(sane.astype(F32)) < 0.5)
    def _():
        run_tiles(None)

    outs = []
    for h in range(N_ATTN_HEADS):
        a = acc_ref[h]
        outs.append(a[:ATTN_HEAD_DIM, :] / a[ATTN_HEAD_DIM:ATTN_HEAD_DIM + 1, :])
    o_ref[0] = jnp.concatenate(outs, axis=0).T.astype(BF16)


def _dsa(qt, qit, wt, ki, k, vt, bias, *, tq, tk, k_sel):
    bsz, _, _, s = qt.shape
    assert tq >= MAX_DISTANCE and tk % tq == 0 and s % tk == 0
    body = functools.partial(_dsa_body, tq=tq, tk=tk, k_sel=k_sel)
    return pl.pallas_call(
        body,
        grid=(bsz, s // tq),
        in_specs=[pl.BlockSpec((1, N_IDX_HEADS, IDX_HEAD_DIM, tq), lambda b, i: (b, 0, 0, i)),
                  pl.BlockSpec((1, N_IDX_HEADS, tq), lambda b, i: (b, 0, i)),
                  pl.BlockSpec((1, N_ATTN_HEADS, ATTN_HEAD_DIM, tq), lambda b, i: (b, 0, 0, i)),
                  pl.BlockSpec((1, s, IDX_HEAD_DIM), lambda b, i: (b, 0, 0)),
                  pl.BlockSpec((1, N_KV_GROUPS, s, K_AUG_ROWS), lambda b, i: (b, 0, 0, 0)),
                  pl.BlockSpec((1, N_KV_GROUPS, V_AUG_ROWS, s), lambda b, i: (b, 0, 0, 0)),
                  _const_spec(bias.shape)],
        out_specs=pl.BlockSpec((1, tq, ATTN_WIDTH), lambda b, i: (b, i, 0)),
        out_shape=jax.ShapeDtypeStruct((bsz, s, ATTN_WIDTH), BF16),
        scratch_shapes=[pltpu.VMEM((s, tq), F32),
                        pltpu.VMEM((N_ATTN_HEADS, V_AUG_ROWS, tq), F32),
                        pltpu.VMEM((N_ATTN_HEADS, 1, tq), F32),
                        pltpu.VMEM((N_ATTN_HEADS, tk, tq), F32),
                        pltpu.VMEM((N_ATTN_HEADS, tk, tq), BF16),
                        pltpu.VMEM((tk, tq), F32),
                        pltpu.VMEM((N_ATTN_HEADS, K_AUG_ROWS, tq), BF16)],
        compiler_params=_params("parallel", "arbitrary"),
        name="dsa",
    )(qit, wt, qt, ki, k, vt, bias)


def _split3(a):
    hi = a.astype(BF16)
    r1 = a - hi.astype(F32)
    mid = r1.astype(BF16)
    lo = (r1 - mid.astype(F32)).astype(BF16)
    return hi, mid, lo


def _hgrn2_body(rq_ref, ri_ref, rg_ref, rf_ref, lb_ref, nw_ref, o_ref,
                state_scr, tril_scr, lvl_scr, *, chunk):
    t_len = rq_ref.shape[1]
    c = chunk
    d = REC_HEAD_DIM
    nw = nw_ref[...]
    eye = (lax.broadcasted_iota(jnp.int32, (d, d), 0)
           == lax.broadcasted_iota(jnp.int32, (d, d), 1)).astype(F32)
    sub = lax.broadcasted_iota(jnp.int32, (c, d), 0) & (REC_BAND - 1)
    level_sizes = []
    b = REC_BAND
    while b < c:
        level_sizes.append(b)
        b *= 2

    @pl.when(pl.program_id(1) == 0)
    def _():
        state_scr[...] = jnp.zeros(state_scr.shape, F32)
        ti = lax.broadcasted_iota(jnp.int32, (c, c), 0)
        si = lax.broadcasted_iota(jnp.int32, (c, c), 1)
        tril_scr[...] = (si <= ti).astype(BF16)
        for li, b in enumerate(level_sizes):
            blk = 2 * b
            sh = blk.bit_length() - 1
            m = jnp.logical_and(
                lax.shift_right_logical(ti, sh) == lax.shift_right_logical(si, sh),
                jnp.logical_and((ti & (blk - 1)) >= b, (si & (blk - 1)) < b))
            lvl_scr[li] = m.astype(F32)

    def head_step(hh, t0):
        cols = slice(hh * d, (hh + 1) * d)
        lb = lb_ref[:, cols]
        state = state_scr[hh]
        f = lb + (1.0 - lb) * _sigmoid_rel(rf_ref[0, pl.ds(t0, c), cols])
        g = jnp.log(f)
        kk = 1.0 - f
        qq = _silu(rq_ref[0, pl.ds(t0, c), cols].astype(F32))
        vb = ri_ref[0, pl.ds(t0, c), cols]
        vv = vb.astype(F32)

        g3 = _split3(g)
        tril = tril_scr[...]
        a = (jnp.dot(tril, g3[0], preferred_element_type=F32)
             + jnp.dot(tril, g3[1], preferred_element_type=F32)
             + jnp.dot(tril, g3[2], preferred_element_type=F32))
        a_last = a[c - 1:c, :]

        o = jnp.dot((qq * jnp.exp(a)).astype(BF16), state.astype(BF16),
                    preferred_element_type=F32)

        p = jnp.zeros((c, c), F32)
        for li, b in enumerate(level_sizes):
            blk = 2 * b
            r = jnp.broadcast_to(a.reshape(c // blk, blk, d)[:, b - 1:b, :],
                                 (c // blk, blk, d)).reshape(c, d)
            qt = (qq * jnp.exp(jnp.minimum(a - r, 0.0))).astype(BF16)
            kt = (kk * jnp.exp(jnp.minimum(r - a, 0.0))).astype(BF16)
            pb = lax.dot_general(qt, kt, (((1,), (1,)), ((), ())), preferred_element_type=F32)
            p = p + pb * lvl_scr[li]
        o = o + jnp.dot(p.astype(BF16), vb, preferred_element_type=F32)

        def back(x, dl):
            if dl == 0:
                return x
            return pltpu.roll(x.reshape(c // SUBLANES, SUBLANES, d), dl, axis=1).reshape(c, d)

        for dl in range(REC_BAND):
            a_s = back(a, dl)
            k_s = back(kk, dl)
            v_s = back(vv, dl)
            e = jnp.exp(jnp.where(sub >= dl, a - a_s, NEG_BIG))
            pd = jnp.sum(qq * k_s * e, axis=1, keepdims=True)
            o = o + pd * v_s

        kd = (kk * jnp.exp(a_last - a)).astype(BF16)
        upd = lax.dot_general(kd, vb, (((0,), (0,)), ((), ())), preferred_element_type=F32)
        e_col = jnp.sum(eye * jnp.exp(a_last), axis=1, keepdims=True)
        state_scr[hh] = e_col * state + upd

        y = _rms(o, nw) * _silu(rg_ref[0, pl.ds(t0, c), cols].astype(F32))
        o_ref[0, pl.ds(t0, c), cols] = y.astype(BF16)

    def step(n, carry):
        t0 = pl.multiple_of(n * c, c)
        for hh in range(N_REC_HEADS):
            head_step(hh, t0)
        return carry

    lax.fori_loop(0, t_len // c, step, 0)


def _hgrn2(rec3, rf, lb, nw, *, chunk):
    bsz, s, _ = rec3.shape
    d = REC_HEAD_DIM
    h = N_REC_HEADS
    body = functools.partial(_hgrn2_body, chunk=chunk)
    tt = _pick_tile(s, REC_TIME_TILE)
    col = lambda part: pl.BlockSpec((1, tt, REC_WIDTH), lambda b, t: (b, t, part))
    return pl.pallas_call(
        body,
        grid=(bsz, s // tt),
        in_specs=[col(0), col(1), col(2), col(0),
                  pl.BlockSpec((1, REC_WIDTH), lambda b, t: (0, 0)),
                  pl.BlockSpec((1, d), lambda b, t: (0, 0))],
        out_specs=col(0),
        out_shape=jax.ShapeDtypeStruct((bsz, s, REC_WIDTH), BF16),
        scratch_shapes=[pltpu.VMEM((h, d, d), F32)]
        + [pltpu.VMEM((chunk, chunk), BF16),
           pltpu.VMEM(((chunk // REC_BAND).bit_length() - 1, chunk, chunk), F32)],
        compiler_params=_params("parallel", "arbitrary"),
        name="hgrn2",
    )(rec3, rec3, rec3, rf, lb.reshape(1, REC_WIDTH), nw.reshape(1, d))


def _out_ffn_body(x_ref, attn_ref, rec_ref, gates_ref, mod_ref, wba_ref, wbr_ref, wo_ref,
                  n2_ref, wg_ref, wu_ref, wd_ref, fn_ref, o_ref, *, final):
    d = x_ref.shape[2]
    x = x_ref[0]
    g1 = mod_ref[0, 2:3, :]
    sh2 = mod_ref[0, 3:4, :]
    sc2 = mod_ref[0, 4:5, :]
    g2 = mod_ref[0, 5:6, :]
    ya = jnp.dot(attn_ref[0], wba_ref[...], preferred_element_type=F32)
    yr = jnp.dot(rec_ref[0], wbr_ref[...], preferred_element_type=F32)
    gates = gates_ref[0].astype(F32)
    mix = (_sigmoid(gates[:, :d]) * ya + _sigmoid(gates[:, d:]) * yr).astype(BF16)
    x1 = x + g1 * jnp.dot(mix, wo_ref[...], preferred_element_type=F32)
    h2 = (_rms(x1, n2_ref[...]) * (1.0 + sc2) + sh2).astype(BF16)
    gate = jnp.dot(h2, wg_ref[...], preferred_element_type=F32)
    up = jnp.dot(h2, wu_ref[...], preferred_element_type=F32)
    act = (_silu(gate) * up).astype(BF16)
    x2 = x1 + g2 * jnp.dot(act, wd_ref[...], preferred_element_type=F32)
    o_ref[0] = _rms(x2, fn_ref[...]) if final else x2


def _out_ffn(x, attn, rec, gates, mod, wba, wbr, wo, n2w, wg, wu, wd, fnw, tm, final):
    bsz, s, d = x.shape
    hid = wg.shape[1]
    tok = lambda w: pl.BlockSpec((1, tm, w), lambda b, i: (b, i, 0))
    return pl.pallas_call(
        functools.partial(_out_ffn_body, final=final),
        grid=(bsz, s // tm),
        in_specs=[tok(d), tok(ATTN_WIDTH), tok(REC_WIDTH), tok(2 * d),
                  pl.BlockSpec((1, 6, d), lambda b, i: (b, 0, 0)),
                  _const_spec((ATTN_WIDTH, d)), _const_spec((REC_WIDTH, d)), _const_spec((d, d)),
                  _const_spec((1, d)), _const_spec((d, hid)), _const_spec((d, hid)),
                  _const_spec((hid, d)), _const_spec((1, d))],
        out_specs=tok(d),
        out_shape=jax.ShapeDtypeStruct((bsz, s, d), F32),
        compiler_params=_params("parallel", "parallel"),
        name="out_ffn",
    )(x, attn, rec, gates, mod, wba, wbr, wo, n2w, wg, wu, wd, fnw)


def _pick_tile(n, want):
    t = min(want, n)
    while n % t:
        t //= 2
    return t


def _dsa_inputs(kv, kiw):
    bsz, s, _ = kv.shape
    ki = kiw[:, :, :IDX_HEAD_DIM].astype(BF16)
    wt = kiw[:, :, IDX_HEAD_DIM:IDX_HEAD_DIM + N_IDX_HEADS].transpose(0, 2, 1)
    k = kv[:, :, :KV_WIDTH].reshape(bsz, s, N_KV_GROUPS, ATTN_HEAD_DIM).transpose(0, 2, 1, 3)
    one = jnp.ones((bsz, N_KV_GROUPS, s, 1), BF16)
    zero = jnp.zeros((bsz, N_KV_GROUPS, s, K_AUG_ROWS - ATTN_HEAD_DIM - 1), BF16)
    k = jnp.concatenate([k, one, zero], axis=3)
    vt = kv[:, :, KV_WIDTH:].reshape(bsz, s, N_KV_GROUPS, ATTN_HEAD_DIM).transpose(0, 2, 3, 1)
    ones = jnp.ones((bsz, N_KV_GROUPS, V_AUG_ROWS - ATTN_HEAD_DIM, s), BF16)
    return ki, wt, k, jnp.concatenate([vt, ones], axis=2)


def kernel(x, c, w_ada, b_ada, norm1_w, w_in, q_norm_w, w_uq, w_uq_idx, rel_bias, lb_logits,
           rec_norm_w, w_branch_attn, w_branch_rec, w_out, norm2_w, w_ffn_gate, w_ffn_up,
           w_ffn_down, final_norm_w):
    bsz, s, d = x.shape
    depth = w_in.shape[0]
    k_sel = min(TOPK_MAX, s // 4)
    tm = _pick_tile(s, TOKEN_TILE)
    tq = _pick_tile(s, Q_TILE)
    tk = max(tq, _pick_tile(s, K_TILE))
    chunk = _pick_tile(s, REC_CHUNK)
    lower_bounds = jnp.cumsum(jax.nn.softmax(lb_logits.astype(F32), axis=0), axis=0)
    bias = _near_bias(rel_bias, tq, tk)

    o_cq = 0
    o_k = o_cq + Q_LORA_RANK
    o_v = o_k + KV_WIDTH
    o_ki = o_v + KV_WIDTH
    o_wi = o_ki + IDX_HEAD_DIM
    o_rq = o_wi + N_IDX_HEADS
    o_rf = o_rq + REC_WIDTH
    o_ri = o_rf + REC_WIDTH
    o_rg = o_ri + REC_WIDTH
    o_gt = o_rg + REC_WIDTH

    for layer in range(depth):
        mod = _ada(c, w_ada[layer], b_ada[layer]).reshape(bsz, 6, d)
        wl = w_in[layer]
        pad_a = LANES - IDX_HEAD_DIM - N_IDX_HEADS
        wa = jnp.concatenate([wl[:, o_cq:o_rq], jnp.zeros((d, pad_a), wl.dtype)], axis=1).astype(BF16)
        wr = jnp.concatenate([wl[:, o_rq:o_rf], wl[:, o_ri:o_rg], wl[:, o_rg:o_gt],
                              wl[:, o_rf:o_ri]], axis=1).astype(BF16)
        wg = wl[:, o_gt:].astype(BF16)
        wuq = (w_uq[layer] * (ATTN_HEAD_DIM ** -0.5 * LOG2_E)).astype(BF16).T
        wui = (w_uq_idx[layer] * (IDX_HEAD_DIM ** -0.5)).astype(BF16).T

        qt, qit, kv, kiw, rec3, rf, gates = _in_proj(
            x, mod, norm1_w[layer].reshape(1, d), wa, wr, wg,
            q_norm_w[layer].reshape(1, Q_LORA_RANK), wuq, wui, tm)

        ki, wt, k, vt = _dsa_inputs(kv, kiw)
        attn = _dsa(qt, qit, wt, ki, k, vt, bias, tq=tq, tk=tk, k_sel=k_sel)
        rec = _hgrn2(rec3, rf, lower_bounds[layer], rec_norm_w[layer], chunk=chunk)

        x = _out_ffn(x, attn, rec, gates, mod,
                     w_branch_attn[layer].astype(BF16), w_branch_rec[layer].astype(BF16),
                     w_out[layer].astype(BF16), norm2_w[layer].reshape(1, d),
                     w_ffn_gate[layer].astype(BF16), w_ffn_up[layer].astype(BF16),
                     w_ffn_down[layer].astype(BF16),
                     final_norm_w.reshape(1, d), tm, layer == depth - 1)
    return x
```

```python
import functools
import math

import jax
import jax.numpy as jnp
from jax import lax
from jax.experimental import pallas as pl
from jax.experimental.pallas import tpu as pltpu

F32 = jnp.float32
BF16 = jnp.bfloat16

N_ATTN_HEADS = 8
ATTN_HEAD_DIM = 64
N_KV_GROUPS = 2
HEADS_PER_GROUP = N_ATTN_HEADS // N_KV_GROUPS
Q_LORA_RANK = 256
N_IDX_HEADS = 8
IDX_HEAD_DIM = 64
TOPK_MAX = 256
N_BUCKETS = 32
MAX_DISTANCE = 128
N_REC_HEADS = 4
REC_HEAD_DIM = 128
EPS = 1e-6

ATTN_WIDTH = N_ATTN_HEADS * ATTN_HEAD_DIM
KV_WIDTH = N_KV_GROUPS * ATTN_HEAD_DIM
REC_WIDTH = N_REC_HEADS * REC_HEAD_DIM

LANES = 128
SUBLANES = 8
BF16_SUBLANES = 16
VMEM_LIMIT_BYTES = 56 * 1024 * 1024

ADA_TILE = 1024
TOKEN_TILE = 512
Q_TILE = 256
K_TILE = 512
REC_CHUNK = 128
REC_TIME_TILE = 1024
REC_BAND = 4
NEG_BIG = -1e30
BISECT_CAP = 700
SNAP_AFTER = 14
FOLD_CHAINS = 4
PASS_ROWS = 128
LOG2_E = math.log2(math.e)
MAX_SANE_DENOM = 1e30
MIN_SANE_DENOM = 1e-18
V_AUG_ROWS = ATTN_HEAD_DIM + BF16_SUBLANES


def _rms(x, w):
    return x * lax.rsqrt(jnp.mean(x * x, axis=-1, keepdims=True) + EPS) * w


def _sigmoid(x):
    return 0.5 * jnp.tanh(0.5 * x) + 0.5


def _sigmoid_rel(x):
    return 1.0 / (1.0 + jnp.exp(-x))


def _silu(x):
    return x * _sigmoid(x)


def _params(*sem):
    return pltpu.CompilerParams(dimension_semantics=sem, vmem_limit_bytes=VMEM_LIMIT_BYTES)


def _const_spec(shape):
    nd = len(shape)
    return pl.BlockSpec(shape, lambda *_: (0,) * nd, pipeline_mode=pl.Buffered(1))


def _ada_body(c_ref, w_ref, b_ref, o_ref):
    ca = _silu(c_ref[...])
    o_ref[...] = jnp.dot(ca, w_ref[...], precision=lax.Precision.HIGHEST,
                         preferred_element_type=F32) + b_ref[...]


def _ada(c, w, b):
    bsz, d = c.shape
    n = w.shape[1]
    tn = _pick_tile(n, ADA_TILE)
    return pl.pallas_call(
        _ada_body,
        grid=(n // tn,),
        in_specs=[pl.BlockSpec((bsz, d), lambda j: (0, 0)),
                  pl.BlockSpec((d, tn), lambda j: (0, j)),
                  pl.BlockSpec((1, tn), lambda j: (0, j))],
        out_specs=pl.BlockSpec((bsz, tn), lambda j: (0, j)),
        out_shape=jax.ShapeDtypeStruct((bsz, n), F32),
        compiler_params=_params("arbitrary"),
        name="ada",
    )(c, w, b.reshape(1, n))


def _in_proj_body(x_ref, mod_ref, n1_ref, wa_ref, wr_ref, wg_ref, qn_ref, wuq_ref, wui_ref,
                  qt_ref, qit_ref, kv_ref, kiw_ref, rec_ref, rf_ref, gates_ref):
    tm = x_ref.shape[1]
    x = x_ref[0]
    sh1 = mod_ref[0, 0:1, :]
    sc1 = mod_ref[0, 1:2, :]
    h = (_rms(x, n1_ref[...]) * (1.0 + sc1) + sh1).astype(BF16)

    pa = jnp.dot(h, wa_ref[...], preferred_element_type=F32)
    kv_ref[0] = pa[:, Q_LORA_RANK:Q_LORA_RANK + 2 * KV_WIDTH].astype(BF16)
    kiw_ref[0] = pa[:, Q_LORA_RANK + 2 * KV_WIDTH:]
    cqn = _rms(pa[:, :Q_LORA_RANK], qn_ref[...]).astype(BF16)
    nt = (((1,), (1,)), ((), ()))
    qf = lax.dot_general(wuq_ref[...], cqn, nt, preferred_element_type=F32)
    qt_ref[0] = qf.reshape(N_ATTN_HEADS, ATTN_HEAD_DIM, tm).astype(BF16)
    qif = lax.dot_general(wui_ref[...], cqn, nt, preferred_element_type=F32)
    qit_ref[0] = qif.reshape(N_IDX_HEADS, IDX_HEAD_DIM, tm).astype(BF16)

    pr = jnp.dot(h, wr_ref[...], preferred_element_type=F32)
    rec_ref[0] = pr[:, :3 * REC_WIDTH].astype(BF16)
    rf_ref[0] = pr[:, 3 * REC_WIDTH:]
    gates_ref[0] = jnp.dot(h, wg_ref[...], preferred_element_type=F32).astype(BF16)


def _in_proj(x, mod, n1w, wa, wr, wg, qnw, wuq, wui, tm):
    bsz, s, d = x.shape
    wa_n = wa.shape[1]
    grid = (bsz, s // tm)
    tok = lambda w: pl.BlockSpec((1, tm, w), lambda b, i: (b, i, 0))
    head = lambda nh, hd: pl.BlockSpec((1, nh, hd, tm), lambda b, i: (b, 0, 0, i))
    out_shape = (
        jax.ShapeDtypeStruct((bsz, N_ATTN_HEADS, ATTN_HEAD_DIM, s), BF16),
        jax.ShapeDtypeStruct((bsz, N_IDX_HEADS, IDX_HEAD_DIM, s), BF16),
        jax.ShapeDtypeStruct((bsz, s, 2 * KV_WIDTH), BF16),
        jax.ShapeDtypeStruct((bsz, s, LANES), F32),
        jax.ShapeDtypeStruct((bsz, s, 3 * REC_WIDTH), BF16),
        jax.ShapeDtypeStruct((bsz, s, REC_WIDTH), F32),
        jax.ShapeDtypeStruct((bsz, s, 2 * d), BF16),
    )
    out_specs = (head(N_ATTN_HEADS, ATTN_HEAD_DIM), head(N_IDX_HEADS, IDX_HEAD_DIM),
                 tok(2 * KV_WIDTH), tok(LANES), tok(3 * REC_WIDTH), tok(REC_WIDTH), tok(2 * d))
    return pl.pallas_call(
        _in_proj_body,
        grid=grid,
        in_specs=[tok(d),
                  pl.BlockSpec((1, 6, d), lambda b, i: (b, 0, 0)),
                  _const_spec((1, d)),
                  _const_spec((d, wa_n)),
                  _const_spec((d, 4 * REC_WIDTH)),
                  _const_spec((d, 2 * d)),
                  _const_spec((1, Q_LORA_RANK)),
                  _const_spec((ATTN_WIDTH, Q_LORA_RANK)),
                  _const_spec((N_IDX_HEADS * IDX_HEAD_DIM, Q_LORA_RANK))],
        out_specs=out_specs,
        out_shape=out_shape,
        compiler_params=_params("parallel", "parallel"),
        name="in_proj",
    )(x, mod, n1w, wa, wr, wg, qnw, wuq, wui)


def _near_bias(rel_bias, tq, tk):
    n = jnp.arange(MAX_DISTANCE + 1, dtype=jnp.int32)
    max_exact = N_BUCKETS // 2
    nf = jnp.maximum(n, 1).astype(F32)
    large = max_exact + (jnp.log(nf / max_exact) / math.log(MAX_DISTANCE / max_exact)
                         * (N_BUCKETS - max_exact)).astype(jnp.int32)
    bucket = jnp.where(n < max_exact, n, jnp.minimum(large, N_BUCKETS - 1))
    bucket = bucket.at[MAX_DISTANCE].set(N_BUCKETS - 1)
    ids = jnp.arange(N_BUCKETS, dtype=jnp.int32)
    start = jnp.sum((bucket[None, :] < ids[:, None]).astype(jnp.int32), axis=1)
    rb = (rel_bias.astype(F32) - rel_bias[N_BUCKETS - 1].astype(F32)[None, :]) * LOG2_E
    step = rb - jnp.concatenate([jnp.zeros_like(rb[:1]), rb[:-1]], axis=0)
    dist = (jnp.arange(tq, dtype=jnp.int32)[None, :]
            - jnp.arange(2 * tk, dtype=jnp.int32)[:, None] + tk)
    reached = (dist[None, :, :] >= start[:, None, None]).astype(F32)
    return jnp.einsum("bh,but->hut", step, reached, precision=lax.Precision.HIGHEST)


def _dsa_body(qit_ref, w_ref, qt_ref, ki_ref, k_ref, vt_ref, bias_ref, o_ref,
              sc_ref, acc_ref, m_ref, s_scr, p_scr, mask_scr, *, tq, tk, k_sel):
    i = pl.program_id(1)
    row0 = i * tq
    n_kt = lax.div(row0 + tq + (tk - 1), tk)
    j_near = lax.div(jnp.maximum(row0 - (MAX_DISTANCE - 1), 0), tk)
    krow = lax.broadcasted_iota(jnp.int32, (tk, tq), 0)
    qcol = lax.broadcasted_iota(jnp.int32, (tk, tq), 1)
    rel = krow - qcol
    kf = float(k_sel)

    def fold(a, op, group=SUBLANES):
        n = a.shape[0] // group
        a = a.reshape(n // FOLD_CHAINS, FOLD_CHAINS, group, tq)
        r = a[0]
        for t in range(1, n // FOLD_CHAINS):
            r = op(r, a[t])
        while r.shape[0] > 1:
            half = r.shape[0] // 2
            r = op(r[:half], r[half:])
        return r[0]

    w = w_ref[0] * (N_IDX_HEADS ** -0.5)

    def score_tile(j, carry):
        mx, mn = carry
        c0 = pl.multiple_of(j * tk, tk)
        kin = ki_ref[0, pl.ds(c0, tk), :]
        sc = None
        for h in range(N_IDX_HEADS):
            lg = jnp.dot(kin, qit_ref[0, h], preferred_element_type=F32)
            term = jnp.maximum(lg, 0.0) * w[h:h + 1, :]
            sc = term if sc is None else sc + term
        causal = rel <= (row0 - c0)
        scm = jnp.where(causal, sc, NEG_BIG)
        sc_ref[pl.ds(c0, tk), :] = scm
        mx = jnp.maximum(mx, fold(scm, jnp.maximum))
        mn = jnp.minimum(mn, fold(jnp.where(causal, sc, -NEG_BIG), jnp.minimum))
        return mx, mn

    def score_pair(t, carry):
        carry = score_tile(2 * t, carry)
        return score_tile(jnp.minimum(2 * t + 1, n_kt - 1), carry)

    mx8, mn8 = lax.fori_loop(0, lax.div(n_kt + 1, 2), score_pair,
                             (jnp.full((SUBLANES, tq), NEG_BIG, F32),
                              jnp.full((SUBLANES, tq), -NEG_BIG, F32)))
    mx = jnp.max(mx8, axis=0, keepdims=True)
    mn = jnp.min(mn8, axis=0, keepdims=True)

    def key_pass(fn, op, init):
        def body(j, acc):
            c0 = pl.multiple_of(j * tk, tk)
            for u in range(tk // PASS_ROWS):
                s = sc_ref[pl.ds(c0 + u * PASS_ROWS, PASS_ROWS), :]
                acc = op(acc, fold(fn(s), op))
            return acc
        return lax.fori_loop(0, n_kt, body, jnp.full((SUBLANES, tq), init, F32))

    def count(pred):
        part = key_pass(lambda s: jnp.where(pred(s), 1.0, 0.0), jnp.add, 0.0)
        return jnp.sum(part, axis=0, keepdims=True)

    n_valid = (row0 + 1 + lax.broadcasted_iota(jnp.int32, (1, tq), 1)).astype(F32)
    take_all = n_valid <= kf
    cnt_top = count(lambda s: s >= mx)
    at_top = jnp.logical_and(jnp.logical_not(take_all), cnt_top >= kf)
    lo0 = jnp.where(take_all, NEG_BIG, jnp.where(at_top, mx, mn))
    done0 = jnp.logical_or(take_all, at_top)

    def bis_cond(st):
        it, lo, hi, c_lo, done_f = st
        return jnp.logical_and(it < BISECT_CAP, jnp.min(done_f) < 0.5)

    def halve(st):
        lo, hi, c_lo, done_f = st
        done = done_f > 0.5
        mid = lo + (hi - lo) * 0.5
        stuck = jnp.logical_or(mid <= lo, mid >= hi)
        cnt = count(lambda s: s >= mid)
        ge = cnt >= kf
        upd = jnp.logical_not(jnp.logical_or(done, stuck))
        raise_lo = jnp.logical_and(upd, ge)
        lo = jnp.where(raise_lo, mid, lo)
        c_lo = jnp.where(raise_lo, cnt, c_lo)
        hi = jnp.where(jnp.logical_and(upd, jnp.logical_not(ge)), mid, hi)
        done = jnp.logical_or(jnp.logical_or(done, stuck), jnp.logical_and(ge, cnt == kf))
        return lo, hi, c_lo, done.astype(F32)

    def snap(st):
        lo, hi, c_lo, done_f = st
        done = done_f > 0.5

        below = jnp.max(key_pass(lambda s: jnp.where(s < hi, s, NEG_BIG), jnp.maximum, NEG_BIG),
                        axis=0, keepdims=True)
        cnt = count(lambda s: s >= below)
        found = jnp.logical_and(jnp.logical_not(done), cnt >= kf)
        lower_hi = jnp.logical_and(jnp.logical_not(done), cnt < kf)
        lo = jnp.where(found, below, lo)
        c_lo = jnp.where(found, cnt, c_lo)
        hi = jnp.where(lower_hi, below, hi)
        return lo, hi, c_lo, jnp.logical_or(done, found).astype(F32)

    def bis_step(st):
        it, lo, hi, c_lo, done_f = st
        lo, hi, c_lo, done_f = lax.cond((it & 1) == 0, snap, halve, (lo, hi, c_lo, done_f))
        return it + 1, lo, hi, c_lo, done_f

    c_lo0 = jnp.where(at_top, cnt_top, n_valid)
    st = lax.fori_loop(0, SNAP_AFTER, lambda _, s: halve(s),
                       (lo0, mx, c_lo0, done0.astype(F32)))
    _, thr, _, c_thr, _ = lax.while_loop(bis_cond, bis_step, (jnp.int32(0),) + st)

    excess = jnp.logical_and(jnp.logical_not(take_all), c_thr > kf)

    @pl.when(jnp.max(excess.astype(F32)) > 0.5)
    def _():
        need = jnp.where(excess, kf - count(lambda s: s > thr), float(sc_ref.shape[0]))
        tril = (lax.broadcasted_iota(jnp.int32, (tk, tk), 1)
                <= lax.broadcasted_iota(jnp.int32, (tk, tk), 0)).astype(BF16)

        def drop_tile(j, seen):
            c0 = pl.multiple_of(j * tk, tk)
            s = sc_ref[pl.ds(c0, tk), :]
            eq = s == thr
            rank = seen + jnp.dot(tril, jnp.where(eq, 1.0, 0.0).astype(BF16),
                                  preferred_element_type=F32)
            sc_ref[pl.ds(c0, tk), :] = jnp.where(jnp.logical_and(eq, rank > need), NEG_BIG, s)
            return rank[tk - 1:tk, :]

        lax.fori_loop(0, n_kt, drop_tile, jnp.zeros((1, tq), F32))

    def logits(j, h, near):
        c0 = pl.multiple_of(j * tk, tk)
        g = h // HEADS_PER_GROUP
        s = jnp.dot(k_ref[0, g, pl.ds(c0, tk), :], qt_ref[0, h],
                    preferred_element_type=F32)
        if near:
            bias_row = pl.multiple_of(c0 - row0 + tk, tq)
            s = s + bias_ref[h, pl.ds(bias_row, tk), :]
        return s

    def selected(j, near):
        c0 = pl.multiple_of(j * tk, tk)
        keep = sc_ref[pl.ds(c0, tk), :] >= thr
        if near:
            keep = jnp.logical_and(keep, rel <= (row0 - c0))
        return keep

    def attend_exact(j, near):
        c0 = pl.multiple_of(j * tk, tk)
        mask_scr[...] = jnp.where(selected(j, near), 0.0, NEG_BIG)
        tile_max = []
        for h in range(N_ATTN_HEADS):
            s = logits(j, h, near) + mask_scr[...]
            s_scr[h] = s
            tile_max.append(jnp.max(fold(s, jnp.maximum), axis=0, keepdims=True))
        for h in range(N_ATTN_HEADS):
            g = h // HEADS_PER_GROUP
            m_old = m_ref[h]
            m_new = jnp.maximum(m_old, tile_max[h])
            alpha = jnp.exp2(m_old - m_new)
            p_scr[h] = jnp.exp2(s_scr[h] - m_new).astype(BF16)
            m_ref[h] = m_new
            pv = jnp.dot(vt_ref[0, g, :, pl.ds(c0, tk)], p_scr[h], preferred_element_type=F32)
            acc_ref[h] = acc_ref[h] * alpha + pv

    def attend_fast(j, near):
        c0 = pl.multiple_of(j * tk, tk)
        mask_scr[...] = jnp.where(selected(j, near), 0.0, NEG_BIG)
        for h in range(N_ATTN_HEADS):
            p_scr[h] = jnp.exp2(logits(j, h, near) + mask_scr[...]).astype(BF16)
        for h in range(N_ATTN_HEADS):
            g = h // HEADS_PER_GROUP
            acc_ref[h] += jnp.dot(vt_ref[0, g, :, pl.ds(c0, tk)], p_scr[h],
                                  preferred_element_type=F32)

    def run_tiles(attend):
        acc_ref[...] = jnp.zeros(acc_ref.shape, F32)
        m_ref[...] = jnp.full(m_ref.shape, NEG_BIG, F32)

        def far_step(j, c):
            attend(j, False)
            return c

        lax.fori_loop(0, j_near, far_step, 0)
        for u in range(2):
            @pl.when(j_near + u < n_kt)
            def _():
                attend(j_near + u, True)

    run_tiles(attend_fast)
    denom = acc_ref[:, ATTN_HEAD_DIM:ATTN_HEAD_DIM + 1, :]
    sane = jnp.logical_and(denom > MIN_SANE_DENOM, denom < MAX_SANE_DENOM)

    @pl.when(jnp.min(sane.astype(F32)) < 0.5)
    def _():
        run_tiles(attend_exact)

    outs = []
    for h in range(N_ATTN_HEADS):
        a = acc_ref[h]
        outs.append(a[:ATTN_HEAD_DIM, :] / a[ATTN_HEAD_DIM:ATTN_HEAD_DIM + 1, :])
    o_ref[0] = jnp.concatenate(outs, axis=0).T.astype(BF16)


def _dsa(qt, qit, wt, ki, k, vt, bias, *, tq, tk, k_sel):
    bsz, _, _, s = qt.shape
    assert tq >= MAX_DISTANCE and tk % tq == 0 and s % tk == 0
    body = functools.partial(_dsa_body, tq=tq, tk=tk, k_sel=k_sel)
    return pl.pallas_call(
        body,
        grid=(bsz, s // tq),
        in_specs=[pl.BlockSpec((1, N_IDX_HEADS, IDX_HEAD_DIM, tq), lambda b, i: (b, 0, 0, i)),
                  pl.BlockSpec((1, N_IDX_HEADS, tq), lambda b, i: (b, 0, i)),
                  pl.BlockSpec((1, N_ATTN_HEADS, ATTN_HEAD_DIM, tq), lambda b, i: (b, 0, 0, i)),
                  pl.BlockSpec((1, s, IDX_HEAD_DIM), lambda b, i: (b, 0, 0)),
                  pl.BlockSpec((1, N_KV_GROUPS, s, ATTN_HEAD_DIM), lambda b, i: (b, 0, 0, 0)),
                  pl.BlockSpec((1, N_KV_GROUPS, V_AUG_ROWS, s), lambda b, i: (b, 0, 0, 0)),
                  _const_spec(bias.shape)],
        out_specs=pl.BlockSpec((1, tq, ATTN_WIDTH), lambda b, i: (b, i, 0)),
        out_shape=jax.ShapeDtypeStruct((bsz, s, ATTN_WIDTH), BF16),
        scratch_shapes=[pltpu.VMEM((s, tq), F32),
                        pltpu.VMEM((N_ATTN_HEADS, V_AUG_ROWS, tq), F32),
                        pltpu.VMEM((N_ATTN_HEADS, 1, tq), F32),
                        pltpu.VMEM((N_ATTN_HEADS, tk, tq), F32),
                        pltpu.VMEM((N_ATTN_HEADS, tk, tq), BF16),
                        pltpu.VMEM((tk, tq), F32)],
        compiler_params=_params("parallel", "arbitrary"),
        name="dsa",
    )(qit, wt, qt, ki, k, vt, bias)


def _split3(a):
    hi = a.astype(BF16)
    r1 = a - hi.astype(F32)
    mid = r1.astype(BF16)
    lo = (r1 - mid.astype(F32)).astype(BF16)
    return hi, mid, lo


def _hgrn2_body(rq_ref, ri_ref, rg_ref, rf_ref, lb_ref, nw_ref, o_ref,
                state_scr, tril_scr, lvl_scr, *, chunk):
    t_len = rq_ref.shape[1]
    c = chunk
    d = REC_HEAD_DIM
    nw = nw_ref[...]
    eye = (lax.broadcasted_iota(jnp.int32, (d, d), 0)
           == lax.broadcasted_iota(jnp.int32, (d, d), 1)).astype(F32)
    sub = lax.broadcasted_iota(jnp.int32, (c, d), 0) & (REC_BAND - 1)
    level_sizes = []
    b = REC_BAND
    while b < c:
        level_sizes.append(b)
        b *= 2

    @pl.when(pl.program_id(1) == 0)
    def _():
        state_scr[...] = jnp.zeros(state_scr.shape, F32)
        ti = lax.broadcasted_iota(jnp.int32, (c, c), 0)
        si = lax.broadcasted_iota(jnp.int32, (c, c), 1)
        tril_scr[...] = (si <= ti).astype(BF16)
        for li, b in enumerate(level_sizes):
            blk = 2 * b
            sh = blk.bit_length() - 1
            m = jnp.logical_and(
                lax.shift_right_logical(ti, sh) == lax.shift_right_logical(si, sh),
                jnp.logical_and((ti & (blk - 1)) >= b, (si & (blk - 1)) < b))
            lvl_scr[li] = m.astype(F32)

    def head_step(hh, t0):
        cols = slice(hh * d, (hh + 1) * d)
        lb = lb_ref[:, cols]
        state = state_scr[hh]
        f = lb + (1.0 - lb) * _sigmoid_rel(rf_ref[0, pl.ds(t0, c), cols])
        g = jnp.log(f)
        kk = 1.0 - f
        qq = _silu(rq_ref[0, pl.ds(t0, c), cols].astype(F32))
        vb = ri_ref[0, pl.ds(t0, c), cols]
        vv = vb.astype(F32)

        g3 = _split3(g)
        tril = tril_scr[...]
        a = (jnp.dot(tril, g3[0], preferred_element_type=F32)
             + jnp.dot(tril, g3[1], preferred_element_type=F32)
             + jnp.dot(tril, g3[2], preferred_element_type=F32))
        a_last = a[c - 1:c, :]

        o = jnp.dot((qq * jnp.exp(a)).astype(BF16), state.astype(BF16),
                    preferred_element_type=F32)

        p = jnp.zeros((c, c), F32)
        for li, b in enumerate(level_sizes):
            blk = 2 * b
            r = jnp.broadcast_to(a.reshape(c // blk, blk, d)[:, b - 1:b, :],
                                 (c // blk, blk, d)).reshape(c, d)
            qt = (qq * jnp.exp(jnp.minimum(a - r, 0.0))).astype(BF16)
            kt = (kk * jnp.exp(jnp.minimum(r - a, 0.0))).astype(BF16)
            pb = lax.dot_general(qt, kt, (((1,), (1,)), ((), ())), preferred_element_type=F32)
            p = p + pb * lvl_scr[li]
        o = o + jnp.dot(p.astype(BF16), vb, preferred_element_type=F32)

        def back(x, dl):
            if dl == 0:
                return x
            return pltpu.roll(x.reshape(c // SUBLANES, SUBLANES, d), dl, axis=1).reshape(c, d)

        for dl in range(REC_BAND):
            a_s = back(a, dl)
            k_s = back(kk, dl)
            v_s = back(vv, dl)
            e = jnp.exp(jnp.where(sub >= dl, a - a_s, NEG_BIG))
            pd = jnp.sum(qq * k_s * e, axis=1, keepdims=True)
            o = o + pd * v_s

        kd = (kk * jnp.exp(a_last - a)).astype(BF16)
        upd = lax.dot_general(kd, vb, (((0,), (0,)), ((), ())), preferred_element_type=F32)
        e_col = jnp.sum(eye * jnp.exp(a_last), axis=1, keepdims=True)
        state_scr[hh] = e_col * state + upd

        y = _rms(o, nw) * _silu(rg_ref[0, pl.ds(t0, c), cols].astype(F32))
        o_ref[0, pl.ds(t0, c), cols] = y.astype(BF16)

    def step(n, carry):
        t0 = pl.multiple_of(n * c, c)
        for hh in range(N_REC_HEADS):
            head_step(hh, t0)
        return carry

    lax.fori_loop(0, t_len // c, step, 0)


def _hgrn2(rec3, rf, lb, nw, *, chunk):
    bsz, s, _ = rec3.shape
    d = REC_HEAD_DIM
    h = N_REC_HEADS
    body = functools.partial(_hgrn2_body, chunk=chunk)
    tt = _pick_tile(s, REC_TIME_TILE)
    col = lambda part: pl.BlockSpec((1, tt, REC_WIDTH), lambda b, t: (b, t, part))
    return pl.pallas_call(
        body,
        grid=(bsz, s // tt),
        in_specs=[col(0), col(1), col(2), col(0),
                  pl.BlockSpec((1, REC_WIDTH), lambda b, t: (0, 0)),
                  pl.BlockSpec((1, d), lambda b, t: (0, 0))],
        out_specs=col(0),
        out_shape=jax.ShapeDtypeStruct((bsz, s, REC_WIDTH), BF16),
        scratch_shapes=[pltpu.VMEM((h, d, d), F32)]
        + [pltpu.VMEM((chunk, chunk), BF16),
           pltpu.VMEM(((chunk // REC_BAND).bit_length() - 1, chunk, chunk), F32)],
        compiler_params=_params("parallel", "arbitrary"),
        name="hgrn2",
    )(rec3, rec3, rec3, rf, lb.reshape(1, REC_WIDTH), nw.reshape(1, d))


def _out_ffn_body(x_ref, attn_ref, rec_ref, gates_ref, mod_ref, wba_ref, wbr_ref, wo_ref,
                  n2_ref, wg_ref, wu_ref, wd_ref, fn_ref, o_ref, *, final):
    d = x_ref.shape[2]
    x = x_ref[0]
    g1 = mod_ref[0, 2:3, :]
    sh2 = mod_ref[0, 3:4, :]
    sc2 = mod_ref[0, 4:5, :]
    g2 = mod_ref[0, 5:6, :]
    ya = jnp.dot(attn_ref[0], wba_ref[...], preferred_element_type=F32)
    yr = jnp.dot(rec_ref[0], wbr_ref[...], preferred_element_type=F32)
    gates = gates_ref[0].astype(F32)
    mix = (_sigmoid(gates[:, :d]) * ya + _sigmoid(gates[:, d:]) * yr).astype(BF16)
    x1 = x + g1 * jnp.dot(mix, wo_ref[...], preferred_element_type=F32)
    h2 = (_rms(x1, n2_ref[...]) * (1.0 + sc2) + sh2).astype(BF16)
    gate = jnp.dot(h2, wg_ref[...], preferred_element_type=F32)
    up = jnp.dot(h2, wu_ref[...], preferred_element_type=F32)
    act = (_silu(gate) * up).astype(BF16)
    x2 = x1 + g2 * jnp.dot(act, wd_ref[...], preferred_element_type=F32)
    o_ref[0] = _rms(x2, fn_ref[...]) if final else x2


def _out_ffn(x, attn, rec, gates, mod, wba, wbr, wo, n2w, wg, wu, wd, fnw, tm, final):
    bsz, s, d = x.shape
    hid = wg.shape[1]
    tok = lambda w: pl.BlockSpec((1, tm, w), lambda b, i: (b, i, 0))
    return pl.pallas_call(
        functools.partial(_out_ffn_body, final=final),
        grid=(bsz, s // tm),
        in_specs=[tok(d), tok(ATTN_WIDTH), tok(REC_WIDTH), tok(2 * d),
                  pl.BlockSpec((1, 6, d), lambda b, i: (b, 0, 0)),
                  _const_spec((ATTN_WIDTH, d)), _const_spec((REC_WIDTH, d)), _const_spec((d, d)),
                  _const_spec((1, d)), _const_spec((d, hid)), _const_spec((d, hid)),
                  _const_spec((hid, d)), _const_spec((1, d))],
        out_specs=tok(d),
        out_shape=jax.ShapeDtypeStruct((bsz, s, d), F32),
        compiler_params=_params("parallel", "parallel"),
        name="out_ffn",
    )(x, attn, rec, gates, mod, wba, wbr, wo, n2w, wg, wu, wd, fnw)


def _pick_tile(n, want):
    t = min(want, n)
    while n % t:
        t //= 2
    return t


def _dsa_inputs(kv, kiw):
    bsz, s, _ = kv.shape
    ki = kiw[:, :, :IDX_HEAD_DIM].astype(BF16)
    wt = kiw[:, :, IDX_HEAD_DIM:IDX_HEAD_DIM + N_IDX_HEADS].transpose(0, 2, 1)
    k = kv[:, :, :KV_WIDTH].reshape(bsz, s, N_KV_GROUPS, ATTN_HEAD_DIM).transpose(0, 2, 1, 3)
    vt = kv[:, :, KV_WIDTH:].reshape(bsz, s, N_KV_GROUPS, ATTN_HEAD_DIM).transpose(0, 2, 3, 1)
    ones = jnp.ones((bsz, N_KV_GROUPS, V_AUG_ROWS - ATTN_HEAD_DIM, s), BF16)
    return ki, wt, k, jnp.concatenate([vt, ones], axis=2)


def kernel(x, c, w_ada, b_ada, norm1_w, w_in, q_norm_w, w_uq, w_uq_idx, rel_bias, lb_logits,
           rec_norm_w, w_branch_attn, w_branch_rec, w_out, norm2_w, w_ffn_gate, w_ffn_up,
           w_ffn_down, final_norm_w):
    bsz, s, d = x.shape
    depth = w_in.shape[0]
    k_sel = min(TOPK_MAX, s // 4)
    tm = _pick_tile(s, TOKEN_TILE)
    tq = _pick_tile(s, Q_TILE)
    tk = max(tq, _pick_tile(s, K_TILE))
    chunk = _pick_tile(s, REC_CHUNK)
    lower_bounds = jnp.cumsum(jax.nn.softmax(lb_logits.astype(F32), axis=0), axis=0)
    bias = _near_bias(rel_bias, tq, tk)

    o_cq = 0
    o_k = o_cq + Q_LORA_RANK
    o_v = o_k + KV_WIDTH
    o_ki = o_v + KV_WIDTH
    o_wi = o_ki + IDX_HEAD_DIM
    o_rq = o_wi + N_IDX_HEADS
    o_rf = o_rq + REC_WIDTH
    o_ri = o_rf + REC_WIDTH
    o_rg = o_ri + REC_WIDTH
    o_gt = o_rg + REC_WIDTH

    for layer in range(depth):
        mod = _ada(c, w_ada[layer], b_ada[layer]).reshape(bsz, 6, d)
        wl = w_in[layer]
        pad_a = LANES - IDX_HEAD_DIM - N_IDX_HEADS
        wa = jnp.concatenate([wl[:, o_cq:o_rq], jnp.zeros((d, pad_a), wl.dtype)], axis=1).astype(BF16)
        wr = jnp.concatenate([wl[:, o_rq:o_rf], wl[:, o_ri:o_rg], wl[:, o_rg:o_gt],
                              wl[:, o_rf:o_ri]], axis=1).astype(BF16)
        wg = wl[:, o_gt:].astype(BF16)
        wuq = (w_uq[layer] * (ATTN_HEAD_DIM ** -0.5 * LOG2_E)).astype(BF16).T
        wui = (w_uq_idx[layer] * (IDX_HEAD_DIM ** -0.5)).astype(BF16).T

        qt, qit, kv, kiw, rec3, rf, gates = _in_proj(
            x, mod, norm1_w[layer].reshape(1, d), wa, wr, wg,
            q_norm_w[layer].reshape(1, Q_LORA_RANK), wuq, wui, tm)

        ki, wt, k, vt = _dsa_inputs(kv, kiw)
        attn = _dsa(qt, qit, wt, ki, k, vt, bias, tq=tq, tk=tk, k_sel=k_sel)
        rec = _hgrn2(rec3, rf, lower_bounds[layer], rec_norm_w[layer], chunk=chunk)

        x = _out_ffn(x, attn, rec, gates, mod,
                     w_branch_attn[layer].astype(BF16), w_branch_rec[layer].astype(BF16),
                     w_out[layer].astype(BF16), norm2_w[layer].reshape(1, d),
                     w_ffn_gate[layer].astype(BF16), w_ffn_up[layer].astype(BF16),
                     w_ffn_down[layer].astype(BF16),
                     final_norm_w.reshape(1, d), tm, layer == depth - 1)
    return x
```

```python
import functools
import math

import jax
import jax.numpy as jnp
from jax import lax
from jax.experimental import pallas as pl
from jax.experimental.pallas import tpu as pltpu

F32 = jnp.float32
BF16 = jnp.bfloat16

N_ATTN_HEADS = 8
ATTN_HEAD_DIM = 64
N_KV_GROUPS = 2
HEADS_PER_GROUP = N_ATTN_HEADS // N_KV_GROUPS
Q_LORA_RANK = 256
N_IDX_HEADS = 8
IDX_HEAD_DIM = 64
TOPK_MAX = 256
N_BUCKETS = 32
MAX_DISTANCE = 128
N_REC_HEADS = 4
REC_HEAD_DIM = 128
EPS = 1e-6

ATTN_WIDTH = N_ATTN_HEADS * ATTN_HEAD_DIM
KV_WIDTH = N_KV_GROUPS * ATTN_HEAD_DIM
REC_WIDTH = N_REC_HEADS * REC_HEAD_DIM

LANES = 128
SUBLANES = 8
BF16_SUBLANES = 16
VMEM_LIMIT_BYTES = 56 * 1024 * 1024

ADA_TILE = 1024
TOKEN_TILE = 512
Q_TILE = 256
K_TILE = 512
REC_CHUNK = 128
REC_TIME_TILE = 1024
REC_BAND = 4
NEG_BIG = -1e30
BISECT_CAP = 700
SNAP_AFTER = 14
FOLD_CHAINS = 4
PASS_ROWS = 128
LOG2_E = math.log2(math.e)
MAX_SANE_DENOM = 1e30
MIN_SANE_DENOM = 1e-18
V_AUG_ROWS = ATTN_HEAD_DIM + BF16_SUBLANES


def _rms(x, w):
    return x * lax.rsqrt(jnp.mean(x * x, axis=-1, keepdims=True) + EPS) * w


def _sigmoid(x):
    return 0.5 * jnp.tanh(0.5 * x) + 0.5


def _sigmoid_rel(x):
    return 1.0 / (1.0 + jnp.exp(-x))


def _silu(x):
    return x * _sigmoid(x)


def _params(*sem):
    return pltpu.CompilerParams(dimension_semantics=sem, vmem_limit_bytes=VMEM_LIMIT_BYTES)


def _const_spec(shape):
    nd = len(shape)
    return pl.BlockSpec(shape, lambda *_: (0,) * nd, pipeline_mode=pl.Buffered(1))


def _ada_body(c_ref, w_ref, b_ref, o_ref):
    ca = _silu(c_ref[...])
    o_ref[...] = jnp.dot(ca, w_ref[...], precision=lax.Precision.HIGHEST,
                         preferred_element_type=F32) + b_ref[...]


def _ada(c, w, b):
    bsz, d = c.shape
    n = w.shape[1]
    tn = _pick_tile(n, ADA_TILE)
    return pl.pallas_call(
        _ada_body,
        grid=(n // tn,),
        in_specs=[pl.BlockSpec((bsz, d), lambda j: (0, 0)),
                  pl.BlockSpec((d, tn), lambda j: (0, j)),
                  pl.BlockSpec((1, tn), lambda j: (0, j))],
        out_specs=pl.BlockSpec((bsz, tn), lambda j: (0, j)),
        out_shape=jax.ShapeDtypeStruct((bsz, n), F32),
        compiler_params=_params("arbitrary"),
        name="ada",
    )(c, w, b.reshape(1, n))


def _in_proj_body(x_ref, mod_ref, n1_ref, wa_ref, wr_ref, wg_ref, qn_ref, wuq_ref, wui_ref,
                  qt_ref, qit_ref, kv_ref, kiw_ref, rec_ref, rf_ref, gates_ref):
    tm = x_ref.shape[1]
    x = x_ref[0]
    sh1 = mod_ref[0, 0:1, :]
    sc1 = mod_ref[0, 1:2, :]
    h = (_rms(x, n1_ref[...]) * (1.0 + sc1) + sh1).astype(BF16)

    pa = jnp.dot(h, wa_ref[...], preferred_element_type=F32)
    kv_ref[0] = pa[:, Q_LORA_RANK:Q_LORA_RANK + 2 * KV_WIDTH].astype(BF16)
    kiw_ref[0] = pa[:, Q_LORA_RANK + 2 * KV_WIDTH:]
    cqn = _rms(pa[:, :Q_LORA_RANK], qn_ref[...]).astype(BF16)
    nt = (((1,), (1,)), ((), ()))
    qf = lax.dot_general(wuq_ref[...], cqn, nt, preferred_element_type=F32)
    qt_ref[0] = qf.reshape(N_ATTN_HEADS, ATTN_HEAD_DIM, tm).astype(BF16)
    qif = lax.dot_general(wui_ref[...], cqn, nt, preferred_element_type=F32)
    qit_ref[0] = qif.reshape(N_IDX_HEADS, IDX_HEAD_DIM, tm).astype(BF16)

    pr = jnp.dot(h, wr_ref[...], preferred_element_type=F32)
    rec_ref[0] = pr[:, :3 * REC_WIDTH].astype(BF16)
    rf_ref[0] = pr[:, 3 * REC_WIDTH:]
    gates_ref[0] = jnp.dot(h, wg_ref[...], preferred_element_type=F32).astype(BF16)


def _in_proj(x, mod, n1w, wa, wr, wg, qnw, wuq, wui, tm):
    bsz, s, d = x.shape
    wa_n = wa.shape[1]
    grid = (bsz, s // tm)
    tok = lambda w: pl.BlockSpec((1, tm, w), lambda b, i: (b, i, 0))
    head = lambda nh, hd: pl.BlockSpec((1, nh, hd, tm), lambda b, i: (b, 0, 0, i))
    out_shape = (
        jax.ShapeDtypeStruct((bsz, N_ATTN_HEADS, ATTN_HEAD_DIM, s), BF16),
        jax.ShapeDtypeStruct((bsz, N_IDX_HEADS, IDX_HEAD_DIM, s), BF16),
        jax.ShapeDtypeStruct((bsz, s, 2 * KV_WIDTH), BF16),
        jax.ShapeDtypeStruct((bsz, s, LANES), F32),
        jax.ShapeDtypeStruct((bsz, s, 3 * REC_WIDTH), BF16),
        jax.ShapeDtypeStruct((bsz, s, REC_WIDTH), F32),
        jax.ShapeDtypeStruct((bsz, s, 2 * d), BF16),
    )
    out_specs = (head(N_ATTN_HEADS, ATTN_HEAD_DIM), head(N_IDX_HEADS, IDX_HEAD_DIM),
                 tok(2 * KV_WIDTH), tok(LANES), tok(3 * REC_WIDTH), tok(REC_WIDTH), tok(2 * d))
    return pl.pallas_call(
        _in_proj_body,
        grid=grid,
        in_specs=[tok(d),
                  pl.BlockSpec((1, 6, d), lambda b, i: (b, 0, 0)),
                  _const_spec((1, d)),
                  _const_spec((d, wa_n)),
                  _const_spec((d, 4 * REC_WIDTH)),
                  _const_spec((d, 2 * d)),
                  _const_spec((1, Q_LORA_RANK)),
                  _const_spec((ATTN_WIDTH, Q_LORA_RANK)),
                  _const_spec((N_IDX_HEADS * IDX_HEAD_DIM, Q_LORA_RANK))],
        out_specs=out_specs,
        out_shape=out_shape,
        compiler_params=_params("parallel", "parallel"),
        name="in_proj",
    )(x, mod, n1w, wa, wr, wg, qnw, wuq, wui)


def _near_bias(rel_bias, tq, tk):
    n = jnp.arange(MAX_DISTANCE + 1, dtype=jnp.int32)
    max_exact = N_BUCKETS // 2
    nf = jnp.maximum(n, 1).astype(F32)
    large = max_exact + (jnp.log(nf / max_exact) / math.log(MAX_DISTANCE / max_exact)
                         * (N_BUCKETS - max_exact)).astype(jnp.int32)
    bucket = jnp.where(n < max_exact, n, jnp.minimum(large, N_BUCKETS - 1))
    bucket = bucket.at[MAX_DISTANCE].set(N_BUCKETS - 1)
    ids = jnp.arange(N_BUCKETS, dtype=jnp.int32)
    start = jnp.sum((bucket[None, :] < ids[:, None]).astype(jnp.int32), axis=1)
    rb = (rel_bias.astype(F32) - rel_bias[N_BUCKETS - 1].astype(F32)[None, :]) * LOG2_E
    step = rb - jnp.concatenate([jnp.zeros_like(rb[:1]), rb[:-1]], axis=0)
    dist = (jnp.arange(tq, dtype=jnp.int32)[None, :]
            - jnp.arange(2 * tk, dtype=jnp.int32)[:, None] + tk)
    reached = (dist[None, :, :] >= start[:, None, None]).astype(F32)
    return jnp.einsum("bh,but->hut", step, reached, precision=lax.Precision.HIGHEST)


def _dsa_body(qit_ref, w_ref, qt_ref, ki_ref, k_ref, vt_ref, bias_ref, o_ref,
              sc_ref, acc_ref, m_ref, s_scr, p_scr, mask_scr, *, tq, tk, k_sel):
    i = pl.program_id(1)
    row0 = i * tq
    n_kt = lax.div(row0 + tq + (tk - 1), tk)
    j_near = lax.div(jnp.maximum(row0 - (MAX_DISTANCE - 1), 0), tk)
    krow = lax.broadcasted_iota(jnp.int32, (tk, tq), 0)
    qcol = lax.broadcasted_iota(jnp.int32, (tk, tq), 1)
    rel = krow - qcol
    kf = float(k_sel)

    def fold(a, op, group=SUBLANES):
        n = a.shape[0] // group
        a = a.reshape(n // FOLD_CHAINS, FOLD_CHAINS, group, tq)
        r = a[0]
        for t in range(1, n // FOLD_CHAINS):
            r = op(r, a[t])
        while r.shape[0] > 1:
            half = r.shape[0] // 2
            r = op(r[:half], r[half:])
        return r[0]

    w = w_ref[0] * (N_IDX_HEADS ** -0.5)

    def score_tile(j, carry):
        mx, mn = carry
        c0 = pl.multiple_of(j * tk, tk)
        kin = ki_ref[0, pl.ds(c0, tk), :]
        sc = None
        for h in range(N_IDX_HEADS):
            lg = jnp.dot(kin, qit_ref[0, h], preferred_element_type=F32)
            term = jnp.maximum(lg, 0.0) * w[h:h + 1, :]
            sc = term if sc is None else sc + term
        causal = rel <= (row0 - c0)
        scm = jnp.where(causal, sc, NEG_BIG)
        sc_ref[pl.ds(c0, tk), :] = scm
        mx = jnp.maximum(mx, fold(scm, jnp.maximum))
        mn = jnp.minimum(mn, fold(jnp.where(causal, sc, -NEG_BIG), jnp.minimum))
        return mx, mn

    def score_pair(t, carry):
        carry = score_tile(2 * t, carry)
        return score_tile(jnp.minimum(2 * t + 1, n_kt - 1), carry)

    mx8, mn8 = lax.fori_loop(0, lax.div(n_kt + 1, 2), score_pair,
                             (jnp.full((SUBLANES, tq), NEG_BIG, F32),
                              jnp.full((SUBLANES, tq), -NEG_BIG, F32)))
    mx = jnp.max(mx8, axis=0, keepdims=True)
    mn = jnp.min(mn8, axis=0, keepdims=True)

    def key_pass(fn, op, init):
        def one_tile(j, acc):
            c0 = pl.multiple_of(j * tk, tk)
            for u in range(tk // PASS_ROWS):
                s = sc_ref[pl.ds(c0 + u * PASS_ROWS, PASS_ROWS), :]
                acc = op(acc, fold(fn(s), op))
            return acc

        pairs = lax.shift_right_logical(n_kt, 1)
        acc = lax.fori_loop(0, pairs, lambda t, a: one_tile(t + pairs, one_tile(t, a)),
                            jnp.full((SUBLANES, tq), init, F32))
        return lax.cond((n_kt & 1) == 1, lambda a: one_tile(n_kt - 1, a), lambda a: a, acc)

    def count(pred):
        part = key_pass(lambda s: jnp.where(pred(s), 1.0, 0.0), jnp.add, 0.0)
        return jnp.sum(part, axis=0, keepdims=True)

    n_valid = (row0 + 1 + lax.broadcasted_iota(jnp.int32, (1, tq), 1)).astype(F32)
    take_all = n_valid <= kf
    cnt_top = count(lambda s: s >= mx)
    at_top = jnp.logical_and(jnp.logical_not(take_all), cnt_top >= kf)
    lo0 = jnp.where(take_all, NEG_BIG, jnp.where(at_top, mx, mn))
    done0 = jnp.logical_or(take_all, at_top)

    def bis_cond(st):
        it, lo, hi, c_lo, done_f = st
        return jnp.logical_and(it < BISECT_CAP, jnp.min(done_f) < 0.5)

    def halve(st):
        lo, hi, c_lo, done_f = st
        done = done_f > 0.5
        mid = lo + (hi - lo) * 0.5
        stuck = jnp.logical_or(mid <= lo, mid >= hi)
        cnt = count(lambda s: s >= mid)
        ge = cnt >= kf
        upd = jnp.logical_not(jnp.logical_or(done, stuck))
        raise_lo = jnp.logical_and(upd, ge)
        lo = jnp.where(raise_lo, mid, lo)
        c_lo = jnp.where(raise_lo, cnt, c_lo)
        hi = jnp.where(jnp.logical_and(upd, jnp.logical_not(ge)), mid, hi)
        done = jnp.logical_or(jnp.logical_or(done, stuck), jnp.logical_and(ge, cnt == kf))
        return lo, hi, c_lo, done.astype(F32)

    def snap(st):
        lo, hi, c_lo, done_f = st
        done = done_f > 0.5

        below = jnp.max(key_pass(lambda s: jnp.where(s < hi, s, NEG_BIG), jnp.maximum, NEG_BIG),
                        axis=0, keepdims=True)
        cnt = count(lambda s: s >= below)
        found = jnp.logical_and(jnp.logical_not(done), cnt >= kf)
        lower_hi = jnp.logical_and(jnp.logical_not(done), cnt < kf)
        lo = jnp.where(found, below, lo)
        c_lo = jnp.where(found, cnt, c_lo)
        hi = jnp.where(lower_hi, below, hi)
        return lo, hi, c_lo, jnp.logical_or(done, found).astype(F32)

    def bis_step(st):
        it, lo, hi, c_lo, done_f = st
        lo, hi, c_lo, done_f = lax.cond((it & 1) == 0, snap, halve, (lo, hi, c_lo, done_f))
        return it + 1, lo, hi, c_lo, done_f

    c_lo0 = jnp.where(at_top, cnt_top, n_valid)
    st = lax.fori_loop(0, SNAP_AFTER, lambda _, s: halve(s),
                       (lo0, mx, c_lo0, done0.astype(F32)))
    _, thr, _, c_thr, _ = lax.while_loop(bis_cond, bis_step, (jnp.int32(0),) + st)

    excess = jnp.logical_and(jnp.logical_not(take_all), c_thr > kf)

    @pl.when(jnp.max(excess.astype(F32)) > 0.5)
    def _():
        need = jnp.where(excess, kf - count(lambda s: s > thr), float(sc_ref.shape[0]))
        tril = (lax.broadcasted_iota(jnp.int32, (tk, tk), 1)
                <= lax.broadcasted_iota(jnp.int32, (tk, tk), 0)).astype(BF16)

        def drop_tile(j, seen):
            c0 = pl.multiple_of(j * tk, tk)
            s = sc_ref[pl.ds(c0, tk), :]
            eq = s == thr
            rank = seen + jnp.dot(tril, jnp.where(eq, 1.0, 0.0).astype(BF16),
                                  preferred_element_type=F32)
            sc_ref[pl.ds(c0, tk), :] = jnp.where(jnp.logical_and(eq, rank > need), NEG_BIG, s)
            return rank[tk - 1:tk, :]

        lax.fori_loop(0, n_kt, drop_tile, jnp.zeros((1, tq), F32))

    def logits(j, h, near):
        c0 = pl.multiple_of(j * tk, tk)
        g = h // HEADS_PER_GROUP
        s = jnp.dot(k_ref[0, g, pl.ds(c0, tk), :], qt_ref[0, h],
                    preferred_element_type=F32)
        if near:
            bias_row = pl.multiple_of(c0 - row0 + tk, tq)
            s = s + bias_ref[h, pl.ds(bias_row, tk), :]
        return s

    def selected(j, near):
        c0 = pl.multiple_of(j * tk, tk)
        keep = sc_ref[pl.ds(c0, tk), :] >= thr
        if near:
            keep = jnp.logical_and(keep, rel <= (row0 - c0))
        return keep

    def attend_exact(j, near):
        c0 = pl.multiple_of(j * tk, tk)
        mask_scr[...] = jnp.where(selected(j, near), 0.0, NEG_BIG)
        tile_max = []
        for h in range(N_ATTN_HEADS):
            s = logits(j, h, near) + mask_scr[...]
            s_scr[h] = s
            tile_max.append(jnp.max(fold(s, jnp.maximum), axis=0, keepdims=True))
        for h in range(N_ATTN_HEADS):
            g = h // HEADS_PER_GROUP
            m_old = m_ref[h]
            m_new = jnp.maximum(m_old, tile_max[h])
            alpha = jnp.exp2(m_old - m_new)
            p_scr[h] = jnp.exp2(s_scr[h] - m_new).astype(BF16)
            m_ref[h] = m_new
            pv = jnp.dot(vt_ref[0, g, :, pl.ds(c0, tk)], p_scr[h], preferred_element_type=F32)
            acc_ref[h] = acc_ref[h] * alpha + pv

    def attend_fast(j, near):
        c0 = pl.multiple_of(j * tk, tk)
        mask_scr[...] = jnp.where(selected(j, near), 0.0, NEG_BIG)
        for h in range(N_ATTN_HEADS):
            p_scr[h] = jnp.exp2(logits(j, h, near) + mask_scr[...]).astype(BF16)
        for h in range(N_ATTN_HEADS):
            g = h // HEADS_PER_GROUP
            acc_ref[h] += jnp.dot(vt_ref[0, g, :, pl.ds(c0, tk)], p_scr[h],
                                  preferred_element_type=F32)

    def run_tiles(attend):
        acc_ref[...] = jnp.zeros(acc_ref.shape, F32)
        m_ref[...] = jnp.full(m_ref.shape, NEG_BIG, F32)

        def far_step(j, c):
            attend(j, False)
            return c

        lax.fori_loop(0, j_near, far_step, 0)
        for u in range(2):
            @pl.when(j_near + u < n_kt)
            def _():
                attend(j_near + u, True)

    run_tiles(attend_fast)
    denom = acc_ref[:, ATTN_HEAD_DIM:ATTN_HEAD_DIM + 1, :]
    sane = jnp.logical_and(denom > MIN_SANE_DENOM, denom < MAX_SANE_DENOM)

    @pl.when(jnp.min(sane.astype(F32)) < 0.5)
    def _():
        run_tiles(attend_exact)

    outs = []
    for h in range(N_ATTN_HEADS):
        a = acc_ref[h]
        outs.append(a[:ATTN_HEAD_DIM, :] / a[ATTN_HEAD_DIM:ATTN_HEAD_DIM + 1, :])
    o_ref[0] = jnp.concatenate(outs, axis=0).T.astype(BF16)


def _dsa(qt, qit, wt, ki, k, vt, bias, *, tq, tk, k_sel):
    bsz, _, _, s = qt.shape
    assert tq >= MAX_DISTANCE and tk % tq == 0 and s % tk == 0
    body = functools.partial(_dsa_body, tq=tq, tk=tk, k_sel=k_sel)
    return pl.pallas_call(
        body,
        grid=(bsz, s // tq),
        in_specs=[pl.BlockSpec((1, N_IDX_HEADS, IDX_HEAD_DIM, tq), lambda b, i: (b, 0, 0, i)),
                  pl.BlockSpec((1, N_IDX_HEADS, tq), lambda b, i: (b, 0, i)),
                  pl.BlockSpec((1, N_ATTN_HEADS, ATTN_HEAD_DIM, tq), lambda b, i: (b, 0, 0, i)),
                  pl.BlockSpec((1, s, IDX_HEAD_DIM), lambda b, i: (b, 0, 0)),
                  pl.BlockSpec((1, N_KV_GROUPS, s, ATTN_HEAD_DIM), lambda b, i: (b, 0, 0, 0)),
                  pl.BlockSpec((1, N_KV_GROUPS, V_AUG_ROWS, s), lambda b, i: (b, 0, 0, 0)),
                  _const_spec(bias.shape)],
        out_specs=pl.BlockSpec((1, tq, ATTN_WIDTH), lambda b, i: (b, i, 0)),
        out_shape=jax.ShapeDtypeStruct((bsz, s, ATTN_WIDTH), BF16),
        scratch_shapes=[pltpu.VMEM((s, tq), F32),
                        pltpu.VMEM((N_ATTN_HEADS, V_AUG_ROWS, tq), F32),
                        pltpu.VMEM((N_ATTN_HEADS, 1, tq), F32),
                        pltpu.VMEM((N_ATTN_HEADS, tk, tq), F32),
                        pltpu.VMEM((N_ATTN_HEADS, tk, tq), BF16),
                        pltpu.VMEM((tk, tq), F32)],
        compiler_params=_params("parallel", "arbitrary"),
        name="dsa",
    )(qit, wt, qt, ki, k, vt, bias)


def _split3(a):
    hi = a.astype(BF16)
    r1 = a - hi.astype(F32)
    mid = r1.astype(BF16)
    lo = (r1 - mid.astype(F32)).astype(BF16)
    return hi, mid, lo


def _hgrn2_body(rq_ref, ri_ref, rg_ref, rf_ref, lb_ref, nw_ref, o_ref,
                state_scr, tril_scr, lvl_scr, *, chunk):
    t_len = rq_ref.shape[1]
    c = chunk
    d = REC_HEAD_DIM
    nw = nw_ref[...]
    eye = (lax.broadcasted_iota(jnp.int32, (d, d), 0)
           == lax.broadcasted_iota(jnp.int32, (d, d), 1)).astype(F32)
    sub = lax.broadcasted_iota(jnp.int32, (c, d), 0) & (REC_BAND - 1)
    level_sizes = []
    b = REC_BAND
    while b < c:
        level_sizes.append(b)
        b *= 2

    @pl.when(pl.program_id(1) == 0)
    def _():
        state_scr[...] = jnp.zeros(state_scr.shape, F32)
        ti = lax.broadcasted_iota(jnp.int32, (c, c), 0)
        si = lax.broadcasted_iota(jnp.int32, (c, c), 1)
        tril_scr[...] = (si <= ti).astype(BF16)
        for li, b in enumerate(level_sizes):
            blk = 2 * b
            sh = blk.bit_length() - 1
            m = jnp.logical_and(
                lax.shift_right_logical(ti, sh) == lax.shift_right_logical(si, sh),
                jnp.logical_and((ti & (blk - 1)) >= b, (si & (blk - 1)) < b))
            lvl_scr[li] = m.astype(F32)

    def head_step(hh, t0):
        cols = slice(hh * d, (hh + 1) * d)
        lb = lb_ref[:, cols]
        state = state_scr[hh]
        f = lb + (1.0 - lb) * _sigmoid_rel(rf_ref[0, pl.ds(t0, c), cols])
        g = jnp.log(f)
        kk = 1.0 - f
        qq = _silu(rq_ref[0, pl.ds(t0, c), cols].astype(F32))
        vb = ri_ref[0, pl.ds(t0, c), cols]
        vv = vb.astype(F32)

        g3 = _split3(g)
        tril = tril_scr[...]
        a = (jnp.dot(tril, g3[0], preferred_element_type=F32)
             + jnp.dot(tril, g3[1], preferred_element_type=F32)
             + jnp.dot(tril, g3[2], preferred_element_type=F32))
        a_last = a[c - 1:c, :]

        o = jnp.dot((qq * jnp.exp(a)).astype(BF16), state.astype(BF16),
                    preferred_element_type=F32)

        p = jnp.zeros((c, c), F32)
        for li, b in enumerate(level_sizes):
            blk = 2 * b
            r = jnp.broadcast_to(a.reshape(c // blk, blk, d)[:, b - 1:b, :],
                                 (c // blk, blk, d)).reshape(c, d)
            qt = (qq * jnp.exp(jnp.minimum(a - r, 0.0))).astype(BF16)
            kt = (kk * jnp.exp(jnp.minimum(r - a, 0.0))).astype(BF16)
            pb = lax.dot_general(qt, kt, (((1,), (1,)), ((), ())), preferred_element_type=F32)
            p = p + pb * lvl_scr[li]
        o = o + jnp.dot(p.astype(BF16), vb, preferred_element_type=F32)

        def back(x, dl):
            if dl == 0:
                return x
            return pltpu.roll(x.reshape(c // SUBLANES, SUBLANES, d), dl, axis=1).reshape(c, d)

        for dl in range(REC_BAND):
            a_s = back(a, dl)
            k_s = back(kk, dl)
            v_s = back(vv, dl)
            e = jnp.exp(jnp.where(sub >= dl, a - a_s, NEG_BIG))
            pd = jnp.sum(qq * k_s * e, axis=1, keepdims=True)
            o = o + pd * v_s

        kd = (kk * jnp.exp(a_last - a)).astype(BF16)
        upd = lax.dot_general(kd, vb, (((0,), (0,)), ((), ())), preferred_element_type=F32)
        e_col = jnp.sum(eye * jnp.exp(a_last), axis=1, keepdims=True)
        state_scr[hh] = e_col * state + upd

        y = _rms(o, nw) * _silu(rg_ref[0, pl.ds(t0, c), cols].astype(F32))
        o_ref[0, pl.ds(t0, c), cols] = y.astype(BF16)

    def step(n, carry):
        t0 = pl.multiple_of(n * c, c)
        for hh in range(N_REC_HEADS):
            head_step(hh, t0)
        return carry

    lax.fori_loop(0, t_len // c, step, 0)


def _hgrn2(rec3, rf, lb, nw, *, chunk):
    bsz, s, _ = rec3.shape
    d = REC_HEAD_DIM
    h = N_REC_HEADS
    body = functools.partial(_hgrn2_body, chunk=chunk)
    tt = _pick_tile(s, REC_TIME_TILE)
    col = lambda part: pl.BlockSpec((1, tt, REC_WIDTH), lambda b, t: (b, t, part))
    return pl.pallas_call(
        body,
        grid=(bsz, s // tt),
        in_specs=[col(0), col(1), col(2), col(0),
                  pl.BlockSpec((1, REC_WIDTH), lambda b, t: (0, 0)),
                  pl.BlockSpec((1, d), lambda b, t: (0, 0))],
        out_specs=col(0),
        out_shape=jax.ShapeDtypeStruct((bsz, s, REC_WIDTH), BF16),
        scratch_shapes=[pltpu.VMEM((h, d, d), F32)]
        + [pltpu.VMEM((chunk, chunk), BF16),
           pltpu.VMEM(((chunk // REC_BAND).bit_length() - 1, chunk, chunk), F32)],
        compiler_params=_params("parallel", "arbitrary"),
        name="hgrn2",
    )(rec3, rec3, rec3, rf, lb.reshape(1, REC_WIDTH), nw.reshape(1, d))


def _out_ffn_body(x_ref, attn_ref, rec_ref, gates_ref, mod_ref, wba_ref, wbr_ref, wo_ref,
                  n2_ref, wg_ref, wu_ref, wd_ref, fn_ref, o_ref, *, final):
    d = x_ref.shape[2]
    x = x_ref[0]
    g1 = mod_ref[0, 2:3, :]
    sh2 = mod_ref[0, 3:4, :]
    sc2 = mod_ref[0, 4:5, :]
    g2 = mod_ref[0, 5:6, :]
    ya = jnp.dot(attn_ref[0], wba_ref[...], preferred_element_type=F32)
    yr = jnp.dot(rec_ref[0], wbr_ref[...], preferred_element_type=F32)
    gates = gates_ref[0].astype(F32)
    mix = (_sigmoid(gates[:, :d]) * ya + _sigmoid(gates[:, d:]) * yr).astype(BF16)
    x1 = x + g1 * jnp.dot(mix, wo_ref[...], preferred_element_type=F32)
    h2 = (_rms(x1, n2_ref[...]) * (1.0 + sc2) + sh2).astype(BF16)
    gate = jnp.dot(h2, wg_ref[...], preferred_element_type=F32)
    up = jnp.dot(h2, wu_ref[...], preferred_element_type=F32)
    act = (_silu(gate) * up).astype(BF16)
    x2 = x1 + g2 * jnp.dot(act, wd_ref[...], preferred_element_type=F32)
    o_ref[0] = _rms(x2, fn_ref[...]) if final else x2


def _out_ffn(x, attn, rec, gates, mod, wba, wbr, wo, n2w, wg, wu, wd, fnw, tm, final):
    bsz, s, d = x.shape
    hid = wg.shape[1]
    tok = lambda w: pl.BlockSpec((1, tm, w), lambda b, i: (b, i, 0))
    return pl.pallas_call(
        functools.partial(_out_ffn_body, final=final),
        grid=(bsz, s // tm),
        in_specs=[tok(d), tok(ATTN_WIDTH), tok(REC_WIDTH), tok(2 * d),
                  pl.BlockSpec((1, 6, d), lambda b, i: (b, 0, 0)),
                  _const_spec((ATTN_WIDTH, d)), _const_spec((REC_WIDTH, d)), _const_spec((d, d)),
                  _const_spec((1, d)), _const_spec((d, hid)), _const_spec((d, hid)),
                  _const_spec((hid, d)), _const_spec((1, d))],
        out_specs=tok(d),
        out_shape=jax.ShapeDtypeStruct((bsz, s, d), F32),
        compiler_params=_params("parallel", "parallel"),
        name="out_ffn",
    )(x, attn, rec, gates, mod, wba, wbr, wo, n2w, wg, wu, wd, fnw)


def _pick_tile(n, want):
    t = min(want, n)
    while n % t:
        t //= 2
    return t


def _dsa_inputs(kv, kiw):
    bsz, s, _ = kv.shape
    ki = kiw[:, :, :IDX_HEAD_DIM].astype(BF16)
    wt = kiw[:, :, IDX_HEAD_DIM:IDX_HEAD_DIM + N_IDX_HEADS].transpose(0, 2, 1)
    k = kv[:, :, :KV_WIDTH].reshape(bsz, s, N_KV_GROUPS, ATTN_HEAD_DIM).transpose(0, 2, 1, 3)
    vt = kv[:, :, KV_WIDTH:].reshape(bsz, s, N_KV_GROUPS, ATTN_HEAD_DIM).transpose(0, 2, 3, 1)
    ones = jnp.ones((bsz, N_KV_GROUPS, V_AUG_ROWS - ATTN_HEAD_DIM, s), BF16)
    return ki, wt, k, jnp.concatenate([vt, ones], axis=2)


def kernel(x, c, w_ada, b_ada, norm1_w, w_in, q_norm_w, w_uq, w_uq_idx, rel_bias, lb_logits,
           rec_norm_w, w_branch_attn, w_branch_rec, w_out, norm2_w, w_ffn_gate, w_ffn_up,
           w_ffn_down, final_norm_w):
    bsz, s, d = x.shape
    depth = w_in.shape[0]
    k_sel = min(TOPK_MAX, s // 4)
    tm = _pick_tile(s, TOKEN_TILE)
    tq = _pick_tile(s, Q_TILE)
    tk = max(tq, _pick_tile(s, K_TILE))
    chunk = _pick_tile(s, REC_CHUNK)
    lower_bounds = jnp.cumsum(jax.nn.softmax(lb_logits.astype(F32), axis=0), axis=0)
    bias = _near_bias(rel_bias, tq, tk)

    o_cq = 0
    o_k = o_cq + Q_LORA_RANK
    o_v = o_k + KV_WIDTH
    o_ki = o_v + KV_WIDTH
    o_wi = o_ki + IDX_HEAD_DIM
    o_rq = o_wi + N_IDX_HEADS
    o_rf = o_rq + REC_WIDTH
    o_ri = o_rf + REC_WIDTH
    o_rg = o_ri + REC_WIDTH
    o_gt = o_rg + REC_WIDTH

    for layer in range(depth):
        mod = _ada(c, w_ada[layer], b_ada[layer]).reshape(bsz, 6, d)
        wl = w_in[layer]
        pad_a = LANES - IDX_HEAD_DIM - N_IDX_HEADS
        wa = jnp.concatenate([wl[:, o_cq:o_rq], jnp.zeros((d, pad_a), wl.dtype)], axis=1).astype(BF16)
        wr = jnp.concatenate([wl[:, o_rq:o_rf], wl[:, o_ri:o_rg], wl[:, o_rg:o_gt],
                              wl[:, o_rf:o_ri]], axis=1).astype(BF16)
        wg = wl[:, o_gt:].astype(BF16)
        wuq = (w_uq[layer] * (ATTN_HEAD_DIM ** -0.5 * LOG2_E)).astype(BF16).T
        wui = (w_uq_idx[layer] * (IDX_HEAD_DIM ** -0.5)).astype(BF16).T

        qt, qit, kv, kiw, rec3, rf, gates = _in_proj(
            x, mod, norm1_w[layer].reshape(1, d), wa, wr, wg,
            q_norm_w[layer].reshape(1, Q_LORA_RANK), wuq, wui, tm)

        ki, wt, k, vt = _dsa_inputs(kv, kiw)
        attn = _dsa(qt, qit, wt, ki, k, vt, bias, tq=tq, tk=tk, k_sel=k_sel)
        rec = _hgrn2(rec3, rf, lower_bounds[layer], rec_norm_w[layer], chunk=chunk)

        x = _out_ffn(x, attn, rec, gates, mod,
                     w_branch_attn[layer].astype(BF16), w_branch_rec[layer].astype(BF16),
                     w_out[layer].astype(BF16), norm2_w[layer].reshape(1, d),
                     w_ffn_gate[layer].astype(BF16), w_ffn_up[layer].astype(BF16),
                     w_ffn_down[layer].astype(BF16),
                     final_norm_w.reshape(1, d), tm, layer == depth - 1)
    return x
```

```python
import functools
import math

import jax
import jax.numpy as jnp
from jax import lax
from jax.experimental import pallas as pl
from jax.experimental.pallas import tpu as pltpu

F32 = jnp.float32
BF16 = jnp.bfloat16

N_ATTN_HEADS = 8
ATTN_HEAD_DIM = 64
N_KV_GROUPS = 2
HEADS_PER_GROUP = N_ATTN_HEADS // N_KV_GROUPS
Q_LORA_RANK = 256
N_IDX_HEADS = 8
IDX_HEAD_DIM = 64
TOPK_MAX = 256
N_BUCKETS = 32
MAX_DISTANCE = 128
N_REC_HEADS = 4
REC_HEAD_DIM = 128
EPS = 1e-6

ATTN_WIDTH = N_ATTN_HEADS * ATTN_HEAD_DIM
KV_WIDTH = N_KV_GROUPS * ATTN_HEAD_DIM
REC_WIDTH = N_REC_HEADS * REC_HEAD_DIM

LANES = 128
SUBLANES = 8
BF16_SUBLANES = 16
VMEM_LIMIT_BYTES = 56 * 1024 * 1024

ADA_TILE = 1024
TOKEN_TILE = 512
Q_TILE = 256
K_TILE = 512
REC_CHUNK = 128
REC_TIME_TILE = 1024
REC_BAND = 4
NEG_BIG = -1e30
BISECT_CAP = 700
SNAP_AFTER = 14
FOLD_CHAINS = 4
PASS_ROWS = 128
FAST_FAR_TILES = 2
LOG2_E = math.log2(math.e)
MAX_SANE_DENOM = 1e30
MIN_SANE_DENOM = 1e-18
V_AUG_ROWS = ATTN_HEAD_DIM + BF16_SUBLANES


def _rms(x, w):
    return x * lax.rsqrt(jnp.mean(x * x, axis=-1, keepdims=True) + EPS) * w


def _sigmoid(x):
    return 0.5 * jnp.tanh(0.5 * x) + 0.5


def _sigmoid_rel(x):
    return 1.0 / (1.0 + jnp.exp(-x))


def _silu(x):
    return x * _sigmoid(x)


def _params(*sem):
    return pltpu.CompilerParams(dimension_semantics=sem, vmem_limit_bytes=VMEM_LIMIT_BYTES)


def _const_spec(shape):
    nd = len(shape)
    return pl.BlockSpec(shape, lambda *_: (0,) * nd, pipeline_mode=pl.Buffered(1))


def _ada_body(c_ref, w_ref, b_ref, o_ref):
    ca = _silu(c_ref[...])
    o_ref[...] = jnp.dot(ca, w_ref[...], precision=lax.Precision.HIGHEST,
                         preferred_element_type=F32) + b_ref[...]


def _ada(c, w, b):
    bsz, d = c.shape
    n = w.shape[1]
    tn = _pick_tile(n, ADA_TILE)
    return pl.pallas_call(
        _ada_body,
        grid=(n // tn,),
        in_specs=[pl.BlockSpec((bsz, d), lambda j: (0, 0)),
                  pl.BlockSpec((d, tn), lambda j: (0, j)),
                  pl.BlockSpec((1, tn), lambda j: (0, j))],
        out_specs=pl.BlockSpec((bsz, tn), lambda j: (0, j)),
        out_shape=jax.ShapeDtypeStruct((bsz, n), F32),
        compiler_params=_params("arbitrary"),
        name="ada",
    )(c, w, b.reshape(1, n))


def _in_proj_body(x_ref, mod_ref, n1_ref, wa_ref, wr_ref, wg_ref, qn_ref, wuq_ref, wui_ref,
                  qt_ref, qit_ref, kv_ref, kiw_ref, rec_ref, rf_ref, gates_ref):
    tm = x_ref.shape[1]
    x = x_ref[0]
    sh1 = mod_ref[0, 0:1, :]
    sc1 = mod_ref[0, 1:2, :]
    h = (_rms(x, n1_ref[...]) * (1.0 + sc1) + sh1).astype(BF16)

    pa = jnp.dot(h, wa_ref[...], preferred_element_type=F32)
    kv_ref[0] = pa[:, Q_LORA_RANK:Q_LORA_RANK + 2 * KV_WIDTH].astype(BF16)
    kiw_ref[0] = pa[:, Q_LORA_RANK + 2 * KV_WIDTH:]
    cqn = _rms(pa[:, :Q_LORA_RANK], qn_ref[...]).astype(BF16)
    nt = (((1,), (1,)), ((), ()))
    qf = lax.dot_general(wuq_ref[...], cqn, nt, preferred_element_type=F32)
    qt_ref[0] = qf.reshape(N_ATTN_HEADS, ATTN_HEAD_DIM, tm).astype(BF16)
    qif = lax.dot_general(wui_ref[...], cqn, nt, preferred_element_type=F32)
    qit_ref[0] = qif.reshape(N_IDX_HEADS, IDX_HEAD_DIM, tm).astype(BF16)

    pr = jnp.dot(h, wr_ref[...], preferred_element_type=F32)
    rec_ref[0] = pr[:, :3 * REC_WIDTH].astype(BF16)
    rf_ref[0] = pr[:, 3 * REC_WIDTH:]
    gates_ref[0] = jnp.dot(h, wg_ref[...], preferred_element_type=F32).astype(BF16)


def _in_proj(x, mod, n1w, wa, wr, wg, qnw, wuq, wui, tm):
    bsz, s, d = x.shape
    wa_n = wa.shape[1]
    grid = (bsz, s // tm)
    tok = lambda w: pl.BlockSpec((1, tm, w), lambda b, i: (b, i, 0))
    head = lambda nh, hd: pl.BlockSpec((1, nh, hd, tm), lambda b, i: (b, 0, 0, i))
    out_shape = (
        jax.ShapeDtypeStruct((bsz, N_ATTN_HEADS, ATTN_HEAD_DIM, s), BF16),
        jax.ShapeDtypeStruct((bsz, N_IDX_HEADS, IDX_HEAD_DIM, s), BF16),
        jax.ShapeDtypeStruct((bsz, s, 2 * KV_WIDTH), BF16),
        jax.ShapeDtypeStruct((bsz, s, LANES), F32),
        jax.ShapeDtypeStruct((bsz, s, 3 * REC_WIDTH), BF16),
        jax.ShapeDtypeStruct((bsz, s, REC_WIDTH), F32),
        jax.ShapeDtypeStruct((bsz, s, 2 * d), BF16),
    )
    out_specs = (head(N_ATTN_HEADS, ATTN_HEAD_DIM), head(N_IDX_HEADS, IDX_HEAD_DIM),
                 tok(2 * KV_WIDTH), tok(LANES), tok(3 * REC_WIDTH), tok(REC_WIDTH), tok(2 * d))
    return pl.pallas_call(
        _in_proj_body,
        grid=grid,
        in_specs=[tok(d),
                  pl.BlockSpec((1, 6, d), lambda b, i: (b, 0, 0)),
                  _const_spec((1, d)),
                  _const_spec((d, wa_n)),
                  _const_spec((d, 4 * REC_WIDTH)),
                  _const_spec((d, 2 * d)),
                  _const_spec((1, Q_LORA_RANK)),
                  _const_spec((ATTN_WIDTH, Q_LORA_RANK)),
                  _const_spec((N_IDX_HEADS * IDX_HEAD_DIM, Q_LORA_RANK))],
        out_specs=out_specs,
        out_shape=out_shape,
        compiler_params=_params("parallel", "parallel"),
        name="in_proj",
    )(x, mod, n1w, wa, wr, wg, qnw, wuq, wui)


def _near_bias(rel_bias, tq, tk):
    n = jnp.arange(MAX_DISTANCE + 1, dtype=jnp.int32)
    max_exact = N_BUCKETS // 2
    nf = jnp.maximum(n, 1).astype(F32)
    large = max_exact + (jnp.log(nf / max_exact) / math.log(MAX_DISTANCE / max_exact)
                         * (N_BUCKETS - max_exact)).astype(jnp.int32)
    bucket = jnp.where(n < max_exact, n, jnp.minimum(large, N_BUCKETS - 1))
    bucket = bucket.at[MAX_DISTANCE].set(N_BUCKETS - 1)
    ids = jnp.arange(N_BUCKETS, dtype=jnp.int32)
    start = jnp.sum((bucket[None, :] < ids[:, None]).astype(jnp.int32), axis=1)
    rb = (rel_bias.astype(F32) - rel_bias[N_BUCKETS - 1].astype(F32)[None, :]) * LOG2_E
    step = rb - jnp.concatenate([jnp.zeros_like(rb[:1]), rb[:-1]], axis=0)
    dist = (jnp.arange(tq, dtype=jnp.int32)[None, :]
            - jnp.arange(2 * tk, dtype=jnp.int32)[:, None] + tk)
    reached = (dist[None, :, :] >= start[:, None, None]).astype(F32)
    return jnp.einsum("bh,but->hut", step, reached, precision=lax.Precision.HIGHEST)


def _dsa_body(qit_ref, w_ref, qt_ref, ki_ref, k_ref, vt_ref, bias_ref, o_ref,
              sc_ref, acc_ref, m_ref, s_scr, p_scr, mask_scr, *, tq, tk, k_sel):
    i = pl.program_id(1)
    row0 = i * tq
    n_kt = lax.div(row0 + tq + (tk - 1), tk)
    j_near = lax.div(jnp.maximum(row0 - (MAX_DISTANCE - 1), 0), tk)
    krow = lax.broadcasted_iota(jnp.int32, (tk, tq), 0)
    qcol = lax.broadcasted_iota(jnp.int32, (tk, tq), 1)
    rel = krow - qcol
    kf = float(k_sel)

    def fold(a, op, group=SUBLANES):
        n = a.shape[0] // group
        a = a.reshape(n // FOLD_CHAINS, FOLD_CHAINS, group, tq)
        r = a[0]
        for t in range(1, n // FOLD_CHAINS):
            r = op(r, a[t])
        while r.shape[0] > 1:
            half = r.shape[0] // 2
            r = op(r[:half], r[half:])
        return r[0]

    w = w_ref[0] * (N_IDX_HEADS ** -0.5)

    def score_tile(j, carry):
        mx, mn = carry
        c0 = pl.multiple_of(j * tk, tk)
        kin = ki_ref[0, pl.ds(c0, tk), :]
        sc = None
        for h in range(N_IDX_HEADS):
            lg = jnp.dot(kin, qit_ref[0, h], preferred_element_type=F32)
            term = jnp.maximum(lg, 0.0) * w[h:h + 1, :]
            sc = term if sc is None else sc + term
        causal = rel <= (row0 - c0)
        scm = jnp.where(causal, sc, NEG_BIG)
        sc_ref[pl.ds(c0, tk), :] = scm
        mx = jnp.maximum(mx, fold(scm, jnp.maximum))
        mn = jnp.minimum(mn, fold(jnp.where(causal, sc, -NEG_BIG), jnp.minimum))
        return mx, mn

    def score_pair(t, carry):
        carry = score_tile(2 * t, carry)
        return score_tile(jnp.minimum(2 * t + 1, n_kt - 1), carry)

    mx8, mn8 = lax.fori_loop(0, lax.div(n_kt + 1, 2), score_pair,
                             (jnp.full((SUBLANES, tq), NEG_BIG, F32),
                              jnp.full((SUBLANES, tq), -NEG_BIG, F32)))
    mx = jnp.max(mx8, axis=0, keepdims=True)
    mn = jnp.min(mn8, axis=0, keepdims=True)

    def key_pass(fn, op, init):
        def one_tile(j, acc):
            c0 = pl.multiple_of(j * tk, tk)
            for u in range(tk // PASS_ROWS):
                s = sc_ref[pl.ds(c0 + u * PASS_ROWS, PASS_ROWS), :]
                acc = op(acc, fold(fn(s), op))
            return acc

        pairs = lax.shift_right_logical(n_kt, 1)
        acc = lax.fori_loop(0, pairs, lambda t, a: one_tile(t + pairs, one_tile(t, a)),
                            jnp.full((SUBLANES, tq), init, F32))
        return lax.cond((n_kt & 1) == 1, lambda a: one_tile(n_kt - 1, a), lambda a: a, acc)

    def count(pred):
        part = key_pass(lambda s: jnp.where(pred(s), 1.0, 0.0), jnp.add, 0.0)
        return jnp.sum(part, axis=0, keepdims=True)

    n_valid = (row0 + 1 + lax.broadcasted_iota(jnp.int32, (1, tq), 1)).astype(F32)
    take_all = n_valid <= kf
    cnt_top = count(lambda s: s >= mx)
    at_top = jnp.logical_and(jnp.logical_not(take_all), cnt_top >= kf)
    lo0 = jnp.where(take_all, NEG_BIG, jnp.where(at_top, mx, mn))
    done0 = jnp.logical_or(take_all, at_top)

    def bis_cond(st):
        it, lo, hi, c_lo, done_f = st
        return jnp.logical_and(it < BISECT_CAP, jnp.min(done_f) < 0.5)

    def halve(st):
        lo, hi, c_lo, done_f = st
        done = done_f > 0.5
        mid = lo + (hi - lo) * 0.5
        stuck = jnp.logical_or(mid <= lo, mid >= hi)
        cnt = count(lambda s: s >= mid)
        ge = cnt >= kf
        upd = jnp.logical_not(jnp.logical_or(done, stuck))
        raise_lo = jnp.logical_and(upd, ge)
        lo = jnp.where(raise_lo, mid, lo)
        c_lo = jnp.where(raise_lo, cnt, c_lo)
        hi = jnp.where(jnp.logical_and(upd, jnp.logical_not(ge)), mid, hi)
        done = jnp.logical_or(jnp.logical_or(done, stuck), jnp.logical_and(ge, cnt == kf))
        return lo, hi, c_lo, done.astype(F32)

    def snap(st):
        lo, hi, c_lo, done_f = st
        done = done_f > 0.5

        below = jnp.max(key_pass(lambda s: jnp.where(s < hi, s, NEG_BIG), jnp.maximum, NEG_BIG),
                        axis=0, keepdims=True)
        cnt = count(lambda s: s >= below)
        found = jnp.logical_and(jnp.logical_not(done), cnt >= kf)
        lower_hi = jnp.logical_and(jnp.logical_not(done), cnt < kf)
        lo = jnp.where(found, below, lo)
        c_lo = jnp.where(found, cnt, c_lo)
        hi = jnp.where(lower_hi, below, hi)
        return lo, hi, c_lo, jnp.logical_or(done, found).astype(F32)

    def bis_step(st):
        it, lo, hi, c_lo, done_f = st
        lo, hi, c_lo, done_f = lax.cond((it & 1) == 0, snap, halve, (lo, hi, c_lo, done_f))
        return it + 1, lo, hi, c_lo, done_f

    c_lo0 = jnp.where(at_top, cnt_top, n_valid)
    st = lax.fori_loop(0, SNAP_AFTER, lambda _, s: halve(s),
                       (lo0, mx, c_lo0, done0.astype(F32)))
    _, thr, _, c_thr, _ = lax.while_loop(bis_cond, bis_step, (jnp.int32(0),) + st)

    excess = jnp.logical_and(jnp.logical_not(take_all), c_thr > kf)

    @pl.when(jnp.max(excess.astype(F32)) > 0.5)
    def _():
        need = jnp.where(excess, kf - count(lambda s: s > thr), float(sc_ref.shape[0]))
        tril = (lax.broadcasted_iota(jnp.int32, (tk, tk), 1)
                <= lax.broadcasted_iota(jnp.int32, (tk, tk), 0)).astype(BF16)

        def drop_tile(j, seen):
            c0 = pl.multiple_of(j * tk, tk)
            s = sc_ref[pl.ds(c0, tk), :]
            eq = s == thr
            rank = seen + jnp.dot(tril, jnp.where(eq, 1.0, 0.0).astype(BF16),
                                  preferred_element_type=F32)
            sc_ref[pl.ds(c0, tk), :] = jnp.where(jnp.logical_and(eq, rank > need), NEG_BIG, s)
            return rank[tk - 1:tk, :]

        lax.fori_loop(0, n_kt, drop_tile, jnp.zeros((1, tq), F32))

    def logits(j, h, near, rows=tk):
        c0 = pl.multiple_of(j * tk, tk)
        g = h // HEADS_PER_GROUP
        s = jnp.dot(k_ref[0, g, pl.ds(c0, rows), :], qt_ref[0, h],
                    preferred_element_type=F32)
        if near:
            bias_row = pl.multiple_of(c0 - row0 + tk, tq)
            s = s + bias_ref[h, pl.ds(bias_row, tk), :]
        return s

    def selected(j, near, rows=tk):
        c0 = pl.multiple_of(j * tk, tk)
        keep = sc_ref[pl.ds(c0, rows), :] >= thr
        if near:
            keep = jnp.logical_and(keep, rel <= (row0 - c0))
        return keep

    def attend_exact(j, near):
        c0 = pl.multiple_of(j * tk, tk)
        mask_scr[:tk] = jnp.where(selected(j, near), 0.0, NEG_BIG)
        tile_max = []
        for h in range(N_ATTN_HEADS):
            s = logits(j, h, near) + mask_scr[:tk]
            s_scr[h] = s
            tile_max.append(jnp.max(fold(s, jnp.maximum), axis=0, keepdims=True))
        for h in range(N_ATTN_HEADS):
            g = h // HEADS_PER_GROUP
            m_old = m_ref[h]
            m_new = jnp.maximum(m_old, tile_max[h])
            alpha = jnp.exp2(m_old - m_new)
            p_scr[h, :tk] = jnp.exp2(s_scr[h] - m_new).astype(BF16)
            m_ref[h] = m_new
            pv = jnp.dot(vt_ref[0, g, :, pl.ds(c0, tk)], p_scr[h, :tk],
                         preferred_element_type=F32)
            acc_ref[h] = acc_ref[h] * alpha + pv

    def attend_fast(j, near, rows=tk):
        c0 = pl.multiple_of(j * tk, tk)
        mask_scr[:rows] = jnp.where(selected(j, near, rows), 0.0, NEG_BIG)
        for h in range(N_ATTN_HEADS):
            p_scr[h, :rows] = jnp.exp2(logits(j, h, near, rows) + mask_scr[:rows]).astype(BF16)
        for h in range(N_ATTN_HEADS):
            g = h // HEADS_PER_GROUP
            acc_ref[h] += jnp.dot(vt_ref[0, g, :, pl.ds(c0, rows)], p_scr[h, :rows],
                                  preferred_element_type=F32)

    def run_tiles(attend, far_tiles_per_step):
        acc_ref[...] = jnp.zeros(acc_ref.shape, F32)
        m_ref[...] = jnp.full(m_ref.shape, NEG_BIG, F32)
        w = far_tiles_per_step

        def far_step(t, c):
            if w == 1:
                attend(t, False)
            else:
                attend(t * w, False, w * tk)
            return c

        n_wide = lax.div(j_near, w)
        lax.fori_loop(0, n_wide, far_step, 0)
        for u in range(w - 1):
            @pl.when(n_wide * w + u < j_near)
            def _():
                attend(n_wide * w + u, False)
        for u in range(2):
            @pl.when(j_near + u < n_kt)
            def _():
                attend(j_near + u, True)

    run_tiles(attend_fast, FAST_FAR_TILES)
    denom = acc_ref[:, ATTN_HEAD_DIM:ATTN_HEAD_DIM + 1, :]
    sane = jnp.logical_and(denom > MIN_SANE_DENOM, denom < MAX_SANE_DENOM)

    @pl.when(jnp.min(sane.astype(F32)) < 0.5)
    def _():
        run_tiles(attend_exact, 1)

    outs = []
    for h in range(N_ATTN_HEADS):
        a = acc_ref[h]
        outs.append(a[:ATTN_HEAD_DIM, :] / a[ATTN_HEAD_DIM:ATTN_HEAD_DIM + 1, :])
    o_ref[0] = jnp.concatenate(outs, axis=0).T.astype(BF16)


def _dsa(qt, qit, wt, ki, k, vt, bias, *, tq, tk, k_sel):
    bsz, _, _, s = qt.shape
    assert tq >= MAX_DISTANCE and tk % tq == 0 and s % tk == 0
    body = functools.partial(_dsa_body, tq=tq, tk=tk, k_sel=k_sel)
    return pl.pallas_call(
        body,
        grid=(bsz, s // tq),
        in_specs=[pl.BlockSpec((1, N_IDX_HEADS, IDX_HEAD_DIM, tq), lambda b, i: (b, 0, 0, i)),
                  pl.BlockSpec((1, N_IDX_HEADS, tq), lambda b, i: (b, 0, i)),
                  pl.BlockSpec((1, N_ATTN_HEADS, ATTN_HEAD_DIM, tq), lambda b, i: (b, 0, 0, i)),
                  pl.BlockSpec((1, s, IDX_HEAD_DIM), lambda b, i: (b, 0, 0)),
                  pl.BlockSpec((1, N_KV_GROUPS, s, ATTN_HEAD_DIM), lambda b, i: (b, 0, 0, 0)),
                  pl.BlockSpec((1, N_KV_GROUPS, V_AUG_ROWS, s), lambda b, i: (b, 0, 0, 0)),
                  _const_spec(bias.shape)],
        out_specs=pl.BlockSpec((1, tq, ATTN_WIDTH), lambda b, i: (b, i, 0)),
        out_shape=jax.ShapeDtypeStruct((bsz, s, ATTN_WIDTH), BF16),
        scratch_shapes=[pltpu.VMEM((s, tq), F32),
                        pltpu.VMEM((N_ATTN_HEADS, V_AUG_ROWS, tq), F32),
                        pltpu.VMEM((N_ATTN_HEADS, 1, tq), F32),
                        pltpu.VMEM((N_ATTN_HEADS, tk, tq), F32),
                        pltpu.VMEM((N_ATTN_HEADS, FAST_FAR_TILES * tk, tq), BF16),
                        pltpu.VMEM((FAST_FAR_TILES * tk, tq), F32)],
        compiler_params=_params("parallel", "arbitrary"),
        name="dsa",
    )(qit, wt, qt, ki, k, vt, bias)


def _split3(a):
    hi = a.astype(BF16)
    r1 = a - hi.astype(F32)
    mid = r1.astype(BF16)
    lo = (r1 - mid.astype(F32)).astype(BF16)
    return hi, mid, lo


def _hgrn2_body(rq_ref, ri_ref, rg_ref, rf_ref, lb_ref, nw_ref, o_ref,
                state_scr, tril_scr, lvl_scr, *, chunk):
    t_len = rq_ref.shape[1]
    c = chunk
    d = REC_HEAD_DIM
    nw = nw_ref[...]
    eye = (lax.broadcasted_iota(jnp.int32, (d, d), 0)
           == lax.broadcasted_iota(jnp.int32, (d, d), 1)).astype(F32)
    sub = lax.broadcasted_iota(jnp.int32, (c, d), 0) & (REC_BAND - 1)
    level_sizes = []
    b = REC_BAND
    while b < c:
        level_sizes.append(b)
        b *= 2

    @pl.when(pl.program_id(1) == 0)
    def _():
        state_scr[...] = jnp.zeros(state_scr.shape, F32)
        ti = lax.broadcasted_iota(jnp.int32, (c, c), 0)
        si = lax.broadcasted_iota(jnp.int32, (c, c), 1)
        tril_scr[...] = (si <= ti).astype(BF16)
        for li, b in enumerate(level_sizes):
            blk = 2 * b
            sh = blk.bit_length() - 1
            m = jnp.logical_and(
                lax.shift_right_logical(ti, sh) == lax.shift_right_logical(si, sh),
                jnp.logical_and((ti & (blk - 1)) >= b, (si & (blk - 1)) < b))
            lvl_scr[li] = m.astype(F32)

    def head_step(hh, t0):
        cols = slice(hh * d, (hh + 1) * d)
        lb = lb_ref[:, cols]
        state = state_scr[hh]
        f = lb + (1.0 - lb) * _sigmoid_rel(rf_ref[0, pl.ds(t0, c), cols])
        g = jnp.log(f)
        kk = 1.0 - f
        qq = _silu(rq_ref[0, pl.ds(t0, c), cols].astype(F32))
        vb = ri_ref[0, pl.ds(t0, c), cols]
        vv = vb.astype(F32)

        g3 = _split3(g)
        tril = tril_scr[...]
        a = (jnp.dot(tril, g3[0], preferred_element_type=F32)
             + jnp.dot(tril, g3[1], preferred_element_type=F32)
             + jnp.dot(tril, g3[2], preferred_element_type=F32))
        a_last = a[c - 1:c, :]

        o = jnp.dot((qq * jnp.exp(a)).astype(BF16), state.astype(BF16),
                    preferred_element_type=F32)

        p = jnp.zeros((c, c), F32)
        for li, b in enumerate(level_sizes):
            blk = 2 * b
            r = jnp.broadcast_to(a.reshape(c // blk, blk, d)[:, b - 1:b, :],
                                 (c // blk, blk, d)).reshape(c, d)
            qt = (qq * jnp.exp(jnp.minimum(a - r, 0.0))).astype(BF16)
            kt = (kk * jnp.exp(jnp.minimum(r - a, 0.0))).astype(BF16)
            pb = lax.dot_general(qt, kt, (((1,), (1,)), ((), ())), preferred_element_type=F32)
            p = p + pb * lvl_scr[li]
        o = o + jnp.dot(p.astype(BF16), vb, preferred_element_type=F32)

        def back(x, dl):
            if dl == 0:
                return x
            return pltpu.roll(x.reshape(c // SUBLANES, SUBLANES, d), dl, axis=1).reshape(c, d)

        for dl in range(REC_BAND):
            a_s = back(a, dl)
            k_s = back(kk, dl)
            v_s = back(vv, dl)
            e = jnp.exp(jnp.where(sub >= dl, a - a_s, NEG_BIG))
            pd = jnp.sum(qq * k_s * e, axis=1, keepdims=True)
            o = o + pd * v_s

        kd = (kk * jnp.exp(a_last - a)).astype(BF16)
        upd = lax.dot_general(kd, vb, (((0,), (0,)), ((), ())), preferred_element_type=F32)
        e_col = jnp.sum(eye * jnp.exp(a_last), axis=1, keepdims=True)
        state_scr[hh] = e_col * state + upd

        y = _rms(o, nw) * _silu(rg_ref[0, pl.ds(t0, c), cols].astype(F32))
        o_ref[0, pl.ds(t0, c), cols] = y.astype(BF16)

    def step(n, carry):
        t0 = pl.multiple_of(n * c, c)
        for hh in range(N_REC_HEADS):
            head_step(hh, t0)
        return carry

    lax.fori_loop(0, t_len // c, step, 0)


def _hgrn2(rec3, rf, lb, nw, *, chunk):
    bsz, s, _ = rec3.shape
    d = REC_HEAD_DIM
    h = N_REC_HEADS
    body = functools.partial(_hgrn2_body, chunk=chunk)
    tt = _pick_tile(s, REC_TIME_TILE)
    col = lambda part: pl.BlockSpec((1, tt, REC_WIDTH), lambda b, t: (b, t, part))
    return pl.pallas_call(
        body,
        grid=(bsz, s // tt),
        in_specs=[col(0), col(1), col(2), col(0),
                  pl.BlockSpec((1, REC_WIDTH), lambda b, t: (0, 0)),
                  pl.BlockSpec((1, d), lambda b, t: (0, 0))],
        out_specs=col(0),
        out_shape=jax.ShapeDtypeStruct((bsz, s, REC_WIDTH), BF16),
        scratch_shapes=[pltpu.VMEM((h, d, d), F32)]
        + [pltpu.VMEM((chunk, chunk), BF16),
           pltpu.VMEM(((chunk // REC_BAND).bit_length() - 1, chunk, chunk), F32)],
        compiler_params=_params("parallel", "arbitrary"),
        name="hgrn2",
    )(rec3, rec3, rec3, rf, lb.reshape(1, REC_WIDTH), nw.reshape(1, d))


def _out_ffn_body(x_ref, attn_ref, rec_ref, gates_ref, mod_ref, wba_ref, wbr_ref, wo_ref,
                  n2_ref, wg_ref, wu_ref, wd_ref, fn_ref, o_ref, *, final):
    d = x_ref.shape[2]
    x = x_ref[0]
    g1 = mod_ref[0, 2:3, :]
    sh2 = mod_ref[0, 3:4, :]
    sc2 = mod_ref[0, 4:5, :]
    g2 = mod_ref[0, 5:6, :]
    ya = jnp.dot(attn_ref[0], wba_ref[...], preferred_element_type=F32)
    yr = jnp.dot(rec_ref[0], wbr_ref[...], preferred_element_type=F32)
    gates = gates_ref[0].astype(F32)
    mix = (_sigmoid(gates[:, :d]) * ya + _sigmoid(gates[:, d:]) * yr).astype(BF16)
    x1 = x + g1 * jnp.dot(mix, wo_ref[...], preferred_element_type=F32)
    h2 = (_rms(x1, n2_ref[...]) * (1.0 + sc2) + sh2).astype(BF16)
    gate = jnp.dot(h2, wg_ref[...], preferred_element_type=F32)
    up = jnp.dot(h2, wu_ref[...], preferred_element_type=F32)
    act = (_silu(gate) * up).astype(BF16)
    x2 = x1 + g2 * jnp.dot(act, wd_ref[...], preferred_element_type=F32)
    o_ref[0] = _rms(x2, fn_ref[...]) if final else x2


def _out_ffn(x, attn, rec, gates, mod, wba, wbr, wo, n2w, wg, wu, wd, fnw, tm, final):
    bsz, s, d = x.shape
    hid = wg.shape[1]
    tok = lambda w: pl.BlockSpec((1, tm, w), lambda b, i: (b, i, 0))
    return pl.pallas_call(
        functools.partial(_out_ffn_body, final=final),
        grid=(bsz, s // tm),
        in_specs=[tok(d), tok(ATTN_WIDTH), tok(REC_WIDTH), tok(2 * d),
                  pl.BlockSpec((1, 6, d), lambda b, i: (b, 0, 0)),
                  _const_spec((ATTN_WIDTH, d)), _const_spec((REC_WIDTH, d)), _const_spec((d, d)),
                  _const_spec((1, d)), _const_spec((d, hid)), _const_spec((d, hid)),
                  _const_spec((hid, d)), _const_spec((1, d))],
        out_specs=tok(d),
        out_shape=jax.ShapeDtypeStruct((bsz, s, d), F32),
        compiler_params=_params("parallel", "parallel"),
        name="out_ffn",
    )(x, attn, rec, gates, mod, wba, wbr, wo, n2w, wg, wu, wd, fnw)


def _pick_tile(n, want):
    t = min(want, n)
    while n % t:
        t //= 2
    return t


def _dsa_inputs(kv, kiw):
    bsz, s, _ = kv.shape
    ki = kiw[:, :, :IDX_HEAD_DIM].astype(BF16)
    wt = kiw[:, :, IDX_HEAD_DIM:IDX_HEAD_DIM + N_IDX_HEADS].transpose(0, 2, 1)
    k = kv[:, :, :KV_WIDTH].reshape(bsz, s, N_KV_GROUPS, ATTN_HEAD_DIM).transpose(0, 2, 1, 3)
    vt = kv[:, :, KV_WIDTH:].reshape(bsz, s, N_KV_GROUPS, ATTN_HEAD_DIM).transpose(0, 2, 3, 1)
    ones = jnp.ones((bsz, N_KV_GROUPS, V_AUG_ROWS - ATTN_HEAD_DIM, s), BF16)
    return ki, wt, k, jnp.concatenate([vt, ones], axis=2)


def kernel(x, c, w_ada, b_ada, norm1_w, w_in, q_norm_w, w_uq, w_uq_idx, rel_bias, lb_logits,
           rec_norm_w, w_branch_attn, w_branch_rec, w_out, norm2_w, w_ffn_gate, w_ffn_up,
           w_ffn_down, final_norm_w):
    bsz, s, d = x.shape
    depth = w_in.shape[0]
    k_sel = min(TOPK_MAX, s // 4)
    tm = _pick_tile(s, TOKEN_TILE)
    tq = _pick_tile(s, Q_TILE)
    tk = max(tq, _pick_tile(s, K_TILE))
    chunk = _pick_tile(s, REC_CHUNK)
    lower_bounds = jnp.cumsum(jax.nn.softmax(lb_logits.astype(F32), axis=0), axis=0)
    bias = _near_bias(rel_bias, tq, tk)

    o_cq = 0
    o_k = o_cq + Q_LORA_RANK
    o_v = o_k + KV_WIDTH
    o_ki = o_v + KV_WIDTH
    o_wi = o_ki + IDX_HEAD_DIM
    o_rq = o_wi + N_IDX_HEADS
    o_rf = o_rq + REC_WIDTH
    o_ri = o_rf + REC_WIDTH
    o_rg = o_ri + REC_WIDTH
    o_gt = o_rg + REC_WIDTH

    for layer in range(depth):
        mod = _ada(c, w_ada[layer], b_ada[layer]).reshape(bsz, 6, d)
        wl = w_in[layer]
        pad_a = LANES - IDX_HEAD_DIM - N_IDX_HEADS
        wa = jnp.concatenate([wl[:, o_cq:o_rq], jnp.zeros((d, pad_a), wl.dtype)], axis=1).astype(BF16)
        wr = jnp.concatenate([wl[:, o_rq:o_rf], wl[:, o_ri:o_rg], wl[:, o_rg:o_gt],
                              wl[:, o_rf:o_ri]], axis=1).astype(BF16)
        wg = wl[:, o_gt:].astype(BF16)
        wuq = (w_uq[layer] * (ATTN_HEAD_DIM ** -0.5 * LOG2_E)).astype(BF16).T
        wui = (w_uq_idx[layer] * (IDX_HEAD_DIM ** -0.5)).astype(BF16).T

        qt, qit, kv, kiw, rec3, rf, gates = _in_proj(
            x, mod, norm1_w[layer].reshape(1, d), wa, wr, wg,
            q_norm_w[layer].reshape(1, Q_LORA_RANK), wuq, wui, tm)

        ki, wt, k, vt = _dsa_inputs(kv, kiw)
        attn = _dsa(qt, qit, wt, ki, k, vt, bias, tq=tq, tk=tk, k_sel=k_sel)
        rec = _hgrn2(rec3, rf, lower_bounds[layer], rec_norm_w[layer], chunk=chunk)

        x = _out_ffn(x, attn, rec, gates, mod,
                     w_branch_attn[layer].astype(BF16), w_branch_rec[layer].astype(BF16),
                     w_out[layer].astype(BF16), norm2_w[layer].reshape(1, d),
                     w_ffn_gate[layer].astype(BF16), w_ffn_up[layer].astype(BF16),
                     w_ffn_down[layer].astype(BF16),
                     final_norm_w.reshape(1, d), tm, layer == depth - 1)
    return x
```

```python
import functools
import math

import jax
import jax.numpy as jnp
from jax import lax
from jax.experimental import pallas as pl
from jax.experimental.pallas import tpu as pltpu

F32 = jnp.float32
BF16 = jnp.bfloat16

N_ATTN_HEADS = 8
ATTN_HEAD_DIM = 64
N_KV_GROUPS = 2
HEADS_PER_GROUP = N_ATTN_HEADS // N_KV_GROUPS
Q_LORA_RANK = 256
N_IDX_HEADS = 8
IDX_HEAD_DIM = 64
TOPK_MAX = 256
N_BUCKETS = 32
MAX_DISTANCE = 128
N_REC_HEADS = 4
REC_HEAD_DIM = 128
EPS = 1e-6

ATTN_WIDTH = N_ATTN_HEADS * ATTN_HEAD_DIM
KV_WIDTH = N_KV_GROUPS * ATTN_HEAD_DIM
REC_WIDTH = N_REC_HEADS * REC_HEAD_DIM

LANES = 128
SUBLANES = 8
BF16_SUBLANES = 16
VMEM_LIMIT_BYTES = 56 * 1024 * 1024

ADA_TILE = 1024
TOKEN_TILE = 512
Q_TILE = 256
K_TILE = 512
REC_CHUNK = 128
REC_TIME_TILE = 1024
REC_BAND = 4
NEG_BIG = -1e30
BISECT_CAP = 700
SNAP_AFTER = 14
FOLD_CHAINS = 4
PASS_ROWS = 128
FAST_FAR_TILES = 2
LOG2_E = math.log2(math.e)
MAX_SANE_DENOM = 1e30
MIN_SANE_DENOM = 1e-18
V_AUG_ROWS = ATTN_HEAD_DIM + BF16_SUBLANES


def _rms(x, w):
    return x * lax.rsqrt(jnp.mean(x * x, axis=-1, keepdims=True) + EPS) * w


def _sigmoid(x):
    return 0.5 * jnp.tanh(0.5 * x) + 0.5


def _sigmoid_rel(x):
    return 1.0 / (1.0 + jnp.exp(-x))


def _silu(x):
    return x * _sigmoid(x)


def _params(*sem):
    return pltpu.CompilerParams(dimension_semantics=sem, vmem_limit_bytes=VMEM_LIMIT_BYTES)


def _const_spec(shape):
    nd = len(shape)
    return pl.BlockSpec(shape, lambda *_: (0,) * nd, pipeline_mode=pl.Buffered(1))


def _ada_body(c_ref, w_ref, b_ref, o_ref):
    ca = _silu(c_ref[...])
    o_ref[...] = jnp.dot(ca, w_ref[...], precision=lax.Precision.HIGHEST,
                         preferred_element_type=F32) + b_ref[...]


def _ada(c, w, b):
    bsz, d = c.shape
    n = w.shape[1]
    tn = _pick_tile(n, ADA_TILE)
    return pl.pallas_call(
        _ada_body,
        grid=(n // tn,),
        in_specs=[pl.BlockSpec((bsz, d), lambda j: (0, 0)),
                  pl.BlockSpec((d, tn), lambda j: (0, j)),
                  pl.BlockSpec((1, tn), lambda j: (0, j))],
        out_specs=pl.BlockSpec((bsz, tn), lambda j: (0, j)),
        out_shape=jax.ShapeDtypeStruct((bsz, n), F32),
        compiler_params=_params("arbitrary"),
        name="ada",
    )(c, w, b.reshape(1, n))


def _in_proj_body(x_ref, mod_ref, n1_ref, wa_ref, wr_ref, wg_ref, qn_ref, wuq_ref, wui_ref,
                  qt_ref, qit_ref, k_ref, vt_ref, ki_ref, wt_ref, rec_ref, rf_ref, gates_ref):
    tm = x_ref.shape[1]
    x = x_ref[0]
    sh1 = mod_ref[0, 0:1, :]
    sc1 = mod_ref[0, 1:2, :]
    h = (_rms(x, n1_ref[...]) * (1.0 + sc1) + sh1).astype(BF16)

    pa = jnp.dot(h, wa_ref[...], preferred_element_type=F32)
    o_k, o_v, o_ki = Q_LORA_RANK, Q_LORA_RANK + KV_WIDTH, Q_LORA_RANK + 2 * KV_WIDTH
    v_t = pa[:, o_v:o_ki].T
    kiw_t = pa[:, o_ki:].T
    for g in range(N_KV_GROUPS):
        k_ref[0, g] = pa[:, o_k + g * ATTN_HEAD_DIM:o_k + (g + 1) * ATTN_HEAD_DIM].astype(BF16)
        vt_ref[0, g, :ATTN_HEAD_DIM, :] = v_t[g * ATTN_HEAD_DIM:(g + 1) * ATTN_HEAD_DIM].astype(BF16)
        vt_ref[0, g, ATTN_HEAD_DIM:, :] = jnp.ones((V_AUG_ROWS - ATTN_HEAD_DIM, tm), BF16)
    ki_ref[0] = pa[:, o_ki:o_ki + IDX_HEAD_DIM].astype(BF16)
    wt_ref[0] = kiw_t[IDX_HEAD_DIM:IDX_HEAD_DIM + N_IDX_HEADS]
    cqn = _rms(pa[:, :Q_LORA_RANK], qn_ref[...]).astype(BF16)
    nt = (((1,), (1,)), ((), ()))
    qf = lax.dot_general(wuq_ref[...], cqn, nt, preferred_element_type=F32)
    qt_ref[0] = qf.reshape(N_ATTN_HEADS, ATTN_HEAD_DIM, tm).astype(BF16)
    qif = lax.dot_general(wui_ref[...], cqn, nt, preferred_element_type=F32)
    qit_ref[0] = qif.reshape(N_IDX_HEADS, IDX_HEAD_DIM, tm).astype(BF16)

    pr = jnp.dot(h, wr_ref[...], preferred_element_type=F32)
    rec_ref[0] = pr[:, :3 * REC_WIDTH].astype(BF16)
    rf_ref[0] = pr[:, 3 * REC_WIDTH:]
    gates_ref[0] = jnp.dot(h, wg_ref[...], preferred_element_type=F32).astype(BF16)


def _in_proj(x, mod, n1w, wa, wr, wg, qnw, wuq, wui, tm):
    bsz, s, d = x.shape
    wa_n = wa.shape[1]
    grid = (bsz, s // tm)
    tok = lambda w: pl.BlockSpec((1, tm, w), lambda b, i: (b, i, 0))
    head = lambda nh, hd: pl.BlockSpec((1, nh, hd, tm), lambda b, i: (b, 0, 0, i))
    out_shape = (
        jax.ShapeDtypeStruct((bsz, N_ATTN_HEADS, ATTN_HEAD_DIM, s), BF16),
        jax.ShapeDtypeStruct((bsz, N_IDX_HEADS, IDX_HEAD_DIM, s), BF16),
        jax.ShapeDtypeStruct((bsz, N_KV_GROUPS, s, ATTN_HEAD_DIM), BF16),
        jax.ShapeDtypeStruct((bsz, N_KV_GROUPS, V_AUG_ROWS, s), BF16),
        jax.ShapeDtypeStruct((bsz, s, IDX_HEAD_DIM), BF16),
        jax.ShapeDtypeStruct((bsz, N_IDX_HEADS, s), F32),
        jax.ShapeDtypeStruct((bsz, s, 3 * REC_WIDTH), BF16),
        jax.ShapeDtypeStruct((bsz, s, REC_WIDTH), F32),
        jax.ShapeDtypeStruct((bsz, s, 2 * d), BF16),
    )
    out_specs = (head(N_ATTN_HEADS, ATTN_HEAD_DIM), head(N_IDX_HEADS, IDX_HEAD_DIM),
                 pl.BlockSpec((1, N_KV_GROUPS, tm, ATTN_HEAD_DIM), lambda b, i: (b, 0, i, 0)),
                 head(N_KV_GROUPS, V_AUG_ROWS),
                 tok(IDX_HEAD_DIM),
                 pl.BlockSpec((1, N_IDX_HEADS, tm), lambda b, i: (b, 0, i)),
                 tok(3 * REC_WIDTH), tok(REC_WIDTH), tok(2 * d))
    return pl.pallas_call(
        _in_proj_body,
        grid=grid,
        in_specs=[tok(d),
                  pl.BlockSpec((1, 6, d), lambda b, i: (b, 0, 0)),
                  _const_spec((1, d)),
                  _const_spec((d, wa_n)),
                  _const_spec((d, 4 * REC_WIDTH)),
                  _const_spec((d, 2 * d)),
                  _const_spec((1, Q_LORA_RANK)),
                  _const_spec((ATTN_WIDTH, Q_LORA_RANK)),
                  _const_spec((N_IDX_HEADS * IDX_HEAD_DIM, Q_LORA_RANK))],
        out_specs=out_specs,
        out_shape=out_shape,
        compiler_params=_params("parallel", "parallel"),
        name="in_proj",
    )(x, mod, n1w, wa, wr, wg, qnw, wuq, wui)


def _near_bias(rel_bias, tq, tk):
    n = jnp.arange(MAX_DISTANCE + 1, dtype=jnp.int32)
    max_exact = N_BUCKETS // 2
    nf = jnp.maximum(n, 1).astype(F32)
    large = max_exact + (jnp.log(nf / max_exact) / math.log(MAX_DISTANCE / max_exact)
                         * (N_BUCKETS - max_exact)).astype(jnp.int32)
    bucket = jnp.where(n < max_exact, n, jnp.minimum(large, N_BUCKETS - 1))
    bucket = bucket.at[MAX_DISTANCE].set(N_BUCKETS - 1)
    ids = jnp.arange(N_BUCKETS, dtype=jnp.int32)
    start = jnp.sum((bucket[None, :] < ids[:, None]).astype(jnp.int32), axis=1)
    rb = (rel_bias.astype(F32) - rel_bias[N_BUCKETS - 1].astype(F32)[None, :]) * LOG2_E
    step = rb - jnp.concatenate([jnp.zeros_like(rb[:1]), rb[:-1]], axis=0)
    dist = (jnp.arange(tq, dtype=jnp.int32)[None, :]
            - jnp.arange(2 * tk, dtype=jnp.int32)[:, None] + tk)
    reached = (dist[None, :, :] >= start[:, None, None]).astype(F32)
    return jnp.einsum("bh,but->hut", step, reached, precision=lax.Precision.HIGHEST)


def _dsa_body(qit_ref, w_ref, qt_ref, ki_ref, k_ref, vt_ref, bias_ref, o_ref,
              sc_ref, acc_ref, m_ref, s_scr, p_scr, mask_scr, *, tq, tk, k_sel):
    i = pl.program_id(1)
    row0 = i * tq
    n_kt = lax.div(row0 + tq + (tk - 1), tk)
    j_near = lax.div(jnp.maximum(row0 - (MAX_DISTANCE - 1), 0), tk)
    krow = lax.broadcasted_iota(jnp.int32, (tk, tq), 0)
    qcol = lax.broadcasted_iota(jnp.int32, (tk, tq), 1)
    rel = krow - qcol
    kf = float(k_sel)

    def fold(a, op, group=SUBLANES):
        n = a.shape[0] // group
        a = a.reshape(n // FOLD_CHAINS, FOLD_CHAINS, group, tq)
        r = a[0]
        for t in range(1, n // FOLD_CHAINS):
            r = op(r, a[t])
        while r.shape[0] > 1:
            half = r.shape[0] // 2
            r = op(r[:half], r[half:])
        return r[0]

    w = w_ref[0] * (N_IDX_HEADS ** -0.5)

    def score_tile(j, carry):
        mx, mn = carry
        c0 = pl.multiple_of(j * tk, tk)
        kin = ki_ref[0, pl.ds(c0, tk), :]
        sc = None
        for h in range(N_IDX_HEADS):
            lg = jnp.dot(kin, qit_ref[0, h], preferred_element_type=F32)
            term = jnp.maximum(lg, 0.0) * w[h:h + 1, :]
            sc = term if sc is None else sc + term
        causal = rel <= (row0 - c0)
        scm = jnp.where(causal, sc, NEG_BIG)
        sc_ref[pl.ds(c0, tk), :] = scm
        mx = jnp.maximum(mx, fold(scm, jnp.maximum))
        mn = jnp.minimum(mn, fold(jnp.where(causal, sc, -NEG_BIG), jnp.minimum))
        return mx, mn

    def score_pair(t, carry):
        carry = score_tile(2 * t, carry)
        return score_tile(jnp.minimum(2 * t + 1, n_kt - 1), carry)

    mx8, mn8 = lax.fori_loop(0, lax.div(n_kt + 1, 2), score_pair,
                             (jnp.full((SUBLANES, tq), NEG_BIG, F32),
                              jnp.full((SUBLANES, tq), -NEG_BIG, F32)))
    mx = jnp.max(mx8, axis=0, keepdims=True)
    mn = jnp.min(mn8, axis=0, keepdims=True)

    def key_pass(fn, op, init):
        def one_tile(j, acc):
            c0 = pl.multiple_of(j * tk, tk)
            for u in range(tk // PASS_ROWS):
                s = sc_ref[pl.ds(c0 + u * PASS_ROWS, PASS_ROWS), :]
                acc = op(acc, fold(fn(s), op))
            return acc

        pairs = lax.shift_right_logical(n_kt, 1)
        acc = lax.fori_loop(0, pairs, lambda t, a: one_tile(t + pairs, one_tile(t, a)),
                            jnp.full((SUBLANES, tq), init, F32))
        return lax.cond((n_kt & 1) == 1, lambda a: one_tile(n_kt - 1, a), lambda a: a, acc)

    def count(pred):
        part = key_pass(lambda s: jnp.where(pred(s), 1.0, 0.0), jnp.add, 0.0)
        return jnp.sum(part, axis=0, keepdims=True)

    n_valid = (row0 + 1 + lax.broadcasted_iota(jnp.int32, (1, tq), 1)).astype(F32)
    take_all = n_valid <= kf
    cnt_top = count(lambda s: s >= mx)
    at_top = jnp.logical_and(jnp.logical_not(take_all), cnt_top >= kf)
    lo0 = jnp.where(take_all, NEG_BIG, jnp.where(at_top, mx, mn))
    done0 = jnp.logical_or(take_all, at_top)

    def bis_cond(st):
        it, lo, hi, c_lo, done_f = st
        return jnp.logical_and(it < BISECT_CAP, jnp.min(done_f) < 0.5)

    def halve(st):
        lo, hi, c_lo, done_f = st
        done = done_f > 0.5
        mid = lo + (hi - lo) * 0.5
        stuck = jnp.logical_or(mid <= lo, mid >= hi)
        cnt = count(lambda s: s >= mid)
        ge = cnt >= kf
        upd = jnp.logical_not(jnp.logical_or(done, stuck))
        raise_lo = jnp.logical_and(upd, ge)
        lo = jnp.where(raise_lo, mid, lo)
        c_lo = jnp.where(raise_lo, cnt, c_lo)
        hi = jnp.where(jnp.logical_and(upd, jnp.logical_not(ge)), mid, hi)
        done = jnp.logical_or(jnp.logical_or(done, stuck), jnp.logical_and(ge, cnt == kf))
        return lo, hi, c_lo, done.astype(F32)

    def snap(st):
        lo, hi, c_lo, done_f = st
        done = done_f > 0.5

        below = jnp.max(key_pass(lambda s: jnp.where(s < hi, s, NEG_BIG), jnp.maximum, NEG_BIG),
                        axis=0, keepdims=True)
        cnt = count(lambda s: s >= below)
        found = jnp.logical_and(jnp.logical_not(done), cnt >= kf)
        lower_hi = jnp.logical_and(jnp.logical_not(done), cnt < kf)
        lo = jnp.where(found, below, lo)
        c_lo = jnp.where(found, cnt, c_lo)
        hi = jnp.where(lower_hi, below, hi)
        return lo, hi, c_lo, jnp.logical_or(done, found).astype(F32)

    def bis_step(st):
        it, lo, hi, c_lo, done_f = st
        lo, hi, c_lo, done_f = lax.cond((it & 1) == 0, snap, halve, (lo, hi, c_lo, done_f))
        return it + 1, lo, hi, c_lo, done_f

    c_lo0 = jnp.where(at_top, cnt_top, n_valid)
    st = lax.fori_loop(0, SNAP_AFTER, lambda _, s: halve(s),
                       (lo0, mx, c_lo0, done0.astype(F32)))
    _, thr, _, c_thr, _ = lax.while_loop(bis_cond, bis_step, (jnp.int32(0),) + st)

    excess = jnp.logical_and(jnp.logical_not(take_all), c_thr > kf)

    @pl.when(jnp.max(excess.astype(F32)) > 0.5)
    def _():
        need = jnp.where(excess, kf - count(lambda s: s > thr), float(sc_ref.shape[0]))
        tril = (lax.broadcasted_iota(jnp.int32, (tk, tk), 1)
                <= lax.broadcasted_iota(jnp.int32, (tk, tk), 0)).astype(BF16)

        def drop_tile(j, seen):
            c0 = pl.multiple_of(j * tk, tk)
            s = sc_ref[pl.ds(c0, tk), :]
            eq = s == thr
            rank = seen + jnp.dot(tril, jnp.where(eq, 1.0, 0.0).astype(BF16),
                                  preferred_element_type=F32)
            sc_ref[pl.ds(c0, tk), :] = jnp.where(jnp.logical_and(eq, rank > need), NEG_BIG, s)
            return rank[tk - 1:tk, :]

        lax.fori_loop(0, n_kt, drop_tile, jnp.zeros((1, tq), F32))

    def logits(j, h, near, rows=tk):
        c0 = pl.multiple_of(j * tk, tk)
        g = h // HEADS_PER_GROUP
        s = jnp.dot(k_ref[0, g, pl.ds(c0, rows), :], qt_ref[0, h],
                    preferred_element_type=F32)
        if near:
            bias_row = pl.multiple_of(c0 - row0 + tk, tq)
            s = s + bias_ref[h, pl.ds(bias_row, tk), :]
        return s

    def selected(j, near, rows=tk):
        c0 = pl.multiple_of(j * tk, tk)
        keep = sc_ref[pl.ds(c0, rows), :] >= thr
        if near:
            keep = jnp.logical_and(keep, rel <= (row0 - c0))
        return keep

    def attend_exact(j, near):
        c0 = pl.multiple_of(j * tk, tk)
        mask_scr[:tk] = jnp.where(selected(j, near), 0.0, NEG_BIG)
        tile_max = []
        for h in range(N_ATTN_HEADS):
            s = logits(j, h, near) + mask_scr[:tk]
            s_scr[h] = s
            tile_max.append(jnp.max(fold(s, jnp.maximum), axis=0, keepdims=True))
        for h in range(N_ATTN_HEADS):
            g = h // HEADS_PER_GROUP
            m_old = m_ref[h]
            m_new = jnp.maximum(m_old, tile_max[h])
            alpha = jnp.exp2(m_old - m_new)
            p_scr[h, :tk] = jnp.exp2(s_scr[h] - m_new).astype(BF16)
            m_ref[h] = m_new
            pv = jnp.dot(vt_ref[0, g, :, pl.ds(c0, tk)], p_scr[h, :tk],
                         preferred_element_type=F32)
            acc_ref[h] = acc_ref[h] * alpha + pv

    def attend_fast(j, near, rows=tk):
        c0 = pl.multiple_of(j * tk, tk)
        mask_scr[:rows] = jnp.where(selected(j, near, rows), 0.0, NEG_BIG)
        for h in range(N_ATTN_HEADS):
            p_scr[h, :rows] = jnp.exp2(logits(j, h, near, rows) + mask_scr[:rows]).astype(BF16)
        for h in range(N_ATTN_HEADS):
            g = h // HEADS_PER_GROUP
            acc_ref[h] += jnp.dot(vt_ref[0, g, :, pl.ds(c0, rows)], p_scr[h, :rows],
                                  preferred_element_type=F32)

    def run_tiles(attend, far_tiles_per_step):
        acc_ref[...] = jnp.zeros(acc_ref.shape, F32)
        m_ref[...] = jnp.full(m_ref.shape, NEG_BIG, F32)
        w = far_tiles_per_step

        def far_step(t, c):
            if w == 1:
                attend(t, False)
            else:
                attend(t * w, False, w * tk)
            return c

        n_wide = lax.div(j_near, w)
        lax.fori_loop(0, n_wide, far_step, 0)
        for u in range(w - 1):
            @pl.when(n_wide * w + u < j_near)
            def _():
                attend(n_wide * w + u, False)
        for u in range(2):
            @pl.when(j_near + u < n_kt)
            def _():
                attend(j_near + u, True)

    run_tiles(attend_fast, FAST_FAR_TILES)
    denom = acc_ref[:, ATTN_HEAD_DIM:ATTN_HEAD_DIM + 1, :]
    sane = jnp.logical_and(denom > MIN_SANE_DENOM, denom < MAX_SANE_DENOM)

    @pl.when(jnp.min(sane.astype(F32)) < 0.5)
    def _():
        run_tiles(attend_exact, 1)

    outs = []
    for h in range(N_ATTN_HEADS):
        a = acc_ref[h]
        outs.append(a[:ATTN_HEAD_DIM, :] / a[ATTN_HEAD_DIM:ATTN_HEAD_DIM + 1, :])
    o_ref[0] = jnp.concatenate(outs, axis=0).T.astype(BF16)


def _dsa(qt, qit, wt, ki, k, vt, bias, *, tq, tk, k_sel):
    bsz, _, _, s = qt.shape
    assert tq >= MAX_DISTANCE and tk % tq == 0 and s % tk == 0
    body = functools.partial(_dsa_body, tq=tq, tk=tk, k_sel=k_sel)
    return pl.pallas_call(
        body,
        grid=(bsz, s // tq),
        in_specs=[pl.BlockSpec((1, N_IDX_HEADS, IDX_HEAD_DIM, tq), lambda b, i: (b, 0, 0, i)),
                  pl.BlockSpec((1, N_IDX_HEADS, tq), lambda b, i: (b, 0, i)),
                  pl.BlockSpec((1, N_ATTN_HEADS, ATTN_HEAD_DIM, tq), lambda b, i: (b, 0, 0, i)),
                  pl.BlockSpec((1, s, IDX_HEAD_DIM), lambda b, i: (b, 0, 0)),
                  pl.BlockSpec((1, N_KV_GROUPS, s, ATTN_HEAD_DIM), lambda b, i: (b, 0, 0, 0)),
                  pl.BlockSpec((1, N_KV_GROUPS, V_AUG_ROWS, s), lambda b, i: (b, 0, 0, 0)),
                  _const_spec(bias.shape)],
        out_specs=pl.BlockSpec((1, tq, ATTN_WIDTH), lambda b, i: (b, i, 0)),
        out_shape=jax.ShapeDtypeStruct((bsz, s, ATTN_WIDTH), BF16),
        scratch_shapes=[pltpu.VMEM((s, tq), F32),
                        pltpu.VMEM((N_ATTN_HEADS, V_AUG_ROWS, tq), F32),
                        pltpu.VMEM((N_ATTN_HEADS, 1, tq), F32),
                        pltpu.VMEM((N_ATTN_HEADS, tk, tq), F32),
                        pltpu.VMEM((N_ATTN_HEADS, FAST_FAR_TILES * tk, tq), BF16),
                        pltpu.VMEM((FAST_FAR_TILES * tk, tq), F32)],
        compiler_params=_params("parallel", "arbitrary"),
        name="dsa",
    )(qit, wt, qt, ki, k, vt, bias)


def _split3(a):
    hi = a.astype(BF16)
    r1 = a - hi.astype(F32)
    mid = r1.astype(BF16)
    lo = (r1 - mid.astype(F32)).astype(BF16)
    return hi, mid, lo


def _hgrn2_body(rq_ref, ri_ref, rg_ref, rf_ref, lb_ref, nw_ref, o_ref,
                state_scr, tril_scr, lvl_scr, *, chunk):
    t_len = rq_ref.shape[1]
    c = chunk
    d = REC_HEAD_DIM
    nw = nw_ref[...]
    eye = (lax.broadcasted_iota(jnp.int32, (d, d), 0)
           == lax.broadcasted_iota(jnp.int32, (d, d), 1)).astype(F32)
    sub = lax.broadcasted_iota(jnp.int32, (c, d), 0) & (REC_BAND - 1)
    level_sizes = []
    b = REC_BAND
    while b < c:
        level_sizes.append(b)
        b *= 2

    @pl.when(pl.program_id(1) == 0)
    def _():
        state_scr[...] = jnp.zeros(state_scr.shape, F32)
        ti = lax.broadcasted_iota(jnp.int32, (c, c), 0)
        si = lax.broadcasted_iota(jnp.int32, (c, c), 1)
        tril_scr[...] = (si <= ti).astype(BF16)
        for li, b in enumerate(level_sizes):
            blk = 2 * b
            sh = blk.bit_length() - 1
            m = jnp.logical_and(
                lax.shift_right_logical(ti, sh) == lax.shift_right_logical(si, sh),
                jnp.logical_and((ti & (blk - 1)) >= b, (si & (blk - 1)) < b))
            lvl_scr[li] = m.astype(F32)

    def head_step(hh, t0):
        cols = slice(hh * d, (hh + 1) * d)
        lb = lb_ref[:, cols]
        state = state_scr[hh]
        f = lb + (1.0 - lb) * _sigmoid_rel(rf_ref[0, pl.ds(t0, c), cols])
        g = jnp.log(f)
        kk = 1.0 - f
        qq = _silu(rq_ref[0, pl.ds(t0, c), cols].astype(F32))
        vb = ri_ref[0, pl.ds(t0, c), cols]
        vv = vb.astype(F32)

        g3 = _split3(g)
        tril = tril_scr[...]
        a = (jnp.dot(tril, g3[0], preferred_element_type=F32)
             + jnp.dot(tril, g3[1], preferred_element_type=F32)
             + jnp.dot(tril, g3[2], preferred_element_type=F32))
        a_last = a[c - 1:c, :]

        o = jnp.dot((qq * jnp.exp(a)).astype(BF16), state.astype(BF16),
                    preferred_element_type=F32)

        p = jnp.zeros((c, c), F32)
        for li, b in enumerate(level_sizes):
            blk = 2 * b
            r = jnp.broadcast_to(a.reshape(c // blk, blk, d)[:, b - 1:b, :],
                                 (c // blk, blk, d)).reshape(c, d)
            qt = (qq * jnp.exp(jnp.minimum(a - r, 0.0))).astype(BF16)
            kt = (kk * jnp.exp(jnp.minimum(r - a, 0.0))).astype(BF16)
            pb = lax.dot_general(qt, kt, (((1,), (1,)), ((), ())), preferred_element_type=F32)
            p = p + pb * lvl_scr[li]
        o = o + jnp.dot(p.astype(BF16), vb, preferred_element_type=F32)

        def back(x, dl):
            if dl == 0:
                return x
            return pltpu.roll(x.reshape(c // SUBLANES, SUBLANES, d), dl, axis=1).reshape(c, d)

        for dl in range(REC_BAND):
            a_s = back(a, dl)
            k_s = back(kk, dl)
            v_s = back(vv, dl)
            e = jnp.exp(jnp.where(sub >= dl, a - a_s, NEG_BIG))
            pd = jnp.sum(qq * k_s * e, axis=1, keepdims=True)
            o = o + pd * v_s

        kd = (kk * jnp.exp(a_last - a)).astype(BF16)
        upd = lax.dot_general(kd, vb, (((0,), (0,)), ((), ())), preferred_element_type=F32)
        e_col = jnp.sum(eye * jnp.exp(a_last), axis=1, keepdims=True)
        state_scr[hh] = e_col * state + upd

        y = _rms(o, nw) * _silu(rg_ref[0, pl.ds(t0, c), cols].astype(F32))
        o_ref[0, pl.ds(t0, c), cols] = y.astype(BF16)

    def step(n, carry):
        t0 = pl.multiple_of(n * c, c)
        for hh in range(N_REC_HEADS):
            head_step(hh, t0)
        return carry

    lax.fori_loop(0, t_len // c, step, 0)


def _hgrn2(rec3, rf, lb, nw, *, chunk):
    bsz, s, _ = rec3.shape
    d = REC_HEAD_DIM
    h = N_REC_HEADS
    body = functools.partial(_hgrn2_body, chunk=chunk)
    tt = _pick_tile(s, REC_TIME_TILE)
    col = lambda part: pl.BlockSpec((1, tt, REC_WIDTH), lambda b, t: (b, t, part))
    return pl.pallas_call(
        body,
        grid=(bsz, s // tt),
        in_specs=[col(0), col(1), col(2), col(0),
                  pl.BlockSpec((1, REC_WIDTH), lambda b, t: (0, 0)),
                  pl.BlockSpec((1, d), lambda b, t: (0, 0))],
        out_specs=col(0),
        out_shape=jax.ShapeDtypeStruct((bsz, s, REC_WIDTH), BF16),
        scratch_shapes=[pltpu.VMEM((h, d, d), F32)]
        + [pltpu.VMEM((chunk, chunk), BF16),
           pltpu.VMEM(((chunk // REC_BAND).bit_length() - 1, chunk, chunk), F32)],
        compiler_params=_params("parallel", "arbitrary"),
        name="hgrn2",
    )(rec3, rec3, rec3, rf, lb.reshape(1, REC_WIDTH), nw.reshape(1, d))


def _out_ffn_body(x_ref, attn_ref, rec_ref, gates_ref, mod_ref, wba_ref, wbr_ref, wo_ref,
                  n2_ref, wg_ref, wu_ref, wd_ref, fn_ref, o_ref, *, final):
    d = x_ref.shape[2]
    x = x_ref[0]
    g1 = mod_ref[0, 2:3, :]
    sh2 = mod_ref[0, 3:4, :]
    sc2 = mod_ref[0, 4:5, :]
    g2 = mod_ref[0, 5:6, :]
    ya = jnp.dot(attn_ref[0], wba_ref[...], preferred_element_type=F32)
    yr = jnp.dot(rec_ref[0], wbr_ref[...], preferred_element_type=F32)
    gates = gates_ref[0].astype(F32)
    mix = (_sigmoid(gates[:, :d]) * ya + _sigmoid(gates[:, d:]) * yr).astype(BF16)
    x1 = x + g1 * jnp.dot(mix, wo_ref[...], preferred_element_type=F32)
    h2 = (_rms(x1, n2_ref[...]) * (1.0 + sc2) + sh2).astype(BF16)
    gate = jnp.dot(h2, wg_ref[...], preferred_element_type=F32)
    up = jnp.dot(h2, wu_ref[...], preferred_element_type=F32)
    act = (_silu(gate) * up).astype(BF16)
    x2 = x1 + g2 * jnp.dot(act, wd_ref[...], preferred_element_type=F32)
    o_ref[0] = _rms(x2, fn_ref[...]) if final else x2


def _out_ffn(x, attn, rec, gates, mod, wba, wbr, wo, n2w, wg, wu, wd, fnw, tm, final):
    bsz, s, d = x.shape
    hid = wg.shape[1]
    tok = lambda w: pl.BlockSpec((1, tm, w), lambda b, i: (b, i, 0))
    return pl.pallas_call(
        functools.partial(_out_ffn_body, final=final),
        grid=(bsz, s // tm),
        in_specs=[tok(d), tok(ATTN_WIDTH), tok(REC_WIDTH), tok(2 * d),
                  pl.BlockSpec((1, 6, d), lambda b, i: (b, 0, 0)),
                  _const_spec((ATTN_WIDTH, d)), _const_spec((REC_WIDTH, d)), _const_spec((d, d)),
                  _const_spec((1, d)), _const_spec((d, hid)), _const_spec((d, hid)),
                  _const_spec((hid, d)), _const_spec((1, d))],
        out_specs=tok(d),
        out_shape=jax.ShapeDtypeStruct((bsz, s, d), F32),
        compiler_params=_params("parallel", "parallel"),
        name="out_ffn",
    )(x, attn, rec, gates, mod, wba, wbr, wo, n2w, wg, wu, wd, fnw)


def _pick_tile(n, want):
    t = min(want, n)
    while n % t:
        t //= 2
    return t


def kernel(x, c, w_ada, b_ada, norm1_w, w_in, q_norm_w, w_uq, w_uq_idx, rel_bias, lb_logits,
           rec_norm_w, w_branch_attn, w_branch_rec, w_out, norm2_w, w_ffn_gate, w_ffn_up,
           w_ffn_down, final_norm_w):
    bsz, s, d = x.shape
    depth = w_in.shape[0]
    k_sel = min(TOPK_MAX, s // 4)
    tm = _pick_tile(s, TOKEN_TILE)
    tq = _pick_tile(s, Q_TILE)
    tk = max(tq, _pick_tile(s, K_TILE))
    chunk = _pick_tile(s, REC_CHUNK)
    lower_bounds = jnp.cumsum(jax.nn.softmax(lb_logits.astype(F32), axis=0), axis=0)
    bias = _near_bias(rel_bias, tq, tk)

    o_cq = 0
    o_k = o_cq + Q_LORA_RANK
    o_v = o_k + KV_WIDTH
    o_ki = o_v + KV_WIDTH
    o_wi = o_ki + IDX_HEAD_DIM
    o_rq = o_wi + N_IDX_HEADS
    o_rf = o_rq + REC_WIDTH
    o_ri = o_rf + REC_WIDTH
    o_rg = o_ri + REC_WIDTH
    o_gt = o_rg + REC_WIDTH

    for layer in range(depth):
        mod = _ada(c, w_ada[layer], b_ada[layer]).reshape(bsz, 6, d)
        wl = w_in[layer]
        pad_a = LANES - IDX_HEAD_DIM - N_IDX_HEADS
        wa = jnp.concatenate([wl[:, o_cq:o_rq], jnp.zeros((d, pad_a), wl.dtype)], axis=1).astype(BF16)
        wr = jnp.concatenate([wl[:, o_rq:o_rf], wl[:, o_ri:o_rg], wl[:, o_rg:o_gt],
                              wl[:, o_rf:o_ri]], axis=1).astype(BF16)
        wg = wl[:, o_gt:].astype(BF16)
        wuq = (w_uq[layer] * (ATTN_HEAD_DIM ** -0.5 * LOG2_E)).astype(BF16).T
        wui = (w_uq_idx[layer] * (IDX_HEAD_DIM ** -0.5)).astype(BF16).T

        qt, qit, k, vt, ki, wt, rec3, rf, gates = _in_proj(
            x, mod, norm1_w[layer].reshape(1, d), wa, wr, wg,
            q_norm_w[layer].reshape(1, Q_LORA_RANK), wuq, wui, tm)

        attn = _dsa(qt, qit, wt, ki, k, vt, bias, tq=tq, tk=tk, k_sel=k_sel)
        rec = _hgrn2(rec3, rf, lower_bounds[layer], rec_norm_w[layer], chunk=chunk)

        x = _out_ffn(x, attn, rec, gates, mod,
                     w_branch_attn[layer].astype(BF16), w_branch_rec[layer].astype(BF16),
                     w_out[layer].astype(BF16), norm2_w[layer].reshape(1, d),
                     w_ffn_gate[layer].astype(BF16), w_ffn_up[layer].astype(BF16),
                     w_ffn_down[layer].astype(BF16),
                     final_norm_w.reshape(1, d), tm, layer == depth - 1)
    return x
```

```python
import functools
import math

import jax
import jax.numpy as jnp
from jax import lax
from jax.experimental import pallas as pl
from jax.experimental.pallas import tpu as pltpu

F32 = jnp.float32
BF16 = jnp.bfloat16

N_ATTN_HEADS = 8
ATTN_HEAD_DIM = 64
N_KV_GROUPS = 2
HEADS_PER_GROUP = N_ATTN_HEADS // N_KV_GROUPS
Q_LORA_RANK = 256
N_IDX_HEADS = 8
IDX_HEAD_DIM = 64
TOPK_MAX = 256
N_BUCKETS = 32
MAX_DISTANCE = 128
N_REC_HEADS = 4
REC_HEAD_DIM = 128
EPS = 1e-6

ATTN_WIDTH = N_ATTN_HEADS * ATTN_HEAD_DIM
KV_WIDTH = N_KV_GROUPS * ATTN_HEAD_DIM
REC_WIDTH = N_REC_HEADS * REC_HEAD_DIM

LANES = 128
SUBLANES = 8
BF16_SUBLANES = 16
VMEM_LIMIT_BYTES = 56 * 1024 * 1024

ADA_TILE = 1024
TOKEN_TILE = 512
Q_TILE = 256
K_TILE = 512
REC_CHUNK = 128
REC_TIME_TILE = 1024
REC_BAND = 4
NEG_BIG = -1e30
BISECT_CAP = 700
SNAP_AFTER = 14
FOLD_CHAINS = 4
PASS_ROWS = 128
FAST_FAR_TILES = 4
LOG2_E = math.log2(math.e)
MAX_SANE_DENOM = 1e30
MIN_SANE_DENOM = 1e-18
V_AUG_ROWS = ATTN_HEAD_DIM + BF16_SUBLANES


def _rms(x, w):
    return x * lax.rsqrt(jnp.mean(x * x, axis=-1, keepdims=True) + EPS) * w


def _sigmoid(x):
    return 0.5 * jnp.tanh(0.5 * x) + 0.5


def _sigmoid_rel(x):
    return 1.0 / (1.0 + jnp.exp(-x))


def _silu(x):
    return x * _sigmoid(x)


def _params(*sem):
    return pltpu.CompilerParams(dimension_semantics=sem, vmem_limit_bytes=VMEM_LIMIT_BYTES)


def _const_spec(shape):
    nd = len(shape)
    return pl.BlockSpec(shape, lambda *_: (0,) * nd, pipeline_mode=pl.Buffered(1))


def _ada_body(c_ref, w_ref, b_ref, o_ref):
    ca = _silu(c_ref[...])
    o_ref[...] = jnp.dot(ca, w_ref[...], precision=lax.Precision.HIGHEST,
                         preferred_element_type=F32) + b_ref[...]


def _ada(c, w, b):
    bsz, d = c.shape
    n = w.shape[1]
    tn = _pick_tile(n, ADA_TILE)
    return pl.pallas_call(
        _ada_body,
        grid=(n // tn,),
        in_specs=[pl.BlockSpec((bsz, d), lambda j: (0, 0)),
                  pl.BlockSpec((d, tn), lambda j: (0, j)),
                  pl.BlockSpec((1, tn), lambda j: (0, j))],
        out_specs=pl.BlockSpec((bsz, tn), lambda j: (0, j)),
        out_shape=jax.ShapeDtypeStruct((bsz, n), F32),
        compiler_params=_params("arbitrary"),
        name="ada",
    )(c, w, b.reshape(1, n))


def _in_proj_body(x_ref, mod_ref, n1_ref, wa_ref, wr_ref, wg_ref, qn_ref, wuq_ref, wui_ref,
                  qt_ref, qit_ref, k_ref, vt_ref, ki_ref, wt_ref, rec_ref, rf_ref, gates_ref):
    tm = x_ref.shape[1]
    x = x_ref[0]
    sh1 = mod_ref[0, 0:1, :]
    sc1 = mod_ref[0, 1:2, :]
    h = (_rms(x, n1_ref[...]) * (1.0 + sc1) + sh1).astype(BF16)

    pa = jnp.dot(h, wa_ref[...], preferred_element_type=F32)
    o_k, o_v, o_ki = Q_LORA_RANK, Q_LORA_RANK + KV_WIDTH, Q_LORA_RANK + 2 * KV_WIDTH
    v_t = pa[:, o_v:o_ki].T
    kiw_t = pa[:, o_ki:].T
    for g in range(N_KV_GROUPS):
        k_ref[0, g] = pa[:, o_k + g * ATTN_HEAD_DIM:o_k + (g + 1) * ATTN_HEAD_DIM].astype(BF16)
        vt_ref[0, g, :ATTN_HEAD_DIM, :] = v_t[g * ATTN_HEAD_DIM:(g + 1) * ATTN_HEAD_DIM].astype(BF16)
        vt_ref[0, g, ATTN_HEAD_DIM:, :] = jnp.ones((V_AUG_ROWS - ATTN_HEAD_DIM, tm), BF16)
    ki_ref[0] = pa[:, o_ki:o_ki + IDX_HEAD_DIM].astype(BF16)
    wt_ref[0] = kiw_t[IDX_HEAD_DIM:IDX_HEAD_DIM + N_IDX_HEADS]
    cqn = _rms(pa[:, :Q_LORA_RANK], qn_ref[...]).astype(BF16)
    nt = (((1,), (1,)), ((), ()))
    qf = lax.dot_general(wuq_ref[...], cqn, nt, preferred_element_type=F32)
    qt_ref[0] = qf.reshape(N_ATTN_HEADS, ATTN_HEAD_DIM, tm).astype(BF16)
    qif = lax.dot_general(wui_ref[...], cqn, nt, preferred_element_type=F32)
    qit_ref[0] = qif.reshape(N_IDX_HEADS, IDX_HEAD_DIM, tm).astype(BF16)

    pr = jnp.dot(h, wr_ref[...], preferred_element_type=F32)
    rec_ref[0] = pr[:, :3 * REC_WIDTH].astype(BF16)
    rf_ref[0] = pr[:, 3 * REC_WIDTH:]
    gates_ref[0] = jnp.dot(h, wg_ref[...], preferred_element_type=F32).astype(BF16)


def _in_proj(x, mod, n1w, wa, wr, wg, qnw, wuq, wui, tm):
    bsz, s, d = x.shape
    wa_n = wa.shape[1]
    grid = (bsz, s // tm)
    tok = lambda w: pl.BlockSpec((1, tm, w), lambda b, i: (b, i, 0))
    head = lambda nh, hd: pl.BlockSpec((1, nh, hd, tm), lambda b, i: (b, 0, 0, i))
    out_shape = (
        jax.ShapeDtypeStruct((bsz, N_ATTN_HEADS, ATTN_HEAD_DIM, s), BF16),
        jax.ShapeDtypeStruct((bsz, N_IDX_HEADS, IDX_HEAD_DIM, s), BF16),
        jax.ShapeDtypeStruct((bsz, N_KV_GROUPS, s, ATTN_HEAD_DIM), BF16),
        jax.ShapeDtypeStruct((bsz, N_KV_GROUPS, V_AUG_ROWS, s), BF16),
        jax.ShapeDtypeStruct((bsz, s, IDX_HEAD_DIM), BF16),
        jax.ShapeDtypeStruct((bsz, N_IDX_HEADS, s), F32),
        jax.ShapeDtypeStruct((bsz, s, 3 * REC_WIDTH), BF16),
        jax.ShapeDtypeStruct((bsz, s, REC_WIDTH), F32),
        jax.ShapeDtypeStruct((bsz, s, 2 * d), BF16),
    )
    out_specs = (head(N_ATTN_HEADS, ATTN_HEAD_DIM), head(N_IDX_HEADS, IDX_HEAD_DIM),
                 pl.BlockSpec((1, N_KV_GROUPS, tm, ATTN_HEAD_DIM), lambda b, i: (b, 0, i, 0)),
                 head(N_KV_GROUPS, V_AUG_ROWS),
                 tok(IDX_HEAD_DIM),
                 pl.BlockSpec((1, N_IDX_HEADS, tm), lambda b, i: (b, 0, i)),
                 tok(3 * REC_WIDTH), tok(REC_WIDTH), tok(2 * d))
    return pl.pallas_call(
        _in_proj_body,
        grid=grid,
        in_specs=[tok(d),
                  pl.BlockSpec((1, 6, d), lambda b, i: (b, 0, 0)),
                  _const_spec((1, d)),
                  _const_spec((d, wa_n)),
                  _const_spec((d, 4 * REC_WIDTH)),
                  _const_spec((d, 2 * d)),
                  _const_spec((1, Q_LORA_RANK)),
                  _const_spec((ATTN_WIDTH, Q_LORA_RANK)),
                  _const_spec((N_IDX_HEADS * IDX_HEAD_DIM, Q_LORA_RANK))],
        out_specs=out_specs,
        out_shape=out_shape,
        compiler_params=_params("parallel", "parallel"),
        name="in_proj",
    )(x, mod, n1w, wa, wr, wg, qnw, wuq, wui)


def _near_bias(rel_bias, tq, tk):
    n = jnp.arange(MAX_DISTANCE + 1, dtype=jnp.int32)
    max_exact = N_BUCKETS // 2
    nf = jnp.maximum(n, 1).astype(F32)
    large = max_exact + (jnp.log(nf / max_exact) / math.log(MAX_DISTANCE / max_exact)
                         * (N_BUCKETS - max_exact)).astype(jnp.int32)
    bucket = jnp.where(n < max_exact, n, jnp.minimum(large, N_BUCKETS - 1))
    bucket = bucket.at[MAX_DISTANCE].set(N_BUCKETS - 1)
    ids = jnp.arange(N_BUCKETS, dtype=jnp.int32)
    start = jnp.sum((bucket[None, :] < ids[:, None]).astype(jnp.int32), axis=1)
    rb = (rel_bias.astype(F32) - rel_bias[N_BUCKETS - 1].astype(F32)[None, :]) * LOG2_E
    step = rb - jnp.concatenate([jnp.zeros_like(rb[:1]), rb[:-1]], axis=0)
    dist = (jnp.arange(tq, dtype=jnp.int32)[None, :]
            - jnp.arange(2 * tk, dtype=jnp.int32)[:, None] + tk)
    reached = (dist[None, :, :] >= start[:, None, None]).astype(F32)
    return jnp.einsum("bh,but->hut", step, reached, precision=lax.Precision.HIGHEST)


def _dsa_body(qit_ref, w_ref, qt_ref, ki_ref, k_ref, vt_ref, bias_ref, o_ref,
              sc_ref, acc_ref, m_ref, s_scr, p_scr, mask_scr, *, tq, tk, k_sel):
    i = pl.program_id(1)
    row0 = i * tq
    n_kt = lax.div(row0 + tq + (tk - 1), tk)
    j_near = lax.div(jnp.maximum(row0 - (MAX_DISTANCE - 1), 0), tk)
    krow = lax.broadcasted_iota(jnp.int32, (tk, tq), 0)
    qcol = lax.broadcasted_iota(jnp.int32, (tk, tq), 1)
    rel = krow - qcol
    kf = float(k_sel)

    def fold(a, op, group=SUBLANES):
        n = a.shape[0] // group
        a = a.reshape(n // FOLD_CHAINS, FOLD_CHAINS, group, tq)
        r = a[0]
        for t in range(1, n // FOLD_CHAINS):
            r = op(r, a[t])
        while r.shape[0] > 1:
            half = r.shape[0] // 2
            r = op(r[:half], r[half:])
        return r[0]

    w = w_ref[0] * (N_IDX_HEADS ** -0.5)

    def score_tile(j, carry):
        mx, mn = carry
        c0 = pl.multiple_of(j * tk, tk)
        kin = ki_ref[0, pl.ds(c0, tk), :]
        sc = None
        for h in range(N_IDX_HEADS):
            lg = jnp.dot(kin, qit_ref[0, h], preferred_element_type=F32)
            term = jnp.maximum(lg, 0.0) * w[h:h + 1, :]
            sc = term if sc is None else sc + term
        causal = rel <= (row0 - c0)
        scm = jnp.where(causal, sc, NEG_BIG)
        sc_ref[pl.ds(c0, tk), :] = scm
        mx = jnp.maximum(mx, fold(scm, jnp.maximum))
        mn = jnp.minimum(mn, fold(jnp.where(causal, sc, -NEG_BIG), jnp.minimum))
        return mx, mn

    def score_pair(t, carry):
        carry = score_tile(2 * t, carry)
        return score_tile(jnp.minimum(2 * t + 1, n_kt - 1), carry)

    mx8, mn8 = lax.fori_loop(0, lax.div(n_kt + 1, 2), score_pair,
                             (jnp.full((SUBLANES, tq), NEG_BIG, F32),
                              jnp.full((SUBLANES, tq), -NEG_BIG, F32)))
    mx = jnp.max(mx8, axis=0, keepdims=True)
    mn = jnp.min(mn8, axis=0, keepdims=True)

    def key_pass(fn, op, init):
        def one_tile(j, acc):
            c0 = pl.multiple_of(j * tk, tk)
            for u in range(tk // PASS_ROWS):
                s = sc_ref[pl.ds(c0 + u * PASS_ROWS, PASS_ROWS), :]
                acc = op(acc, fold(fn(s), op))
            return acc

        pairs = lax.shift_right_logical(n_kt, 1)
        acc = lax.fori_loop(0, pairs, lambda t, a: one_tile(t + pairs, one_tile(t, a)),
                            jnp.full((SUBLANES, tq), init, F32))
        return lax.cond((n_kt & 1) == 1, lambda a: one_tile(n_kt - 1, a), lambda a: a, acc)

    def count(pred):
        part = key_pass(lambda s: jnp.where(pred(s), 1.0, 0.0), jnp.add, 0.0)
        return jnp.sum(part, axis=0, keepdims=True)

    n_valid = (row0 + 1 + lax.broadcasted_iota(jnp.int32, (1, tq), 1)).astype(F32)
    take_all = n_valid <= kf
    cnt_top = count(lambda s: s >= mx)
    at_top = jnp.logical_and(jnp.logical_not(take_all), cnt_top >= kf)
    lo0 = jnp.where(take_all, NEG_BIG, jnp.where(at_top, mx, mn))
    done0 = jnp.logical_or(take_all, at_top)

    def bis_cond(st):
        it, lo, hi, c_lo, done_f = st
        return jnp.logical_and(it < BISECT_CAP, jnp.min(done_f) < 0.5)

    def halve(st):
        lo, hi, c_lo, done_f = st
        done = done_f > 0.5
        mid = lo + (hi - lo) * 0.5
        stuck = jnp.logical_or(mid <= lo, mid >= hi)
        cnt = count(lambda s: s >= mid)
        ge = cnt >= kf
        upd = jnp.logical_not(jnp.logical_or(done, stuck))
        raise_lo = jnp.logical_and(upd, ge)
        lo = jnp.where(raise_lo, mid, lo)
        c_lo = jnp.where(raise_lo, cnt, c_lo)
        hi = jnp.where(jnp.logical_and(upd, jnp.logical_not(ge)), mid, hi)
        done = jnp.logical_or(jnp.logical_or(done, stuck), jnp.logical_and(ge, cnt == kf))
        return lo, hi, c_lo, done.astype(F32)

    def snap(st):
        lo, hi, c_lo, done_f = st
        done = done_f > 0.5

        below = jnp.max(key_pass(lambda s: jnp.where(s < hi, s, NEG_BIG), jnp.maximum, NEG_BIG),
                        axis=0, keepdims=True)
        cnt = count(lambda s: s >= below)
        found = jnp.logical_and(jnp.logical_not(done), cnt >= kf)
        lower_hi = jnp.logical_and(jnp.logical_not(done), cnt < kf)
        lo = jnp.where(found, below, lo)
        c_lo = jnp.where(found, cnt, c_lo)
        hi = jnp.where(lower_hi, below, hi)
        return lo, hi, c_lo, jnp.logical_or(done, found).astype(F32)

    def bis_step(st):
        it, lo, hi, c_lo, done_f = st
        lo, hi, c_lo, done_f = lax.cond((it & 1) == 0, snap, halve, (lo, hi, c_lo, done_f))
        return it + 1, lo, hi, c_lo, done_f

    c_lo0 = jnp.where(at_top, cnt_top, n_valid)
    st = lax.fori_loop(0, SNAP_AFTER, lambda _, s: halve(s),
                       (lo0, mx, c_lo0, done0.astype(F32)))
    _, thr, _, c_thr, _ = lax.while_loop(bis_cond, bis_step, (jnp.int32(0),) + st)

    excess = jnp.logical_and(jnp.logical_not(take_all), c_thr > kf)

    @pl.when(jnp.max(excess.astype(F32)) > 0.5)
    def _():
        need = jnp.where(excess, kf - count(lambda s: s > thr), float(sc_ref.shape[0]))
        tril = (lax.broadcasted_iota(jnp.int32, (tk, tk), 1)
                <= lax.broadcasted_iota(jnp.int32, (tk, tk), 0)).astype(BF16)

        def drop_tile(j, seen):
            c0 = pl.multiple_of(j * tk, tk)
            s = sc_ref[pl.ds(c0, tk), :]
            eq = s == thr
            rank = seen + jnp.dot(tril, jnp.where(eq, 1.0, 0.0).astype(BF16),
                                  preferred_element_type=F32)
            sc_ref[pl.ds(c0, tk), :] = jnp.where(jnp.logical_and(eq, rank > need), NEG_BIG, s)
            return rank[tk - 1:tk, :]

        lax.fori_loop(0, n_kt, drop_tile, jnp.zeros((1, tq), F32))

    def logits(j, h, near, rows=tk):
        c0 = pl.multiple_of(j * tk, tk)
        g = h // HEADS_PER_GROUP
        s = jnp.dot(k_ref[0, g, pl.ds(c0, rows), :], qt_ref[0, h],
                    preferred_element_type=F32)
        if near:
            bias_row = pl.multiple_of(c0 - row0 + tk, tq)
            s = s + bias_ref[h, pl.ds(bias_row, tk), :]
        return s

    def selected(j, near, rows=tk):
        c0 = pl.multiple_of(j * tk, tk)
        keep = sc_ref[pl.ds(c0, rows), :] >= thr
        if near:
            keep = jnp.logical_and(keep, rel <= (row0 - c0))
        return keep

    def attend_exact(j, near):
        c0 = pl.multiple_of(j * tk, tk)
        mask_scr[:tk] = jnp.where(selected(j, near), 0.0, NEG_BIG)
        tile_max = []
        for h in range(N_ATTN_HEADS):
            s = logits(j, h, near) + mask_scr[:tk]
            s_scr[h] = s
            tile_max.append(jnp.max(fold(s, jnp.maximum), axis=0, keepdims=True))
        for h in range(N_ATTN_HEADS):
            g = h // HEADS_PER_GROUP
            m_old = m_ref[h]
            m_new = jnp.maximum(m_old, tile_max[h])
            alpha = jnp.exp2(m_old - m_new)
            p_scr[h, :tk] = jnp.exp2(s_scr[h] - m_new).astype(BF16)
            m_ref[h] = m_new
            pv = jnp.dot(vt_ref[0, g, :, pl.ds(c0, tk)], p_scr[h, :tk],
                         preferred_element_type=F32)
            acc_ref[h] = acc_ref[h] * alpha + pv

    def attend_fast(j, near, rows=tk):
        c0 = pl.multiple_of(j * tk, tk)
        mask_scr[:rows] = jnp.where(selected(j, near, rows), 0.0, NEG_BIG)
        for h in range(N_ATTN_HEADS):
            p_scr[h, :rows] = jnp.exp2(logits(j, h, near, rows) + mask_scr[:rows]).astype(BF16)
        for h in range(N_ATTN_HEADS):
            g = h // HEADS_PER_GROUP
            acc_ref[h] += jnp.dot(vt_ref[0, g, :, pl.ds(c0, rows)], p_scr[h, :rows],
                                  preferred_element_type=F32)

    def run_tiles(attend, far_tiles_per_step):
        acc_ref[...] = jnp.zeros(acc_ref.shape, F32)
        m_ref[...] = jnp.full(m_ref.shape, NEG_BIG, F32)
        w = far_tiles_per_step

        def far_step(t, c):
            if w == 1:
                attend(t, False)
            else:
                attend(t * w, False, w * tk)
            return c

        n_wide = lax.div(j_near, w)
        lax.fori_loop(0, n_wide, far_step, 0)
        for u in range(w - 1):
            @pl.when(n_wide * w + u < j_near)
            def _():
                attend(n_wide * w + u, False)
        for u in range(2):
            @pl.when(j_near + u < n_kt)
            def _():
                attend(j_near + u, True)

    run_tiles(attend_fast, FAST_FAR_TILES)
    denom = acc_ref[:, ATTN_HEAD_DIM:ATTN_HEAD_DIM + 1, :]
    sane = jnp.logical_and(denom > MIN_SANE_DENOM, denom < MAX_SANE_DENOM)

    @pl.when(jnp.min(sane.astype(F32)) < 0.5)
    def _():
        run_tiles(attend_exact, 1)

    outs = []
    for h in range(N_ATTN_HEADS):
        a = acc_ref[h]
        outs.append(a[:ATTN_HEAD_DIM, :] / a[ATTN_HEAD_DIM:ATTN_HEAD_DIM + 1, :])
    o_ref[0] = jnp.concatenate(outs, axis=0).T.astype(BF16)


def _dsa(qt, qit, wt, ki, k, vt, bias, *, tq, tk, k_sel):
    bsz, _, _, s = qt.shape
    assert tq >= MAX_DISTANCE and tk % tq == 0 and s % tk == 0
    body = functools.partial(_dsa_body, tq=tq, tk=tk, k_sel=k_sel)
    return pl.pallas_call(
        body,
        grid=(bsz, s // tq),
        in_specs=[pl.BlockSpec((1, N_IDX_HEADS, IDX_HEAD_DIM, tq), lambda b, i: (b, 0, 0, i)),
                  pl.BlockSpec((1, N_IDX_HEADS, tq), lambda b, i: (b, 0, i)),
                  pl.BlockSpec((1, N_ATTN_HEADS, ATTN_HEAD_DIM, tq), lambda b, i: (b, 0, 0, i)),
                  pl.BlockSpec((1, s, IDX_HEAD_DIM), lambda b, i: (b, 0, 0)),
                  pl.BlockSpec((1, N_KV_GROUPS, s, ATTN_HEAD_DIM), lambda b, i: (b, 0, 0, 0)),
                  pl.BlockSpec((1, N_KV_GROUPS, V_AUG_ROWS, s), lambda b, i: (b, 0, 0, 0)),
                  _const_spec(bias.shape)],
        out_specs=pl.BlockSpec((1, tq, ATTN_WIDTH), lambda b, i: (b, i, 0)),
        out_shape=jax.ShapeDtypeStruct((bsz, s, ATTN_WIDTH), BF16),
        scratch_shapes=[pltpu.VMEM((s, tq), F32),
                        pltpu.VMEM((N_ATTN_HEADS, V_AUG_ROWS, tq), F32),
                        pltpu.VMEM((N_ATTN_HEADS, 1, tq), F32),
                        pltpu.VMEM((N_ATTN_HEADS, tk, tq), F32),
                        pltpu.VMEM((N_ATTN_HEADS, FAST_FAR_TILES * tk, tq), BF16),
                        pltpu.VMEM((FAST_FAR_TILES * tk, tq), F32)],
        compiler_params=_params("parallel", "arbitrary"),
        name="dsa",
    )(qit, wt, qt, ki, k, vt, bias)


def _split3(a):
    hi = a.astype(BF16)
    r1 = a - hi.astype(F32)
    mid = r1.astype(BF16)
    lo = (r1 - mid.astype(F32)).astype(BF16)
    return hi, mid, lo


def _hgrn2_body(rq_ref, ri_ref, rg_ref, rf_ref, lb_ref, nw_ref, o_ref,
                state_scr, tril_scr, lvl_scr, *, chunk):
    t_len = rq_ref.shape[1]
    c = chunk
    d = REC_HEAD_DIM
    nw = nw_ref[...]
    eye = (lax.broadcasted_iota(jnp.int32, (d, d), 0)
           == lax.broadcasted_iota(jnp.int32, (d, d), 1)).astype(F32)
    sub = lax.broadcasted_iota(jnp.int32, (c, d), 0) & (REC_BAND - 1)
    level_sizes = []
    b = REC_BAND
    while b < c:
        level_sizes.append(b)
        b *= 2

    @pl.when(pl.program_id(1) == 0)
    def _():
        state_scr[...] = jnp.zeros(state_scr.shape, F32)
        ti = lax.broadcasted_iota(jnp.int32, (c, c), 0)
        si = lax.broadcasted_iota(jnp.int32, (c, c), 1)
        tril_scr[...] = (si <= ti).astype(BF16)
        for li, b in enumerate(level_sizes):
            blk = 2 * b
            sh = blk.bit_length() - 1
            m = jnp.logical_and(
                lax.shift_right_logical(ti, sh) == lax.shift_right_logical(si, sh),
                jnp.logical_and((ti & (blk - 1)) >= b, (si & (blk - 1)) < b))
            lvl_scr[li] = m.astype(F32)

    def head_step(hh, t0):
        cols = slice(hh * d, (hh + 1) * d)
        lb = lb_ref[:, cols]
        state = state_scr[hh]
        f = lb + (1.0 - lb) * _sigmoid_rel(rf_ref[0, pl.ds(t0, c), cols])
        g = jnp.log(f)
        kk = 1.0 - f
        qq = _silu(rq_ref[0, pl.ds(t0, c), cols].astype(F32))
        vb = ri_ref[0, pl.ds(t0, c), cols]
        vv = vb.astype(F32)

        g3 = _split3(g)
        tril = tril_scr[...]
        a = (jnp.dot(tril, g3[0], preferred_element_type=F32)
             + jnp.dot(tril, g3[1], preferred_element_type=F32)
             + jnp.dot(tril, g3[2], preferred_element_type=F32))
        a_last = a[c - 1:c, :]

        o = jnp.dot((qq * jnp.exp(a)).astype(BF16), state.astype(BF16),
                    preferred_element_type=F32)

        p = jnp.zeros((c, c), F32)
        for li, b in enumerate(level_sizes):
            blk = 2 * b
            r = jnp.broadcast_to(a.reshape(c // blk, blk, d)[:, b - 1:b, :],
                                 (c // blk, blk, d)).reshape(c, d)
            qt = (qq * jnp.exp(jnp.minimum(a - r, 0.0))).astype(BF16)
            kt = (kk * jnp.exp(jnp.minimum(r - a, 0.0))).astype(BF16)
            pb = lax.dot_general(qt, kt, (((1,), (1,)), ((), ())), preferred_element_type=F32)
            p = p + pb * lvl_scr[li]
        o = o + jnp.dot(p.astype(BF16), vb, preferred_element_type=F32)

        def back(x, dl):
            if dl == 0:
                return x
            return pltpu.roll(x.reshape(c // SUBLANES, SUBLANES, d), dl, axis=1).reshape(c, d)

        for dl in range(REC_BAND):
            a_s = back(a, dl)
            k_s = back(kk, dl)
            v_s = back(vv, dl)
            e = jnp.exp(jnp.where(sub >= dl, a - a_s, NEG_BIG))
            pd = jnp.sum(qq * k_s * e, axis=1, keepdims=True)
            o = o + pd * v_s

        kd = (kk * jnp.exp(a_last - a)).astype(BF16)
        upd = lax.dot_general(kd, vb, (((0,), (0,)), ((), ())), preferred_element_type=F32)
        e_col = jnp.sum(eye * jnp.exp(a_last), axis=1, keepdims=True)
        state_scr[hh] = e_col * state + upd

        y = _rms(o, nw) * _silu(rg_ref[0, pl.ds(t0, c), cols].astype(F32))
        o_ref[0, pl.ds(t0, c), cols] = y.astype(BF16)

    def step(n, carry):
        t0 = pl.multiple_of(n * c, c)
        for hh in range(N_REC_HEADS):
            head_step(hh, t0)
        return carry

    lax.fori_loop(0, t_len // c, step, 0)


def _hgrn2(rec3, rf, lb, nw, *, chunk):
    bsz, s, _ = rec3.shape
    d = REC_HEAD_DIM
    h = N_REC_HEADS
    body = functools.partial(_hgrn2_body, chunk=chunk)
    tt = _pick_tile(s, REC_TIME_TILE)
    col = lambda part: pl.BlockSpec((1, tt, REC_WIDTH), lambda b, t: (b, t, part))
    return pl.pallas_call(
        body,
        grid=(bsz, s // tt),
        in_specs=[col(0), col(1), col(2), col(0),
                  pl.BlockSpec((1, REC_WIDTH), lambda b, t: (0, 0)),
                  pl.BlockSpec((1, d), lambda b, t: (0, 0))],
        out_specs=col(0),
        out_shape=jax.ShapeDtypeStruct((bsz, s, REC_WIDTH), BF16),
        scratch_shapes=[pltpu.VMEM((h, d, d), F32)]
        + [pltpu.VMEM((chunk, chunk), BF16),
           pltpu.VMEM(((chunk // REC_BAND).bit_length() - 1, chunk, chunk), F32)],
        compiler_params=_params("parallel", "arbitrary"),
        name="hgrn2",
    )(rec3, rec3, rec3, rf, lb.reshape(1, REC_WIDTH), nw.reshape(1, d))


def _out_ffn_body(x_ref, attn_ref, rec_ref, gates_ref, mod_ref, wba_ref, wbr_ref, wo_ref,
                  n2_ref, wg_ref, wu_ref, wd_ref, fn_ref, o_ref, *, final):
    d = x_ref.shape[2]
    x = x_ref[0]
    g1 = mod_ref[0, 2:3, :]
    sh2 = mod_ref[0, 3:4, :]
    sc2 = mod_ref[0, 4:5, :]
    g2 = mod_ref[0, 5:6, :]
    ya = jnp.dot(attn_ref[0], wba_ref[...], preferred_element_type=F32)
    yr = jnp.dot(rec_ref[0], wbr_ref[...], preferred_element_type=F32)
    gates = gates_ref[0].astype(F32)
    mix = (_sigmoid(gates[:, :d]) * ya + _sigmoid(gates[:, d:]) * yr).astype(BF16)
    x1 = x + g1 * jnp.dot(mix, wo_ref[...], preferred_element_type=F32)
    h2 = (_rms(x1, n2_ref[...]) * (1.0 + sc2) + sh2).astype(BF16)
    gate = jnp.dot(h2, wg_ref[...], preferred_element_type=F32)
    up = jnp.dot(h2, wu_ref[...], preferred_element_type=F32)
    act = (_silu(gate) * up).astype(BF16)
    x2 = x1 + g2 * jnp.dot(act, wd_ref[...], preferred_element_type=F32)
    o_ref[0] = _rms(x2, fn_ref[...]) if final else x2


def _out_ffn(x, attn, rec, gates, mod, wba, wbr, wo, n2w, wg, wu, wd, fnw, tm, final):
    bsz, s, d = x.shape
    hid = wg.shape[1]
    tok = lambda w: pl.BlockSpec((1, tm, w), lambda b, i: (b, i, 0))
    return pl.pallas_call(
        functools.partial(_out_ffn_body, final=final),
        grid=(bsz, s // tm),
        in_specs=[tok(d), tok(ATTN_WIDTH), tok(REC_WIDTH), tok(2 * d),
                  pl.BlockSpec((1, 6, d), lambda b, i: (b, 0, 0)),
                  _const_spec((ATTN_WIDTH, d)), _const_spec((REC_WIDTH, d)), _const_spec((d, d)),
                  _const_spec((1, d)), _const_spec((d, hid)), _const_spec((d, hid)),
                  _const_spec((hid, d)), _const_spec((1, d))],
        out_specs=tok(d),
        out_shape=jax.ShapeDtypeStruct((bsz, s, d), F32),
        compiler_params=_params("parallel", "parallel"),
        name="out_ffn",
    )(x, attn, rec, gates, mod, wba, wbr, wo, n2w, wg, wu, wd, fnw)


def _pick_tile(n, want):
    t = min(want, n)
    while n % t:
        t //= 2
    return t


def kernel(x, c, w_ada, b_ada, norm1_w, w_in, q_norm_w, w_uq, w_uq_idx, rel_bias, lb_logits,
           rec_norm_w, w_branch_attn, w_branch_rec, w_out, norm2_w, w_ffn_gate, w_ffn_up,
           w_ffn_down, final_norm_w):
    bsz, s, d = x.shape
    depth = w_in.shape[0]
    k_sel = min(TOPK_MAX, s // 4)
    tm = _pick_tile(s, TOKEN_TILE)
    tq = _pick_tile(s, Q_TILE)
    tk = max(tq, _pick_tile(s, K_TILE))
    chunk = _pick_tile(s, REC_CHUNK)
    lower_bounds = jnp.cumsum(jax.nn.softmax(lb_logits.astype(F32), axis=0), axis=0)
    bias = _near_bias(rel_bias, tq, tk)

    o_cq = 0
    o_k = o_cq + Q_LORA_RANK
    o_v = o_k + KV_WIDTH
    o_ki = o_v + KV_WIDTH
    o_wi = o_ki + IDX_HEAD_DIM
    o_rq = o_wi + N_IDX_HEADS
    o_rf = o_rq + REC_WIDTH
    o_ri = o_rf + REC_WIDTH
    o_rg = o_ri + REC_WIDTH
    o_gt = o_rg + REC_WIDTH

    for layer in range(depth):
        mod = _ada(c, w_ada[layer], b_ada[layer]).reshape(bsz, 6, d)
        wl = w_in[layer]
        pad_a = LANES - IDX_HEAD_DIM - N_IDX_HEADS
        wa = jnp.concatenate([wl[:, o_cq:o_rq], jnp.zeros((d, pad_a), wl.dtype)], axis=1).astype(BF16)
        wr = jnp.concatenate([wl[:, o_rq:o_rf], wl[:, o_ri:o_rg], wl[:, o_rg:o_gt],
                              wl[:, o_rf:o_ri]], axis=1).astype(BF16)
        wg = wl[:, o_gt:].astype(BF16)
        wuq = (w_uq[layer] * (ATTN_HEAD_DIM ** -0.5 * LOG2_E)).astype(BF16).T
        wui = (w_uq_idx[layer] * (IDX_HEAD_DIM ** -0.5)).astype(BF16).T

        qt, qit, k, vt, ki, wt, rec3, rf, gates = _in_proj(
            x, mod, norm1_w[layer].reshape(1, d), wa, wr, wg,
            q_norm_w[layer].reshape(1, Q_LORA_RANK), wuq, wui, tm)

        attn = _dsa(qt, qit, wt, ki, k, vt, bias, tq=tq, tk=tk, k_sel=k_sel)
        rec = _hgrn2(rec3, rf, lower_bounds[layer], rec_norm_w[layer], chunk=chunk)

        x = _out_ffn(x, attn, rec, gates, mod,
                     w_branch_attn[layer].astype(BF16), w_branch_rec[layer].astype(BF16),
                     w_out[layer].astype(BF16), norm2_w[layer].reshape(1, d),
                     w_ffn_gate[layer].astype(BF16), w_ffn_up[layer].astype(BF16),
                     w_ffn_down[layer].astype(BF16),
                     final_norm_w.reshape(1, d), tm, layer == depth - 1)
    return x
```

```python
import functools
import math

import jax
import jax.numpy as jnp
from jax import lax
from jax.experimental import pallas as pl
from jax.experimental.pallas import tpu as pltpu

F32 = jnp.float32
BF16 = jnp.bfloat16

N_ATTN_HEADS = 8
ATTN_HEAD_DIM = 64
N_KV_GROUPS = 2
HEADS_PER_GROUP = N_ATTN_HEADS // N_KV_GROUPS
Q_LORA_RANK = 256
N_IDX_HEADS = 8
IDX_HEAD_DIM = 64
TOPK_MAX = 256
N_BUCKETS = 32
MAX_DISTANCE = 128
N_REC_HEADS = 4
REC_HEAD_DIM = 128
EPS = 1e-6

ATTN_WIDTH = N_ATTN_HEADS * ATTN_HEAD_DIM
KV_WIDTH = N_KV_GROUPS * ATTN_HEAD_DIM
REC_WIDTH = N_REC_HEADS * REC_HEAD_DIM

LANES = 128
SUBLANES = 8
BF16_SUBLANES = 16
VMEM_LIMIT_BYTES = 56 * 1024 * 1024

ADA_TILE = 1024
TOKEN_TILE = 512
Q_TILE = 256
K_TILE = 512
REC_CHUNK = 128
REC_TIME_TILE = 1024
REC_BAND = 4
NEG_BIG = -1e30
BISECT_CAP = 700
SNAP_AFTER = 14
FOLD_CHAINS = 4
HI_MARGIN = 2.0 ** -20
HI_MARGIN_MIN = 1e-30
PASS_ROWS = 128
FAST_FAR_TILES = 4
LOG2_E = math.log2(math.e)
MAX_SANE_DENOM = 1e30
MIN_SANE_DENOM = 1e-18
V_AUG_ROWS = ATTN_HEAD_DIM + BF16_SUBLANES


def _rms(x, w):
    return x * lax.rsqrt(jnp.mean(x * x, axis=-1, keepdims=True) + EPS) * w


def _sigmoid(x):
    return 0.5 * jnp.tanh(0.5 * x) + 0.5


def _sigmoid_rel(x):
    return 1.0 / (1.0 + jnp.exp(-x))


def _silu(x):
    return x * _sigmoid(x)


def _params(*sem):
    return pltpu.CompilerParams(dimension_semantics=sem, vmem_limit_bytes=VMEM_LIMIT_BYTES)


def _const_spec(shape):
    nd = len(shape)
    return pl.BlockSpec(shape, lambda *_: (0,) * nd, pipeline_mode=pl.Buffered(1))


def _ada_body(c_ref, w_ref, b_ref, o_ref):
    ca = _silu(c_ref[...])
    o_ref[...] = jnp.dot(ca, w_ref[...], precision=lax.Precision.HIGHEST,
                         preferred_element_type=F32) + b_ref[...]


def _ada(c, w, b):
    bsz, d = c.shape
    n = w.shape[1]
    tn = _pick_tile(n, ADA_TILE)
    return pl.pallas_call(
        _ada_body,
        grid=(n // tn,),
        in_specs=[pl.BlockSpec((bsz, d), lambda j: (0, 0)),
                  pl.BlockSpec((d, tn), lambda j: (0, j)),
                  pl.BlockSpec((1, tn), lambda j: (0, j))],
        out_specs=pl.BlockSpec((bsz, tn), lambda j: (0, j)),
        out_shape=jax.ShapeDtypeStruct((bsz, n), F32),
        compiler_params=_params("arbitrary"),
        name="ada",
    )(c, w, b.reshape(1, n))


def _in_proj_body(x_ref, mod_ref, n1_ref, wa_ref, wr_ref, wg_ref, qn_ref, wuq_ref, wui_ref,
                  qt_ref, qit_ref, k_ref, vt_ref, ki_ref, wt_ref, rec_ref, rf_ref, gates_ref):
    tm = x_ref.shape[1]
    x = x_ref[0]
    sh1 = mod_ref[0, 0:1, :]
    sc1 = mod_ref[0, 1:2, :]
    h = (_rms(x, n1_ref[...]) * (1.0 + sc1) + sh1).astype(BF16)

    pa = jnp.dot(h, wa_ref[...], preferred_element_type=F32)
    o_k, o_v, o_ki = Q_LORA_RANK, Q_LORA_RANK + KV_WIDTH, Q_LORA_RANK + 2 * KV_WIDTH
    v_t = pa[:, o_v:o_ki].T
    kiw_t = pa[:, o_ki:].T
    for g in range(N_KV_GROUPS):
        k_ref[0, g] = pa[:, o_k + g * ATTN_HEAD_DIM:o_k + (g + 1) * ATTN_HEAD_DIM].astype(BF16)
        vt_ref[0, g, :ATTN_HEAD_DIM, :] = v_t[g * ATTN_HEAD_DIM:(g + 1) * ATTN_HEAD_DIM].astype(BF16)
        vt_ref[0, g, ATTN_HEAD_DIM:, :] = jnp.ones((V_AUG_ROWS - ATTN_HEAD_DIM, tm), BF16)
    ki_ref[0] = pa[:, o_ki:o_ki + IDX_HEAD_DIM].astype(BF16)
    wt_ref[0] = kiw_t[IDX_HEAD_DIM:IDX_HEAD_DIM + N_IDX_HEADS]
    cqn = _rms(pa[:, :Q_LORA_RANK], qn_ref[...]).astype(BF16)
    nt = (((1,), (1,)), ((), ()))
    qf = lax.dot_general(wuq_ref[...], cqn, nt, preferred_element_type=F32)
    qt_ref[0] = qf.reshape(N_ATTN_HEADS, ATTN_HEAD_DIM, tm).astype(BF16)
    qif = lax.dot_general(wui_ref[...], cqn, nt, preferred_element_type=F32)
    qit_ref[0] = qif.reshape(N_IDX_HEADS, IDX_HEAD_DIM, tm).astype(BF16)

    pr = jnp.dot(h, wr_ref[...], preferred_element_type=F32)
    rec_ref[0] = pr[:, :3 * REC_WIDTH].astype(BF16)
    rf_ref[0] = pr[:, 3 * REC_WIDTH:]
    gates_ref[0] = jnp.dot(h, wg_ref[...], preferred_element_type=F32).astype(BF16)


def _in_proj(x, mod, n1w, wa, wr, wg, qnw, wuq, wui, tm):
    bsz, s, d = x.shape
    wa_n = wa.shape[1]
    grid = (bsz, s // tm)
    tok = lambda w: pl.BlockSpec((1, tm, w), lambda b, i: (b, i, 0))
    head = lambda nh, hd: pl.BlockSpec((1, nh, hd, tm), lambda b, i: (b, 0, 0, i))
    out_shape = (
        jax.ShapeDtypeStruct((bsz, N_ATTN_HEADS, ATTN_HEAD_DIM, s), BF16),
        jax.ShapeDtypeStruct((bsz, N_IDX_HEADS, IDX_HEAD_DIM, s), BF16),
        jax.ShapeDtypeStruct((bsz, N_KV_GROUPS, s, ATTN_HEAD_DIM), BF16),
        jax.ShapeDtypeStruct((bsz, N_KV_GROUPS, V_AUG_ROWS, s), BF16),
        jax.ShapeDtypeStruct((bsz, s, IDX_HEAD_DIM), BF16),
        jax.ShapeDtypeStruct((bsz, N_IDX_HEADS, s), F32),
        jax.ShapeDtypeStruct((bsz, s, 3 * REC_WIDTH), BF16),
        jax.ShapeDtypeStruct((bsz, s, REC_WIDTH), F32),
        jax.ShapeDtypeStruct((bsz, s, 2 * d), BF16),
    )
    out_specs = (head(N_ATTN_HEADS, ATTN_HEAD_DIM), head(N_IDX_HEADS, IDX_HEAD_DIM),
                 pl.BlockSpec((1, N_KV_GROUPS, tm, ATTN_HEAD_DIM), lambda b, i: (b, 0, i, 0)),
                 head(N_KV_GROUPS, V_AUG_ROWS),
                 tok(IDX_HEAD_DIM),
                 pl.BlockSpec((1, N_IDX_HEADS, tm), lambda b, i: (b, 0, i)),
                 tok(3 * REC_WIDTH), tok(REC_WIDTH), tok(2 * d))
    return pl.pallas_call(
        _in_proj_body,
        grid=grid,
        in_specs=[tok(d),
                  pl.BlockSpec((1, 6, d), lambda b, i: (b, 0, 0)),
                  _const_spec((1, d)),
                  _const_spec((d, wa_n)),
                  _const_spec((d, 4 * REC_WIDTH)),
                  _const_spec((d, 2 * d)),
                  _const_spec((1, Q_LORA_RANK)),
                  _const_spec((ATTN_WIDTH, Q_LORA_RANK)),
                  _const_spec((N_IDX_HEADS * IDX_HEAD_DIM, Q_LORA_RANK))],
        out_specs=out_specs,
        out_shape=out_shape,
        compiler_params=_params("parallel", "parallel"),
        name="in_proj",
    )(x, mod, n1w, wa, wr, wg, qnw, wuq, wui)


def _near_bias(rel_bias, tq, tk):
    n = jnp.arange(MAX_DISTANCE + 1, dtype=jnp.int32)
    max_exact = N_BUCKETS // 2
    nf = jnp.maximum(n, 1).astype(F32)
    large = max_exact + (jnp.log(nf / max_exact) / math.log(MAX_DISTANCE / max_exact)
                         * (N_BUCKETS - max_exact)).astype(jnp.int32)
    bucket = jnp.where(n < max_exact, n, jnp.minimum(large, N_BUCKETS - 1))
    bucket = bucket.at[MAX_DISTANCE].set(N_BUCKETS - 1)
    ids = jnp.arange(N_BUCKETS, dtype=jnp.int32)
    start = jnp.sum((bucket[None, :] < ids[:, None]).astype(jnp.int32), axis=1)
    rb = (rel_bias.astype(F32) - rel_bias[N_BUCKETS - 1].astype(F32)[None, :]) * LOG2_E
    step = rb - jnp.concatenate([jnp.zeros_like(rb[:1]), rb[:-1]], axis=0)
    dist = (jnp.arange(tq, dtype=jnp.int32)[None, :]
            - jnp.arange(2 * tk, dtype=jnp.int32)[:, None] + tk)
    reached = (dist[None, :, :] >= start[:, None, None]).astype(F32)
    return jnp.einsum("bh,but->hut", step, reached, precision=lax.Precision.HIGHEST)


def _dsa_body(qit_ref, w_ref, qt_ref, ki_ref, k_ref, vt_ref, bias_ref, o_ref,
              sc_ref, acc_ref, m_ref, s_scr, p_scr, mask_scr, *, tq, tk, k_sel):
    i = pl.program_id(1)
    row0 = i * tq
    n_kt = lax.div(row0 + tq + (tk - 1), tk)
    j_near = lax.div(jnp.maximum(row0 - (MAX_DISTANCE - 1), 0), tk)
    krow = lax.broadcasted_iota(jnp.int32, (tk, tq), 0)
    qcol = lax.broadcasted_iota(jnp.int32, (tk, tq), 1)
    rel = krow - qcol
    kf = float(k_sel)

    def fold(a, op, group=SUBLANES):
        n = a.shape[0] // group
        a = a.reshape(n // FOLD_CHAINS, FOLD_CHAINS, group, tq)
        r = a[0]
        for t in range(1, n // FOLD_CHAINS):
            r = op(r, a[t])
        while r.shape[0] > 1:
            half = r.shape[0] // 2
            r = op(r[:half], r[half:])
        return r[0]

    w = w_ref[0] * (N_IDX_HEADS ** -0.5)

    def score_tile(j, carry):
        mx, mn = carry
        c0 = pl.multiple_of(j * tk, tk)
        kin = ki_ref[0, pl.ds(c0, tk), :]
        sc = None
        for h in range(N_IDX_HEADS):
            lg = jnp.dot(kin, qit_ref[0, h], preferred_element_type=F32)
            term = jnp.maximum(lg, 0.0) * w[h:h + 1, :]
            sc = term if sc is None else sc + term
        causal = rel <= (row0 - c0)
        scm = jnp.where(causal, sc, NEG_BIG)
        sc_ref[pl.ds(c0, tk), :] = scm
        mx = jnp.maximum(mx, fold(scm, jnp.maximum))
        mn = jnp.minimum(mn, fold(jnp.where(causal, sc, -NEG_BIG), jnp.minimum))
        return mx, mn

    def score_pair(t, carry):
        carry = score_tile(2 * t, carry)
        return score_tile(jnp.minimum(2 * t + 1, n_kt - 1), carry)

    mx8, mn8 = lax.fori_loop(0, lax.div(n_kt + 1, 2), score_pair,
                             (jnp.full((SUBLANES, tq), NEG_BIG, F32),
                              jnp.full((SUBLANES, tq), -NEG_BIG, F32)))
    mx = jnp.max(mx8, axis=0, keepdims=True)
    mn = jnp.min(mn8, axis=0, keepdims=True)

    def key_pass(fn, op, init):
        def one_tile(j, acc):
            c0 = pl.multiple_of(j * tk, tk)
            for u in range(tk // PASS_ROWS):
                s = sc_ref[pl.ds(c0 + u * PASS_ROWS, PASS_ROWS), :]
                acc = op(acc, fold(fn(s), op))
            return acc

        pairs = lax.shift_right_logical(n_kt, 1)
        acc = lax.fori_loop(0, pairs, lambda t, a: one_tile(t + pairs, one_tile(t, a)),
                            jnp.full((SUBLANES, tq), init, F32))
        return lax.cond((n_kt & 1) == 1, lambda a: one_tile(n_kt - 1, a), lambda a: a, acc)

    def count(pred):
        part = key_pass(lambda s: jnp.where(pred(s), 1.0, 0.0), jnp.add, 0.0)
        return jnp.sum(part, axis=0, keepdims=True)

    n_valid = (row0 + 1 + lax.broadcasted_iota(jnp.int32, (1, tq), 1)).astype(F32)
    take_all = n_valid <= kf
    lo0 = jnp.where(take_all, NEG_BIG, mn)
    hi0 = mx + (jnp.abs(mx) * HI_MARGIN + HI_MARGIN_MIN)

    def bis_cond(st):
        it, lo, hi, c_lo, done_f = st
        return jnp.logical_and(it < BISECT_CAP, jnp.min(done_f) < 0.5)

    def halve(st):
        lo, hi, c_lo, done_f = st
        done = done_f > 0.5
        mid = lo + (hi - lo) * 0.5
        stuck = jnp.logical_or(mid <= lo, mid >= hi)
        cnt = count(lambda s: s >= mid)
        ge = cnt >= kf
        upd = jnp.logical_not(jnp.logical_or(done, stuck))
        raise_lo = jnp.logical_and(upd, ge)
        lo = jnp.where(raise_lo, mid, lo)
        c_lo = jnp.where(raise_lo, cnt, c_lo)
        hi = jnp.where(jnp.logical_and(upd, jnp.logical_not(ge)), mid, hi)
        done = jnp.logical_or(jnp.logical_or(done, stuck), jnp.logical_and(ge, cnt == kf))
        return lo, hi, c_lo, done.astype(F32)

    def snap(st):
        lo, hi, c_lo, done_f = st
        done = done_f > 0.5

        below = jnp.max(key_pass(lambda s: jnp.where(s < hi, s, NEG_BIG), jnp.maximum, NEG_BIG),
                        axis=0, keepdims=True)
        cnt = count(lambda s: s >= below)
        found = jnp.logical_and(jnp.logical_not(done), cnt >= kf)
        lower_hi = jnp.logical_and(jnp.logical_not(done), cnt < kf)
        lo = jnp.where(found, below, lo)
        c_lo = jnp.where(found, cnt, c_lo)
        hi = jnp.where(lower_hi, below, hi)
        return lo, hi, c_lo, jnp.logical_or(done, found).astype(F32)

    def bis_step(st):
        it, lo, hi, c_lo, done_f = st
        lo, hi, c_lo, done_f = lax.cond((it & 1) == 0, snap, halve, (lo, hi, c_lo, done_f))
        return it + 1, lo, hi, c_lo, done_f

    st = lax.fori_loop(0, SNAP_AFTER, lambda _, s: halve(s),
                       (lo0, hi0, n_valid, take_all.astype(F32)))
    _, thr, _, c_thr, _ = lax.while_loop(bis_cond, bis_step, (jnp.int32(0),) + st)

    excess = jnp.logical_and(jnp.logical_not(take_all), c_thr > kf)

    @pl.when(jnp.max(excess.astype(F32)) > 0.5)
    def _():
        need = jnp.where(excess, kf - count(lambda s: s > thr), float(sc_ref.shape[0]))
        tril = (lax.broadcasted_iota(jnp.int32, (tk, tk), 1)
                <= lax.broadcasted_iota(jnp.int32, (tk, tk), 0)).astype(BF16)

        def drop_tile(j, seen):
            c0 = pl.multiple_of(j * tk, tk)
            s = sc_ref[pl.ds(c0, tk), :]
            eq = s == thr
            rank = seen + jnp.dot(tril, jnp.where(eq, 1.0, 0.0).astype(BF16),
                                  preferred_element_type=F32)
            sc_ref[pl.ds(c0, tk), :] = jnp.where(jnp.logical_and(eq, rank > need), NEG_BIG, s)
            return rank[tk - 1:tk, :]

        lax.fori_loop(0, n_kt, drop_tile, jnp.zeros((1, tq), F32))

    def logits(j, h, near, rows=tk):
        c0 = pl.multiple_of(j * tk, tk)
        g = h // HEADS_PER_GROUP
        s = jnp.dot(k_ref[0, g, pl.ds(c0, rows), :], qt_ref[0, h],
                    preferred_element_type=F32)
        if near:
            bias_row = pl.multiple_of(c0 - row0 + tk, tq)
            s = s + bias_ref[h, pl.ds(bias_row, tk), :]
        return s

    def selected(j, near, rows=tk):
        c0 = pl.multiple_of(j * tk, tk)
        keep = sc_ref[pl.ds(c0, rows), :] >= thr
        if near:
            keep = jnp.logical_and(keep, rel <= (row0 - c0))
        return keep

    def attend_exact(j, near):
        c0 = pl.multiple_of(j * tk, tk)
        mask_scr[:tk] = jnp.where(selected(j, near), 0.0, NEG_BIG)
        tile_max = []
        for h in range(N_ATTN_HEADS):
            s = logits(j, h, near) + mask_scr[:tk]
            s_scr[h] = s
            tile_max.append(jnp.max(fold(s, jnp.maximum), axis=0, keepdims=True))
        for h in range(N_ATTN_HEADS):
            g = h // HEADS_PER_GROUP
            m_old = m_ref[h]
            m_new = jnp.maximum(m_old, tile_max[h])
            alpha = jnp.exp2(m_old - m_new)
            p_scr[h, :tk] = jnp.exp2(s_scr[h] - m_new).astype(BF16)
            m_ref[h] = m_new
            pv = jnp.dot(vt_ref[0, g, :, pl.ds(c0, tk)], p_scr[h, :tk],
                         preferred_element_type=F32)
            acc_ref[h] = acc_ref[h] * alpha + pv

    def attend_fast(j, near, rows=tk):
        c0 = pl.multiple_of(j * tk, tk)
        mask_scr[:rows] = jnp.where(selected(j, near, rows), 0.0, NEG_BIG)
        for h in range(N_ATTN_HEADS):
            p_scr[h, :rows] = jnp.exp2(logits(j, h, near, rows) + mask_scr[:rows]).astype(BF16)
        for h in range(N_ATTN_HEADS):
            g = h // HEADS_PER_GROUP
            acc_ref[h] += jnp.dot(vt_ref[0, g, :, pl.ds(c0, rows)], p_scr[h, :rows],
                                  preferred_element_type=F32)

    def run_tiles(attend, far_tiles_per_step):
        acc_ref[...] = jnp.zeros(acc_ref.shape, F32)
        m_ref[...] = jnp.full(m_ref.shape, NEG_BIG, F32)
        w = far_tiles_per_step

        def far_step(t, c):
            if w == 1:
                attend(t, False)
            else:
                attend(t * w, False, w * tk)
            return c

        n_wide = lax.div(j_near, w)
        lax.fori_loop(0, n_wide, far_step, 0)
        for u in range(w - 1):
            @pl.when(n_wide * w + u < j_near)
            def _():
                attend(n_wide * w + u, False)
        for u in range(2):
            @pl.when(j_near + u < n_kt)
            def _():
                attend(j_near + u, True)

    run_tiles(attend_fast, FAST_FAR_TILES)
    denom = acc_ref[:, ATTN_HEAD_DIM:ATTN_HEAD_DIM + 1, :]
    sane = jnp.logical_and(denom > MIN_SANE_DENOM, denom < MAX_SANE_DENOM)

    @pl.when(jnp.min(sane.astype(F32)) < 0.5)
    def _():
        run_tiles(attend_exact, 1)

    outs = []
    for h in range(N_ATTN_HEADS):
        a = acc_ref[h]
        outs.append(a[:ATTN_HEAD_DIM, :] / a[ATTN_HEAD_DIM:ATTN_HEAD_DIM + 1, :])
    o_ref[0] = jnp.concatenate(outs, axis=0).T.astype(BF16)


def _dsa(qt, qit, wt, ki, k, vt, bias, *, tq, tk, k_sel):
    bsz, _, _, s = qt.shape
    assert tq >= MAX_DISTANCE and tk % tq == 0 and s % tk == 0
    body = functools.partial(_dsa_body, tq=tq, tk=tk, k_sel=k_sel)
    return pl.pallas_call(
        body,
        grid=(bsz, s // tq),
        in_specs=[pl.BlockSpec((1, N_IDX_HEADS, IDX_HEAD_DIM, tq), lambda b, i: (b, 0, 0, i)),
                  pl.BlockSpec((1, N_IDX_HEADS, tq), lambda b, i: (b, 0, i)),
                  pl.BlockSpec((1, N_ATTN_HEADS, ATTN_HEAD_DIM, tq), lambda b, i: (b, 0, 0, i)),
                  pl.BlockSpec((1, s, IDX_HEAD_DIM), lambda b, i: (b, 0, 0)),
                  pl.BlockSpec((1, N_KV_GROUPS, s, ATTN_HEAD_DIM), lambda b, i: (b, 0, 0, 0)),
                  pl.BlockSpec((1, N_KV_GROUPS, V_AUG_ROWS, s), lambda b, i: (b, 0, 0, 0)),
                  _const_spec(bias.shape)],
        out_specs=pl.BlockSpec((1, tq, ATTN_WIDTH), lambda b, i: (b, i, 0)),
        out_shape=jax.ShapeDtypeStruct((bsz, s, ATTN_WIDTH), BF16),
        scratch_shapes=[pltpu.VMEM((s, tq), F32),
                        pltpu.VMEM((N_ATTN_HEADS, V_AUG_ROWS, tq), F32),
                        pltpu.VMEM((N_ATTN_HEADS, 1, tq), F32),
                        pltpu.VMEM((N_ATTN_HEADS, tk, tq), F32),
                        pltpu.VMEM((N_ATTN_HEADS, FAST_FAR_TILES * tk, tq), BF16),
                        pltpu.VMEM((FAST_FAR_TILES * tk, tq), F32)],
        compiler_params=_params("parallel", "arbitrary"),
        name="dsa",
    )(qit, wt, qt, ki, k, vt, bias)


def _split3(a):
    hi = a.astype(BF16)
    r1 = a - hi.astype(F32)
    mid = r1.astype(BF16)
    lo = (r1 - mid.astype(F32)).astype(BF16)
    return hi, mid, lo


def _hgrn2_body(rq_ref, ri_ref, rg_ref, rf_ref, lb_ref, nw_ref, o_ref,
                state_scr, tril_scr, lvl_scr, *, chunk):
    t_len = rq_ref.shape[1]
    c = chunk
    d = REC_HEAD_DIM
    nw = nw_ref[...]
    eye = (lax.broadcasted_iota(jnp.int32, (d, d), 0)
           == lax.broadcasted_iota(jnp.int32, (d, d), 1)).astype(F32)
    sub = lax.broadcasted_iota(jnp.int32, (c, d), 0) & (REC_BAND - 1)
    level_sizes = []
    b = REC_BAND
    while b < c:
        level_sizes.append(b)
        b *= 2

    @pl.when(pl.program_id(1) == 0)
    def _():
        state_scr[...] = jnp.zeros(state_scr.shape, F32)
        ti = lax.broadcasted_iota(jnp.int32, (c, c), 0)
        si = lax.broadcasted_iota(jnp.int32, (c, c), 1)
        tril_scr[...] = (si <= ti).astype(BF16)
        for li, b in enumerate(level_sizes):
            blk = 2 * b
            sh = blk.bit_length() - 1
            m = jnp.logical_and(
                lax.shift_right_logical(ti, sh) == lax.shift_right_logical(si, sh),
                jnp.logical_and((ti & (blk - 1)) >= b, (si & (blk - 1)) < b))
            lvl_scr[li] = m.astype(F32)

    def head_step(hh, t0):
        cols = slice(hh * d, (hh + 1) * d)
        lb = lb_ref[:, cols]
        state = state_scr[hh]
        f = lb + (1.0 - lb) * _sigmoid_rel(rf_ref[0, pl.ds(t0, c), cols])
        g = jnp.log(f)
        kk = 1.0 - f
        qq = _silu(rq_ref[0, pl.ds(t0, c), cols].astype(F32))
        vb = ri_ref[0, pl.ds(t0, c), cols]
        vv = vb.astype(F32)

        g3 = _split3(g)
        tril = tril_scr[...]
        a = (jnp.dot(tril, g3[0], preferred_element_type=F32)
             + jnp.dot(tril, g3[1], preferred_element_type=F32)
             + jnp.dot(tril, g3[2], preferred_element_type=F32))
        a_last = a[c - 1:c, :]

        o = jnp.dot((qq * jnp.exp(a)).astype(BF16), state.astype(BF16),
                    preferred_element_type=F32)

        p = jnp.zeros((c, c), F32)
        for li, b in enumerate(level_sizes):
            blk = 2 * b
            r = jnp.broadcast_to(a.reshape(c // blk, blk, d)[:, b - 1:b, :],
                                 (c // blk, blk, d)).reshape(c, d)
            qt = (qq * jnp.exp(jnp.minimum(a - r, 0.0))).astype(BF16)
            kt = (kk * jnp.exp(jnp.minimum(r - a, 0.0))).astype(BF16)
            pb = lax.dot_general(qt, kt, (((1,), (1,)), ((), ())), preferred_element_type=F32)
            p = p + pb * lvl_scr[li]
        o = o + jnp.dot(p.astype(BF16), vb, preferred_element_type=F32)

        def back(x, dl):
            if dl == 0:
                return x
            return pltpu.roll(x.reshape(c // SUBLANES, SUBLANES, d), dl, axis=1).reshape(c, d)

        for dl in range(REC_BAND):
            a_s = back(a, dl)
            k_s = back(kk, dl)
            v_s = back(vv, dl)
            e = jnp.exp(jnp.where(sub >= dl, a - a_s, NEG_BIG))
            pd = jnp.sum(qq * k_s * e, axis=1, keepdims=True)
            o = o + pd * v_s

        kd = (kk * jnp.exp(a_last - a)).astype(BF16)
        upd = lax.dot_general(kd, vb, (((0,), (0,)), ((), ())), preferred_element_type=F32)
        e_col = jnp.sum(eye * jnp.exp(a_last), axis=1, keepdims=True)
        state_scr[hh] = e_col * state + upd

        y = _rms(o, nw) * _silu(rg_ref[0, pl.ds(t0, c), cols].astype(F32))
        o_ref[0, pl.ds(t0, c), cols] = y.astype(BF16)

    def step(n, carry):
        t0 = pl.multiple_of(n * c, c)
        for hh in range(N_REC_HEADS):
            head_step(hh, t0)
        return carry

    lax.fori_loop(0, t_len // c, step, 0)


def _hgrn2(rec3, rf, lb, nw, *, chunk):
    bsz, s, _ = rec3.shape
    d = REC_HEAD_DIM
    h = N_REC_HEADS
    body = functools.partial(_hgrn2_body, chunk=chunk)
    tt = _pick_tile(s, REC_TIME_TILE)
    col = lambda part: pl.BlockSpec((1, tt, REC_WIDTH), lambda b, t: (b, t, part))
    return pl.pallas_call(
        body,
        grid=(bsz, s // tt),
        in_specs=[col(0), col(1), col(2), col(0),
                  pl.BlockSpec((1, REC_WIDTH), lambda b, t: (0, 0)),
                  pl.BlockSpec((1, d), lambda b, t: (0, 0))],
        out_specs=col(0),
        out_shape=jax.ShapeDtypeStruct((bsz, s, REC_WIDTH), BF16),
        scratch_shapes=[pltpu.VMEM((h, d, d), F32)]
        + [pltpu.VMEM((chunk, chunk), BF16),
           pltpu.VMEM(((chunk // REC_BAND).bit_length() - 1, chunk, chunk), F32)],
        compiler_params=_params("parallel", "arbitrary"),
        name="hgrn2",
    )(rec3, rec3, rec3, rf, lb.reshape(1, REC_WIDTH), nw.reshape(1, d))


def _out_ffn_body(x_ref, attn_ref, rec_ref, gates_ref, mod_ref, wba_ref, wbr_ref, wo_ref,
                  n2_ref, wg_ref, wu_ref, wd_ref, fn_ref, o_ref, *, final):
    d = x_ref.shape[2]
    x = x_ref[0]
    g1 = mod_ref[0, 2:3, :]
    sh2 = mod_ref[0, 3:4, :]
    sc2 = mod_ref[0, 4:5, :]
    g2 = mod_ref[0, 5:6, :]
    ya = jnp.dot(attn_ref[0], wba_ref[...], preferred_element_type=F32)
    yr = jnp.dot(rec_ref[0], wbr_ref[...], preferred_element_type=F32)
    gates = gates_ref[0].astype(F32)
    mix = (_sigmoid(gates[:, :d]) * ya + _sigmoid(gates[:, d:]) * yr).astype(BF16)
    x1 = x + g1 * jnp.dot(mix, wo_ref[...], preferred_element_type=F32)
    h2 = (_rms(x1, n2_ref[...]) * (1.0 + sc2) + sh2).astype(BF16)
    gate = jnp.dot(h2, wg_ref[...], preferred_element_type=F32)
    up = jnp.dot(h2, wu_ref[...], preferred_element_type=F32)
    act = (_silu(gate) * up).astype(BF16)
    x2 = x1 + g2 * jnp.dot(act, wd_ref[...], preferred_element_type=F32)
    o_ref[0] = _rms(x2, fn_ref[...]) if final else x2


def _out_ffn(x, attn, rec, gates, mod, wba, wbr, wo, n2w, wg, wu, wd, fnw, tm, final):
    bsz, s, d = x.shape
    hid = wg.shape[1]
    tok = lambda w: pl.BlockSpec((1, tm, w), lambda b, i: (b, i, 0))
    return pl.pallas_call(
        functools.partial(_out_ffn_body, final=final),
        grid=(bsz, s // tm),
        in_specs=[tok(d), tok(ATTN_WIDTH), tok(REC_WIDTH), tok(2 * d),
                  pl.BlockSpec((1, 6, d), lambda b, i: (b, 0, 0)),
                  _const_spec((ATTN_WIDTH, d)), _const_spec((REC_WIDTH, d)), _const_spec((d, d)),
                  _const_spec((1, d)), _const_spec((d, hid)), _const_spec((d, hid)),
                  _const_spec((hid, d)), _const_spec((1, d))],
        out_specs=tok(d),
        out_shape=jax.ShapeDtypeStruct((bsz, s, d), F32),
        compiler_params=_params("parallel", "parallel"),
        name="out_ffn",
    )(x, attn, rec, gates, mod, wba, wbr, wo, n2w, wg, wu, wd, fnw)


def _pick_tile(n, want):
    t = min(want, n)
    while n % t:
        t //= 2
    return t


def kernel(x, c, w_ada, b_ada, norm1_w, w_in, q_norm_w, w_uq, w_uq_idx, rel_bias, lb_logits,
           rec_norm_w, w_branch_attn, w_branch_rec, w_out, norm2_w, w_ffn_gate, w_ffn_up,
           w_ffn_down, final_norm_w):
    bsz, s, d = x.shape
    depth = w_in.shape[0]
    k_sel = min(TOPK_MAX, s // 4)
    tm = _pick_tile(s, TOKEN_TILE)
    tq = _pick_tile(s, Q_TILE)
    tk = max(tq, _pick_tile(s, K_TILE))
    chunk = _pick_tile(s, REC_CHUNK)
    lower_bounds = jnp.cumsum(jax.nn.softmax(lb_logits.astype(F32), axis=0), axis=0)
    bias = _near_bias(rel_bias, tq, tk)

    o_cq = 0
    o_k = o_cq + Q_LORA_RANK
    o_v = o_k + KV_WIDTH
    o_ki = o_v + KV_WIDTH
    o_wi = o_ki + IDX_HEAD_DIM
    o_rq = o_wi + N_IDX_HEADS
    o_rf = o_rq + REC_WIDTH
    o_ri = o_rf + REC_WIDTH
    o_rg = o_ri + REC_WIDTH
    o_gt = o_rg + REC_WIDTH

    for layer in range(depth):
        mod = _ada(c, w_ada[layer], b_ada[layer]).reshape(bsz, 6, d)
        wl = w_in[layer]
        pad_a = LANES - IDX_HEAD_DIM - N_IDX_HEADS
        wa = jnp.concatenate([wl[:, o_cq:o_rq], jnp.zeros((d, pad_a), wl.dtype)], axis=1).astype(BF16)
        wr = jnp.concatenate([wl[:, o_rq:o_rf], wl[:, o_ri:o_rg], wl[:, o_rg:o_gt],
                              wl[:, o_rf:o_ri]], axis=1).astype(BF16)
        wg = wl[:, o_gt:].astype(BF16)
        wuq = (w_uq[layer] * (ATTN_HEAD_DIM ** -0.5 * LOG2_E)).astype(BF16).T
        wui = (w_uq_idx[layer] * (IDX_HEAD_DIM ** -0.5)).astype(BF16).T

        qt, qit, k, vt, ki, wt, rec3, rf, gates = _in_proj(
            x, mod, norm1_w[layer].reshape(1, d), wa, wr, wg,
            q_norm_w[layer].reshape(1, Q_LORA_RANK), wuq, wui, tm)

        attn = _dsa(qt, qit, wt, ki, k, vt, bias, tq=tq, tk=tk, k_sel=k_sel)
        rec = _hgrn2(rec3, rf, lower_bounds[layer], rec_norm_w[layer], chunk=chunk)

        x = _out_ffn(x, attn, rec, gates, mod,
                     w_branch_attn[layer].astype(BF16), w_branch_rec[layer].astype(BF16),
                     w_out[layer].astype(BF16), norm2_w[layer].reshape(1, d),
                     w_ffn_gate[layer].astype(BF16), w_ffn_up[layer].astype(BF16),
                     w_ffn_down[layer].astype(BF16),
                     final_norm_w.reshape(1, d), tm, layer == depth - 1)
    return x
```

```python
import functools
import math

import jax
import jax.numpy as jnp
from jax import lax
from jax.experimental import pallas as pl
from jax.experimental.pallas import tpu as pltpu

F32 = jnp.float32
BF16 = jnp.bfloat16

N_ATTN_HEADS = 8
ATTN_HEAD_DIM = 64
N_KV_GROUPS = 2
HEADS_PER_GROUP = N_ATTN_HEADS // N_KV_GROUPS
Q_LORA_RANK = 256
N_IDX_HEADS = 8
IDX_HEAD_DIM = 64
TOPK_MAX = 256
N_BUCKETS = 32
MAX_DISTANCE = 128
N_REC_HEADS = 4
REC_HEAD_DIM = 128
EPS = 1e-6

ATTN_WIDTH = N_ATTN_HEADS * ATTN_HEAD_DIM
KV_WIDTH = N_KV_GROUPS * ATTN_HEAD_DIM
REC_WIDTH = N_REC_HEADS * REC_HEAD_DIM

LANES = 128
SUBLANES = 8
BF16_SUBLANES = 16
VMEM_LIMIT_BYTES = 56 * 1024 * 1024

ADA_TILE = 1024
TOKEN_TILE = 512
Q_TILE = 256
K_TILE = 512
REC_CHUNK = 128
REC_TIME_TILE = 1024
REC_BAND = 4
NEG_BIG = -1e30
BISECT_CAP = 700
SNAP_AFTER = 14
FOLD_CHAINS = 4
HI_MARGIN = 2.0 ** -20
HI_MARGIN_MIN = 1e-30
PASS_ROWS = 128
FAST_FAR_TILES = 4
LOG2_E = math.log2(math.e)
MAX_SANE_DENOM = 1e30
MIN_SANE_DENOM = 1e-18
V_AUG_ROWS = ATTN_HEAD_DIM + BF16_SUBLANES


def _rms(x, w):
    return x * lax.rsqrt(jnp.mean(x * x, axis=-1, keepdims=True) + EPS) * w


def _sigmoid(x):
    return 0.5 * jnp.tanh(0.5 * x) + 0.5


def _sigmoid_rel(x):
    return 1.0 / (1.0 + jnp.exp(-x))


def _silu(x):
    return x * _sigmoid(x)


def _params(*sem):
    return pltpu.CompilerParams(dimension_semantics=sem, vmem_limit_bytes=VMEM_LIMIT_BYTES)


def _const_spec(shape):
    nd = len(shape)
    return pl.BlockSpec(shape, lambda *_: (0,) * nd, pipeline_mode=pl.Buffered(1))


def _ada_body(c_ref, w_ref, b_ref, o_ref):
    ca = _silu(c_ref[...])
    o_ref[...] = jnp.dot(ca, w_ref[...], precision=lax.Precision.HIGHEST,
                         preferred_element_type=F32) + b_ref[...]


def _ada(c, w, b):
    bsz, d = c.shape
    n = w.shape[1]
    tn = _pick_tile(n, ADA_TILE)
    return pl.pallas_call(
        _ada_body,
        grid=(n // tn,),
        in_specs=[pl.BlockSpec((bsz, d), lambda j: (0, 0)),
                  pl.BlockSpec((d, tn), lambda j: (0, j)),
                  pl.BlockSpec((1, tn), lambda j: (0, j))],
        out_specs=pl.BlockSpec((bsz, tn), lambda j: (0, j)),
        out_shape=jax.ShapeDtypeStruct((bsz, n), F32),
        compiler_params=_params("arbitrary"),
        name="ada",
    )(c, w, b.reshape(1, n))


def _in_proj_body(x_ref, mod_ref, n1_ref, wa_ref, wr_ref, wg_ref, qn_ref, wuq_ref, wui_ref,
                  qt_ref, qit_ref, k_ref, vt_ref, ki_ref, wt_ref, rec_ref, rf_ref, gates_ref):
    tm = x_ref.shape[1]
    x = x_ref[0]
    sh1 = mod_ref[0, 0:1, :]
    sc1 = mod_ref[0, 1:2, :]
    h = (_rms(x, n1_ref[...]) * (1.0 + sc1) + sh1).astype(BF16)

    pa = jnp.dot(h, wa_ref[...], preferred_element_type=F32)
    o_k, o_v, o_ki = Q_LORA_RANK, Q_LORA_RANK + KV_WIDTH, Q_LORA_RANK + 2 * KV_WIDTH
    v_t = pa[:, o_v:o_ki].T
    kiw_t = pa[:, o_ki:].T
    for g in range(N_KV_GROUPS):
        k_ref[0, g] = pa[:, o_k + g * ATTN_HEAD_DIM:o_k + (g + 1) * ATTN_HEAD_DIM].astype(BF16)
        vt_ref[0, g, :ATTN_HEAD_DIM, :] = v_t[g * ATTN_HEAD_DIM:(g + 1) * ATTN_HEAD_DIM].astype(BF16)
        vt_ref[0, g, ATTN_HEAD_DIM:, :] = jnp.ones((V_AUG_ROWS - ATTN_HEAD_DIM, tm), BF16)
    ki_ref[0] = pa[:, o_ki:o_ki + IDX_HEAD_DIM].astype(BF16)
    wt_ref[0] = kiw_t[IDX_HEAD_DIM:IDX_HEAD_DIM + N_IDX_HEADS]
    cqn = _rms(pa[:, :Q_LORA_RANK], qn_ref[...]).astype(BF16)
    nt = (((1,), (1,)), ((), ()))
    qf = lax.dot_general(wuq_ref[...], cqn, nt, preferred_element_type=F32)
    qt_ref[0] = qf.reshape(N_ATTN_HEADS, ATTN_HEAD_DIM, tm).astype(BF16)
    qif = lax.dot_general(wui_ref[...], cqn, nt, preferred_element_type=F32)
    qit_ref[0] = qif.reshape(N_IDX_HEADS, IDX_HEAD_DIM, tm).astype(BF16)

    pr = jnp.dot(h, wr_ref[...], preferred_element_type=F32)
    rec_ref[0] = pr[:, :3 * REC_WIDTH].astype(BF16)
    rf_ref[0] = pr[:, 3 * REC_WIDTH:]
    gates_ref[0] = jnp.dot(h, wg_ref[...], preferred_element_type=F32).astype(BF16)


def _in_proj(x, mod, n1w, wa, wr, wg, qnw, wuq, wui, tm):
    bsz, s, d = x.shape
    wa_n = wa.shape[1]
    grid = (bsz, s // tm)
    tok = lambda w: pl.BlockSpec((1, tm, w), lambda b, i: (b, i, 0))
    head = lambda nh, hd: pl.BlockSpec((1, nh, hd, tm), lambda b, i: (b, 0, 0, i))
    out_shape = (
        jax.ShapeDtypeStruct((bsz, N_ATTN_HEADS, ATTN_HEAD_DIM, s), BF16),
        jax.ShapeDtypeStruct((bsz, N_IDX_HEADS, IDX_HEAD_DIM, s), BF16),
        jax.ShapeDtypeStruct((bsz, N_KV_GROUPS, s, ATTN_HEAD_DIM), BF16),
        jax.ShapeDtypeStruct((bsz, N_KV_GROUPS, V_AUG_ROWS, s), BF16),
        jax.ShapeDtypeStruct((bsz, s, IDX_HEAD_DIM), BF16),
        jax.ShapeDtypeStruct((bsz, N_IDX_HEADS, s), F32),
        jax.ShapeDtypeStruct((bsz, s, 3 * REC_WIDTH), BF16),
        jax.ShapeDtypeStruct((bsz, s, REC_WIDTH), F32),
        jax.ShapeDtypeStruct((bsz, s, 2 * d), BF16),
    )
    out_specs = (head(N_ATTN_HEADS, ATTN_HEAD_DIM), head(N_IDX_HEADS, IDX_HEAD_DIM),
                 pl.BlockSpec((1, N_KV_GROUPS, tm, ATTN_HEAD_DIM), lambda b, i: (b, 0, i, 0)),
                 head(N_KV_GROUPS, V_AUG_ROWS),
                 tok(IDX_HEAD_DIM),
                 pl.BlockSpec((1, N_IDX_HEADS, tm), lambda b, i: (b, 0, i)),
                 tok(3 * REC_WIDTH), tok(REC_WIDTH), tok(2 * d))
    return pl.pallas_call(
        _in_proj_body,
        grid=grid,
        in_specs=[tok(d),
                  pl.BlockSpec((1, 6, d), lambda b, i: (b, 0, 0)),
                  _const_spec((1, d)),
                  _const_spec((d, wa_n)),
                  _const_spec((d, 4 * REC_WIDTH)),
                  _const_spec((d, 2 * d)),
                  _const_spec((1, Q_LORA_RANK)),
                  _const_spec((ATTN_WIDTH, Q_LORA_RANK)),
                  _const_spec((N_IDX_HEADS * IDX_HEAD_DIM, Q_LORA_RANK))],
        out_specs=out_specs,
        out_shape=out_shape,
        compiler_params=_params("parallel", "parallel"),
        name="in_proj",
    )(x, mod, n1w, wa, wr, wg, qnw, wuq, wui)


def _near_bias(rel_bias, tq, tk):
    n = jnp.arange(MAX_DISTANCE + 1, dtype=jnp.int32)
    max_exact = N_BUCKETS // 2
    nf = jnp.maximum(n, 1).astype(F32)
    large = max_exact + (jnp.log(nf / max_exact) / math.log(MAX_DISTANCE / max_exact)
                         * (N_BUCKETS - max_exact)).astype(jnp.int32)
    bucket = jnp.where(n < max_exact, n, jnp.minimum(large, N_BUCKETS - 1))
    bucket = bucket.at[MAX_DISTANCE].set(N_BUCKETS - 1)
    ids = jnp.arange(N_BUCKETS, dtype=jnp.int32)
    start = jnp.sum((bucket[None, :] < ids[:, None]).astype(jnp.int32), axis=1)
    rb = (rel_bias.astype(F32) - rel_bias[N_BUCKETS - 1].astype(F32)[None, :]) * LOG2_E
    step = rb - jnp.concatenate([jnp.zeros_like(rb[:1]), rb[:-1]], axis=0)
    dist = (jnp.arange(tq, dtype=jnp.int32)[None, :]
            - jnp.arange(2 * tk, dtype=jnp.int32)[:, None] + tk)
    reached = (dist[None, :, :] >= start[:, None, None]).astype(F32)
    return jnp.einsum("bh,but->hut", step, reached, precision=lax.Precision.HIGHEST)


def _dsa_body(qit_ref, w_ref, qt_ref, ki_ref, k_ref, vt_ref, bias_ref, o_ref,
              sc_ref, acc_ref, m_ref, s_scr, p_scr, mask_scr, *, tq, tk, k_sel):
    i = pl.program_id(1)
    row0 = i * tq
    n_kt = lax.div(row0 + tq + (tk - 1), tk)
    j_near = lax.div(jnp.maximum(row0 - (MAX_DISTANCE - 1), 0), tk)
    krow = lax.broadcasted_iota(jnp.int32, (tk, tq), 0)
    qcol = lax.broadcasted_iota(jnp.int32, (tk, tq), 1)
    rel = krow - qcol
    kf = float(k_sel)

    def fold(a, op, group=SUBLANES):
        n = a.shape[0] // group
        a = a.reshape(n // FOLD_CHAINS, FOLD_CHAINS, group, tq)
        r = a[0]
        for t in range(1, n // FOLD_CHAINS):
            r = op(r, a[t])
        while r.shape[0] > 1:
            half = r.shape[0] // 2
            r = op(r[:half], r[half:])
        return r[0]

    w = w_ref[0] * (N_IDX_HEADS ** -0.5)

    def score_tile(j, carry):
        mx, mn = carry
        c0 = pl.multiple_of(j * tk, tk)
        kin = ki_ref[0, pl.ds(c0, tk), :]
        sc = None
        for h in range(N_IDX_HEADS):
            lg = jnp.dot(kin, qit_ref[0, h], preferred_element_type=F32)
            term = jnp.maximum(lg, 0.0) * w[h:h + 1, :]
            sc = term if sc is None else sc + term
        causal = rel <= (row0 - c0)
        scm = jnp.where(causal, sc, NEG_BIG)
        sc_ref[pl.ds(c0, tk), :] = scm
        mx = jnp.maximum(mx, fold(scm, jnp.maximum))
        mn = jnp.minimum(mn, fold(jnp.where(causal, sc, -NEG_BIG), jnp.minimum))
        return mx, mn

    def score_pair(t, carry):
        carry = score_tile(2 * t, carry)
        return score_tile(jnp.minimum(2 * t + 1, n_kt - 1), carry)

    mx8, mn8 = lax.fori_loop(0, lax.div(n_kt + 1, 2), score_pair,
                             (jnp.full((SUBLANES, tq), NEG_BIG, F32),
                              jnp.full((SUBLANES, tq), -NEG_BIG, F32)))
    mx = jnp.max(mx8, axis=0, keepdims=True)
    mn = jnp.min(mn8, axis=0, keepdims=True)

    def key_pass(fn, op, init):
        def one_tile(j, acc):
            c0 = pl.multiple_of(j * tk, tk)
            for u in range(tk // PASS_ROWS):
                s = sc_ref[pl.ds(c0 + u * PASS_ROWS, PASS_ROWS), :]
                acc = op(acc, fold(fn(s), op))
            return acc

        pairs = lax.shift_right_logical(n_kt, 1)
        acc = lax.fori_loop(0, pairs, lambda t, a: one_tile(t + pairs, one_tile(t, a)),
                            jnp.full((SUBLANES, tq), init, F32))
        return lax.cond((n_kt & 1) == 1, lambda a: one_tile(n_kt - 1, a), lambda a: a, acc)

    def count(pred):
        part = key_pass(lambda s: jnp.where(pred(s), 1.0, 0.0), jnp.add, 0.0)
        return jnp.sum(part, axis=0, keepdims=True)

    n_valid = (row0 + 1 + lax.broadcasted_iota(jnp.int32, (1, tq), 1)).astype(F32)
    take_all = n_valid <= kf
    lo0 = jnp.where(take_all, NEG_BIG, mn)
    hi0 = mx + (jnp.abs(mx) * HI_MARGIN + HI_MARGIN_MIN)

    def bis_cond(st):
        it, lo, hi, c_lo, done_f = st
        return jnp.logical_and(it < BISECT_CAP, jnp.min(done_f) < 0.5)

    def halve(st):
        lo, hi, c_lo, done_f = st
        done = done_f > 0.5
        mid = lo + (hi - lo) * 0.5
        stuck = jnp.logical_or(mid <= lo, mid >= hi)
        cnt = count(lambda s: s >= mid)
        ge = cnt >= kf
        upd = jnp.logical_not(jnp.logical_or(done, stuck))
        raise_lo = jnp.logical_and(upd, ge)
        lo = jnp.where(raise_lo, mid, lo)
        c_lo = jnp.where(raise_lo, cnt, c_lo)
        hi = jnp.where(jnp.logical_and(upd, jnp.logical_not(ge)), mid, hi)
        done = jnp.logical_or(jnp.logical_or(done, stuck), jnp.logical_and(ge, cnt == kf))
        return lo, hi, c_lo, done.astype(F32)

    def snap(st):
        lo, hi, c_lo, done_f = st
        done = done_f > 0.5

        below = jnp.max(key_pass(lambda s: jnp.where(s < hi, s, NEG_BIG), jnp.maximum, NEG_BIG),
                        axis=0, keepdims=True)
        cnt = count(lambda s: s >= below)
        found = jnp.logical_and(jnp.logical_not(done), cnt >= kf)
        lower_hi = jnp.logical_and(jnp.logical_not(done), cnt < kf)
        lo = jnp.where(found, below, lo)
        c_lo = jnp.where(found, cnt, c_lo)
        hi = jnp.where(lower_hi, below, hi)
        return lo, hi, c_lo, jnp.logical_or(done, found).astype(F32)

    def bis_step(st):
        it, lo, hi, c_lo, done_f = st
        lo, hi, c_lo, done_f = lax.cond((it & 1) == 0, snap, halve, (lo, hi, c_lo, done_f))
        return it + 1, lo, hi, c_lo, done_f

    st = lax.fori_loop(0, SNAP_AFTER, lambda _, s: halve(s),
                       (lo0, hi0, n_valid, take_all.astype(F32)))
    _, thr, _, c_thr, _ = lax.while_loop(bis_cond, bis_step, (jnp.int32(0),) + st)

    excess = jnp.logical_and(jnp.logical_not(take_all), c_thr > kf)

    @pl.when(jnp.max(excess.astype(F32)) > 0.5)
    def _():
        surplus = jnp.where(excess, c_thr - kf, 0.0)
        triu = (lax.broadcasted_iota(jnp.int32, (tk, tk), 1)
                >= lax.broadcasted_iota(jnp.int32, (tk, tk), 0)).astype(BF16)

        def drop_tile(t, later):
            c0 = pl.multiple_of((n_kt - 1 - t) * tk, tk)
            s = sc_ref[pl.ds(c0, tk), :]
            eq = s == thr
            rank = later + jnp.dot(triu, jnp.where(eq, 1.0, 0.0).astype(BF16),
                                   preferred_element_type=F32)
            sc_ref[pl.ds(c0, tk), :] = jnp.where(jnp.logical_and(eq, rank <= surplus), NEG_BIG, s)
            return rank[0:1, :]

        lax.fori_loop(0, n_kt, drop_tile, jnp.zeros((1, tq), F32))

    def logits(j, h, near, rows=tk):
        c0 = pl.multiple_of(j * tk, tk)
        g = h // HEADS_PER_GROUP
        s = jnp.dot(k_ref[0, g, pl.ds(c0, rows), :], qt_ref[0, h],
                    preferred_element_type=F32)
        if near:
            bias_row = pl.multiple_of(c0 - row0 + tk, tq)
            s = s + bias_ref[h, pl.ds(bias_row, tk), :]
        return s

    def selected(j, near, rows=tk):
        c0 = pl.multiple_of(j * tk, tk)
        keep = sc_ref[pl.ds(c0, rows), :] >= thr
        if near:
            keep = jnp.logical_and(keep, rel <= (row0 - c0))
        return keep

    def attend_exact(j, near):
        c0 = pl.multiple_of(j * tk, tk)
        mask_scr[:tk] = jnp.where(selected(j, near), 0.0, NEG_BIG)
        tile_max = []
        for h in range(N_ATTN_HEADS):
            s = logits(j, h, near) + mask_scr[:tk]
            s_scr[h] = s
            tile_max.append(jnp.max(fold(s, jnp.maximum), axis=0, keepdims=True))
        for h in range(N_ATTN_HEADS):
            g = h // HEADS_PER_GROUP
            m_old = m_ref[h]
            m_new = jnp.maximum(m_old, tile_max[h])
            alpha = jnp.exp2(m_old - m_new)
            p_scr[h, :tk] = jnp.exp2(s_scr[h] - m_new).astype(BF16)
            m_ref[h] = m_new
            pv = jnp.dot(vt_ref[0, g, :, pl.ds(c0, tk)], p_scr[h, :tk],
                         preferred_element_type=F32)
            acc_ref[h] = acc_ref[h] * alpha + pv

    def attend_fast(j, near, rows=tk):
        c0 = pl.multiple_of(j * tk, tk)
        mask_scr[:rows] = jnp.where(selected(j, near, rows), 0.0, NEG_BIG)
        for h in range(N_ATTN_HEADS):
            p_scr[h, :rows] = jnp.exp2(logits(j, h, near, rows) + mask_scr[:rows]).astype(BF16)
        for h in range(N_ATTN_HEADS):
            g = h // HEADS_PER_GROUP
            acc_ref[h] += jnp.dot(vt_ref[0, g, :, pl.ds(c0, rows)], p_scr[h, :rows],
                                  preferred_element_type=F32)

    def run_tiles(attend, far_tiles_per_step):
        acc_ref[...] = jnp.zeros(acc_ref.shape, F32)
        m_ref[...] = jnp.full(m_ref.shape, NEG_BIG, F32)
        w = far_tiles_per_step

        def far_step(t, c):
            if w == 1:
                attend(t, False)
            else:
                attend(t * w, False, w * tk)
            return c

        n_wide = lax.div(j_near, w)
        lax.fori_loop(0, n_wide, far_step, 0)
        for u in range(w - 1):
            @pl.when(n_wide * w + u < j_near)
            def _():
                attend(n_wide * w + u, False)
        for u in range(2):
            @pl.when(j_near + u < n_kt)
            def _():
                attend(j_near + u, True)

    run_tiles(attend_fast, FAST_FAR_TILES)
    denom = acc_ref[:, ATTN_HEAD_DIM:ATTN_HEAD_DIM + 1, :]
    sane = jnp.logical_and(denom > MIN_SANE_DENOM, denom < MAX_SANE_DENOM)

    @pl.when(jnp.min(sane.astype(F32)) < 0.5)
    def _():
        run_tiles(attend_exact, 1)

    outs = []
    for h in range(N_ATTN_HEADS):
        a = acc_ref[h]
        outs.append(a[:ATTN_HEAD_DIM, :] / a[ATTN_HEAD_DIM:ATTN_HEAD_DIM + 1, :])
    o_ref[0] = jnp.concatenate(outs, axis=0).T.astype(BF16)


def _dsa(qt, qit, wt, ki, k, vt, bias, *, tq, tk, k_sel):
    bsz, _, _, s = qt.shape
    assert tq >= MAX_DISTANCE and tk % tq == 0 and s % tk == 0
    body = functools.partial(_dsa_body, tq=tq, tk=tk, k_sel=k_sel)
    return pl.pallas_call(
        body,
        grid=(bsz, s // tq),
        in_specs=[pl.BlockSpec((1, N_IDX_HEADS, IDX_HEAD_DIM, tq), lambda b, i: (b, 0, 0, i)),
                  pl.BlockSpec((1, N_IDX_HEADS, tq), lambda b, i: (b, 0, i)),
                  pl.BlockSpec((1, N_ATTN_HEADS, ATTN_HEAD_DIM, tq), lambda b, i: (b, 0, 0, i)),
                  pl.BlockSpec((1, s, IDX_HEAD_DIM), lambda b, i: (b, 0, 0)),
                  pl.BlockSpec((1, N_KV_GROUPS, s, ATTN_HEAD_DIM), lambda b, i: (b, 0, 0, 0)),
                  pl.BlockSpec((1, N_KV_GROUPS, V_AUG_ROWS, s), lambda b, i: (b, 0, 0, 0)),
                  _const_spec(bias.shape)],
        out_specs=pl.BlockSpec((1, tq, ATTN_WIDTH), lambda b, i: (b, i, 0)),
        out_shape=jax.ShapeDtypeStruct((bsz, s, ATTN_WIDTH), BF16),
        scratch_shapes=[pltpu.VMEM((s, tq), F32),
                        pltpu.VMEM((N_ATTN_HEADS, V_AUG_ROWS, tq), F32),
                        pltpu.VMEM((N_ATTN_HEADS, 1, tq), F32),
                        pltpu.VMEM((N_ATTN_HEADS, tk, tq), F32),
                        pltpu.VMEM((N_ATTN_HEADS, FAST_FAR_TILES * tk, tq), BF16),
                        pltpu.VMEM((FAST_FAR_TILES * tk, tq), F32)],
        compiler_params=_params("parallel", "arbitrary"),
        name="dsa",
    )(qit, wt, qt, ki, k, vt, bias)


def _split3(a):
    hi = a.astype(BF16)
    r1 = a - hi.astype(F32)
    mid = r1.astype(BF16)
    lo = (r1 - mid.astype(F32)).astype(BF16)
    return hi, mid, lo


def _hgrn2_body(rq_ref, ri_ref, rg_ref, rf_ref, lb_ref, nw_ref, o_ref,
                state_scr, tril_scr, lvl_scr, *, chunk):
    t_len = rq_ref.shape[1]
    c = chunk
    d = REC_HEAD_DIM
    nw = nw_ref[...]
    eye = (lax.broadcasted_iota(jnp.int32, (d, d), 0)
           == lax.broadcasted_iota(jnp.int32, (d, d), 1)).astype(F32)
    sub = lax.broadcasted_iota(jnp.int32, (c, d), 0) & (REC_BAND - 1)
    level_sizes = []
    b = REC_BAND
    while b < c:
        level_sizes.append(b)
        b *= 2

    @pl.when(pl.program_id(1) == 0)
    def _():
        state_scr[...] = jnp.zeros(state_scr.shape, F32)
        ti = lax.broadcasted_iota(jnp.int32, (c, c), 0)
        si = lax.broadcasted_iota(jnp.int32, (c, c), 1)
        tril_scr[...] = (si <= ti).astype(BF16)
        for li, b in enumerate(level_sizes):
            blk = 2 * b
            sh = blk.bit_length() - 1
            m = jnp.logical_and(
                lax.shift_right_logical(ti, sh) == lax.shift_right_logical(si, sh),
                jnp.logical_and((ti & (blk - 1)) >= b, (si & (blk - 1)) < b))
            lvl_scr[li] = m.astype(F32)

    def head_step(hh, t0):
        cols = slice(hh * d, (hh + 1) * d)
        lb = lb_ref[:, cols]
        state = state_scr[hh]
        f = lb + (1.0 - lb) * _sigmoid_rel(rf_ref[0, pl.ds(t0, c), cols])
        g = jnp.log(f)
        kk = 1.0 - f
        qq = _silu(rq_ref[0, pl.ds(t0, c), cols].astype(F32))
        vb = ri_ref[0, pl.ds(t0, c), cols]
        vv = vb.astype(F32)

        g3 = _split3(g)
        tril = tril_scr[...]
        a = (jnp.dot(tril, g3[0], preferred_element_type=F32)
             + jnp.dot(tril, g3[1], preferred_element_type=F32)
             + jnp.dot(tril, g3[2], preferred_element_type=F32))
        a_last = a[c - 1:c, :]

        o = jnp.dot((qq * jnp.exp(a)).astype(BF16), state.astype(BF16),
                    preferred_element_type=F32)

        p = jnp.zeros((c, c), F32)
        for li, b in enumerate(level_sizes):
            blk = 2 * b
            r = jnp.broadcast_to(a.reshape(c // blk, blk, d)[:, b - 1:b, :],
                                 (c // blk, blk, d)).reshape(c, d)
            qt = (qq * jnp.exp(jnp.minimum(a - r, 0.0))).astype(BF16)
            kt = (kk * jnp.exp(jnp.minimum(r - a, 0.0))).astype(BF16)
            pb = lax.dot_general(qt, kt, (((1,), (1,)), ((), ())), preferred_element_type=F32)
            p = p + pb * lvl_scr[li]
        o = o + jnp.dot(p.astype(BF16), vb, preferred_element_type=F32)

        def back(x, dl):
            if dl == 0:
                return x
            return pltpu.roll(x.reshape(c // SUBLANES, SUBLANES, d), dl, axis=1).reshape(c, d)

        for dl in range(REC_BAND):
            a_s = back(a, dl)
            k_s = back(kk, dl)
            v_s = back(vv, dl)
            e = jnp.exp(jnp.where(sub >= dl, a - a_s, NEG_BIG))
            pd = jnp.sum(qq * k_s * e, axis=1, keepdims=True)
            o = o + pd * v_s

        kd = (kk * jnp.exp(a_last - a)).astype(BF16)
        upd = lax.dot_general(kd, vb, (((0,), (0,)), ((), ())), preferred_element_type=F32)
        e_col = jnp.sum(eye * jnp.exp(a_last), axis=1, keepdims=True)
        state_scr[hh] = e_col * state + upd

        y = _rms(o, nw) * _silu(rg_ref[0, pl.ds(t0, c), cols].astype(F32))
        o_ref[0, pl.ds(t0, c), cols] = y.astype(BF16)

    def step(n, carry):
        t0 = pl.multiple_of(n * c, c)
        for hh in range(N_REC_HEADS):
            head_step(hh, t0)
        return carry

    lax.fori_loop(0, t_len // c, step, 0)


def _hgrn2(rec3, rf, lb, nw, *, chunk):
    bsz, s, _ = rec3.shape
    d = REC_HEAD_DIM
    h = N_REC_HEADS
    body = functools.partial(_hgrn2_body, chunk=chunk)
    tt = _pick_tile(s, REC_TIME_TILE)
    col = lambda part: pl.BlockSpec((1, tt, REC_WIDTH), lambda b, t: (b, t, part))
    return pl.pallas_call(
        body,
        grid=(bsz, s // tt),
        in_specs=[col(0), col(1), col(2), col(0),
                  pl.BlockSpec((1, REC_WIDTH), lambda b, t: (0, 0)),
                  pl.BlockSpec((1, d), lambda b, t: (0, 0))],
        out_specs=col(0),
        out_shape=jax.ShapeDtypeStruct((bsz, s, REC_WIDTH), BF16),
        scratch_shapes=[pltpu.VMEM((h, d, d), F32)]
        + [pltpu.VMEM((chunk, chunk), BF16),
           pltpu.VMEM(((chunk // REC_BAND).bit_length() - 1, chunk, chunk), F32)],
        compiler_params=_params("parallel", "arbitrary"),
        name="hgrn2",
    )(rec3, rec3, rec3, rf, lb.reshape(1, REC_WIDTH), nw.reshape(1, d))


def _out_ffn_body(x_ref, attn_ref, rec_ref, gates_ref, mod_ref, wba_ref, wbr_ref, wo_ref,
                  n2_ref, wg_ref, wu_ref, wd_ref, fn_ref, o_ref, *, final):
    d = x_ref.shape[2]
    x = x_ref[0]
    g1 = mod_ref[0, 2:3, :]
    sh2 = mod_ref[0, 3:4, :]
    sc2 = mod_ref[0, 4:5, :]
    g2 = mod_ref[0, 5:6, :]
    ya = jnp.dot(attn_ref[0], wba_ref[...], preferred_element_type=F32)
    yr = jnp.dot(rec_ref[0], wbr_ref[...], preferred_element_type=F32)
    gates = gates_ref[0].astype(F32)
    mix = (_sigmoid(gates[:, :d]) * ya + _sigmoid(gates[:, d:]) * yr).astype(BF16)
    x1 = x + g1 * jnp.dot(mix, wo_ref[...], preferred_element_type=F32)
    h2 = (_rms(x1, n2_ref[...]) * (1.0 + sc2) + sh2).astype(BF16)
    gate = jnp.dot(h2, wg_ref[...], preferred_element_type=F32)
    up = jnp.dot(h2, wu_ref[...], preferred_element_type=F32)
    act = (_silu(gate) * up).astype(BF16)
    x2 = x1 + g2 * jnp.dot(act, wd_ref[...], preferred_element_type=F32)
    o_ref[0] = _rms(x2, fn_ref[...]) if final else x2


def _out_ffn(x, attn, rec, gates, mod, wba, wbr, wo, n2w, wg, wu, wd, fnw, tm, final):
    bsz, s, d = x.shape
    hid = wg.shape[1]
    tok = lambda w: pl.BlockSpec((1, tm, w), lambda b, i: (b, i, 0))
    return pl.pallas_call(
        functools.partial(_out_ffn_body, final=final),
        grid=(bsz, s // tm),
        in_specs=[tok(d), tok(ATTN_WIDTH), tok(REC_WIDTH), tok(2 * d),
                  pl.BlockSpec((1, 6, d), lambda b, i: (b, 0, 0)),
                  _const_spec((ATTN_WIDTH, d)), _const_spec((REC_WIDTH, d)), _const_spec((d, d)),
                  _const_spec((1, d)), _const_spec((d, hid)), _const_spec((d, hid)),
                  _const_spec((hid, d)), _const_spec((1, d))],
        out_specs=tok(d),
        out_shape=jax.ShapeDtypeStruct((bsz, s, d), F32),
        compiler_params=_params("parallel", "parallel"),
        name="out_ffn",
    )(x, attn, rec, gates, mod, wba, wbr, wo, n2w, wg, wu, wd, fnw)


def _pick_tile(n, want):
    t = min(want, n)
    while n % t:
        t //= 2
    return t


def kernel(x, c, w_ada, b_ada, norm1_w, w_in, q_norm_w, w_uq, w_uq_idx, rel_bias, lb_logits,
           rec_norm_w, w_branch_attn, w_branch_rec, w_out, norm2_w, w_ffn_gate, w_ffn_up,
           w_ffn_down, final_norm_w):
    bsz, s, d = x.shape
    depth = w_in.shape[0]
    k_sel = min(TOPK_MAX, s // 4)
    tm = _pick_tile(s, TOKEN_TILE)
    tq = _pick_tile(s, Q_TILE)
    tk = max(tq, _pick_tile(s, K_TILE))
    chunk = _pick_tile(s, REC_CHUNK)
    lower_bounds = jnp.cumsum(jax.nn.softmax(lb_logits.astype(F32), axis=0), axis=0)
    bias = _near_bias(rel_bias, tq, tk)

    o_cq = 0
    o_k = o_cq + Q_LORA_RANK
    o_v = o_k + KV_WIDTH
    o_ki = o_v + KV_WIDTH
    o_wi = o_ki + IDX_HEAD_DIM
    o_rq = o_wi + N_IDX_HEADS
    o_rf = o_rq + REC_WIDTH
    o_ri = o_rf + REC_WIDTH
    o_rg = o_ri + REC_WIDTH
    o_gt = o_rg + REC_WIDTH

    for layer in range(depth):
        mod = _ada(c, w_ada[layer], b_ada[layer]).reshape(bsz, 6, d)
        wl = w_in[layer]
        pad_a = LANES - IDX_HEAD_DIM - N_IDX_HEADS
        wa = jnp.concatenate([wl[:, o_cq:o_rq], jnp.zeros((d, pad_a), wl.dtype)], axis=1).astype(BF16)
        wr = jnp.concatenate([wl[:, o_rq:o_rf], wl[:, o_ri:o_rg], wl[:, o_rg:o_gt],
                              wl[:, o_rf:o_ri]], axis=1).astype(BF16)
        wg = wl[:, o_gt:].astype(BF16)
        wuq = (w_uq[layer] * (ATTN_HEAD_DIM ** -0.5 * LOG2_E)).astype(BF16).T
        wui = (w_uq_idx[layer] * (IDX_HEAD_DIM ** -0.5)).astype(BF16).T

        qt, qit, k, vt, ki, wt, rec3, rf, gates = _in_proj(
            x, mod, norm1_w[layer].reshape(1, d), wa, wr, wg,
            q_norm_w[layer].reshape(1, Q_LORA_RANK), wuq, wui, tm)

        attn = _dsa(qt, qit, wt, ki, k, vt, bias, tq=tq, tk=tk, k_sel=k_sel)
        rec = _hgrn2(rec3, rf, lower_bounds[layer], rec_norm_w[layer], chunk=chunk)

        x = _out_ffn(x, attn, rec, gates, mod,
                     w_branch_attn[layer].astype(BF16), w_branch_rec[layer].astype(BF16),
                     w_out[layer].astype(BF16), norm2_w[layer].reshape(1, d),
                     w_ffn_gate[layer].astype(BF16), w_ffn_up[layer].astype(BF16),
                     w_ffn_down[layer].astype(BF16),
                     final_norm_w.reshape(1, d), tm, layer == depth - 1)
    return x
```

```python
import functools
import math

import jax
import jax.numpy as jnp
from jax import lax
from jax.experimental import pallas as pl
from jax.experimental.pallas import tpu as pltpu

F32 = jnp.float32
BF16 = jnp.bfloat16

N_ATTN_HEADS = 8
ATTN_HEAD_DIM = 64
N_KV_GROUPS = 2
HEADS_PER_GROUP = N_ATTN_HEADS // N_KV_GROUPS
Q_LORA_RANK = 256
N_IDX_HEADS = 8
IDX_HEAD_DIM = 64
TOPK_MAX = 256
N_BUCKETS = 32
MAX_DISTANCE = 128
N_REC_HEADS = 4
REC_HEAD_DIM = 128
EPS = 1e-6

ATTN_WIDTH = N_ATTN_HEADS * ATTN_HEAD_DIM
KV_WIDTH = N_KV_GROUPS * ATTN_HEAD_DIM
REC_WIDTH = N_REC_HEADS * REC_HEAD_DIM

LANES = 128
SUBLANES = 8
BF16_SUBLANES = 16
VMEM_LIMIT_BYTES = 56 * 1024 * 1024

ADA_TILE = 1024
TOKEN_TILE = 512
Q_TILE = 256
K_TILE = 512
REC_CHUNK = 128
REC_TIME_TILE = 1024
REC_BAND = 4
NEG_BIG = -1e30
BISECT_CAP = 700
SNAP_AFTER = 14
FOLD_CHAINS = 4
HI_MARGIN = 2.0 ** -20
HI_MARGIN_MIN = 1e-30
PASS_ROWS = 128
FAST_FAR_TILES = 4
LOG2_E = math.log2(math.e)
MAX_SANE_DENOM = 1e30
MIN_SANE_DENOM = 1e-18
V_AUG_ROWS = ATTN_HEAD_DIM + BF16_SUBLANES


def _rms(x, w):
    return x * lax.rsqrt(jnp.mean(x * x, axis=-1, keepdims=True) + EPS) * w


def _sigmoid(x):
    return 0.5 * jnp.tanh(0.5 * x) + 0.5


def _sigmoid_rel(x):
    return 1.0 / (1.0 + jnp.exp(-x))


def _silu(x):
    return x * _sigmoid(x)


def _params(*sem):
    return pltpu.CompilerParams(dimension_semantics=sem, vmem_limit_bytes=VMEM_LIMIT_BYTES)


def _const_spec(shape):
    nd = len(shape)
    return pl.BlockSpec(shape, lambda *_: (0,) * nd, pipeline_mode=pl.Buffered(1))


def _ada_body(c_ref, w_ref, b_ref, o_ref):
    ca = _silu(c_ref[...])
    o_ref[...] = jnp.dot(ca, w_ref[...], precision=lax.Precision.HIGHEST,
                         preferred_element_type=F32) + b_ref[...]


def _ada(c, w, b):
    bsz, d = c.shape
    n = w.shape[1]
    tn = _pick_tile(n, ADA_TILE)
    return pl.pallas_call(
        _ada_body,
        grid=(n // tn,),
        in_specs=[pl.BlockSpec((bsz, d), lambda j: (0, 0)),
                  pl.BlockSpec((d, tn), lambda j: (0, j)),
                  pl.BlockSpec((1, tn), lambda j: (0, j))],
        out_specs=pl.BlockSpec((bsz, tn), lambda j: (0, j)),
        out_shape=jax.ShapeDtypeStruct((bsz, n), F32),
        compiler_params=_params("arbitrary"),
        name="ada",
    )(c, w, b.reshape(1, n))


def _in_proj_body(x_ref, mod_ref, n1_ref, wa_ref, wr_ref, wg_ref, qn_ref, wuq_ref, wui_ref,
                  qt_ref, qit_ref, k_ref, vt_ref, ki_ref, wt_ref, rec_ref, rf_ref, gates_ref):
    tm = x_ref.shape[1]
    x = x_ref[0]
    sh1 = mod_ref[0, 0:1, :]
    sc1 = mod_ref[0, 1:2, :]
    h = (_rms(x, n1_ref[...]) * (1.0 + sc1) + sh1).astype(BF16)

    pa = jnp.dot(h, wa_ref[...], preferred_element_type=F32)
    o_k, o_v, o_ki = Q_LORA_RANK, Q_LORA_RANK + KV_WIDTH, Q_LORA_RANK + 2 * KV_WIDTH
    v_t = pa[:, o_v:o_ki].T
    kiw_t = pa[:, o_ki:].T
    for g in range(N_KV_GROUPS):
        k_ref[0, g] = pa[:, o_k + g * ATTN_HEAD_DIM:o_k + (g + 1) * ATTN_HEAD_DIM].astype(BF16)
        vt_ref[0, g, :ATTN_HEAD_DIM, :] = v_t[g * ATTN_HEAD_DIM:(g + 1) * ATTN_HEAD_DIM].astype(BF16)
        vt_ref[0, g, ATTN_HEAD_DIM:, :] = jnp.ones((V_AUG_ROWS - ATTN_HEAD_DIM, tm), BF16)
    ki_ref[0] = pa[:, o_ki:o_ki + IDX_HEAD_DIM].astype(BF16)
    wt_ref[0] = kiw_t[IDX_HEAD_DIM:IDX_HEAD_DIM + N_IDX_HEADS]
    cqn = _rms(pa[:, :Q_LORA_RANK], qn_ref[...]).astype(BF16)
    nt = (((1,), (1,)), ((), ()))
    qf = lax.dot_general(wuq_ref[...], cqn, nt, preferred_element_type=F32)
    qt_ref[0] = qf.reshape(N_ATTN_HEADS, ATTN_HEAD_DIM, tm).astype(BF16)
    qif = lax.dot_general(wui_ref[...], cqn, nt, preferred_element_type=F32)
    qit_ref[0] = qif.reshape(N_IDX_HEADS, IDX_HEAD_DIM, tm).astype(BF16)

    pr = jnp.dot(h, wr_ref[...], preferred_element_type=F32)
    rec_ref[0] = pr[:, :3 * REC_WIDTH].astype(BF16)
    rf_ref[0] = pr[:, 3 * REC_WIDTH:]
    gates_ref[0] = jnp.dot(h, wg_ref[...], preferred_element_type=F32).astype(BF16)


def _in_proj(x, mod, n1w, wa, wr, wg, qnw, wuq, wui, tm):
    bsz, s, d = x.shape
    wa_n = wa.shape[1]
    grid = (bsz, s // tm)
    tok = lambda w: pl.BlockSpec((1, tm, w), lambda b, i: (b, i, 0))
    head = lambda nh, hd: pl.BlockSpec((1, nh, hd, tm), lambda b, i: (b, 0, 0, i))
    out_shape = (
        jax.ShapeDtypeStruct((bsz, N_ATTN_HEADS, ATTN_HEAD_DIM, s), BF16),
        jax.ShapeDtypeStruct((bsz, N_IDX_HEADS, IDX_HEAD_DIM, s), BF16),
        jax.ShapeDtypeStruct((bsz, N_KV_GROUPS, s, ATTN_HEAD_DIM), BF16),
        jax.ShapeDtypeStruct((bsz, N_KV_GROUPS, V_AUG_ROWS, s), BF16),
        jax.ShapeDtypeStruct((bsz, s, IDX_HEAD_DIM), BF16),
        jax.ShapeDtypeStruct((bsz, N_IDX_HEADS, s), F32),
        jax.ShapeDtypeStruct((bsz, s, 3 * REC_WIDTH), BF16),
        jax.ShapeDtypeStruct((bsz, s, REC_WIDTH), F32),
        jax.ShapeDtypeStruct((bsz, s, 2 * d), BF16),
    )
    out_specs = (head(N_ATTN_HEADS, ATTN_HEAD_DIM), head(N_IDX_HEADS, IDX_HEAD_DIM),
                 pl.BlockSpec((1, N_KV_GROUPS, tm, ATTN_HEAD_DIM), lambda b, i: (b, 0, i, 0)),
                 head(N_KV_GROUPS, V_AUG_ROWS),
                 tok(IDX_HEAD_DIM),
                 pl.BlockSpec((1, N_IDX_HEADS, tm), lambda b, i: (b, 0, i)),
                 tok(3 * REC_WIDTH), tok(REC_WIDTH), tok(2 * d))
    return pl.pallas_call(
        _in_proj_body,
        grid=grid,
        in_specs=[tok(d),
                  pl.BlockSpec((1, 6, d), lambda b, i: (b, 0, 0)),
                  _const_spec((1, d)),
                  _const_spec((d, wa_n)),
                  _const_spec((d, 4 * REC_WIDTH)),
                  _const_spec((d, 2 * d)),
                  _const_spec((1, Q_LORA_RANK)),
                  _const_spec((ATTN_WIDTH, Q_LORA_RANK)),
                  _const_spec((N_IDX_HEADS * IDX_HEAD_DIM, Q_LORA_RANK))],
        out_specs=out_specs,
        out_shape=out_shape,
        compiler_params=_params("parallel", "parallel"),
        name="in_proj",
    )(x, mod, n1w, wa, wr, wg, qnw, wuq, wui)


def _near_bias(rel_bias, tq, tk):
    n = jnp.arange(MAX_DISTANCE + 1, dtype=jnp.int32)
    max_exact = N_BUCKETS // 2
    nf = jnp.maximum(n, 1).astype(F32)
    large = max_exact + (jnp.log(nf / max_exact) / math.log(MAX_DISTANCE / max_exact)
                         * (N_BUCKETS - max_exact)).astype(jnp.int32)
    bucket = jnp.where(n < max_exact, n, jnp.minimum(large, N_BUCKETS - 1))
    bucket = bucket.at[MAX_DISTANCE].set(N_BUCKETS - 1)
    ids = jnp.arange(N_BUCKETS, dtype=jnp.int32)
    start = jnp.sum((bucket[None, :] < ids[:, None]).astype(jnp.int32), axis=1)
    rb = (rel_bias.astype(F32) - rel_bias[N_BUCKETS - 1].astype(F32)[None, :]) * LOG2_E
    step = rb - jnp.concatenate([jnp.zeros_like(rb[:1]), rb[:-1]], axis=0)
    dist = (jnp.arange(tq, dtype=jnp.int32)[None, :]
            - jnp.arange(2 * tk, dtype=jnp.int32)[:, None] + tk)
    reached = (dist[None, :, :] >= start[:, None, None]).astype(F32)
    return jnp.einsum("bh,but->hut", step, reached, precision=lax.Precision.HIGHEST)


def _dsa_body(qit_ref, w_ref, qt_ref, ki_ref, k_ref, vt_ref, bias_ref, o_ref,
              sc_ref, acc_ref, m_ref, s_scr, p_scr, mask_scr, *, tq, tk, k_sel):
    i = pl.program_id(1)
    row0 = i * tq
    n_kt = lax.div(row0 + tq + (tk - 1), tk)
    j_near = lax.div(jnp.maximum(row0 - (MAX_DISTANCE - 1), 0), tk)
    krow = lax.broadcasted_iota(jnp.int32, (tk, tq), 0)
    qcol = lax.broadcasted_iota(jnp.int32, (tk, tq), 1)
    rel = krow - qcol
    kf = float(k_sel)

    def fold(a, op, group=SUBLANES):
        n = a.shape[0] // group
        a = a.reshape(n // FOLD_CHAINS, FOLD_CHAINS, group, tq)
        r = a[0]
        for t in range(1, n // FOLD_CHAINS):
            r = op(r, a[t])
        while r.shape[0] > 1:
            half = r.shape[0] // 2
            r = op(r[:half], r[half:])
        return r[0]

    w = w_ref[0] * (N_IDX_HEADS ** -0.5)

    def score_tile(j, carry):
        mx, mn = carry
        c0 = pl.multiple_of(j * tk, tk)
        kin = ki_ref[0, pl.ds(c0, tk), :]
        sc = None
        for h in range(N_IDX_HEADS):
            lg = jnp.dot(kin, qit_ref[0, h], preferred_element_type=F32)
            term = jnp.maximum(lg, 0.0) * w[h:h + 1, :]
            sc = term if sc is None else sc + term
        causal = rel <= (row0 - c0)
        scm = jnp.where(causal, sc, NEG_BIG)
        sc_ref[pl.ds(c0, tk), :] = scm
        mx = jnp.maximum(mx, fold(scm, jnp.maximum))
        mn = jnp.minimum(mn, fold(jnp.where(causal, sc, -NEG_BIG), jnp.minimum))
        return mx, mn

    def score_pair(t, carry):
        carry = score_tile(2 * t, carry)
        return score_tile(jnp.minimum(2 * t + 1, n_kt - 1), carry)

    mx8, mn8 = lax.fori_loop(0, lax.div(n_kt + 1, 2), score_pair,
                             (jnp.full((SUBLANES, tq), NEG_BIG, F32),
                              jnp.full((SUBLANES, tq), -NEG_BIG, F32)))
    mx = jnp.max(mx8, axis=0, keepdims=True)
    mn = jnp.min(mn8, axis=0, keepdims=True)

    def key_pass(fn, op, init):
        def one_tile(j, acc):
            c0 = pl.multiple_of(j * tk, tk)
            for u in range(tk // PASS_ROWS):
                s = sc_ref[pl.ds(c0 + u * PASS_ROWS, PASS_ROWS), :]
                acc = op(acc, fold(fn(s), op))
            return acc

        pairs = lax.shift_right_logical(n_kt, 1)
        acc = lax.fori_loop(0, pairs, lambda t, a: one_tile(t + pairs, one_tile(t, a)),
                            jnp.full((SUBLANES, tq), init, F32))
        return lax.cond((n_kt & 1) == 1, lambda a: one_tile(n_kt - 1, a), lambda a: a, acc)

    def count(pred):
        part = key_pass(lambda s: jnp.where(pred(s), 1.0, 0.0), jnp.add, 0.0)
        return jnp.sum(part, axis=0, keepdims=True)

    n_valid = (row0 + 1 + lax.broadcasted_iota(jnp.int32, (1, tq), 1)).astype(F32)
    take_all = n_valid <= kf
    lo0 = jnp.where(take_all, NEG_BIG, mn)
    hi0 = mx + (jnp.abs(mx) * HI_MARGIN + HI_MARGIN_MIN)

    def bis_cond(st):
        it, lo, hi, c_lo, done_f = st
        return jnp.logical_and(it < BISECT_CAP, jnp.min(done_f) < 0.5)

    def halve(st):
        lo, hi, c_lo, done_f = st
        done = done_f > 0.5
        mid = lo + (hi - lo) * 0.5
        stuck = jnp.logical_or(mid <= lo, mid >= hi)
        cnt = count(lambda s: s >= mid)
        ge = cnt >= kf
        upd = jnp.logical_not(jnp.logical_or(done, stuck))
        raise_lo = jnp.logical_and(upd, ge)
        lo = jnp.where(raise_lo, mid, lo)
        c_lo = jnp.where(raise_lo, cnt, c_lo)
        hi = jnp.where(jnp.logical_and(upd, jnp.logical_not(ge)), mid, hi)
        done = jnp.logical_or(jnp.logical_or(done, stuck), jnp.logical_and(ge, cnt == kf))
        return lo, hi, c_lo, done.astype(F32)

    def snap(st):
        lo, hi, c_lo, done_f = st
        done = done_f > 0.5

        below = jnp.max(key_pass(lambda s: jnp.where(s < hi, s, NEG_BIG), jnp.maximum, NEG_BIG),
                        axis=0, keepdims=True)
        cnt = count(lambda s: s >= below)
        found = jnp.logical_and(jnp.logical_not(done), cnt >= kf)
        lower_hi = jnp.logical_and(jnp.logical_not(done), cnt < kf)
        lo = jnp.where(found, below, lo)
        c_lo = jnp.where(found, cnt, c_lo)
        hi = jnp.where(lower_hi, below, hi)
        return lo, hi, c_lo, jnp.logical_or(done, found).astype(F32)

    def bis_step(st):
        it, lo, hi, c_lo, done_f = st
        lo, hi, c_lo, done_f = lax.cond((it & 1) == 0, snap, halve, (lo, hi, c_lo, done_f))
        return it + 1, lo, hi, c_lo, done_f

    st = lax.fori_loop(0, SNAP_AFTER, lambda _, s: halve(s),
                       (lo0, hi0, n_valid, take_all.astype(F32)))
    _, thr, _, c_thr, _ = lax.while_loop(bis_cond, bis_step, (jnp.int32(0),) + st)

    excess = jnp.logical_and(jnp.logical_not(take_all), c_thr > kf)

    @pl.when(jnp.max(excess.astype(F32)) > 0.5)
    def _():
        surplus = jnp.where(excess, c_thr - kf, 0.0)
        triu = (lax.broadcasted_iota(jnp.int32, (tk, tk), 1)
                >= lax.broadcasted_iota(jnp.int32, (tk, tk), 0)).astype(BF16)

        def drop_tile(st):
            t, later = st
            c0 = pl.multiple_of((n_kt - 1 - t) * tk, tk)
            s = sc_ref[pl.ds(c0, tk), :]
            eq = s == thr
            rank = later + jnp.dot(triu, jnp.where(eq, 1.0, 0.0).astype(BF16),
                                   preferred_element_type=F32)
            sc_ref[pl.ds(c0, tk), :] = jnp.where(jnp.logical_and(eq, rank <= surplus), NEG_BIG, s)
            return t + 1, rank[0:1, :]

        def more_to_drop(st):
            t, later = st
            return jnp.logical_and(t < n_kt, jnp.max(surplus - later) > 0.5)

        lax.while_loop(more_to_drop, drop_tile, (jnp.int32(0), jnp.zeros((1, tq), F32)))

    def logits(j, h, near, rows=tk):
        c0 = pl.multiple_of(j * tk, tk)
        g = h // HEADS_PER_GROUP
        s = jnp.dot(k_ref[0, g, pl.ds(c0, rows), :], qt_ref[0, h],
                    preferred_element_type=F32)
        if near:
            bias_row = pl.multiple_of(c0 - row0 + tk, tq)
            s = s + bias_ref[h, pl.ds(bias_row, tk), :]
        return s

    def selected(j, near, rows=tk):
        c0 = pl.multiple_of(j * tk, tk)
        keep = sc_ref[pl.ds(c0, rows), :] >= thr
        if near:
            keep = jnp.logical_and(keep, rel <= (row0 - c0))
        return keep

    def attend_exact(j, near):
        c0 = pl.multiple_of(j * tk, tk)
        mask_scr[:tk] = jnp.where(selected(j, near), 0.0, NEG_BIG)
        tile_max = []
        for h in range(N_ATTN_HEADS):
            s = logits(j, h, near) + mask_scr[:tk]
            s_scr[h] = s
            tile_max.append(jnp.max(fold(s, jnp.maximum), axis=0, keepdims=True))
        for h in range(N_ATTN_HEADS):
            g = h // HEADS_PER_GROUP
            m_old = m_ref[h]
            m_new = jnp.maximum(m_old, tile_max[h])
            alpha = jnp.exp2(m_old - m_new)
            p_scr[h, :tk] = jnp.exp2(s_scr[h] - m_new).astype(BF16)
            m_ref[h] = m_new
            pv = jnp.dot(vt_ref[0, g, :, pl.ds(c0, tk)], p_scr[h, :tk],
                         preferred_element_type=F32)
            acc_ref[h] = acc_ref[h] * alpha + pv

    def attend_fast(j, near, rows=tk):
        c0 = pl.multiple_of(j * tk, tk)
        mask_scr[:rows] = jnp.where(selected(j, near, rows), 0.0, NEG_BIG)
        for h in range(N_ATTN_HEADS):
            p_scr[h, :rows] = jnp.exp2(logits(j, h, near, rows) + mask_scr[:rows]).astype(BF16)
        for h in range(N_ATTN_HEADS):
            g = h // HEADS_PER_GROUP
            acc_ref[h] += jnp.dot(vt_ref[0, g, :, pl.ds(c0, rows)], p_scr[h, :rows],
                                  preferred_element_type=F32)

    def run_tiles(attend, far_tiles_per_step):
        acc_ref[...] = jnp.zeros(acc_ref.shape, F32)
        m_ref[...] = jnp.full(m_ref.shape, NEG_BIG, F32)
        w = far_tiles_per_step

        def far_step(t, c):
            if w == 1:
                attend(t, False)
            else:
                attend(t * w, False, w * tk)
            return c

        n_wide = lax.div(j_near, w)
        lax.fori_loop(0, n_wide, far_step, 0)
        for u in range(w - 1):
            @pl.when(n_wide * w + u < j_near)
            def _():
                attend(n_wide * w + u, False)
        for u in range(2):
            @pl.when(j_near + u < n_kt)
            def _():
                attend(j_near + u, True)

    run_tiles(attend_fast, FAST_FAR_TILES)
    denom = acc_ref[:, ATTN_HEAD_DIM:ATTN_HEAD_DIM + 1, :]
    sane = jnp.logical_and(denom > MIN_SANE_DENOM, denom < MAX_SANE_DENOM)

    @pl.when(jnp.min(sane.astype(F32)) < 0.5)
    def _():
        run_tiles(attend_exact, 1)

    outs = []
    for h in range(N_ATTN_HEADS):
        a = acc_ref[h]
        outs.append(a[:ATTN_HEAD_DIM, :] / a[ATTN_HEAD_DIM:ATTN_HEAD_DIM + 1, :])
    o_ref[0] = jnp.concatenate(outs, axis=0).T.astype(BF16)


def _dsa(qt, qit, wt, ki, k, vt, bias, *, tq, tk, k_sel):
    bsz, _, _, s = qt.shape
    assert tq >= MAX_DISTANCE and tk % tq == 0 and s % tk == 0
    body = functools.partial(_dsa_body, tq=tq, tk=tk, k_sel=k_sel)
    return pl.pallas_call(
        body,
        grid=(bsz, s // tq),
        in_specs=[pl.BlockSpec((1, N_IDX_HEADS, IDX_HEAD_DIM, tq), lambda b, i: (b, 0, 0, i)),
                  pl.BlockSpec((1, N_IDX_HEADS, tq), lambda b, i: (b, 0, i)),
                  pl.BlockSpec((1, N_ATTN_HEADS, ATTN_HEAD_DIM, tq), lambda b, i: (b, 0, 0, i)),
                  pl.BlockSpec((1, s, IDX_HEAD_DIM), lambda b, i: (b, 0, 0)),
                  pl.BlockSpec((1, N_KV_GROUPS, s, ATTN_HEAD_DIM), lambda b, i: (b, 0, 0, 0)),
                  pl.BlockSpec((1, N_KV_GROUPS, V_AUG_ROWS, s), lambda b, i: (b, 0, 0, 0)),
                  _const_spec(bias.shape)],
        out_specs=pl.BlockSpec((1, tq, ATTN_WIDTH), lambda b, i: (b, i, 0)),
        out_shape=jax.ShapeDtypeStruct((bsz, s, ATTN_WIDTH), BF16),
        scratch_shapes=[pltpu.VMEM((s, tq), F32),
                        pltpu.VMEM((N_ATTN_HEADS, V_AUG_ROWS, tq), F32),
                        pltpu.VMEM((N_ATTN_HEADS, 1, tq), F32),
                        pltpu.VMEM((N_ATTN_HEADS, tk, tq), F32),
                        pltpu.VMEM((N_ATTN_HEADS, FAST_FAR_TILES * tk, tq), BF16),
                        pltpu.VMEM((FAST_FAR_TILES * tk, tq), F32)],
        compiler_params=_params("parallel", "arbitrary"),
        name="dsa",
    )(qit, wt, qt, ki, k, vt, bias)


def _split3(a):
    hi = a.astype(BF16)
    r1 = a - hi.astype(F32)
    mid = r1.astype(BF16)
    lo = (r1 - mid.astype(F32)).astype(BF16)
    return hi, mid, lo


def _hgrn2_body(rq_ref, ri_ref, rg_ref, rf_ref, lb_ref, nw_ref, o_ref,
                state_scr, tril_scr, lvl_scr, *, chunk):
    t_len = rq_ref.shape[1]
    c = chunk
    d = REC_HEAD_DIM
    nw = nw_ref[...]
    eye = (lax.broadcasted_iota(jnp.int32, (d, d), 0)
           == lax.broadcasted_iota(jnp.int32, (d, d), 1)).astype(F32)
    sub = lax.broadcasted_iota(jnp.int32, (c, d), 0) & (REC_BAND - 1)
    level_sizes = []
    b = REC_BAND
    while b < c:
        level_sizes.append(b)
        b *= 2

    @pl.when(pl.program_id(1) == 0)
    def _():
        state_scr[...] = jnp.zeros(state_scr.shape, F32)
        ti = lax.broadcasted_iota(jnp.int32, (c, c), 0)
        si = lax.broadcasted_iota(jnp.int32, (c, c), 1)
        tril_scr[...] = (si <= ti).astype(BF16)
        for li, b in enumerate(level_sizes):
            blk = 2 * b
            sh = blk.bit_length() - 1
            m = jnp.logical_and(
                lax.shift_right_logical(ti, sh) == lax.shift_right_logical(si, sh),
                jnp.logical_and((ti & (blk - 1)) >= b, (si & (blk - 1)) < b))
            lvl_scr[li] = m.astype(F32)

    def head_step(hh, t0):
        cols = slice(hh * d, (hh + 1) * d)
        lb = lb_ref[:, cols]
        state = state_scr[hh]
        f = lb + (1.0 - lb) * _sigmoid_rel(rf_ref[0, pl.ds(t0, c), cols])
        g = jnp.log(f)
        kk = 1.0 - f
        qq = _silu(rq_ref[0, pl.ds(t0, c), cols].astype(F32))
        vb = ri_ref[0, pl.ds(t0, c), cols]
        vv = vb.astype(F32)

        g3 = _split3(g)
        tril = tril_scr[...]
        a = (jnp.dot(tril, g3[0], preferred_element_type=F32)
             + jnp.dot(tril, g3[1], preferred_element_type=F32)
             + jnp.dot(tril, g3[2], preferred_element_type=F32))
        a_last = a[c - 1:c, :]

        o = jnp.dot((qq * jnp.exp(a)).astype(BF16), state.astype(BF16),
                    preferred_element_type=F32)

        p = jnp.zeros((c, c), F32)
        for li, b in enumerate(level_sizes):
            blk = 2 * b
            r = jnp.broadcast_to(a.reshape(c // blk, blk, d)[:, b - 1:b, :],
                                 (c // blk, blk, d)).reshape(c, d)
            qt = (qq * jnp.exp(jnp.minimum(a - r, 0.0))).astype(BF16)
            kt = (kk * jnp.exp(jnp.minimum(r - a, 0.0))).astype(BF16)
            pb = lax.dot_general(qt, kt, (((1,), (1,)), ((), ())), preferred_element_type=F32)
            p = p + pb * lvl_scr[li]
        o = o + jnp.dot(p.astype(BF16), vb, preferred_element_type=F32)

        def back(x, dl):
            if dl == 0:
                return x
            return pltpu.roll(x.reshape(c // SUBLANES, SUBLANES, d), dl, axis=1).reshape(c, d)

        for dl in range(REC_BAND):
            a_s = back(a, dl)
            k_s = back(kk, dl)
            v_s = back(vv, dl)
            e = jnp.exp(jnp.where(sub >= dl, a - a_s, NEG_BIG))
            pd = jnp.sum(qq * k_s * e, axis=1, keepdims=True)
            o = o + pd * v_s

        kd = (kk * jnp.exp(a_last - a)).astype(BF16)
        upd = lax.dot_general(kd, vb, (((0,), (0,)), ((), ())), preferred_element_type=F32)
        e_col = jnp.sum(eye * jnp.exp(a_last), axis=1, keepdims=True)
        state_scr[hh] = e_col * state + upd

        y = _rms(o, nw) * _silu(rg_ref[0, pl.ds(t0, c), cols].astype(F32))
        o_ref[0, pl.ds(t0, c), cols] = y.astype(BF16)

    def step(n, carry):
        t0 = pl.multiple_of(n * c, c)
        for hh in range(N_REC_HEADS):
            head_step(hh, t0)
        return carry

    lax.fori_loop(0, t_len // c, step, 0)


def _hgrn2(rec3, rf, lb, nw, *, chunk):
    bsz, s, _ = rec3.shape
    d = REC_HEAD_DIM
    h = N_REC_HEADS
    body = functools.partial(_hgrn2_body, chunk=chunk)
    tt = _pick_tile(s, REC_TIME_TILE)
    col = lambda part: pl.BlockSpec((1, tt, REC_WIDTH), lambda b, t: (b, t, part))
    return pl.pallas_call(
        body,
        grid=(bsz, s // tt),
        in_specs=[col(0), col(1), col(2), col(0),
                  pl.BlockSpec((1, REC_WIDTH), lambda b, t: (0, 0)),
                  pl.BlockSpec((1, d), lambda b, t: (0, 0))],
        out_specs=col(0),
        out_shape=jax.ShapeDtypeStruct((bsz, s, REC_WIDTH), BF16),
        scratch_shapes=[pltpu.VMEM((h, d, d), F32)]
        + [pltpu.VMEM((chunk, chunk), BF16),
           pltpu.VMEM(((chunk // REC_BAND).bit_length() - 1, chunk, chunk), F32)],
        compiler_params=_params("parallel", "arbitrary"),
        name="hgrn2",
    )(rec3, rec3, rec3, rf, lb.reshape(1, REC_WIDTH), nw.reshape(1, d))


def _out_ffn_body(x_ref, attn_ref, rec_ref, gates_ref, mod_ref, wba_ref, wbr_ref, wo_ref,
                  n2_ref, wg_ref, wu_ref, wd_ref, fn_ref, o_ref, *, final):
    d = x_ref.shape[2]
    x = x_ref[0]
    g1 = mod_ref[0, 2:3, :]
    sh2 = mod_ref[0, 3:4, :]
    sc2 = mod_ref[0, 4:5, :]
    g2 = mod_ref[0, 5:6, :]
    ya = jnp.dot(attn_ref[0], wba_ref[...], preferred_element_type=F32)
    yr = jnp.dot(rec_ref[0], wbr_ref[...], preferred_element_type=F32)
    gates = gates_ref[0].astype(F32)
    mix = (_sigmoid(gates[:, :d]) * ya + _sigmoid(gates[:, d:]) * yr).astype(BF16)
    x1 = x + g1 * jnp.dot(mix, wo_ref[...], preferred_element_type=F32)
    h2 = (_rms(x1, n2_ref[...]) * (1.0 + sc2) + sh2).astype(BF16)
    gate = jnp.dot(h2, wg_ref[...], preferred_element_type=F32)
    up = jnp.dot(h2, wu_ref[...], preferred_element_type=F32)
    act = (_silu(gate) * up).astype(BF16)
    x2 = x1 + g2 * jnp.dot(act, wd_ref[...], preferred_element_type=F32)
    o_ref[0] = _rms(x2, fn_ref[...]) if final else x2


def _out_ffn(x, attn, rec, gates, mod, wba, wbr, wo, n2w, wg, wu, wd, fnw, tm, final):
    bsz, s, d = x.shape
    hid = wg.shape[1]
    tok = lambda w: pl.BlockSpec((1, tm, w), lambda b, i: (b, i, 0))
    return pl.pallas_call(
        functools.partial(_out_ffn_body, final=final),
        grid=(bsz, s // tm),
        in_specs=[tok(d), tok(ATTN_WIDTH), tok(REC_WIDTH), tok(2 * d),
                  pl.BlockSpec((1, 6, d), lambda b, i: (b, 0, 0)),
                  _const_spec((ATTN_WIDTH, d)), _const_spec((REC_WIDTH, d)), _const_spec((d, d)),
                  _const_spec((1, d)), _const_spec((d, hid)), _const_spec((d, hid)),
                  _const_spec((hid, d)), _const_spec((1, d))],
        out_specs=tok(d),
        out_shape=jax.ShapeDtypeStruct((bsz, s, d), F32),
        compiler_params=_params("parallel", "parallel"),
        name="out_ffn",
    )(x, attn, rec, gates, mod, wba, wbr, wo, n2w, wg, wu, wd, fnw)


def _pick_tile(n, want):
    t = min(want, n)
    while n % t:
        t //= 2
    return t


def kernel(x, c, w_ada, b_ada, norm1_w, w_in, q_norm_w, w_uq, w_uq_idx, rel_bias, lb_logits,
           rec_norm_w, w_branch_attn, w_branch_rec, w_out, norm2_w, w_ffn_gate, w_ffn_up,
           w_ffn_down, final_norm_w):
    bsz, s, d = x.shape
    depth = w_in.shape[0]
    k_sel = min(TOPK_MAX, s // 4)
    tm = _pick_tile(s, TOKEN_TILE)
    tq = _pick_tile(s, Q_TILE)
    tk = max(tq, _pick_tile(s, K_TILE))
    chunk = _pick_tile(s, REC_CHUNK)
    lower_bounds = jnp.cumsum(jax.nn.softmax(lb_logits.astype(F32), axis=0), axis=0)
    bias = _near_bias(rel_bias, tq, tk)

    o_cq = 0
    o_k = o_cq + Q_LORA_RANK
    o_v = o_k + KV_WIDTH
    o_ki = o_v + KV_WIDTH
    o_wi = o_ki + IDX_HEAD_DIM
    o_rq = o_wi + N_IDX_HEADS
    o_rf = o_rq + REC_WIDTH
    o_ri = o_rf + REC_WIDTH
    o_rg = o_ri + REC_WIDTH
    o_gt = o_rg + REC_WIDTH

    for layer in range(depth):
        mod = _ada(c, w_ada[layer], b_ada[layer]).reshape(bsz, 6, d)
        wl = w_in[layer]
        pad_a = LANES - IDX_HEAD_DIM - N_IDX_HEADS
        wa = jnp.concatenate([wl[:, o_cq:o_rq], jnp.zeros((d, pad_a), wl.dtype)], axis=1).astype(BF16)
        wr = jnp.concatenate([wl[:, o_rq:o_rf], wl[:, o_ri:o_rg], wl[:, o_rg:o_gt],
                              wl[:, o_rf:o_ri]], axis=1).astype(BF16)
        wg = wl[:, o_gt:].astype(BF16)
        wuq = (w_uq[layer] * (ATTN_HEAD_DIM ** -0.5 * LOG2_E)).astype(BF16).T
        wui = (w_uq_idx[layer] * (IDX_HEAD_DIM ** -0.5)).astype(BF16).T

        qt, qit, k, vt, ki, wt, rec3, rf, gates = _in_proj(
            x, mod, norm1_w[layer].reshape(1, d), wa, wr, wg,
            q_norm_w[layer].reshape(1, Q_LORA_RANK), wuq, wui, tm)

        attn = _dsa(qt, qit, wt, ki, k, vt, bias, tq=tq, tk=tk, k_sel=k_sel)
        rec = _hgrn2(rec3, rf, lower_bounds[layer], rec_norm_w[layer], chunk=chunk)

        x = _out_ffn(x, attn, rec, gates, mod,
                     w_branch_attn[layer].astype(BF16), w_branch_rec[layer].astype(BF16),
                     w_out[layer].astype(BF16), norm2_w[layer].reshape(1, d),
                     w_ffn_gate[layer].astype(BF16), w_ffn_up[layer].astype(BF16),
                     w_ffn_down[layer].astype(BF16),
                     final_norm_w.reshape(1, d), tm, layer == depth - 1)
    return x
```

```python
import functools
import math

import jax
import jax.numpy as jnp
from jax import lax
from jax.experimental import pallas as pl
from jax.experimental.pallas import tpu as pltpu

F32 = jnp.float32
BF16 = jnp.bfloat16

N_ATTN_HEADS = 8
ATTN_HEAD_DIM = 64
N_KV_GROUPS = 2
HEADS_PER_GROUP = N_ATTN_HEADS // N_KV_GROUPS
Q_LORA_RANK = 256
N_IDX_HEADS = 8
IDX_HEAD_DIM = 64
TOPK_MAX = 256
N_BUCKETS = 32
MAX_DISTANCE = 128
N_REC_HEADS = 4
REC_HEAD_DIM = 128
EPS = 1e-6

ATTN_WIDTH = N_ATTN_HEADS * ATTN_HEAD_DIM
KV_WIDTH = N_KV_GROUPS * ATTN_HEAD_DIM
REC_WIDTH = N_REC_HEADS * REC_HEAD_DIM

LANES = 128
SUBLANES = 8
BF16_SUBLANES = 16
VMEM_LIMIT_BYTES = 56 * 1024 * 1024

ADA_TILE = 1024
TOKEN_TILE = 512
Q_TILE = 256
K_TILE = 512
REC_CHUNK = 128
REC_TIME_TILE = 1024
REC_BAND = 4
NEG_BIG = -1e30
BISECT_CAP = 1300
SNAP_AFTER = 15
FOLD_CHAINS = 4
HI_MARGIN = 2.0 ** -20
HI_MARGIN_MIN = 1e-30
PASS_ROWS = 128
FAST_FAR_TILES = 4
LOG2_E = math.log2(math.e)
MAX_SANE_DENOM = 1e30
MIN_SANE_DENOM = 1e-18
V_AUG_ROWS = ATTN_HEAD_DIM + BF16_SUBLANES


def _rms(x, w):
    return x * lax.rsqrt(jnp.mean(x * x, axis=-1, keepdims=True) + EPS) * w


def _sigmoid(x):
    return 0.5 * jnp.tanh(0.5 * x) + 0.5


def _sigmoid_rel(x):
    return 1.0 / (1.0 + jnp.exp(-x))


def _silu(x):
    return x * _sigmoid(x)


def _params(*sem):
    return pltpu.CompilerParams(dimension_semantics=sem, vmem_limit_bytes=VMEM_LIMIT_BYTES)


def _const_spec(shape):
    nd = len(shape)
    return pl.BlockSpec(shape, lambda *_: (0,) * nd, pipeline_mode=pl.Buffered(1))


def _ada_body(c_ref, w_ref, b_ref, o_ref):
    ca = _silu(c_ref[...])
    o_ref[...] = jnp.dot(ca, w_ref[...], precision=lax.Precision.HIGHEST,
                         preferred_element_type=F32) + b_ref[...]


def _ada(c, w, b):
    bsz, d = c.shape
    n = w.shape[1]
    tn = _pick_tile(n, ADA_TILE)
    return pl.pallas_call(
        _ada_body,
        grid=(n // tn,),
        in_specs=[pl.BlockSpec((bsz, d), lambda j: (0, 0)),
                  pl.BlockSpec((d, tn), lambda j: (0, j)),
                  pl.BlockSpec((1, tn), lambda j: (0, j))],
        out_specs=pl.BlockSpec((bsz, tn), lambda j: (0, j)),
        out_shape=jax.ShapeDtypeStruct((bsz, n), F32),
        compiler_params=_params("arbitrary"),
        name="ada",
    )(c, w, b.reshape(1, n))


def _in_proj_body(x_ref, mod_ref, n1_ref, wa_ref, wr_ref, wg_ref, qn_ref, wuq_ref, wui_ref,
                  qt_ref, qit_ref, k_ref, vt_ref, ki_ref, wt_ref, rec_ref, rf_ref, gates_ref):
    tm = x_ref.shape[1]
    x = x_ref[0]
    sh1 = mod_ref[0, 0:1, :]
    sc1 = mod_ref[0, 1:2, :]
    h = (_rms(x, n1_ref[...]) * (1.0 + sc1) + sh1).astype(BF16)

    pa = jnp.dot(h, wa_ref[...], preferred_element_type=F32)
    o_k, o_v, o_ki = Q_LORA_RANK, Q_LORA_RANK + KV_WIDTH, Q_LORA_RANK + 2 * KV_WIDTH
    v_t = pa[:, o_v:o_ki].T
    kiw_t = pa[:, o_ki:].T
    for g in range(N_KV_GROUPS):
        k_ref[0, g] = pa[:, o_k + g * ATTN_HEAD_DIM:o_k + (g + 1) * ATTN_HEAD_DIM].astype(BF16)
        vt_ref[0, g, :ATTN_HEAD_DIM, :] = v_t[g * ATTN_HEAD_DIM:(g + 1) * ATTN_HEAD_DIM].astype(BF16)
        vt_ref[0, g, ATTN_HEAD_DIM:, :] = jnp.ones((V_AUG_ROWS - ATTN_HEAD_DIM, tm), BF16)
    ki_ref[0] = pa[:, o_ki:o_ki + IDX_HEAD_DIM].astype(BF16)
    wt_ref[0] = kiw_t[IDX_HEAD_DIM:IDX_HEAD_DIM + N_IDX_HEADS]
    cqn = _rms(pa[:, :Q_LORA_RANK], qn_ref[...]).astype(BF16)
    nt = (((1,), (1,)), ((), ()))
    qf = lax.dot_general(wuq_ref[...], cqn, nt, preferred_element_type=F32)
    qt_ref[0] = qf.reshape(N_ATTN_HEADS, ATTN_HEAD_DIM, tm).astype(BF16)
    qif = lax.dot_general(wui_ref[...], cqn, nt, preferred_element_type=F32)
    qit_ref[0] = qif.reshape(N_IDX_HEADS, IDX_HEAD_DIM, tm).astype(BF16)

    pr = jnp.dot(h, wr_ref[...], preferred_element_type=F32)
    rec_ref[0] = pr[:, :3 * REC_WIDTH].astype(BF16)
    rf_ref[0] = pr[:, 3 * REC_WIDTH:]
    gates_ref[0] = jnp.dot(h, wg_ref[...], preferred_element_type=F32).astype(BF16)


def _in_proj(x, mod, n1w, wa, wr, wg, qnw, wuq, wui, tm):
    bsz, s, d = x.shape
    wa_n = wa.shape[1]
    grid = (bsz, s // tm)
    tok = lambda w: pl.BlockSpec((1, tm, w), lambda b, i: (b, i, 0))
    head = lambda nh, hd: pl.BlockSpec((1, nh, hd, tm), lambda b, i: (b, 0, 0, i))
    out_shape = (
        jax.ShapeDtypeStruct((bsz, N_ATTN_HEADS, ATTN_HEAD_DIM, s), BF16),
        jax.ShapeDtypeStruct((bsz, N_IDX_HEADS, IDX_HEAD_DIM, s), BF16),
        jax.ShapeDtypeStruct((bsz, N_KV_GROUPS, s, ATTN_HEAD_DIM), BF16),
        jax.ShapeDtypeStruct((bsz, N_KV_GROUPS, V_AUG_ROWS, s), BF16),
        jax.ShapeDtypeStruct((bsz, s, IDX_HEAD_DIM), BF16),
        jax.ShapeDtypeStruct((bsz, N_IDX_HEADS, s), F32),
        jax.ShapeDtypeStruct((bsz, s, 3 * REC_WIDTH), BF16),
        jax.ShapeDtypeStruct((bsz, s, REC_WIDTH), F32),
        jax.ShapeDtypeStruct((bsz, s, 2 * d), BF16),
    )
    out_specs = (head(N_ATTN_HEADS, ATTN_HEAD_DIM), head(N_IDX_HEADS, IDX_HEAD_DIM),
                 pl.BlockSpec((1, N_KV_GROUPS, tm, ATTN_HEAD_DIM), lambda b, i: (b, 0, i, 0)),
                 head(N_KV_GROUPS, V_AUG_ROWS),
                 tok(IDX_HEAD_DIM),
                 pl.BlockSpec((1, N_IDX_HEADS, tm), lambda b, i: (b, 0, i)),
                 tok(3 * REC_WIDTH), tok(REC_WIDTH), tok(2 * d))
    return pl.pallas_call(
        _in_proj_body,
        grid=grid,
        in_specs=[tok(d),
                  pl.BlockSpec((1, 6, d), lambda b, i: (b, 0, 0)),
                  _const_spec((1, d)),
                  _const_spec((d, wa_n)),
                  _const_spec((d, 4 * REC_WIDTH)),
                  _const_spec((d, 2 * d)),
                  _const_spec((1, Q_LORA_RANK)),
                  _const_spec((ATTN_WIDTH, Q_LORA_RANK)),
                  _const_spec((N_IDX_HEADS * IDX_HEAD_DIM, Q_LORA_RANK))],
        out_specs=out_specs,
        out_shape=out_shape,
        compiler_params=_params("parallel", "parallel"),
        name="in_proj",
    )(x, mod, n1w, wa, wr, wg, qnw, wuq, wui)


def _near_bias(rel_bias, tq, tk):
    n = jnp.arange(MAX_DISTANCE + 1, dtype=jnp.int32)
    max_exact = N_BUCKETS // 2
    nf = jnp.maximum(n, 1).astype(F32)
    large = max_exact + (jnp.log(nf / max_exact) / math.log(MAX_DISTANCE / max_exact)
                         * (N_BUCKETS - max_exact)).astype(jnp.int32)
    bucket = jnp.where(n < max_exact, n, jnp.minimum(large, N_BUCKETS - 1))
    bucket = bucket.at[MAX_DISTANCE].set(N_BUCKETS - 1)
    ids = jnp.arange(N_BUCKETS, dtype=jnp.int32)
    start = jnp.sum((bucket[None, :] < ids[:, None]).astype(jnp.int32), axis=1)
    rb = (rel_bias.astype(F32) - rel_bias[N_BUCKETS - 1].astype(F32)[None, :]) * LOG2_E
    step = rb - jnp.concatenate([jnp.zeros_like(rb[:1]), rb[:-1]], axis=0)
    dist = (jnp.arange(tq, dtype=jnp.int32)[None, :]
            - jnp.arange(2 * tk, dtype=jnp.int32)[:, None] + tk)
    reached = (dist[None, :, :] >= start[:, None, None]).astype(F32)
    return jnp.einsum("bh,but->hut", step, reached, precision=lax.Precision.HIGHEST)


def _dsa_body(qit_ref, w_ref, qt_ref, ki_ref, k_ref, vt_ref, bias_ref, o_ref,
              sc_ref, acc_ref, m_ref, s_scr, p_scr, mask_scr, *, tq, tk, k_sel):
    i = pl.program_id(1)
    row0 = i * tq
    n_kt = lax.div(row0 + tq + (tk - 1), tk)
    j_near = lax.div(jnp.maximum(row0 - (MAX_DISTANCE - 1), 0), tk)
    krow = lax.broadcasted_iota(jnp.int32, (tk, tq), 0)
    qcol = lax.broadcasted_iota(jnp.int32, (tk, tq), 1)
    rel = krow - qcol
    kf = float(k_sel)

    def fold(a, op, group=SUBLANES):
        n = a.shape[0] // group
        a = a.reshape(n // FOLD_CHAINS, FOLD_CHAINS, group, tq)
        r = a[0]
        for t in range(1, n // FOLD_CHAINS):
            r = op(r, a[t])
        while r.shape[0] > 1:
            half = r.shape[0] // 2
            r = op(r[:half], r[half:])
        return r[0]

    w = w_ref[0] * (N_IDX_HEADS ** -0.5)

    def score_tile(j, carry):
        mx, mn = carry
        c0 = pl.multiple_of(j * tk, tk)
        kin = ki_ref[0, pl.ds(c0, tk), :]
        sc = None
        for h in range(N_IDX_HEADS):
            lg = jnp.dot(kin, qit_ref[0, h], preferred_element_type=F32)
            term = jnp.maximum(lg, 0.0) * w[h:h + 1, :]
            sc = term if sc is None else sc + term
        causal = rel <= (row0 - c0)
        scm = jnp.where(causal, sc, NEG_BIG)
        sc_ref[pl.ds(c0, tk), :] = scm
        mx = jnp.maximum(mx, fold(scm, jnp.maximum))
        mn = jnp.minimum(mn, fold(jnp.where(causal, sc, -NEG_BIG), jnp.minimum))
        return mx, mn

    def score_pair(t, carry):
        carry = score_tile(2 * t, carry)
        return score_tile(jnp.minimum(2 * t + 1, n_kt - 1), carry)

    mx8, mn8 = lax.fori_loop(0, lax.div(n_kt + 1, 2), score_pair,
                             (jnp.full((SUBLANES, tq), NEG_BIG, F32),
                              jnp.full((SUBLANES, tq), -NEG_BIG, F32)))
    mx = jnp.max(mx8, axis=0, keepdims=True)
    mn = jnp.min(mn8, axis=0, keepdims=True)

    def key_pass(fn, op, init):
        def one_tile(j, acc):
            c0 = pl.multiple_of(j * tk, tk)
            for u in range(tk // PASS_ROWS):
                s = sc_ref[pl.ds(c0 + u * PASS_ROWS, PASS_ROWS), :]
                acc = op(acc, fold(fn(s), op))
            return acc

        pairs = lax.shift_right_logical(n_kt, 1)
        acc = lax.fori_loop(0, pairs, lambda t, a: one_tile(t + pairs, one_tile(t, a)),
                            jnp.full((SUBLANES, tq), init, F32))
        return lax.cond((n_kt & 1) == 1, lambda a: one_tile(n_kt - 1, a), lambda a: a, acc)

    def count(pred):
        part = key_pass(lambda s: jnp.where(pred(s), 1.0, 0.0), jnp.add, 0.0)
        return jnp.sum(part, axis=0, keepdims=True)

    n_valid = (row0 + 1 + lax.broadcasted_iota(jnp.int32, (1, tq), 1)).astype(F32)
    take_all = n_valid <= kf
    lo0 = jnp.where(take_all, NEG_BIG, mn)
    hi0 = mx + (jnp.abs(mx) * HI_MARGIN + HI_MARGIN_MIN)

    def bis_cond(st):
        it, lo, hi, c_lo, done_f = st
        return jnp.logical_and(it < BISECT_CAP, jnp.min(done_f) < 0.5)

    def halve(st):
        lo, hi, c_lo, done_f = st
        done = done_f > 0.5
        mid = lo + (hi - lo) * 0.5
        stuck = jnp.logical_or(mid <= lo, mid >= hi)
        cnt = count(lambda s: s >= mid)
        ge = cnt >= kf
        upd = jnp.logical_not(jnp.logical_or(done, stuck))
        raise_lo = jnp.logical_and(upd, ge)
        lo = jnp.where(raise_lo, mid, lo)
        c_lo = jnp.where(raise_lo, cnt, c_lo)
        hi = jnp.where(jnp.logical_and(upd, jnp.logical_not(ge)), mid, hi)
        done = jnp.logical_or(jnp.logical_or(done, stuck), jnp.logical_and(ge, cnt == kf))
        return lo, hi, c_lo, done.astype(F32)

    def snap(st):
        lo, hi, c_lo, done_f = st
        done = done_f > 0.5

        below = jnp.max(key_pass(lambda s: jnp.where(s < hi, s, NEG_BIG), jnp.maximum, NEG_BIG),
                        axis=0, keepdims=True)
        cnt = count(lambda s: s >= below)
        found = jnp.logical_and(jnp.logical_not(done), cnt >= kf)
        lower_hi = jnp.logical_and(jnp.logical_not(done), cnt < kf)
        lo = jnp.where(found, below, lo)
        c_lo = jnp.where(found, cnt, c_lo)
        hi = jnp.where(lower_hi, below, hi)
        return lo, hi, c_lo, jnp.logical_or(done, found).astype(F32)

    def bis_step(st):
        it, lo, hi, c_lo, done_f = st
        lo, hi, c_lo, done_f = lax.cond((it & 3) != 3, snap, halve, (lo, hi, c_lo, done_f))
        return it + 1, lo, hi, c_lo, done_f

    st = lax.fori_loop(0, SNAP_AFTER, lambda _, s: halve(s),
                       (lo0, hi0, n_valid, take_all.astype(F32)))
    _, thr, _, c_thr, _ = lax.while_loop(bis_cond, bis_step, (jnp.int32(0),) + st)

    excess = jnp.logical_and(jnp.logical_not(take_all), c_thr > kf)

    @pl.when(jnp.max(excess.astype(F32)) > 0.5)
    def _():
        surplus = jnp.where(excess, c_thr - kf, 0.0)
        triu = (lax.broadcasted_iota(jnp.int32, (tk, tk), 1)
                >= lax.broadcasted_iota(jnp.int32, (tk, tk), 0)).astype(BF16)

        def drop_tile(t, later):
            c0 = pl.multiple_of((n_kt - 1 - t) * tk, tk)
            s = sc_ref[pl.ds(c0, tk), :]
            eq = s == thr
            rank = later + jnp.dot(triu, jnp.where(eq, 1.0, 0.0).astype(BF16),
                                   preferred_element_type=F32)
            sc_ref[pl.ds(c0, tk), :] = jnp.where(jnp.logical_and(eq, rank <= surplus), NEG_BIG, s)
            return rank[0:1, :]

        lax.fori_loop(0, n_kt, drop_tile, jnp.zeros((1, tq), F32))

    def logits(j, h, near, rows=tk):
        c0 = pl.multiple_of(j * tk, tk)
        g = h // HEADS_PER_GROUP
        s = jnp.dot(k_ref[0, g, pl.ds(c0, rows), :], qt_ref[0, h],
                    preferred_element_type=F32)
        if near:
            bias_row = pl.multiple_of(c0 - row0 + tk, tq)
            s = s + bias_ref[h, pl.ds(bias_row, tk), :]
        return s

    def selected(j, near, rows=tk):
        c0 = pl.multiple_of(j * tk, tk)
        keep = sc_ref[pl.ds(c0, rows), :] >= thr
        if near:
            keep = jnp.logical_and(keep, rel <= (row0 - c0))
        return keep

    def attend_exact(j, near):
        c0 = pl.multiple_of(j * tk, tk)
        mask_scr[:tk] = jnp.where(selected(j, near), 0.0, NEG_BIG)
        tile_max = []
        for h in range(N_ATTN_HEADS):
            s = logits(j, h, near) + mask_scr[:tk]
            s_scr[h] = s
            tile_max.append(jnp.max(fold(s, jnp.maximum), axis=0, keepdims=True))
        for h in range(N_ATTN_HEADS):
            g = h // HEADS_PER_GROUP
            m_old = m_ref[h]
            m_new = jnp.maximum(m_old, tile_max[h])
            alpha = jnp.exp2(m_old - m_new)
            p_scr[h, :tk] = jnp.exp2(s_scr[h] - m_new).astype(BF16)
            m_ref[h] = m_new
            pv = jnp.dot(vt_ref[0, g, :, pl.ds(c0, tk)], p_scr[h, :tk],
                         preferred_element_type=F32)
            acc_ref[h] = acc_ref[h] * alpha + pv

    def attend_fast(j, near, rows=tk):
        c0 = pl.multiple_of(j * tk, tk)
        mask_scr[:rows] = jnp.where(selected(j, near, rows), 0.0, NEG_BIG)
        for h in range(N_ATTN_HEADS):
            p_scr[h, :rows] = jnp.exp2(logits(j, h, near, rows) + mask_scr[:rows]).astype(BF16)
        for h in range(N_ATTN_HEADS):
            g = h // HEADS_PER_GROUP
            acc_ref[h] += jnp.dot(vt_ref[0, g, :, pl.ds(c0, rows)], p_scr[h, :rows],
                                  preferred_element_type=F32)

    def run_tiles(attend, far_tiles_per_step):
        acc_ref[...] = jnp.zeros(acc_ref.shape, F32)
        m_ref[...] = jnp.full(m_ref.shape, NEG_BIG, F32)
        w = far_tiles_per_step

        def far_step(t, c):
            if w == 1:
                attend(t, False)
            else:
                attend(t * w, False, w * tk)
            return c

        n_wide = lax.div(j_near, w)
        lax.fori_loop(0, n_wide, far_step, 0)
        for u in range(w - 1):
            @pl.when(n_wide * w + u < j_near)
            def _():
                attend(n_wide * w + u, False)
        for u in range(2):
            @pl.when(j_near + u < n_kt)
            def _():
                attend(j_near + u, True)

    run_tiles(attend_fast, FAST_FAR_TILES)
    denom = acc_ref[:, ATTN_HEAD_DIM:ATTN_HEAD_DIM + 1, :]
    sane = jnp.logical_and(denom > MIN_SANE_DENOM, denom < MAX_SANE_DENOM)

    @pl.when(jnp.min(sane.astype(F32)) < 0.5)
    def _():
        run_tiles(attend_exact, 1)

    outs = []
    for h in range(N_ATTN_HEADS):
        a = acc_ref[h]
        outs.append(a[:ATTN_HEAD_DIM, :] / a[ATTN_HEAD_DIM:ATTN_HEAD_DIM + 1, :])
    o_ref[0] = jnp.concatenate(outs, axis=0).T.astype(BF16)


def _dsa(qt, qit, wt, ki, k, vt, bias, *, tq, tk, k_sel):
    bsz, _, _, s = qt.shape
    assert tq >= MAX_DISTANCE and tk % tq == 0 and s % tk == 0
    body = functools.partial(_dsa_body, tq=tq, tk=tk, k_sel=k_sel)
    return pl.pallas_call(
        body,
        grid=(bsz, s // tq),
        in_specs=[pl.BlockSpec((1, N_IDX_HEADS, IDX_HEAD_DIM, tq), lambda b, i: (b, 0, 0, i)),
                  pl.BlockSpec((1, N_IDX_HEADS, tq), lambda b, i: (b, 0, i)),
                  pl.BlockSpec((1, N_ATTN_HEADS, ATTN_HEAD_DIM, tq), lambda b, i: (b, 0, 0, i)),
                  pl.BlockSpec((1, s, IDX_HEAD_DIM), lambda b, i: (b, 0, 0)),
                  pl.BlockSpec((1, N_KV_GROUPS, s, ATTN_HEAD_DIM), lambda b, i: (b, 0, 0, 0)),
                  pl.BlockSpec((1, N_KV_GROUPS, V_AUG_ROWS, s), lambda b, i: (b, 0, 0, 0)),
                  _const_spec(bias.shape)],
        out_specs=pl.BlockSpec((1, tq, ATTN_WIDTH), lambda b, i: (b, i, 0)),
        out_shape=jax.ShapeDtypeStruct((bsz, s, ATTN_WIDTH), BF16),
        scratch_shapes=[pltpu.VMEM((s, tq), F32),
                        pltpu.VMEM((N_ATTN_HEADS, V_AUG_ROWS, tq), F32),
                        pltpu.VMEM((N_ATTN_HEADS, 1, tq), F32),
                        pltpu.VMEM((N_ATTN_HEADS, tk, tq), F32),
                        pltpu.VMEM((N_ATTN_HEADS, FAST_FAR_TILES * tk, tq), BF16),
                        pltpu.VMEM((FAST_FAR_TILES * tk, tq), F32)],
        compiler_params=_params("parallel", "arbitrary"),
        name="dsa",
    )(qit, wt, qt, ki, k, vt, bias)


def _split3(a):
    hi = a.astype(BF16)
    r1 = a - hi.astype(F32)
    mid = r1.astype(BF16)
    lo = (r1 - mid.astype(F32)).astype(BF16)
    return hi, mid, lo


def _hgrn2_body(rq_ref, ri_ref, rg_ref, rf_ref, lb_ref, nw_ref, o_ref,
                state_scr, tril_scr, lvl_scr, *, chunk):
    t_len = rq_ref.shape[1]
    c = chunk
    d = REC_HEAD_DIM
    nw = nw_ref[...]
    eye = (lax.broadcasted_iota(jnp.int32, (d, d), 0)
           == lax.broadcasted_iota(jnp.int32, (d, d), 1)).astype(F32)
    sub = lax.broadcasted_iota(jnp.int32, (c, d), 0) & (REC_BAND - 1)
    level_sizes = []
    b = REC_BAND
    while b < c:
        level_sizes.append(b)
        b *= 2

    @pl.when(pl.program_id(1) == 0)
    def _():
        state_scr[...] = jnp.zeros(state_scr.shape, F32)
        ti = lax.broadcasted_iota(jnp.int32, (c, c), 0)
        si = lax.broadcasted_iota(jnp.int32, (c, c), 1)
        tril_scr[...] = (si <= ti).astype(BF16)
        for li, b in enumerate(level_sizes):
            blk = 2 * b
            sh = blk.bit_length() - 1
            m = jnp.logical_and(
                lax.shift_right_logical(ti, sh) == lax.shift_right_logical(si, sh),
                jnp.logical_and((ti & (blk - 1)) >= b, (si & (blk - 1)) < b))
            lvl_scr[li] = m.astype(F32)

    def head_step(hh, t0):
        cols = slice(hh * d, (hh + 1) * d)
        lb = lb_ref[:, cols]
        state = state_scr[hh]
        f = lb + (1.0 - lb) * _sigmoid_rel(rf_ref[0, pl.ds(t0, c), cols])
        g = jnp.log(f)
        kk = 1.0 - f
        qq = _silu(rq_ref[0, pl.ds(t0, c), cols].astype(F32))
        vb = ri_ref[0, pl.ds(t0, c), cols]
        vv = vb.astype(F32)

        g3 = _split3(g)
        tril = tril_scr[...]
        a = (jnp.dot(tril, g3[0], preferred_element_type=F32)
             + jnp.dot(tril, g3[1], preferred_element_type=F32)
             + jnp.dot(tril, g3[2], preferred_element_type=F32))
        a_last = a[c - 1:c, :]

        o = jnp.dot((qq * jnp.exp(a)).astype(BF16), state.astype(BF16),
                    preferred_element_type=F32)

        p = jnp.zeros((c, c), F32)
        for li, b in enumerate(level_sizes):
            blk = 2 * b
            r = jnp.broadcast_to(a.reshape(c // blk, blk, d)[:, b - 1:b, :],
                                 (c // blk, blk, d)).reshape(c, d)
            qt = (qq * jnp.exp(jnp.minimum(a - r, 0.0))).astype(BF16)
            kt = (kk * jnp.exp(jnp.minimum(r - a, 0.0))).astype(BF16)
            pb = lax.dot_general(qt, kt, (((1,), (1,)), ((), ())), preferred_element_type=F32)
            p = p + pb * lvl_scr[li]
        o = o + jnp.dot(p.astype(BF16), vb, preferred_element_type=F32)

        def back(x, dl):
            if dl == 0:
                return x
            return pltpu.roll(x.reshape(c // SUBLANES, SUBLANES, d), dl, axis=1).reshape(c, d)

        for dl in range(REC_BAND):
            a_s = back(a, dl)
            k_s = back(kk, dl)
            v_s = back(vv, dl)
            e = jnp.exp(jnp.where(sub >= dl, a - a_s, NEG_BIG))
            pd = jnp.sum(qq * k_s * e, axis=1, keepdims=True)
            o = o + pd * v_s

        kd = (kk * jnp.exp(a_last - a)).astype(BF16)
        upd = lax.dot_general(kd, vb, (((0,), (0,)), ((), ())), preferred_element_type=F32)
        e_col = jnp.sum(eye * jnp.exp(a_last), axis=1, keepdims=True)
        state_scr[hh] = e_col * state + upd

        y = _rms(o, nw) * _silu(rg_ref[0, pl.ds(t0, c), cols].astype(F32))
        o_ref[0, pl.ds(t0, c), cols] = y.astype(BF16)

    def step(n, carry):
        t0 = pl.multiple_of(n * c, c)
        for hh in range(N_REC_HEADS):
            head_step(hh, t0)
        return carry

    lax.fori_loop(0, t_len // c, step, 0)


def _hgrn2(rec3, rf, lb, nw, *, chunk):
    bsz, s, _ = rec3.shape
    d = REC_HEAD_DIM
    h = N_REC_HEADS
    body = functools.partial(_hgrn2_body, chunk=chunk)
    tt = _pick_tile(s, REC_TIME_TILE)
    col = lambda part: pl.BlockSpec((1, tt, REC_WIDTH), lambda b, t: (b, t, part))
    return pl.pallas_call(
        body,
        grid=(bsz, s // tt),
        in_specs=[col(0), col(1), col(2), col(0),
                  pl.BlockSpec((1, REC_WIDTH), lambda b, t: (0, 0)),
                  pl.BlockSpec((1, d), lambda b, t: (0, 0))],
        out_specs=col(0),
        out_shape=jax.ShapeDtypeStruct((bsz, s, REC_WIDTH), BF16),
        scratch_shapes=[pltpu.VMEM((h, d, d), F32)]
        + [pltpu.VMEM((chunk, chunk), BF16),
           pltpu.VMEM(((chunk // REC_BAND).bit_length() - 1, chunk, chunk), F32)],
        compiler_params=_params("parallel", "arbitrary"),
        name="hgrn2",
    )(rec3, rec3, rec3, rf, lb.reshape(1, REC_WIDTH), nw.reshape(1, d))


def _out_ffn_body(x_ref, attn_ref, rec_ref, gates_ref, mod_ref, wba_ref, wbr_ref, wo_ref,
                  n2_ref, wg_ref, wu_ref, wd_ref, fn_ref, o_ref, *, final):
    d = x_ref.shape[2]
    x = x_ref[0]
    g1 = mod_ref[0, 2:3, :]
    sh2 = mod_ref[0, 3:4, :]
    sc2 = mod_ref[0, 4:5, :]
    g2 = mod_ref[0, 5:6, :]
    ya = jnp.dot(attn_ref[0], wba_ref[...], preferred_element_type=F32)
    yr = jnp.dot(rec_ref[0], wbr_ref[...], preferred_element_type=F32)
    gates = gates_ref[0].astype(F32)
    mix = (_sigmoid(gates[:, :d]) * ya + _sigmoid(gates[:, d:]) * yr).astype(BF16)
    x1 = x + g1 * jnp.dot(mix, wo_ref[...], preferred_element_type=F32)
    h2 = (_rms(x1, n2_ref[...]) * (1.0 + sc2) + sh2).astype(BF16)
    gate = jnp.dot(h2, wg_ref[...], preferred_element_type=F32)
    up = jnp.dot(h2, wu_ref[...], preferred_element_type=F32)
    act = (_silu(gate) * up).astype(BF16)
    x2 = x1 + g2 * jnp.dot(act, wd_ref[...], preferred_element_type=F32)
    o_ref[0] = _rms(x2, fn_ref[...]) if final else x2


def _out_ffn(x, attn, rec, gates, mod, wba, wbr, wo, n2w, wg, wu, wd, fnw, tm, final):
    bsz, s, d = x.shape
    hid = wg.shape[1]
    tok = lambda w: pl.BlockSpec((1, tm, w), lambda b, i: (b, i, 0))
    return pl.pallas_call(
        functools.partial(_out_ffn_body, final=final),
        grid=(bsz, s // tm),
        in_specs=[tok(d), tok(ATTN_WIDTH), tok(REC_WIDTH), tok(2 * d),
                  pl.BlockSpec((1, 6, d), lambda b, i: (b, 0, 0)),
                  _const_spec((ATTN_WIDTH, d)), _const_spec((REC_WIDTH, d)), _const_spec((d, d)),
                  _const_spec((1, d)), _const_spec((d, hid)), _const_spec((d, hid)),
                  _const_spec((hid, d)), _const_spec((1, d))],
        out_specs=tok(d),
        out_shape=jax.ShapeDtypeStruct((bsz, s, d), F32),
        compiler_params=_params("parallel", "parallel"),
        name="out_ffn",
    )(x, attn, rec, gates, mod, wba, wbr, wo, n2w, wg, wu, wd, fnw)


def _pick_tile(n, want):
    t = min(want, n)
    while n % t:
        t //= 2
    return t


def kernel(x, c, w_ada, b_ada, norm1_w, w_in, q_norm_w, w_uq, w_uq_idx, rel_bias, lb_logits,
           rec_norm_w, w_branch_attn, w_branch_rec, w_out, norm2_w, w_ffn_gate, w_ffn_up,
           w_ffn_down, final_norm_w):
    bsz, s, d = x.shape
    depth = w_in.shape[0]
    k_sel = min(TOPK_MAX, s // 4)
    tm = _pick_tile(s, TOKEN_TILE)
    tq = _pick_tile(s, Q_TILE)
    tk = max(tq, _pick_tile(s, K_TILE))
    chunk = _pick_tile(s, REC_CHUNK)
    lower_bounds = jnp.cumsum(jax.nn.softmax(lb_logits.astype(F32), axis=0), axis=0)
    bias = _near_bias(rel_bias, tq, tk)

    o_cq = 0
    o_k = o_cq + Q_LORA_RANK
    o_v = o_k + KV_WIDTH
    o_ki = o_v + KV_WIDTH
    o_wi = o_ki + IDX_HEAD_DIM
    o_rq = o_wi + N_IDX_HEADS
    o_rf = o_rq + REC_WIDTH
    o_ri = o_rf + REC_WIDTH
    o_rg = o_ri + REC_WIDTH
    o_gt = o_rg + REC_WIDTH

    for layer in range(depth):
        mod = _ada(c, w_ada[layer], b_ada[layer]).reshape(bsz, 6, d)
        wl = w_in[layer]
        pad_a = LANES - IDX_HEAD_DIM - N_IDX_HEADS
        wa = jnp.concatenate([wl[:, o_cq:o_rq], jnp.zeros((d, pad_a), wl.dtype)], axis=1).astype(BF16)
        wr = jnp.concatenate([wl[:, o_rq:o_rf], wl[:, o_ri:o_rg], wl[:, o_rg:o_gt],
                              wl[:, o_rf:o_ri]], axis=1).astype(BF16)
        wg = wl[:, o_gt:].astype(BF16)
        wuq = (w_uq[layer] * (ATTN_HEAD_DIM ** -0.5 * LOG2_E)).astype(BF16).T
        wui = (w_uq_idx[layer] * (IDX_HEAD_DIM ** -0.5)).astype(BF16).T

        qt, qit, k, vt, ki, wt, rec3, rf, gates = _in_proj(
            x, mod, norm1_w[layer].reshape(1, d), wa, wr, wg,
            q_norm_w[layer].reshape(1, Q_LORA_RANK), wuq, wui, tm)

        attn = _dsa(qt, qit, wt, ki, k, vt, bias, tq=tq, tk=tk, k_sel=k_sel)
        rec = _hgrn2(rec3, rf, lower_bounds[layer], rec_norm_w[layer], chunk=chunk)

        x = _out_ffn(x, attn, rec, gates, mod,
                     w_branch_attn[layer].astype(BF16), w_branch_rec[layer].astype(BF16),
                     w_out[layer].astype(BF16), norm2_w[layer].reshape(1, d),
                     w_ffn_gate[layer].astype(BF16), w_ffn_up[layer].astype(BF16),
                     w_ffn_down[layer].astype(BF16),
                     final_norm_w.reshape(1, d), tm, layer == depth - 1)
    return x
```
